```python
import math
import jax
import jax.numpy as jnp
from jax import lax
import numpy as np

D_MODEL = 2048
BATCH = 4
SEQ = 2048
DEPTH = 1
DEC_BATCH = 8
DEC_SEQ = 8
PAST_LEN = 16384
PAGE_SIZE = 128

MIX_WIDTH = D_MODEL
M_HEADS = 8
M_V_DIM = (MIX_WIDTH // 2) // M_HEADS
M_QK_DIM = M_V_DIM // 2
M_CHUNK = 64
A_HEADS = 8
A_HEAD_DIM = (MIX_WIDTH - M_HEADS * M_V_DIM) // A_HEADS
MOBA_BLOCK = 256
MOBA_TOP_K = 3
MOBA_Q_BLOCK = 16
NUM_BUCKETS = 32
MAX_EXACT = NUM_BUCKETS // 2
MAX_DISTANCE = 128
D_FF = 4 * D_MODEL
PLE_DIM = 256
EPS = 1e-6
SPLITS = (M_HEADS * M_QK_DIM, M_HEADS * M_QK_DIM, M_HEADS * M_V_DIM, M_HEADS * M_V_DIM,
          M_HEADS, M_HEADS, A_HEADS * A_HEAD_DIM, A_HEADS * A_HEAD_DIM, A_HEADS * A_HEAD_DIM)
D_IN = sum(SPLITS)

kernel_name = "hymba_mlstm_moba_decoder_step"


def rms_norm(x, g):
    x32 = x.astype(jnp.float32)
    y = x32 * lax.rsqrt(jnp.mean(x32 * x32, axis=-1, keepdims=True) + EPS)
    return (y * g.astype(jnp.float32)).astype(x.dtype)


def split_cols(z):
    offsets = [int(o) for o in np.cumsum(SPLITS)[:-1]]
    return jnp.split(z, offsets, axis=-1)


def t5_bucket(rel):
    n = jnp.maximum(rel, 0)
    nf = jnp.maximum(n, 1).astype(jnp.float32)
    large = MAX_EXACT + (jnp.log(nf / MAX_EXACT) / math.log(MAX_DISTANCE / MAX_EXACT)
                         * (NUM_BUCKETS - MAX_EXACT)).astype(jnp.int32)
    large = jnp.minimum(large, NUM_BUCKETS - 1)
    return jnp.where(n < MAX_EXACT, n, large)


def mlstm_chunkwise(q, k, v, li, lf, C0, n0, m0):
    B, H, T, dk = q.shape
    dv = v.shape[-1]
    L = math.gcd(T, M_CHUNK)
    NC = T // L

    def to_chunks(a):
        return jnp.moveaxis(a.reshape((B, H, NC, L) + a.shape[3:]), 2, 0)

    causal = jnp.tril(jnp.ones((L, L), dtype=bool))

    def step(carry, inp):
        C, n, m = carry
        qc, kc, vc, lic, lfc = inp
        b = jnp.cumsum(lfc, axis=-1)
        a = b + m[..., None]
        D = b[..., :, None] - b[..., None, :] + lic[..., None, :]
        D = jnp.where(causal, D, -jnp.inf)
        m_t = jnp.maximum(a, jnp.max(D, axis=-1))
        S = jnp.einsum('bhtd,bhsd->bhts', qc, kc) * jnp.exp(D - m_t[..., None])
        inter = jnp.exp(a - m_t)
        num = inter[..., None] * jnp.einsum('bhtd,bhdv->bhtv', qc, C) + jnp.einsum('bhts,bhsv->bhtv', S, vc)
        den = inter * jnp.einsum('bhtd,bhd->bht', qc, n) + jnp.sum(S, axis=-1)
        h = num / jnp.maximum(jnp.abs(den), jnp.exp(-m_t))[..., None]
        m_L = m_t[..., -1]
        w_prev = jnp.exp(b[..., -1] + m - m_L)
        ws = jnp.exp(b[..., -1:] - b + lic - m_L[..., None])
        C_new = w_prev[..., None, None] * C + jnp.einsum('bhs,bhsd,bhsv->bhdv', ws, kc, vc)
        n_new = w_prev[..., None] * n + jnp.einsum('bhs,bhsd->bhd', ws, kc)
        return (C_new, n_new, m_L), h

    (C, n, m), hs = lax.scan(step, (C0, n0, m0),
                             (to_chunks(q), to_chunks(k), to_chunks(v), to_chunks(li), to_chunks(lf)))
    h = jnp.moveaxis(hs, 0, 2).reshape(B, H, T, dv)
    return h, C, n, m


def moba_attention(q, k_all, v_all, q_pos, rel_table):
    B, H, Tq, dh = q.shape
    NB = k_all.shape[2] // MOBA_BLOCK
    kb = k_all.reshape(B, H, NB, MOBA_BLOCK, dh)
    vb = v_all.reshape(B, H, NB, MOBA_BLOCK, dh)
    kmean = jnp.mean(kb, axis=3)
    QB = math.gcd(Tq, MOBA_Q_BLOCK)
    NQ = Tq // QB
    qs = jnp.moveaxis(q.reshape(B, H, NQ, QB, dh), 2, 0)
    ps = q_pos.reshape(NQ, QB)
    bi = jnp.arange(B)[:, None, None, None]
    hi = jnp.arange(H)[None, :, None, None]
    bias_hb = rel_table.T.astype(jnp.float32)
    offs = jnp.arange(MOBA_BLOCK, dtype=jnp.int32)
    k_top = min(MOBA_TOP_K, NB)

    def attend(args):
        qb, pb = args
        own = pb // MOBA_BLOCK
        gate = jnp.einsum('bhqd,bhnd->bhqn', qb, kmean)
        past_ok = jnp.arange(NB)[None, :] < own[:, None]
        gate = jnp.where(past_ok, gate, -jnp.inf)
        _, top_idx = lax.top_k(gate, k_top)
        sel_ok = top_idx < own[:, None]
        idx = jnp.concatenate([top_idx, jnp.broadcast_to(own[:, None], (B, H, QB, 1))], axis=-1)
        ok = jnp.concatenate([sel_ok, jnp.ones((B, H, QB, 1), dtype=bool)], axis=-1)
        kg = kb[bi, hi, idx]
        vg = vb[bi, hi, idx]
        kpos = idx[..., None] * MOBA_BLOCK + offs
        rel = pb[:, None, None] - kpos
        mask = ok[..., None] & (rel >= 0)
        logits = jnp.einsum('bhqd,bhqnkd->bhqnk', qb, kg) + bias_hb[hi[..., None], t5_bucket(rel)]
        logits = jnp.where(mask, logits, -jnp.inf)
        w = jax.nn.softmax(logits.reshape(B, H, QB, -1), axis=-1).reshape(logits.shape)
        return jnp.einsum('bhqnk,bhqnkd->bhqd', w, vg)

    out = lax.map(attend, (qs, ps))
    return jnp.moveaxis(out, 0, 2).reshape(B, H, Tq, dh)


def decoder_layer(h, pe, pos0, past_k, past_v, C0, n0, m0, rel_table, w_in, b_igate, b_fgate,
                  g_mix, g_mhead, w_out, g_ffn, w_up, w_down, g_ple, w_ple_gate, w_ple_proj):
    B, T, _ = h.shape
    dt = h.dtype
    f32 = jnp.float32
    z = rms_norm(h, g_mix) @ w_in
    mq, mk, mv, mo, mi, mf, aq, ak, av = split_cols(z)

    def heads(a, nh):
        return a.reshape(B, T, nh, -1).transpose(0, 2, 1, 3).astype(f32)

    q_m = heads(mq, M_HEADS)
    k_m = heads(mk, M_HEADS) * (M_QK_DIM ** -0.5)
    v_m = heads(mv, M_HEADS)
    li = (mi + b_igate).astype(f32).transpose(0, 2, 1)
    lf = jax.nn.log_sigmoid((mf + b_fgate).astype(f32)).transpose(0, 2, 1)
    h_m, C, n, m = mlstm_chunkwise(q_m, k_m, v_m, li, lf, C0.astype(f32), n0.astype(f32), m0.astype(f32))
    h_m = rms_norm(h_m.transpose(0, 2, 1, 3), g_mhead)
    out_m = (jax.nn.sigmoid(mo.astype(f32).reshape(B, T, M_HEADS, M_V_DIM)) * h_m).reshape(B, T, -1)

    k_new = ak.reshape(B, T, A_HEADS, A_HEAD_DIM)
    v_new = av.reshape(B, T, A_HEADS, A_HEAD_DIM)
    k_all = jnp.concatenate([past_k, k_new.astype(past_k.dtype)], axis=1).astype(f32)
    v_all = jnp.concatenate([past_v, v_new.astype(past_v.dtype)], axis=1).astype(f32)
    pad = (-k_all.shape[1]) % MOBA_BLOCK
    k_all = jnp.pad(k_all, ((0, 0), (0, pad), (0, 0), (0, 0))).transpose(0, 2, 1, 3)
    v_all = jnp.pad(v_all, ((0, 0), (0, pad), (0, 0), (0, 0))).transpose(0, 2, 1, 3)
    q_a = heads(aq, A_HEADS) * (A_HEAD_DIM ** -0.5)
    q_pos = pos0 + jnp.arange(T, dtype=jnp.int32)
    out_a = moba_attention(q_a, k_all, v_all, q_pos, rel_table).transpose(0, 2, 1, 3).reshape(B, T, -1)

    h = h + jnp.concatenate([out_m, out_a], axis=-1).astype(dt) @ w_out
    u = rms_norm(h, g_ffn) @ w_up
    h = h + jnp.square(jax.nn.relu(u)) @ w_down
    gate = jax.nn.sigmoid((rms_norm(h, g_ple) @ w_ple_gate).astype(f32))
    h = h + (gate * (pe @ w_ple_proj).astype(f32)).astype(dt)
    return h, k_new, v_new, C, n, m


def setup_inputs(seed: int = 0) -> dict:
    key = jax.random.key(seed)
    ks = jax.random.split(key, 32)
    f32 = jnp.float32

    def nrm(k, shape, scale):
        return jax.random.normal(k, shape, f32) * scale

    def gain(k, shape):
        return 1.0 + 0.02 * jax.random.normal(k, shape, f32)

    n_pages = PAST_LEN // PAGE_SIZE
    n_used = DEC_BATCH * n_pages
    n_phys = (5 * n_used + 3) // 4
    page_table = jax.random.permutation(ks[0], n_phys)[:n_used].reshape(DEC_BATCH, n_pages).astype(jnp.int32)
    b_f = jnp.broadcast_to(jnp.linspace(3.0, 6.0, M_HEADS, dtype=f32), (DEPTH, M_HEADS)) + nrm(ks[1], (DEPTH, M_HEADS), 0.1)
    return {
        "x_prompt": nrm(ks[2], (BATCH, SEQ, D_MODEL), 1.0),
        "x_sample": nrm(ks[3], (DEC_BATCH, DEC_SEQ, D_MODEL), 1.0),
        "cache_k": nrm(ks[4], (DEPTH, n_phys, PAGE_SIZE, A_HEADS, A_HEAD_DIM), 1.0),
        "cache_v": nrm(ks[5], (DEPTH, n_phys, PAGE_SIZE, A_HEADS, A_HEAD_DIM), 1.0),
        "state_C": nrm(ks[6], (DEPTH, DEC_BATCH, M_HEADS, M_QK_DIM, M_V_DIM), 0.5),
        "state_n": jnp.abs(nrm(ks[7], (DEPTH, DEC_BATCH, M_HEADS, M_QK_DIM), 1.0)),
        "state_m": nrm(ks[8], (DEPTH, DEC_BATCH, M_HEADS), 1.0),
        "page_table": page_table,
        "p_prompt": nrm(ks[9], (DEPTH, BATCH, SEQ, PLE_DIM), 1.0),
        "p_sample": nrm(ks[10], (DEPTH, DEC_BATCH, DEC_SEQ, PLE_DIM), 1.0),
        "rel_bias_table": nrm(ks[11], (NUM_BUCKETS, A_HEADS), 0.5),
        "w_in": nrm(ks[12], (DEPTH, D_MODEL, D_IN), D_MODEL ** -0.5),
        "b_igate": nrm(ks[13], (DEPTH, M_HEADS), 0.1),
        "b_fgate": b_f,
        "g_mix": gain(ks[14], (DEPTH, D_MODEL)),
        "g_mhead": gain(ks[15], (DEPTH, M_HEADS, M_V_DIM)),
        "w_out": nrm(ks[16], (DEPTH, MIX_WIDTH, D_MODEL), MIX_WIDTH ** -0.5),
        "g_ffn": gain(ks[17], (DEPTH, D_MODEL)),
        "w_up": nrm(ks[18], (DEPTH, D_MODEL, D_FF), D_MODEL ** -0.5),
        "w_down": nrm(ks[19], (DEPTH, D_FF, D_MODEL), D_FF ** -0.5),
        "g_ple": gain(ks[20], (DEPTH, D_MODEL)),
        "w_ple_gate": nrm(ks[21], (DEPTH, D_MODEL, D_MODEL), D_MODEL ** -0.5),
        "w_ple_proj": nrm(ks[22], (DEPTH, PLE_DIM, D_MODEL), PLE_DIM ** -0.5),
        "g_final": gain(ks[23], (D_MODEL,)),
    }


def reference(x_prompt, x_sample, cache_k, cache_v, state_C, state_n, state_m, page_table,
              p_prompt, p_sample, rel_bias_table, w_in, b_igate, b_fgate, g_mix, g_mhead, w_out,
              g_ffn, w_up, w_down, g_ple, w_ple_gate, w_ple_proj, g_final):
    Bp = x_prompt.shape[0]
    Bs, _, _ = x_sample.shape
    past_len = page_table.shape[1] * PAGE_SIZE
    h_p = x_prompt
    h_s = x_sample
    kp_l, vp_l, Cp_l, np_l, mp_l = [], [], [], [], []
    ks_l, vs_l, Cs_l, ns_l, ms_l = [], [], [], [], []
    for l in range(DEPTH):
        w = (rel_bias_table, w_in[l], b_igate[l], b_fgate[l], g_mix[l], g_mhead[l], w_out[l],
             g_ffn[l], w_up[l], w_down[l], g_ple[l], w_ple_gate[l], w_ple_proj[l])
        empty = jnp.zeros((Bp, 0, A_HEADS, A_HEAD_DIM), cache_k.dtype)
        C0 = jnp.zeros((Bp, M_HEADS, M_QK_DIM, M_V_DIM), jnp.float32)
        n0 = jnp.zeros((Bp, M_HEADS, M_QK_DIM), jnp.float32)
        m0 = jnp.zeros((Bp, M_HEADS), jnp.float32)
        h_p, kp, vp, Cp, np_, mp = decoder_layer(h_p, p_prompt[l], 0, empty, empty, C0, n0, m0, *w)
        past_k = cache_k[l, page_table].reshape(Bs, past_len, A_HEADS, A_HEAD_DIM)
        past_v = cache_v[l, page_table].reshape(Bs, past_len, A_HEADS, A_HEAD_DIM)
        h_s, ks_, vs_, Cs, ns, ms = decoder_layer(h_s, p_sample[l], past_len, past_k, past_v,
                                                  state_C[l], state_n[l], state_m[l], *w)
        kp_l.append(kp.astype(cache_k.dtype)); vp_l.append(vp.astype(cache_v.dtype))
        Cp_l.append(Cp.astype(state_C.dtype)); np_l.append(np_.astype(state_n.dtype)); mp_l.append(mp.astype(state_m.dtype))
        ks_l.append(ks_.astype(cache_k.dtype)); vs_l.append(vs_.astype(cache_v.dtype))
        Cs_l.append(Cs.astype(state_C.dtype)); ns_l.append(ns.astype(state_n.dtype)); ms_l.append(ms.astype(state_m.dtype))
    y_prompt = rms_norm(h_p, g_final)
    y_sample = rms_norm(h_s, g_final)
    return (y_prompt, y_sample,
            jnp.stack(kp_l), jnp.stack(vp_l), jnp.stack(Cp_l), jnp.stack(np_l), jnp.stack(mp_l),
            jnp.stack(ks_l), jnp.stack(vs_l), jnp.stack(Cs_l), jnp.stack(ns_l), jnp.stack(ms_l))
```

```python
import functools
import math

import jax
import jax.numpy as jnp
from jax import lax
from jax.experimental import pallas as pl
from jax.experimental.pallas import tpu as pltpu

F32 = jnp.float32
BF16 = jnp.bfloat16
HIGHEST = lax.Precision.HIGHEST

D_MODEL = 2048
M_HEADS = 8
M_V_DIM = 128
M_QK_DIM = 64
M_CHUNK = 64
A_HEADS = 8
A_HEAD_DIM = 128
MOBA_BLOCK = 256
MOBA_TOP_K = 3
NUM_BUCKETS = 32
MAX_EXACT = NUM_BUCKETS // 2
MAX_DISTANCE = 128
D_FF = 4 * D_MODEL
PLE_DIM = 256
PAGE_SIZE = 128
EPS = 1e-6

_MQ, _MK, _MV, _MO = 0, 512, 1024, 2048
_GATES = 3072
_AQ = 3088
D_MAIN = 6144
GATE_PAD = 128

VMEM_LIMIT = 56 * 1024 * 1024
NEG_INF = float("-inf")


def _cparams(*sem):
    return pltpu.CompilerParams(dimension_semantics=sem, vmem_limit_bytes=VMEM_LIMIT)


def _rms(x, g):
    return x * lax.rsqrt(jnp.mean(x * x, axis=-1, keepdims=True) + EPS) * g


def _dot_nt(a, b, **kw):
    return lax.dot_general(a, b, (((1,), (1,)), ((), ())), preferred_element_type=F32, **kw)


def _dot_tn(a, b, **kw):
    return lax.dot_general(a, b, (((0,), (0,)), ((), ())), preferred_element_type=F32, **kw)


def _dot(a, b, **kw):
    return jnp.dot(a, b, preferred_element_type=F32, **kw)


def _in_proj_kernel(x_ref, g_ref, w_ref, wg_ref, z_ref, gate_ref, xn_ref):
    @pl.when(pl.program_id(1) == 0)
    def _():
        y = _rms(x_ref[...], g_ref[...])
        xn_ref[...] = y.astype(BF16)
        gate_ref[...] = _dot(y, wg_ref[...], precision=HIGHEST)

    z_ref[...] = _dot(xn_ref[...], w_ref[...])


def _in_proj(x, g, w_main, w_gate):
    m = x.shape[0]
    tm = min(512, m)
    tn = 1024
    return pl.pallas_call(
        _in_proj_kernel,
        grid=(m // tm, D_MAIN // tn),
        in_specs=[
            pl.BlockSpec((tm, D_MODEL), lambda i, j: (i, 0)),
            pl.BlockSpec((1, D_MODEL), lambda i, j: (0, 0)),
            pl.BlockSpec((D_MODEL, tn), lambda i, j: (0, j)),
            pl.BlockSpec((D_MODEL, GATE_PAD), lambda i, j: (0, 0)),
        ],
        out_specs=[
            pl.BlockSpec((tm, tn), lambda i, j: (i, j)),
            pl.BlockSpec((tm, GATE_PAD), lambda i, j: (i, 0)),
        ],
        out_shape=[jax.ShapeDtypeStruct((m, D_MAIN), F32), jax.ShapeDtypeStruct((m, GATE_PAD), F32)],
        scratch_shapes=[pltpu.VMEM((tm, D_MODEL), BF16)],
        compiler_params=_cparams("parallel", "arbitrary"),
        name="in_proj",
    )(x, g, w_main, w_gate)


def _log_sigmoid(x):
    return -(jnp.maximum(-x, 0.0) + jnp.log1p(jnp.exp(-jnp.abs(x))))


def _mlstm_kernel(q_ref, k_ref, v_ref, o_ref, g_ref, gb_ref, gh_ref, c0_ref, m0_ref,
                  out_ref, cfin_ref, mfin_ref, c_scr, m_scr, *, chunk, n_chunks):
    L = chunk
    tb = L * n_chunks
    t = pl.program_id(1)

    @pl.when(t == 0)
    def _():
        c_scr[...] = c0_ref[0]
        m_scr[...] = m0_ref[0]

    g = g_ref[...] + gb_ref[...]
    lane = lax.broadcasted_iota(jnp.int32, (tb, GATE_PAD), 1)
    gl = jnp.where((lane >= M_HEADS) & (lane < 2 * M_HEADS), _log_sigmoid(g), g)
    r = lax.broadcasted_iota(jnp.int32, (tb, tb), 0)
    c = lax.broadcasted_iota(jnp.int32, (tb, tb), 1)
    tri = ((r // L == c // L) & (c <= r)).astype(F32)
    cum = _dot(tri, gl, precision=HIGHEST)

    rl = lax.broadcasted_iota(jnp.int32, (L, L), 0)
    cl = lax.broadcasted_iota(jnp.int32, (L, L), 1)
    eye = rl == cl
    causal = cl <= rl
    ones_col = (lax.broadcasted_iota(jnp.int32, (L, M_V_DIM), 1) == 0).astype(BF16)

    for h in range(M_HEADS):
        m_prev = m_scr[h:h + 1, 0:1]
        for ci in range(n_chunks):
            rows = slice(ci * L, (ci + 1) * L)
            qb = q_ref[rows, h * M_QK_DIM:(h + 1) * M_QK_DIM].astype(BF16)
            kc = k_ref[rows, h * M_QK_DIM:(h + 1) * M_QK_DIM] * (M_QK_DIM ** -0.5)
            vb = v_ref[rows, h * M_V_DIM:(h + 1) * M_V_DIM].astype(BF16)
            vaug = jnp.concatenate([vb, ones_col], axis=1)
            li = gl[rows, h:h + 1]
            b = cum[rows, M_HEADS + h:M_HEADS + h + 1]
            w_row = jnp.sum(jnp.where(eye, li - b, 0.0), axis=0, keepdims=True)
            d = jnp.where(causal, b + w_row, NEG_INF)
            a = b + m_prev
            m_t = jnp.maximum(a, jnp.max(d, axis=1, keepdims=True))
            s = _dot_nt(qb, kc.astype(BF16)) * jnp.exp(d - m_t)
            inter = jnp.exp(a - m_t)
            c_aug = c_scr[h]
            nd = inter * _dot(qb, c_aug.astype(BF16)) + _dot(s.astype(BF16), vaug)
            num = nd[:, :M_V_DIM]
            den = nd[:, M_V_DIM:M_V_DIM + 1]
            hh = num / jnp.maximum(jnp.abs(den), jnp.exp(-m_t))
            m_l = m_t[L - 1:L, :]
            b_l = b[L - 1:L, :]
            w_prev = jnp.exp(b_l + m_prev - m_l)
            ws = jnp.exp(b_l - b + li - m_l)
            c_scr[h] = w_prev * c_aug + _dot_tn((kc * ws).astype(BF16), vaug)
            m_prev = m_l
            hn = _rms(hh, gh_ref[h:h + 1, :])
            og = jax.nn.sigmoid(o_ref[rows, h * M_V_DIM:(h + 1) * M_V_DIM])
            out_ref[rows, h * M_V_DIM:(h + 1) * M_V_DIM] = og * hn
        m_scr[h:h + 1, :] = jnp.broadcast_to(m_prev, (1, GATE_PAD))

    @pl.when(t == pl.num_programs(1) - 1)
    def _():
        cfin_ref[0] = c_scr[...]
        mfin_ref[0] = m_scr[...]


def _mlstm(z, gates, gate_bias, g_mhead, c0_aug, m0, batch, seq):
    L = math.gcd(seq, M_CHUNK)
    n_chunks = max(1, min(seq // L, 128 // L))
    tb = L * n_chunks
    nt = seq // tb
    row = lambda b, t: b * nt + t
    kern = functools.partial(_mlstm_kernel, chunk=L, n_chunks=n_chunks)
    return pl.pallas_call(
        kern,
        grid=(batch, nt),
        in_specs=[
            pl.BlockSpec((tb, 512), lambda b, t: (row(b, t), _MQ // 512)),
            pl.BlockSpec((tb, 512), lambda b, t: (row(b, t), _MK // 512)),
            pl.BlockSpec((tb, 1024), lambda b, t: (row(b, t), _MV // 1024)),
            pl.BlockSpec((tb, 1024), lambda b, t: (row(b, t), _MO // 1024)),
            pl.BlockSpec((tb, GATE_PAD), lambda b, t: (row(b, t), 0)),
            pl.BlockSpec((1, GATE_PAD), lambda b, t: (0, 0)),
            pl.BlockSpec((M_HEADS, M_V_DIM), lambda b, t: (0, 0)),
            pl.BlockSpec((1, M_HEADS, M_QK_DIM, 2 * M_V_DIM), lambda b, t: (b, 0, 0, 0)),
            pl.BlockSpec((1, M_HEADS, GATE_PAD), lambda b, t: (b, 0, 0)),
        ],
        out_specs=[
            pl.BlockSpec((tb, M_HEADS * M_V_DIM), lambda b, t: (row(b, t), 0)),
            pl.BlockSpec((1, M_HEADS, M_QK_DIM, 2 * M_V_DIM), lambda b, t: (b, 0, 0, 0)),
            pl.BlockSpec((1, M_HEADS, GATE_PAD), lambda b, t: (b, 0, 0)),
        ],
        out_shape=[
            jax.ShapeDtypeStruct((batch * seq, M_HEADS * M_V_DIM), F32),
            jax.ShapeDtypeStruct((batch, M_HEADS, M_QK_DIM, 2 * M_V_DIM), F32),
            jax.ShapeDtypeStruct((batch, M_HEADS, GATE_PAD), F32),
        ],
        scratch_shapes=[pltpu.VMEM((M_HEADS, M_QK_DIM, 2 * M_V_DIM), F32),
                        pltpu.VMEM((M_HEADS, GATE_PAD), F32)],
        compiler_params=_cparams("parallel", "arbitrary"),
        name="mlstm",
    )(z, z, z, z, gates, gate_bias, g_mhead, c0_aug, m0)


def _t5_bucket(rel):
    n = jnp.maximum(rel, 0)
    nf = jnp.maximum(n, 1).astype(F32)
    large = MAX_EXACT + (jnp.log(nf / MAX_EXACT) / math.log(MAX_DISTANCE / MAX_EXACT)
                         * (NUM_BUCKETS - MAX_EXACT)).astype(jnp.int32)
    large = jnp.minimum(large, NUM_BUCKETS - 1)
    return jnp.where(n < MAX_EXACT, n, large)


def _bias_kernel(tab_ref, out_ref):
    h = pl.program_id(0)
    i = lax.broadcasted_iota(jnp.int32, (MOBA_BLOCK, MOBA_BLOCK), 0)
    j = lax.broadcasted_iota(jnp.int32, (MOBA_BLOCK, MOBA_BLOCK), 1)
    for kind, rel in ((0, i - j), (1, MOBA_BLOCK + i - j)):
        bucket = _t5_bucket(rel)
        acc = jnp.zeros((MOBA_BLOCK, MOBA_BLOCK), F32)
        for b in range(NUM_BUCKETS):
            acc = jnp.where(bucket == b, tab_ref[h, b], acc)
        out_ref[0, kind] = acc
    out_ref[0, 2] = jnp.full((MOBA_BLOCK, MOBA_BLOCK), tab_ref[h, NUM_BUCKETS - 1], F32)


def _bias_tiles(rel_table):
    assert MOBA_BLOCK + 1 >= MAX_DISTANCE
    tab = rel_table.T.astype(F32)
    return pl.pallas_call(
        _bias_kernel,
        grid=(A_HEADS,),
        in_specs=[pl.BlockSpec(memory_space=pltpu.SMEM)],
        out_specs=pl.BlockSpec((1, 3, MOBA_BLOCK, MOBA_BLOCK), lambda h: (h, 0, 0, 0)),
        out_shape=jax.ShapeDtypeStruct((A_HEADS, 3, MOBA_BLOCK, MOBA_BLOCK), F32),
        compiler_params=_cparams("parallel"),
        name="t5_bias_tiles",
    )(tab)


def _moba_prompt_kernel(q_ref, k_ref, v_ref, bias_ref, o_ref, kb_scr, vb_scr, *, n_blocks):
    nb = n_blocks
    blk = MOBA_BLOCK
    kb_scr[...] = k_ref[...].astype(BF16)
    vb_scr[...] = v_ref[...].astype(BF16)
    need_gate = nb - 1 > MOBA_TOP_K
    if need_gate:
        means = [jnp.mean(k_ref[n * blk:(n + 1) * blk, :], axis=0, keepdims=True) for n in range(nb)]
        kmean = jnp.concatenate(means + [jnp.zeros((128 - nb, A_HEAD_DIM), F32)], axis=0)
    ri = lax.broadcasted_iota(jnp.int32, (blk, blk), 0)
    ci = lax.broadcasted_iota(jnp.int32, (blk, blk), 1)
    causal = ci <= ri

    for i in range(nb):
        q = q_ref[i * blk:(i + 1) * blk, :] * (A_HEAD_DIM ** -0.5)
        qb = q.astype(BF16)
        sel = None
        if i > MOBA_TOP_K:
            gate = _dot_nt(q, kmean, precision=HIGHEST)
            cols = [gate[:, n:n + 1] for n in range(i)]
            sel = []
            for n in range(i):
                rank = jnp.zeros((blk, 1), F32)
                for j in range(i):
                    if j == n:
                        continue
                    beats = (cols[j] >= cols[n]) if j < n else (cols[j] > cols[n])
                    rank = rank + beats.astype(F32)
                sel.append(jnp.where(rank < MOBA_TOP_K, 0.0, NEG_INF))
        s = _dot_nt(qb, kb_scr[0:(i + 1) * blk, :])
        tiles = []
        m = None
        for n in range(i + 1):
            kind = 0 if n == i else (1 if n == i - 1 else 2)
            tl = s[:, n * blk:(n + 1) * blk] + bias_ref[0, kind]
            if n == i:
                tl = jnp.where(causal, tl, NEG_INF)
            elif sel is not None:
                tl = tl + sel[n]
            tiles.append(tl)
            tmax = jnp.max(tl, axis=1, keepdims=True)
            m = tmax if m is None else jnp.maximum(m, tmax)
        l = jnp.zeros((blk, 1), F32)
        acc = jnp.zeros((blk, A_HEAD_DIM), F32)
        for n in range(i + 1):
            p = jnp.exp(tiles[n] - m)
            l = l + jnp.sum(p, axis=1, keepdims=True)
            acc = acc + _dot(p.astype(BF16), vb_scr[n * blk:(n + 1) * blk, :])
        o_ref[i * blk:(i + 1) * blk, :] = acc / l


def _moba_prompt(z, bias_tiles, batch, seq):
    assert seq % MOBA_BLOCK == 0
    nb = seq // MOBA_BLOCK
    qc, kc, vc = _AQ_MAIN // 128, (_AQ_MAIN + 1024) // 128, (_AQ_MAIN + 2048) // 128
    kern = functools.partial(_moba_prompt_kernel, n_blocks=nb)
    return pl.pallas_call(
        kern,
        grid=(batch, A_HEADS),
        in_specs=[
            pl.BlockSpec((seq, A_HEAD_DIM), lambda b, h: (b, qc + h)),
            pl.BlockSpec((seq, A_HEAD_DIM), lambda b, h: (b, kc + h)),
            pl.BlockSpec((seq, A_HEAD_DIM), lambda b, h: (b, vc + h)),
            pl.BlockSpec((1, 3, MOBA_BLOCK, MOBA_BLOCK), lambda b, h: (h, 0, 0, 0)),
        ],
        out_specs=pl.BlockSpec((seq, A_HEAD_DIM), lambda b, h: (b, h)),
        out_shape=jax.ShapeDtypeStruct((batch * seq, A_HEADS * A_HEAD_DIM), F32),
        scratch_shapes=[pltpu.VMEM((seq, A_HEAD_DIM), BF16), pltpu.VMEM((seq, A_HEAD_DIM), BF16)],
        compiler_params=_cparams("parallel", "parallel"),
        name="moba_prompt",
    )(z, z, z, bias_tiles)


_AQ_MAIN = 3072


def _moba_past_kernel(pt_ref, q_ref, k0_ref, k1_ref, v0_ref, v1_ref, bias_ref,
                      o_ref, m_ref, l_ref, km_ref):
    del pt_ref
    rows = PAGE_SIZE * A_HEADS
    k = jnp.concatenate([k0_ref[0, 0].reshape(rows, A_HEAD_DIM), k1_ref[0, 0].reshape(rows, A_HEAD_DIM)], axis=0)
    v = jnp.concatenate([v0_ref[0, 0].reshape(rows, A_HEAD_DIM), v1_ref[0, 0].reshape(rows, A_HEAD_DIM)], axis=0)
    ksum = jnp.sum(k0_ref[0, 0], axis=0) + jnp.sum(k1_ref[0, 0], axis=0)
    km_ref[0, 0] = ksum / MOBA_BLOCK
    qb = (q_ref[0] * (A_HEAD_DIM ** -0.5)).astype(BF16)
    nq = qb.shape[0] // A_HEADS
    lt = _dot_nt(qb, k.astype(BF16))
    r = lax.broadcasted_iota(jnp.int32, lt.shape, 0)
    c = lax.broadcasted_iota(jnp.int32, lt.shape, 1)
    lm = jnp.where((c % A_HEADS) == (r // nq), lt + bias_ref[0], NEG_INF)
    m = jnp.max(lm, axis=1, keepdims=True)
    p = jnp.exp(lm - m)
    l = jnp.sum(p, axis=1, keepdims=True)
    o_ref[0, 0] = _dot(p.astype(BF16), v.astype(BF16))
    m_ref[0, 0] = jnp.broadcast_to(m, o_ref.shape[2:])
    l_ref[0, 0] = jnp.broadcast_to(l, o_ref.shape[2:])


def _moba_past(page_table, q_hq, cache_k, cache_v, past_bias, layer):
    bs, n_pages = page_table.shape
    assert MOBA_BLOCK == 2 * PAGE_SIZE and n_pages % 2 == 0
    nbp = n_pages // 2
    hq = q_hq.shape[1]
    page = (1, 1, PAGE_SIZE, A_HEADS, A_HEAD_DIM)
    part = pl.BlockSpec((1, 1, hq, A_HEAD_DIM), lambda b, n, pt: (b, n, 0, 0))
    grid_spec = pltpu.PrefetchScalarGridSpec(
        num_scalar_prefetch=1,
        grid=(bs, nbp),
        in_specs=[
            pl.BlockSpec((1, hq, A_HEAD_DIM), lambda b, n, pt: (b, 0, 0)),
            pl.BlockSpec(page, lambda b, n, pt: (layer, pt[b, 2 * n], 0, 0, 0)),
            pl.BlockSpec(page, lambda b, n, pt: (layer, pt[b, 2 * n + 1], 0, 0, 0)),
            pl.BlockSpec(page, lambda b, n, pt: (layer, pt[b, 2 * n], 0, 0, 0)),
            pl.BlockSpec(page, lambda b, n, pt: (layer, pt[b, 2 * n + 1], 0, 0, 0)),
            pl.BlockSpec((1, hq, MOBA_BLOCK * A_HEADS), lambda b, n, pt: (jnp.where(n == nbp - 1, 1, 0), 0, 0)),
        ],
        out_specs=[part, part, part,
                   pl.BlockSpec((1, 1, A_HEADS, A_HEAD_DIM), lambda b, n, pt: (b, n, 0, 0))],
    )
    pshape = jax.ShapeDtypeStruct((bs, nbp, hq, A_HEAD_DIM), F32)
    return pl.pallas_call(
        _moba_past_kernel,
        grid_spec=grid_spec,
        out_shape=[pshape, pshape, pshape, jax.ShapeDtypeStruct((bs, nbp, A_HEADS, A_HEAD_DIM), F32)],
        compiler_params=_cparams("parallel", "parallel"),
        name="moba_past_blocks",
    )(page_table, q_hq, cache_k, cache_k, cache_v, cache_v, past_bias)


def _moba_combine_kernel(q_ref, km_ref, kn_ref, vn_ref, ob_ref, op_ref, mp_ref, lp_ref, out_ref):
    nbp = km_ref.shape[2]
    q = q_ref[0] * (A_HEAD_DIM ** -0.5)
    hq = q.shape[0]
    nq = hq // A_HEADS
    gate = jnp.concatenate(
        [_dot_nt(q[h * nq:(h + 1) * nq], km_ref[0, h], precision=HIGHEST) for h in range(A_HEADS)], axis=0)
    lane = lax.broadcasted_iota(jnp.int32, (hq, nbp), 1)
    rank = jnp.zeros((hq, nbp), F32)
    for j in range(nbp):
        gj = gate[:, j:j + 1]
        beats = (gj > gate) | ((gj == gate) & (j < lane))
        rank = rank + beats.astype(F32)
    selneg = jnp.where(rank < MOBA_TOP_K, 0.0, NEG_INF)
    qb = q.astype(BF16)
    lt = _dot_nt(qb, kn_ref[0].astype(BF16))
    r = lax.broadcasted_iota(jnp.int32, lt.shape, 0)
    c = lax.broadcasted_iota(jnp.int32, lt.shape, 1)
    ok = ((c % A_HEADS) == (r // nq)) & ((c // A_HEADS) <= (r % nq))
    lo = jnp.where(ok, lt + ob_ref[...], NEG_INF)
    m = jnp.max(lo, axis=1, keepdims=True)
    for n in range(nbp):
        m = jnp.maximum(m, mp_ref[0, n][:, 0:1] + selneg[:, n:n + 1])
    p = jnp.exp(lo - m)
    l = jnp.sum(p, axis=1, keepdims=True)
    acc = _dot(p.astype(BF16), vn_ref[0].astype(BF16))
    for n in range(nbp):
        w = jnp.exp(mp_ref[0, n][:, 0:1] + selneg[:, n:n + 1] - m)
        l = l + w * lp_ref[0, n][:, 0:1]
        acc = acc + w * op_ref[0, n]
    out_ref[0] = acc / l


def _moba_combine(q_hq, kmean_hn, k_new, v_new, own_bias, o_part, m_part, l_part):
    bs, hq, _ = q_hq.shape
    nbp = o_part.shape[1]
    part = pl.BlockSpec((1, nbp, hq, A_HEAD_DIM), lambda b: (b, 0, 0, 0))
    tok = pl.BlockSpec((1, hq, A_HEAD_DIM), lambda b: (b, 0, 0))
    return pl.pallas_call(
        _moba_combine_kernel,
        grid=(bs,),
        in_specs=[tok,
                  pl.BlockSpec((1, A_HEADS, nbp, A_HEAD_DIM), lambda b: (b, 0, 0, 0)),
                  tok, tok,
                  pl.BlockSpec((hq, hq), lambda b: (0, 0)),
                  part, part, part],
        out_specs=tok,
        out_shape=jax.ShapeDtypeStruct((bs, hq, A_HEAD_DIM), F32),
        compiler_params=_cparams("parallel"),
        name="moba_combine",
    )(q_hq, kmean_hn, k_new, v_new, own_bias, o_part, m_part, l_part)


def _out_proj_kernel(h_ref, om_ref, oa_ref, wm_ref, wa_ref, o_ref):
    o_ref[...] = (h_ref[...] + _dot(om_ref[...].astype(BF16), wm_ref[...])
                  + _dot(oa_ref[...].astype(BF16), wa_ref[...]))


def _out_proj(h, out_m, out_a, w_m, w_a):
    m = h.shape[0]
    tm = min(512, m)
    tn = 1024
    half = M_HEADS * M_V_DIM
    return pl.pallas_call(
        _out_proj_kernel,
        grid=(m // tm, D_MODEL // tn),
        in_specs=[
            pl.BlockSpec((tm, tn), lambda i, j: (i, j)),
            pl.BlockSpec((tm, half), lambda i, j: (i, 0)),
            pl.BlockSpec((tm, half), lambda i, j: (i, 0)),
            pl.BlockSpec((half, tn), lambda i, j: (0, j)),
            pl.BlockSpec((half, tn), lambda i, j: (0, j)),
        ],
        out_specs=pl.BlockSpec((tm, tn), lambda i, j: (i, j)),
        out_shape=jax.ShapeDtypeStruct((m, D_MODEL), F32),
        compiler_params=_cparams("parallel", "arbitrary"),
        name="out_proj",
    )(h, out_m, out_a, w_m, w_a)


def _mlp_kernel(h_ref, g_ref, wu_ref, wd_ref, o_ref, xn_ref):
    f = pl.program_id(1)

    @pl.when(f == 0)
    def _():
        xn_ref[...] = _rms(h_ref[...], g_ref[...]).astype(BF16)
        o_ref[...] = h_ref[...]

    u = _dot(xn_ref[...], wu_ref[...])
    a = jnp.square(jnp.maximum(u, 0.0)).astype(BF16)
    o_ref[...] += _dot(a, wd_ref[...])


def _mlp(h, g, w_up, w_down):
    m = h.shape[0]
    tm = min(512, m)
    tf = 1024
    return pl.pallas_call(
        _mlp_kernel,
        grid=(m // tm, D_FF // tf),
        in_specs=[
            pl.BlockSpec((tm, D_MODEL), lambda i, f: (i, 0)),
            pl.BlockSpec((1, D_MODEL), lambda i, f: (0, 0)),
            pl.BlockSpec((D_MODEL, tf), lambda i, f: (0, f)),
            pl.BlockSpec((tf, D_MODEL), lambda i, f: (f, 0)),
        ],
        out_specs=pl.BlockSpec((tm, D_MODEL), lambda i, f: (i, 0)),
        out_shape=jax.ShapeDtypeStruct((m, D_MODEL), F32),
        scratch_shapes=[pltpu.VMEM((tm, D_MODEL), BF16)],
        compiler_params=_cparams("parallel", "arbitrary"),
        name="mlp",
    )(h, g, w_up, w_down)


def _ple_kernel(h_ref, pe_ref, g_ref, wg_ref, wp_ref, gf_ref, y_ref):
    h = h_ref[...]
    gate = jax.nn.sigmoid(_dot(_rms(h, g_ref[...]).astype(BF16), wg_ref[...]))
    proj = _dot(pe_ref[...].astype(BF16), wp_ref[...])
    y_ref[...] = _rms(h + gate * proj, gf_ref[...])


def _ple_final(h, pe, g_ple, w_gate, w_proj, g_final):
    m = h.shape[0]
    tm = min(256, m)
    vec = pl.BlockSpec((1, D_MODEL), lambda i: (0, 0))
    return pl.pallas_call(
        _ple_kernel,
        grid=(m // tm,),
        in_specs=[
            pl.BlockSpec((tm, D_MODEL), lambda i: (i, 0)),
            pl.BlockSpec((tm, PLE_DIM), lambda i: (i, 0)),
            vec,
            pl.BlockSpec((D_MODEL, D_MODEL), lambda i: (0, 0)),
            pl.BlockSpec((PLE_DIM, D_MODEL), lambda i: (0, 0)),
            vec,
        ],
        out_specs=pl.BlockSpec((tm, D_MODEL), lambda i: (i, 0)),
        out_shape=jax.ShapeDtypeStruct((m, D_MODEL), F32),
        compiler_params=_cparams("parallel"),
        name="ple_final_norm",
    )(h, pe, g_ple, w_gate, w_proj, g_final)


def _layer_weights(w_in, b_igate, b_fgate, g_mix, g_mhead, w_out, g_ffn, w_up, w_down,
                   g_ple, w_ple_gate, w_ple_proj):
    w_main = jnp.concatenate([w_in[:, :_GATES], w_in[:, _AQ:]], axis=1).astype(BF16)
    w_gate = jnp.pad(w_in[:, _GATES:_AQ], ((0, 0), (0, GATE_PAD - 2 * M_HEADS)))
    gate_bias = jnp.pad(jnp.concatenate([b_igate, b_fgate]), (0, GATE_PAD - 2 * M_HEADS)).reshape(1, GATE_PAD)
    half = M_HEADS * M_V_DIM
    return dict(
        w_main=w_main, w_gate=w_gate, gate_bias=gate_bias.astype(F32),
        g_mix=g_mix.reshape(1, D_MODEL), g_mhead=g_mhead,
        w_out_m=w_out[:half].astype(BF16), w_out_a=w_out[half:].astype(BF16),
        g_ffn=g_ffn.reshape(1, D_MODEL), w_up=w_up.astype(BF16), w_down=w_down.astype(BF16),
        g_ple=g_ple.reshape(1, D_MODEL), w_ple_gate=w_ple_gate.astype(BF16),
        w_ple_proj=w_ple_proj.astype(BF16))


def _tail(h, out_m, out_a, pe, w, g_final):
    h = _out_proj(h, out_m, out_a, w["w_out_m"], w["w_out_a"])
    h = _mlp(h, w["g_ffn"], w["w_up"], w["w_down"])
    return _ple_final(h, pe, w["g_ple"], w["w_ple_gate"], w["w_ple_proj"], g_final)


def _split_state(c_aug, m_fin):
    return c_aug[..., :M_V_DIM], c_aug[..., M_V_DIM], m_fin[..., 0]


def kernel(x_prompt, x_sample, cache_k, cache_v, state_C, state_n, state_m, page_table, p_prompt, p_sample, rel_bias_table, w_in, b_igate, b_fgate, g_mix, g_mhead, w_out, g_ffn, w_up, w_down, g_ple, w_ple_gate, w_ple_proj, g_final):
    depth = w_in.shape[0]
    assert depth == 1, "one decoder layer per call"
    layer = 0
    bp, tp, _ = x_prompt.shape
    bs, ts, _ = x_sample.shape
    past_len = page_table.shape[1] * PAGE_SIZE
    assert past_len % MOBA_BLOCK == 0 and ts <= MOBA_BLOCK
    kv_shape = (A_HEADS, A_HEAD_DIM)
    g_fin = g_final.reshape(1, D_MODEL)

    w = _layer_weights(w_in[layer], b_igate[layer], b_fgate[layer], g_mix[layer], g_mhead[layer], w_out[layer],
                       g_ffn[layer], w_up[layer], w_down[layer], g_ple[layer], w_ple_gate[layer],
                       w_ple_proj[layer])
    bias_tiles = _bias_tiles(rel_bias_table)

    hp = x_prompt.reshape(bp * tp, D_MODEL)
    zp, gp = _in_proj(hp, w["g_mix"], w["w_main"], w["w_gate"])
    c0 = jnp.zeros((bp, M_HEADS, M_QK_DIM, 2 * M_V_DIM), F32)
    m0 = jnp.zeros((bp, M_HEADS, GATE_PAD), F32)
    om_p, c_p, m_p = _mlstm(zp, gp, w["gate_bias"], w["g_mhead"], c0, m0, bp, tp)
    oa_p = _moba_prompt(zp, bias_tiles, bp, tp)
    y_p = _tail(hp, om_p, oa_p, p_prompt[layer].reshape(bp * tp, PLE_DIM), w, g_fin)
    k_p = zp[:, _AQ_MAIN + 1024:_AQ_MAIN + 2048].reshape((1, bp, tp) + kv_shape)
    v_p = zp[:, _AQ_MAIN + 2048:].reshape((1, bp, tp) + kv_shape)
    cp, np_, mp = _split_state(c_p, m_p)

    hs = x_sample.reshape(bs * ts, D_MODEL)
    zs, gs = _in_proj(hs, w["g_mix"], w["w_main"], w["w_gate"])
    c0s = jnp.concatenate([state_C[layer], state_n[layer][..., None],
                           jnp.zeros((bs, M_HEADS, M_QK_DIM, M_V_DIM - 1), F32)], axis=-1)
    m0s = jnp.broadcast_to(state_m[layer][..., None], (bs, M_HEADS, GATE_PAD))
    om_s, c_s, m_s = _mlstm(zs, gs, w["gate_bias"], w["g_mhead"], c0s, m0s, bs, ts)

    hq = A_HEADS * ts
    q_s = zs[:, _AQ_MAIN:_AQ_MAIN + 1024].reshape(bs, ts, A_HEADS, A_HEAD_DIM)
    q_hq = q_s.transpose(0, 2, 1, 3).reshape(bs, hq, A_HEAD_DIM)
    k_new = zs[:, _AQ_MAIN + 1024:_AQ_MAIN + 2048].reshape(bs, hq, A_HEAD_DIM)
    v_new = zs[:, _AQ_MAIN + 2048:].reshape(bs, hq, A_HEAD_DIM)
    rows = bias_tiles[:, :, :ts, :]
    expand = lambda a: jnp.repeat(a, A_HEADS, axis=-1).reshape(hq, -1)
    past_bias = jnp.stack([expand(rows[:, 2]), expand(rows[:, 1])])
    own_bias = expand(rows[:, 0, :, :ts])
    o_part, m_part, l_part, kmean = _moba_past(page_table, q_hq, cache_k, cache_v, past_bias, layer)
    oa_hq = _moba_combine(q_hq, kmean.transpose(0, 2, 1, 3), k_new, v_new, own_bias, o_part, m_part, l_part)
    oa_s = oa_hq.reshape(bs, A_HEADS, ts, A_HEAD_DIM).transpose(0, 2, 1, 3).reshape(bs * ts, A_HEADS * A_HEAD_DIM)
    y_s = _tail(hs, om_s, oa_s, p_sample[layer].reshape(bs * ts, PLE_DIM), w, g_fin)
    k_s = k_new.reshape((1, bs, ts) + kv_shape)
    v_s = v_new.reshape((1, bs, ts) + kv_shape)
    cs, ns, ms = _split_state(c_s, m_s)

    return (y_p.reshape(bp, tp, D_MODEL), y_s.reshape(bs, ts, D_MODEL),
            k_p, v_p, cp[None], np_[None], mp[None],
            k_s, v_s, cs[None], ns[None], ms[None])
```

```python
import functools
import math

import jax
import jax.numpy as jnp
from jax import lax
from jax.experimental import pallas as pl
from jax.experimental.pallas import tpu as pltpu

F32 = jnp.float32
BF16 = jnp.bfloat16
HIGHEST = lax.Precision.HIGHEST

D_MODEL = 2048
M_HEADS = 8
M_V_DIM = 128
M_QK_DIM = 64
M_CHUNK = 64
A_HEADS = 8
A_HEAD_DIM = 128
MOBA_BLOCK = 256
MOBA_TOP_K = 3
NUM_BUCKETS = 32
MAX_EXACT = NUM_BUCKETS // 2
MAX_DISTANCE = 128
D_FF = 4 * D_MODEL
PLE_DIM = 256
PAGE_SIZE = 128
EPS = 1e-6

_MQ, _MK, _MV, _MO = 0, 512, 1024, 2048
_GATES = 3072
_AQ = 3088
D_MAIN = 6144
GATE_PAD = 128

VMEM_LIMIT = 56 * 1024 * 1024
NEG_INF = float("-inf")


def _cparams(*sem):
    return pltpu.CompilerParams(dimension_semantics=sem, vmem_limit_bytes=VMEM_LIMIT)


def _rms(x, g):
    return x * lax.rsqrt(jnp.mean(x * x, axis=-1, keepdims=True) + EPS) * g


def _dot_nt(a, b, **kw):
    return lax.dot_general(a, b, (((1,), (1,)), ((), ())), preferred_element_type=F32, **kw)


def _dot_tn(a, b, **kw):
    return lax.dot_general(a, b, (((0,), (0,)), ((), ())), preferred_element_type=F32, **kw)


def _dot(a, b, **kw):
    return jnp.dot(a, b, preferred_element_type=F32, **kw)


_IN_TN = 1024
_N_MLSTM_TILES = _GATES // _IN_TN


def _in_proj_kernel(x_ref, g_ref, w_ref, wg_ref, zm_ref, aq_ref, ak_ref, av_ref, gate_ref, xn_ref):
    j = pl.program_id(1)

    @pl.when(j == 0)
    def _():
        xn = _rms(x_ref[...], g_ref[...]).astype(BF16)
        xn_ref[...] = xn
        r = _dot(xn, wg_ref[...])
        gate_ref[...] = r + pltpu.roll(r, GATE_PAD - 2 * M_HEADS, axis=1)

    acc = _dot(xn_ref[...], w_ref[...])

    @pl.when(j < _N_MLSTM_TILES)
    def _():
        zm_ref[...] = acc

    for t, ref in enumerate((aq_ref, ak_ref, av_ref)):
        @pl.when(j == _N_MLSTM_TILES + t)
        def _(ref=ref):
            ref[...] = acc


def _in_proj(x, g, w_main, w_gate):
    m = x.shape[0]
    tm = min(512, m)
    tn = _IN_TN
    sec = pl.BlockSpec((tm, tn), lambda i, j: (i, 0))
    sec_shape = jax.ShapeDtypeStruct((m, tn), F32)
    return pl.pallas_call(
        _in_proj_kernel,
        grid=(m // tm, D_MAIN // tn),
        in_specs=[
            pl.BlockSpec((tm, D_MODEL), lambda i, j: (i, 0)),
            pl.BlockSpec((1, D_MODEL), lambda i, j: (0, 0)),
            pl.BlockSpec((D_MODEL, tn), lambda i, j: (0, j)),
            pl.BlockSpec((D_MODEL, GATE_PAD), lambda i, j: (0, 0)),
        ],
        out_specs=[
            pl.BlockSpec((tm, tn), lambda i, j: (i, jnp.minimum(j, _N_MLSTM_TILES - 1))),
            sec, sec, sec,
            pl.BlockSpec((tm, GATE_PAD), lambda i, j: (i, 0)),
        ],
        out_shape=[jax.ShapeDtypeStruct((m, _GATES), F32), sec_shape, sec_shape, sec_shape,
                   jax.ShapeDtypeStruct((m, GATE_PAD), F32)],
        scratch_shapes=[pltpu.VMEM((tm, D_MODEL), BF16)],
        compiler_params=_cparams("parallel", "arbitrary"),
        name="in_proj",
    )(x, g, w_main, w_gate)


def _log_sigmoid(x):
    return -(jnp.maximum(-x, 0.0) + jnp.log1p(jnp.exp(-jnp.abs(x))))


def _mlstm_kernel(q_ref, k_ref, v_ref, o_ref, g_ref, gb_ref, gh_ref, c0_ref, m0_ref,
                  out_ref, cfin_ref, mfin_ref, c_scr, m_scr, *, chunk, n_chunks):
    L = chunk
    tb = L * n_chunks
    t = pl.program_id(1)

    @pl.when(t == 0)
    def _():
        c_scr[...] = c0_ref[0]
        m_scr[...] = m0_ref[0]

    g = g_ref[...] + gb_ref[...]
    lane = lax.broadcasted_iota(jnp.int32, (tb, GATE_PAD), 1)
    gl = jnp.where((lane >= M_HEADS) & (lane < 2 * M_HEADS), _log_sigmoid(g), g)
    r = lax.broadcasted_iota(jnp.int32, (tb, tb), 0)
    c = lax.broadcasted_iota(jnp.int32, (tb, tb), 1)
    tri = ((r // L == c // L) & (c <= r)).astype(F32)
    cum = _dot(tri, gl, precision=HIGHEST)

    rl = lax.broadcasted_iota(jnp.int32, (L, L), 0)
    cl = lax.broadcasted_iota(jnp.int32, (L, L), 1)
    eye = rl == cl
    causal = cl <= rl
    ones_col = (lax.broadcasted_iota(jnp.int32, (L, M_V_DIM), 1) == 0).astype(BF16)

    for h in range(M_HEADS):
        m_prev = m_scr[h:h + 1, 0:1]
        for ci in range(n_chunks):
            rows = slice(ci * L, (ci + 1) * L)
            qb = q_ref[rows, h * M_QK_DIM:(h + 1) * M_QK_DIM].astype(BF16)
            kc = k_ref[rows, h * M_QK_DIM:(h + 1) * M_QK_DIM] * (M_QK_DIM ** -0.5)
            vb = v_ref[rows, h * M_V_DIM:(h + 1) * M_V_DIM].astype(BF16)
            vaug = jnp.concatenate([vb, ones_col], axis=1)
            li = gl[rows, h:h + 1]
            b = cum[rows, M_HEADS + h:M_HEADS + h + 1]
            w_row = jnp.sum(jnp.where(eye, li - b, 0.0), axis=0, keepdims=True)
            d = jnp.where(causal, b + w_row, NEG_INF)
            a = b + m_prev
            m_t = jnp.maximum(a, jnp.max(d, axis=1, keepdims=True))
            s = _dot_nt(qb, kc.astype(BF16)) * jnp.exp(d - m_t)
            inter = jnp.exp(a - m_t)
            c_aug = c_scr[h]
            nd = inter * _dot(qb, c_aug.astype(BF16)) + _dot(s.astype(BF16), vaug)
            num = nd[:, :M_V_DIM]
            den = nd[:, M_V_DIM:M_V_DIM + 1]
            hh = num / jnp.maximum(jnp.abs(den), jnp.exp(-m_t))
            m_l = m_t[L - 1:L, :]
            b_l = b[L - 1:L, :]
            w_prev = jnp.exp(b_l + m_prev - m_l)
            ws = jnp.exp(b_l - b + li - m_l)
            c_scr[h] = w_prev * c_aug + _dot_tn((kc * ws).astype(BF16), vaug)
            m_prev = m_l
            hn = _rms(hh, gh_ref[h:h + 1, :])
            og = jax.nn.sigmoid(o_ref[rows, h * M_V_DIM:(h + 1) * M_V_DIM])
            out_ref[rows, h * M_V_DIM:(h + 1) * M_V_DIM] = og * hn
        m_scr[h:h + 1, :] = jnp.broadcast_to(m_prev, (1, GATE_PAD))

    @pl.when(t == pl.num_programs(1) - 1)
    def _():
        cfin_ref[0] = c_scr[...]
        mfin_ref[0] = m_scr[...]


def _mlstm(z, gates, gate_bias, g_mhead, c0_aug, m0, batch, seq):
    L = math.gcd(seq, M_CHUNK)
    n_chunks = max(1, min(seq // L, 128 // L))
    tb = L * n_chunks
    nt = seq // tb
    row = lambda b, t: b * nt + t
    kern = functools.partial(_mlstm_kernel, chunk=L, n_chunks=n_chunks)
    return pl.pallas_call(
        kern,
        grid=(batch, nt),
        in_specs=[
            pl.BlockSpec((tb, 512), lambda b, t: (row(b, t), _MQ // 512)),
            pl.BlockSpec((tb, 512), lambda b, t: (row(b, t), _MK // 512)),
            pl.BlockSpec((tb, 1024), lambda b, t: (row(b, t), _MV // 1024)),
            pl.BlockSpec((tb, 1024), lambda b, t: (row(b, t), _MO // 1024)),
            pl.BlockSpec((tb, GATE_PAD), lambda b, t: (row(b, t), 0)),
            pl.BlockSpec((1, GATE_PAD), lambda b, t: (0, 0)),
            pl.BlockSpec((M_HEADS, M_V_DIM), lambda b, t: (0, 0)),
            pl.BlockSpec((1, M_HEADS, M_QK_DIM, 2 * M_V_DIM), lambda b, t: (b, 0, 0, 0)),
            pl.BlockSpec((1, M_HEADS, GATE_PAD), lambda b, t: (b, 0, 0)),
        ],
        out_specs=[
            pl.BlockSpec((tb, M_HEADS * M_V_DIM), lambda b, t: (row(b, t), 0)),
            pl.BlockSpec((1, M_HEADS, M_QK_DIM, 2 * M_V_DIM), lambda b, t: (b, 0, 0, 0)),
            pl.BlockSpec((1, M_HEADS, GATE_PAD), lambda b, t: (b, 0, 0)),
        ],
        out_shape=[
            jax.ShapeDtypeStruct((batch * seq, M_HEADS * M_V_DIM), F32),
            jax.ShapeDtypeStruct((batch, M_HEADS, M_QK_DIM, 2 * M_V_DIM), F32),
            jax.ShapeDtypeStruct((batch, M_HEADS, GATE_PAD), F32),
        ],
        scratch_shapes=[pltpu.VMEM((M_HEADS, M_QK_DIM, 2 * M_V_DIM), F32),
                        pltpu.VMEM((M_HEADS, GATE_PAD), F32)],
        compiler_params=_cparams("parallel", "arbitrary"),
        name="mlstm",
    )(z, z, z, z, gates, gate_bias, g_mhead, c0_aug, m0)


def _t5_bucket(rel):
    n = jnp.maximum(rel, 0)
    nf = jnp.maximum(n, 1).astype(F32)
    large = MAX_EXACT + (jnp.log(nf / MAX_EXACT) / math.log(MAX_DISTANCE / MAX_EXACT)
                         * (NUM_BUCKETS - MAX_EXACT)).astype(jnp.int32)
    large = jnp.minimum(large, NUM_BUCKETS - 1)
    return jnp.where(n < MAX_EXACT, n, large)


def _bias_kernel(tab_ref, out_ref):
    h = pl.program_id(0)
    i = lax.broadcasted_iota(jnp.int32, (MOBA_BLOCK, MOBA_BLOCK), 0)
    j = lax.broadcasted_iota(jnp.int32, (MOBA_BLOCK, MOBA_BLOCK), 1)
    for kind, rel in ((0, i - j), (1, MOBA_BLOCK + i - j)):
        bucket = _t5_bucket(rel)
        acc = jnp.zeros((MOBA_BLOCK, MOBA_BLOCK), F32)
        for b in range(NUM_BUCKETS):
            acc = jnp.where(bucket == b, tab_ref[h, b], acc)
        out_ref[0, kind] = acc
    out_ref[0, 2] = jnp.full((MOBA_BLOCK, MOBA_BLOCK), tab_ref[h, NUM_BUCKETS - 1], F32)


def _bias_tiles(rel_table):
    assert MOBA_BLOCK + 1 >= MAX_DISTANCE
    tab = rel_table.T.astype(F32)
    return pl.pallas_call(
        _bias_kernel,
        grid=(A_HEADS,),
        in_specs=[pl.BlockSpec(memory_space=pltpu.SMEM)],
        out_specs=pl.BlockSpec((1, 3, MOBA_BLOCK, MOBA_BLOCK), lambda h: (h, 0, 0, 0)),
        out_shape=jax.ShapeDtypeStruct((A_HEADS, 3, MOBA_BLOCK, MOBA_BLOCK), F32),
        compiler_params=_cparams("parallel"),
        name="t5_bias_tiles",
    )(tab)


def _moba_prompt_kernel(q_ref, k_ref, v_ref, bias_ref, o_ref, kb_scr, vb_scr, *, n_blocks):
    nb = n_blocks
    blk = MOBA_BLOCK
    kb_scr[...] = k_ref[...].astype(BF16)
    vb_scr[...] = v_ref[...].astype(BF16)
    need_gate = nb - 1 > MOBA_TOP_K
    if need_gate:
        means = [jnp.mean(k_ref[n * blk:(n + 1) * blk, :], axis=0, keepdims=True) for n in range(nb)]
        kmean = jnp.concatenate(means + [jnp.zeros((128 - nb, A_HEAD_DIM), F32)], axis=0)
    ri = lax.broadcasted_iota(jnp.int32, (blk, blk), 0)
    ci = lax.broadcasted_iota(jnp.int32, (blk, blk), 1)
    causal = ci <= ri

    for i in range(nb):
        q = q_ref[i * blk:(i + 1) * blk, :] * (A_HEAD_DIM ** -0.5)
        qb = q.astype(BF16)
        sel = None
        if i > MOBA_TOP_K:
            gate = _dot_nt(q, kmean, precision=HIGHEST)
            cols = [gate[:, n:n + 1] for n in range(i)]
            sel = []
            for n in range(i):
                rank = jnp.zeros((blk, 1), F32)
                for j in range(i):
                    if j == n:
                        continue
                    beats = (cols[j] >= cols[n]) if j < n else (cols[j] > cols[n])
                    rank = rank + beats.astype(F32)
                sel.append(jnp.where(rank < MOBA_TOP_K, 0.0, NEG_INF))
        s = _dot_nt(qb, kb_scr[0:(i + 1) * blk, :])
        tiles = []
        m = None
        for n in range(i + 1):
            kind = 0 if n == i else (1 if n == i - 1 else 2)
            tl = s[:, n * blk:(n + 1) * blk] + bias_ref[0, kind]
            if n == i:
                tl = jnp.where(causal, tl, NEG_INF)
            elif sel is not None:
                tl = tl + sel[n]
            tiles.append(tl)
            tmax = jnp.max(tl, axis=1, keepdims=True)
            m = tmax if m is None else jnp.maximum(m, tmax)
        l = jnp.zeros((blk, 1), F32)
        acc = jnp.zeros((blk, A_HEAD_DIM), F32)
        for n in range(i + 1):
            p = jnp.exp(tiles[n] - m)
            l = l + jnp.sum(p, axis=1, keepdims=True)
            acc = acc + _dot(p.astype(BF16), vb_scr[n * blk:(n + 1) * blk, :])
        o_ref[i * blk:(i + 1) * blk, :] = acc / l


def _moba_prompt(aq, ak, av, bias_tiles, batch, seq):
    assert seq % MOBA_BLOCK == 0
    nb = seq // MOBA_BLOCK
    head = pl.BlockSpec((seq, A_HEAD_DIM), lambda b, h: (b, h))
    kern = functools.partial(_moba_prompt_kernel, n_blocks=nb)
    return pl.pallas_call(
        kern,
        grid=(batch, A_HEADS),
        in_specs=[head, head, head,
                  pl.BlockSpec((1, 3, MOBA_BLOCK, MOBA_BLOCK), lambda b, h: (h, 0, 0, 0))],
        out_specs=head,
        out_shape=jax.ShapeDtypeStruct((batch * seq, A_HEADS * A_HEAD_DIM), F32),
        scratch_shapes=[pltpu.VMEM((seq, A_HEAD_DIM), BF16), pltpu.VMEM((seq, A_HEAD_DIM), BF16)],
        compiler_params=_cparams("parallel", "parallel"),
        name="moba_prompt",
    )(aq, ak, av, bias_tiles)


PAGES_PER_BLOCK = MOBA_BLOCK // PAGE_SIZE
PAST_BLOCKS_PER_STEP = 4


def _moba_past_kernel(pt_ref, q_ref, *refs, n_pages_step):
    del pt_ref
    k_refs = refs[:n_pages_step]
    v_refs = refs[n_pages_step:2 * n_pages_step]
    bias_ref, o_ref, m_ref, l_ref, km_ref = refs[2 * n_pages_step:]
    rows = PAGE_SIZE * A_HEADS
    qb = (q_ref[0] * (A_HEAD_DIM ** -0.5)).astype(BF16)
    hq = qb.shape[0]
    nq = hq // A_HEADS
    r = lax.broadcasted_iota(jnp.int32, (hq, PAGES_PER_BLOCK * rows), 0)
    c = lax.broadcasted_iota(jnp.int32, (hq, PAGES_PER_BLOCK * rows), 1)
    same_head = (c % A_HEADS) == (r // nq)
    n_blocks = n_pages_step // PAGES_PER_BLOCK
    is_last_step = pl.program_id(1) == pl.num_programs(1) - 1
    for g in range(n_blocks):
        pages = range(g * PAGES_PER_BLOCK, (g + 1) * PAGES_PER_BLOCK)
        k = jnp.concatenate([k_refs[p][0, 0].reshape(rows, A_HEAD_DIM) for p in pages], axis=0)
        v = jnp.concatenate([v_refs[p][0, 0].reshape(rows, A_HEAD_DIM) for p in pages], axis=0)
        ksum = sum(jnp.sum(k_refs[p][0, 0], axis=0) for p in pages)
        km_ref[0, g] = ksum / MOBA_BLOCK
        bias = bias_ref[jnp.where(is_last_step, 1, 0)] if g == n_blocks - 1 else bias_ref[0]
        lt = _dot_nt(qb, k.astype(BF16))
        lm = jnp.where(same_head, lt + bias, NEG_INF)
        m = jnp.max(lm, axis=1, keepdims=True)
        p_ = jnp.exp(lm - m)
        l = jnp.sum(p_, axis=1, keepdims=True)
        o_ref[0, g] = _dot(p_.astype(BF16), v.astype(BF16))
        m_ref[0, g] = jnp.broadcast_to(m, (hq, A_HEAD_DIM))
        l_ref[0, g] = jnp.broadcast_to(l, (hq, A_HEAD_DIM))


def _moba_past(page_table, q_hq, cache_k, cache_v, past_bias, layer):
    bs, n_pages = page_table.shape
    assert n_pages % PAGES_PER_BLOCK == 0
    nbp = n_pages // PAGES_PER_BLOCK
    gb = math.gcd(nbp, PAST_BLOCKS_PER_STEP)
    pps = gb * PAGES_PER_BLOCK
    hq = q_hq.shape[1]
    page = (1, 1, PAGE_SIZE, A_HEADS, A_HEAD_DIM)
    part = pl.BlockSpec((1, gb, hq, A_HEAD_DIM), lambda b, n, pt: (b, n, 0, 0))

    def page_spec(p):
        return pl.BlockSpec(page, lambda b, n, pt: (layer, pt[b, pps * n + p], 0, 0, 0))

    grid_spec = pltpu.PrefetchScalarGridSpec(
        num_scalar_prefetch=1,
        grid=(bs, nbp // gb),
        in_specs=[pl.BlockSpec((1, hq, A_HEAD_DIM), lambda b, n, pt: (b, 0, 0))]
        + [page_spec(p) for p in range(pps)] * 2
        + [pl.BlockSpec((2, hq, MOBA_BLOCK * A_HEADS), lambda b, n, pt: (0, 0, 0))],
        out_specs=[part, part, part,
                   pl.BlockSpec((1, gb, A_HEADS, A_HEAD_DIM), lambda b, n, pt: (b, n, 0, 0))],
    )
    pshape = jax.ShapeDtypeStruct((bs, nbp, hq, A_HEAD_DIM), F32)
    return pl.pallas_call(
        functools.partial(_moba_past_kernel, n_pages_step=pps),
        grid_spec=grid_spec,
        out_shape=[pshape, pshape, pshape, jax.ShapeDtypeStruct((bs, nbp, A_HEADS, A_HEAD_DIM), F32)],
        compiler_params=_cparams("parallel", "parallel"),
        name="moba_past_blocks",
    )(page_table, q_hq, *([cache_k] * pps), *([cache_v] * pps), past_bias)


def _moba_combine_kernel(q_ref, km_ref, kn_ref, vn_ref, ob_ref, op_ref, mp_ref, lp_ref, out_ref):
    nbp = km_ref.shape[2]
    q = q_ref[0] * (A_HEAD_DIM ** -0.5)
    hq = q.shape[0]
    nq = hq // A_HEADS
    gate = jnp.concatenate(
        [_dot_nt(q[h * nq:(h + 1) * nq], km_ref[0, h], precision=HIGHEST) for h in range(A_HEADS)], axis=0)
    lane = lax.broadcasted_iota(jnp.int32, (hq, nbp), 1)
    rank = jnp.zeros((hq, nbp), F32)
    for j in range(nbp):
        gj = gate[:, j:j + 1]
        beats = (gj > gate) | ((gj == gate) & (j < lane))
        rank = rank + beats.astype(F32)
    selneg = jnp.where(rank < MOBA_TOP_K, 0.0, NEG_INF)
    qb = q.astype(BF16)
    lt = _dot_nt(qb, kn_ref[0].astype(BF16))
    r = lax.broadcasted_iota(jnp.int32, lt.shape, 0)
    c = lax.broadcasted_iota(jnp.int32, lt.shape, 1)
    ok = ((c % A_HEADS) == (r // nq)) & ((c // A_HEADS) <= (r % nq))
    lo = jnp.where(ok, lt + ob_ref[...], NEG_INF)
    m = jnp.max(lo, axis=1, keepdims=True)
    for n in range(nbp):
        m = jnp.maximum(m, mp_ref[0, n][:, 0:1] + selneg[:, n:n + 1])
    p = jnp.exp(lo - m)
    l = jnp.sum(p, axis=1, keepdims=True)
    acc = _dot(p.astype(BF16), vn_ref[0].astype(BF16))
    for n in range(nbp):
        w = jnp.exp(mp_ref[0, n][:, 0:1] + selneg[:, n:n + 1] - m)
        l = l + w * lp_ref[0, n][:, 0:1]
        acc = acc + w * op_ref[0, n]
    out_ref[0] = acc / l


def _moba_combine(q_hq, kmean_hn, k_new, v_new, own_bias, o_part, m_part, l_part):
    bs, hq, _ = q_hq.shape
    nbp = o_part.shape[1]
    part = pl.BlockSpec((1, nbp, hq, A_HEAD_DIM), lambda b: (b, 0, 0, 0))
    tok = pl.BlockSpec((1, hq, A_HEAD_DIM), lambda b: (b, 0, 0))
    return pl.pallas_call(
        _moba_combine_kernel,
        grid=(bs,),
        in_specs=[tok,
                  pl.BlockSpec((1, A_HEADS, nbp, A_HEAD_DIM), lambda b: (b, 0, 0, 0)),
                  tok, tok,
                  pl.BlockSpec((hq, hq), lambda b: (0, 0)),
                  part, part, part],
        out_specs=tok,
        out_shape=jax.ShapeDtypeStruct((bs, hq, A_HEAD_DIM), F32),
        compiler_params=_cparams("parallel"),
        name="moba_combine",
    )(q_hq, kmean_hn, k_new, v_new, own_bias, o_part, m_part, l_part)


def _out_proj_kernel(h_ref, om_ref, oa_ref, wm_ref, wa_ref, o_ref):
    o_ref[...] = (h_ref[...] + _dot(om_ref[...].astype(BF16), wm_ref[...])
                  + _dot(oa_ref[...].astype(BF16), wa_ref[...]))


def _out_proj(h, out_m, out_a, w_m, w_a):
    m = h.shape[0]
    tm = min(512, m)
    tn = 1024
    half = M_HEADS * M_V_DIM
    return pl.pallas_call(
        _out_proj_kernel,
        grid=(m // tm, D_MODEL // tn),
        in_specs=[
            pl.BlockSpec((tm, tn), lambda i, j: (i, j)),
            pl.BlockSpec((tm, half), lambda i, j: (i, 0)),
            pl.BlockSpec((tm, half), lambda i, j: (i, 0)),
            pl.BlockSpec((half, tn), lambda i, j: (0, j)),
            pl.BlockSpec((half, tn), lambda i, j: (0, j)),
        ],
        out_specs=pl.BlockSpec((tm, tn), lambda i, j: (i, j)),
        out_shape=jax.ShapeDtypeStruct((m, D_MODEL), F32),
        compiler_params=_cparams("parallel", "arbitrary"),
        name="out_proj",
    )(h, out_m, out_a, w_m, w_a)


def _mlp_kernel(h_ref, g_ref, wu_ref, wd_ref, o_ref, xn_ref):
    f = pl.program_id(1)

    @pl.when(f == 0)
    def _():
        xn_ref[...] = _rms(h_ref[...], g_ref[...]).astype(BF16)
        o_ref[...] = h_ref[...]

    u = _dot(xn_ref[...], wu_ref[...])
    a = jnp.square(jnp.maximum(u, 0.0)).astype(BF16)
    o_ref[...] += _dot(a, wd_ref[...])


def _mlp(h, g, w_up, w_down):
    m = h.shape[0]
    tm = min(512, m)
    tf = 1024
    return pl.pallas_call(
        _mlp_kernel,
        grid=(m // tm, D_FF // tf),
        in_specs=[
            pl.BlockSpec((tm, D_MODEL), lambda i, f: (i, 0)),
            pl.BlockSpec((1, D_MODEL), lambda i, f: (0, 0)),
            pl.BlockSpec((D_MODEL, tf), lambda i, f: (0, f)),
            pl.BlockSpec((tf, D_MODEL), lambda i, f: (f, 0)),
        ],
        out_specs=pl.BlockSpec((tm, D_MODEL), lambda i, f: (i, 0)),
        out_shape=jax.ShapeDtypeStruct((m, D_MODEL), F32),
        scratch_shapes=[pltpu.VMEM((tm, D_MODEL), BF16)],
        compiler_params=_cparams("parallel", "arbitrary"),
        name="mlp",
    )(h, g, w_up, w_down)


def _ple_kernel(h_ref, pe_ref, g_ref, wg_ref, wp_ref, gf_ref, y_ref):
    h = h_ref[...]
    gate = jax.nn.sigmoid(_dot(_rms(h, g_ref[...]).astype(BF16), wg_ref[...]))
    proj = _dot(pe_ref[...].astype(BF16), wp_ref[...])
    y_ref[...] = _rms(h + gate * proj, gf_ref[...])


def _ple_final(h, pe, g_ple, w_gate, w_proj, g_final):
    m = h.shape[0]
    tm = min(256, m)
    vec = pl.BlockSpec((1, D_MODEL), lambda i: (0, 0))
    return pl.pallas_call(
        _ple_kernel,
        grid=(m // tm,),
        in_specs=[
            pl.BlockSpec((tm, D_MODEL), lambda i: (i, 0)),
            pl.BlockSpec((tm, PLE_DIM), lambda i: (i, 0)),
            vec,
            pl.BlockSpec((D_MODEL, D_MODEL), lambda i: (0, 0)),
            pl.BlockSpec((PLE_DIM, D_MODEL), lambda i: (0, 0)),
            vec,
        ],
        out_specs=pl.BlockSpec((tm, D_MODEL), lambda i: (i, 0)),
        out_shape=jax.ShapeDtypeStruct((m, D_MODEL), F32),
        compiler_params=_cparams("parallel"),
        name="ple_final_norm",
    )(h, pe, g_ple, w_gate, w_proj, g_final)


def _layer_weights(w_in, b_igate, b_fgate, g_mix, g_mhead, w_out, g_ffn, w_up, w_down,
                   g_ple, w_ple_gate, w_ple_proj):
    w_main = jnp.concatenate([w_in[:, :_GATES], w_in[:, _AQ:]], axis=1).astype(BF16)
    wg = w_in[:, _GATES:_AQ]
    wg_hi = wg.astype(BF16)
    wg_lo = (wg - wg_hi.astype(F32)).astype(BF16)
    w_gate = jnp.pad(jnp.concatenate([wg_hi, wg_lo], axis=1), ((0, 0), (0, GATE_PAD - 4 * M_HEADS)))
    gate_bias = jnp.pad(jnp.concatenate([b_igate, b_fgate]), (0, GATE_PAD - 2 * M_HEADS)).reshape(1, GATE_PAD)
    half = M_HEADS * M_V_DIM
    return dict(
        w_main=w_main, w_gate=w_gate, gate_bias=gate_bias.astype(F32),
        g_mix=g_mix.reshape(1, D_MODEL), g_mhead=g_mhead,
        w_out_m=w_out[:half].astype(BF16), w_out_a=w_out[half:].astype(BF16),
        g_ffn=g_ffn.reshape(1, D_MODEL), w_up=w_up.astype(BF16), w_down=w_down.astype(BF16),
        g_ple=g_ple.reshape(1, D_MODEL), w_ple_gate=w_ple_gate.astype(BF16),
        w_ple_proj=w_ple_proj.astype(BF16))


def _tail(h, out_m, out_a, pe, w, g_final):
    h = _out_proj(h, out_m, out_a, w["w_out_m"], w["w_out_a"])
    h = _mlp(h, w["g_ffn"], w["w_up"], w["w_down"])
    return _ple_final(h, pe, w["g_ple"], w["w_ple_gate"], w["w_ple_proj"], g_final)


def _split_state(c_aug, m_fin):
    return c_aug[..., :M_V_DIM], c_aug[..., M_V_DIM], m_fin[..., 0]


def kernel(x_prompt, x_sample, cache_k, cache_v, state_C, state_n, state_m, page_table, p_prompt, p_sample, rel_bias_table, w_in, b_igate, b_fgate, g_mix, g_mhead, w_out, g_ffn, w_up, w_down, g_ple, w_ple_gate, w_ple_proj, g_final):
    depth = w_in.shape[0]
    assert depth == 1, "one decoder layer per call"
    layer = 0
    bp, tp, _ = x_prompt.shape
    bs, ts, _ = x_sample.shape
    past_len = page_table.shape[1] * PAGE_SIZE
    assert past_len % MOBA_BLOCK == 0 and ts <= MOBA_BLOCK
    kv_shape = (A_HEADS, A_HEAD_DIM)
    g_fin = g_final.reshape(1, D_MODEL)

    w = _layer_weights(w_in[layer], b_igate[layer], b_fgate[layer], g_mix[layer], g_mhead[layer], w_out[layer],
                       g_ffn[layer], w_up[layer], w_down[layer], g_ple[layer], w_ple_gate[layer],
                       w_ple_proj[layer])
    bias_tiles = _bias_tiles(rel_bias_table)

    hp = x_prompt.reshape(bp * tp, D_MODEL)
    zp, aq_p, ak_p, av_p, gp = _in_proj(hp, w["g_mix"], w["w_main"], w["w_gate"])
    c0 = jnp.zeros((bp, M_HEADS, M_QK_DIM, 2 * M_V_DIM), F32)
    m0 = jnp.zeros((bp, M_HEADS, GATE_PAD), F32)
    om_p, c_p, m_p = _mlstm(zp, gp, w["gate_bias"], w["g_mhead"], c0, m0, bp, tp)
    oa_p = _moba_prompt(aq_p, ak_p, av_p, bias_tiles, bp, tp)
    y_p = _tail(hp, om_p, oa_p, p_prompt[layer].reshape(bp * tp, PLE_DIM), w, g_fin)
    k_p = ak_p.reshape((1, bp, tp) + kv_shape)
    v_p = av_p.reshape((1, bp, tp) + kv_shape)
    cp, np_, mp = _split_state(c_p, m_p)

    hs = x_sample.reshape(bs * ts, D_MODEL)
    zs, aq_s, ak_s, av_s, gs = _in_proj(hs, w["g_mix"], w["w_main"], w["w_gate"])
    c0s = jnp.concatenate([state_C[layer], state_n[layer][..., None],
                           jnp.zeros((bs, M_HEADS, M_QK_DIM, M_V_DIM - 1), F32)], axis=-1)
    m0s = jnp.broadcast_to(state_m[layer][..., None], (bs, M_HEADS, GATE_PAD))
    om_s, c_s, m_s = _mlstm(zs, gs, w["gate_bias"], w["g_mhead"], c0s, m0s, bs, ts)

    hq = A_HEADS * ts
    q_s = aq_s.reshape(bs, ts, A_HEADS, A_HEAD_DIM)
    q_hq = q_s.transpose(0, 2, 1, 3).reshape(bs, hq, A_HEAD_DIM)
    k_new = ak_s.reshape(bs, hq, A_HEAD_DIM)
    v_new = av_s.reshape(bs, hq, A_HEAD_DIM)
    rows = bias_tiles[:, :, :ts, :]
    expand = lambda a: jnp.repeat(a, A_HEADS, axis=-1).reshape(hq, -1)
    past_bias = jnp.stack([expand(rows[:, 2]), expand(rows[:, 1])])
    own_bias = expand(rows[:, 0, :, :ts])
    o_part, m_part, l_part, kmean = _moba_past(page_table, q_hq, cache_k, cache_v, past_bias, layer)
    oa_hq = _moba_combine(q_hq, kmean.transpose(0, 2, 1, 3), k_new, v_new, own_bias, o_part, m_part, l_part)
    oa_s = oa_hq.reshape(bs, A_HEADS, ts, A_HEAD_DIM).transpose(0, 2, 1, 3).reshape(bs * ts, A_HEADS * A_HEAD_DIM)
    y_s = _tail(hs, om_s, oa_s, p_sample[layer].reshape(bs * ts, PLE_DIM), w, g_fin)
    k_s = k_new.reshape((1, bs, ts) + kv_shape)
    v_s = v_new.reshape((1, bs, ts) + kv_shape)
    cs, ns, ms = _split_state(c_s, m_s)

    return (y_p.reshape(bp, tp, D_MODEL), y_s.reshape(bs, ts, D_MODEL),
            k_p, v_p, cp[None], np_[None], mp[None],
            k_s, v_s, cs[None], ns[None], ms[None])
```

```python
import functools
import math

import jax
import jax.numpy as jnp
from jax import lax
from jax.experimental import pallas as pl
from jax.experimental.pallas import tpu as pltpu

F32 = jnp.float32
BF16 = jnp.bfloat16
HIGHEST = lax.Precision.HIGHEST

D_MODEL = 2048
M_HEADS = 8
M_V_DIM = 128
M_QK_DIM = 64
M_CHUNK = 64
A_HEADS = 8
A_HEAD_DIM = 128
MOBA_BLOCK = 256
MOBA_TOP_K = 3
NUM_BUCKETS = 32
MAX_EXACT = NUM_BUCKETS // 2
MAX_DISTANCE = 128
D_FF = 4 * D_MODEL
PLE_DIM = 256
PAGE_SIZE = 128
EPS = 1e-6

_MQ, _MK, _MV, _MO = 0, 512, 1024, 2048
_GATES = 3072
_AQ = 3088
D_MAIN = 6144
GATE_PAD = 128

VMEM_LIMIT = 56 * 1024 * 1024
NEG_INF = float("-inf")


def _cparams(*sem):
    return pltpu.CompilerParams(dimension_semantics=sem, vmem_limit_bytes=VMEM_LIMIT)


def _rms(x, g):
    return x * lax.rsqrt(jnp.mean(x * x, axis=-1, keepdims=True) + EPS) * g


def _dot_nt(a, b, **kw):
    return lax.dot_general(a, b, (((1,), (1,)), ((), ())), preferred_element_type=F32, **kw)


def _dot_tn(a, b, **kw):
    return lax.dot_general(a, b, (((0,), (0,)), ((), ())), preferred_element_type=F32, **kw)


def _dot(a, b, **kw):
    return jnp.dot(a, b, preferred_element_type=F32, **kw)


_IN_TN = 1024
_N_IN_TILES = D_MAIN // _IN_TN
_N_MLSTM_TILES = _GATES // _IN_TN
_GATE_COLS = _AQ - _GATES
_CAST_ROWS = 256


def _norm_gate_kernel(x_ref, g_ref, wg_ref, xn_ref, gate_ref):
    xn = _rms(x_ref[...], g_ref[...]).astype(BF16)
    xn_ref[...] = xn
    wg = wg_ref[...]
    hi = wg.astype(BF16).astype(F32)
    lane = lax.broadcasted_iota(jnp.int32, wg.shape, 1)
    w2 = jnp.where(lane < _GATE_COLS, hi,
                   jnp.where(lane < 2 * _GATE_COLS, pltpu.roll(wg - hi, _GATE_COLS, axis=1), 0.0))
    r = _dot(xn, w2.astype(BF16))
    gate_ref[...] = r + pltpu.roll(r, GATE_PAD - _GATE_COLS, axis=1)


def _norm_gate(x, g, w_in):
    m = x.shape[0]
    tm = min(512, m)
    return pl.pallas_call(
        _norm_gate_kernel,
        grid=(m // tm,),
        in_specs=[
            pl.BlockSpec((tm, D_MODEL), lambda i: (i, 0)),
            pl.BlockSpec((1, D_MODEL), lambda i: (0, 0)),
            pl.BlockSpec((D_MODEL, GATE_PAD), lambda i: (0, _GATES // GATE_PAD)),
        ],
        out_specs=[pl.BlockSpec((tm, D_MODEL), lambda i: (i, 0)),
                   pl.BlockSpec((tm, GATE_PAD), lambda i: (i, 0))],
        out_shape=[jax.ShapeDtypeStruct((m, D_MODEL), BF16), jax.ShapeDtypeStruct((m, GATE_PAD), F32)],
        compiler_params=_cparams("parallel"),
        name="norm_gate",
    )(x, g, w_in)


def _in_proj_kernel(xn_ref, wa_ref, wb_ref, zm_ref, aq_ref, ak_ref, av_ref, w_scr):
    j = pl.program_id(0)
    i = pl.program_id(1)

    @pl.when((i == 0) & (j < _N_MLSTM_TILES))
    def _():
        for r in range(0, D_MODEL, _CAST_ROWS):
            w_scr[r:r + _CAST_ROWS, :] = wa_ref[r:r + _CAST_ROWS, :].astype(BF16)

    @pl.when((i == 0) & (j >= _N_MLSTM_TILES))
    def _():
        for r in range(0, D_MODEL, _CAST_ROWS):
            w = jnp.concatenate([wa_ref[r:r + _CAST_ROWS, _GATE_COLS:], wb_ref[r:r + _CAST_ROWS, :_GATE_COLS]],
                                axis=1)
            w_scr[r:r + _CAST_ROWS, :] = w.astype(BF16)

    acc = _dot(xn_ref[...], w_scr[...])

    @pl.when(j < _N_MLSTM_TILES)
    def _():
        zm_ref[...] = acc

    for t, ref in enumerate((aq_ref, ak_ref, av_ref)):
        @pl.when(j == _N_MLSTM_TILES + t)
        def _(ref=ref):
            ref[...] = acc


def _in_proj(xn, w_in):
    m = xn.shape[0]
    tm = min(512, m)
    tn = _IN_TN
    last = m // tm - 1

    def held(j, i, first_tile, last_tile):
        return jnp.where(j < first_tile, 0, jnp.where(j <= last_tile, i, last))

    def section(t):
        return pl.BlockSpec((tm, tn), lambda j, i: (held(j, i, t, t), 0))

    sec_shape = jax.ShapeDtypeStruct((m, tn), F32)
    nm = _N_MLSTM_TILES
    return pl.pallas_call(
        _in_proj_kernel,
        grid=(_N_IN_TILES, m // tm),
        in_specs=[
            pl.BlockSpec((tm, D_MODEL), lambda j, i: (i, 0)),
            pl.BlockSpec((D_MODEL, tn), lambda j, i: (0, j)),
            pl.BlockSpec((D_MODEL, GATE_PAD), lambda j, i: (0, (j + 1) * (tn // GATE_PAD))),
        ],
        out_specs=[
            pl.BlockSpec((tm, tn), lambda j, i: (held(j, i, 0, nm - 1), jnp.minimum(j, nm - 1))),
            section(nm), section(nm + 1), section(nm + 2),
        ],
        out_shape=[jax.ShapeDtypeStruct((m, _GATES), F32), sec_shape, sec_shape, sec_shape],
        scratch_shapes=[pltpu.VMEM((D_MODEL, tn), BF16)],
        compiler_params=_cparams("arbitrary", "arbitrary"),
        name="in_proj",
    )(xn, w_in, w_in)


def _log_sigmoid(x):
    return -(jnp.maximum(-x, 0.0) + jnp.log1p(jnp.exp(-jnp.abs(x))))


def _mlstm_kernel(q_ref, k_ref, v_ref, o_ref, g_ref, gb_ref, gh_ref, c0_ref, m0_ref,
                  out_ref, cfin_ref, mfin_ref, c_scr, m_scr, *, chunk, n_chunks):
    L = chunk
    tb = L * n_chunks
    t = pl.program_id(1)

    @pl.when(t == 0)
    def _():
        c_scr[...] = c0_ref[0]
        m_scr[...] = m0_ref[0]

    g = g_ref[...] + gb_ref[...]
    lane = lax.broadcasted_iota(jnp.int32, (tb, GATE_PAD), 1)
    gl = jnp.where((lane >= M_HEADS) & (lane < 2 * M_HEADS), _log_sigmoid(g), g)
    r = lax.broadcasted_iota(jnp.int32, (tb, tb), 0)
    c = lax.broadcasted_iota(jnp.int32, (tb, tb), 1)
    tri = ((r // L == c // L) & (c <= r)).astype(F32)
    cum = _dot(tri, gl, precision=HIGHEST)

    rl = lax.broadcasted_iota(jnp.int32, (L, L), 0)
    cl = lax.broadcasted_iota(jnp.int32, (L, L), 1)
    eye = rl == cl
    causal = cl <= rl
    ones_col = (lax.broadcasted_iota(jnp.int32, (L, M_V_DIM), 1) == 0).astype(BF16)

    for h in range(M_HEADS):
        m_prev = m_scr[h:h + 1, 0:1]
        for ci in range(n_chunks):
            rows = slice(ci * L, (ci + 1) * L)
            qb = q_ref[rows, h * M_QK_DIM:(h + 1) * M_QK_DIM].astype(BF16)
            kc = k_ref[rows, h * M_QK_DIM:(h + 1) * M_QK_DIM] * (M_QK_DIM ** -0.5)
            vb = v_ref[rows, h * M_V_DIM:(h + 1) * M_V_DIM].astype(BF16)
            vaug = jnp.concatenate([vb, ones_col], axis=1)
            li = gl[rows, h:h + 1]
            b = cum[rows, M_HEADS + h:M_HEADS + h + 1]
            w_row = jnp.sum(jnp.where(eye, li - b, 0.0), axis=0, keepdims=True)
            d = jnp.where(causal, b + w_row, NEG_INF)
            a = b + m_prev
            m_t = jnp.maximum(a, jnp.max(d, axis=1, keepdims=True))
            s = _dot_nt(qb, kc.astype(BF16)) * jnp.exp(d - m_t)
            inter = jnp.exp(a - m_t)
            c_aug = c_scr[h]
            nd = inter * _dot(qb, c_aug.astype(BF16)) + _dot(s.astype(BF16), vaug)
            num = nd[:, :M_V_DIM]
            den = nd[:, M_V_DIM:M_V_DIM + 1]
            hh = num / jnp.maximum(jnp.abs(den), jnp.exp(-m_t))
            m_l = m_t[L - 1:L, :]
            b_l = b[L - 1:L, :]
            w_prev = jnp.exp(b_l + m_prev - m_l)
            ws = jnp.exp(b_l - b + li - m_l)
            c_scr[h] = w_prev * c_aug + _dot_tn((kc * ws).astype(BF16), vaug)
            m_prev = m_l
            hn = _rms(hh, gh_ref[h:h + 1, :])
            og = jax.nn.sigmoid(o_ref[rows, h * M_V_DIM:(h + 1) * M_V_DIM])
            out_ref[rows, h * M_V_DIM:(h + 1) * M_V_DIM] = (og * hn).astype(out_ref.dtype)
        m_scr[h:h + 1, :] = jnp.broadcast_to(m_prev, (1, GATE_PAD))

    @pl.when(t == pl.num_programs(1) - 1)
    def _():
        cfin_ref[0] = c_scr[...]
        mfin_ref[0] = m_scr[...]


def _mlstm(z, gates, gate_bias, g_mhead, c0_aug, m0, batch, seq):
    L = math.gcd(seq, M_CHUNK)
    n_chunks = max(1, min(seq // L, 128 // L))
    tb = L * n_chunks
    nt = seq // tb
    row = lambda b, t: b * nt + t
    kern = functools.partial(_mlstm_kernel, chunk=L, n_chunks=n_chunks)
    return pl.pallas_call(
        kern,
        grid=(batch, nt),
        in_specs=[
            pl.BlockSpec((tb, 512), lambda b, t: (row(b, t), _MQ // 512)),
            pl.BlockSpec((tb, 512), lambda b, t: (row(b, t), _MK // 512)),
            pl.BlockSpec((tb, 1024), lambda b, t: (row(b, t), _MV // 1024)),
            pl.BlockSpec((tb, 1024), lambda b, t: (row(b, t), _MO // 1024)),
            pl.BlockSpec((tb, GATE_PAD), lambda b, t: (row(b, t), 0)),
            pl.BlockSpec((1, GATE_PAD), lambda b, t: (0, 0)),
            pl.BlockSpec((M_HEADS, M_V_DIM), lambda b, t: (0, 0)),
            pl.BlockSpec((1, M_HEADS, M_QK_DIM, 2 * M_V_DIM), lambda b, t: (b, 0, 0, 0)),
            pl.BlockSpec((1, M_HEADS, GATE_PAD), lambda b, t: (b, 0, 0)),
        ],
        out_specs=[
            pl.BlockSpec((tb, M_HEADS * M_V_DIM), lambda b, t: (row(b, t), 0)),
            pl.BlockSpec((1, M_HEADS, M_QK_DIM, 2 * M_V_DIM), lambda b, t: (b, 0, 0, 0)),
            pl.BlockSpec((1, M_HEADS, GATE_PAD), lambda b, t: (b, 0, 0)),
        ],
        out_shape=[
            jax.ShapeDtypeStruct((batch * seq, M_HEADS * M_V_DIM), BF16 if tb % 16 == 0 else F32),
            jax.ShapeDtypeStruct((batch, M_HEADS, M_QK_DIM, 2 * M_V_DIM), F32),
            jax.ShapeDtypeStruct((batch, M_HEADS, GATE_PAD), F32),
        ],
        scratch_shapes=[pltpu.VMEM((M_HEADS, M_QK_DIM, 2 * M_V_DIM), F32),
                        pltpu.VMEM((M_HEADS, GATE_PAD), F32)],
        compiler_params=_cparams("parallel", "arbitrary"),
        name="mlstm",
    )(z, z, z, z, gates, gate_bias, g_mhead, c0_aug, m0)


def _t5_bucket(rel):
    n = jnp.maximum(rel, 0)
    nf = jnp.maximum(n, 1).astype(F32)
    large = MAX_EXACT + (jnp.log(nf / MAX_EXACT) / math.log(MAX_DISTANCE / MAX_EXACT)
                         * (NUM_BUCKETS - MAX_EXACT)).astype(jnp.int32)
    large = jnp.minimum(large, NUM_BUCKETS - 1)
    return jnp.where(n < MAX_EXACT, n, large)


def _bias_kernel(tab_ref, out_ref):
    h = pl.program_id(0)
    i = lax.broadcasted_iota(jnp.int32, (MOBA_BLOCK, MOBA_BLOCK), 0)
    j = lax.broadcasted_iota(jnp.int32, (MOBA_BLOCK, MOBA_BLOCK), 1)
    for kind, rel in ((0, i - j), (1, MOBA_BLOCK + i - j)):
        bucket = _t5_bucket(rel)
        acc = jnp.zeros((MOBA_BLOCK, MOBA_BLOCK), F32)
        for b in range(NUM_BUCKETS):
            acc = jnp.where(bucket == b, tab_ref[h, b], acc)
        out_ref[0, kind] = acc
    out_ref[0, 2] = jnp.full((MOBA_BLOCK, MOBA_BLOCK), tab_ref[h, NUM_BUCKETS - 1], F32)


def _bias_tiles(rel_table):
    assert MOBA_BLOCK + 1 >= MAX_DISTANCE
    tab = rel_table.T.astype(F32)
    return pl.pallas_call(
        _bias_kernel,
        grid=(A_HEADS,),
        in_specs=[pl.BlockSpec(memory_space=pltpu.SMEM)],
        out_specs=pl.BlockSpec((1, 3, MOBA_BLOCK, MOBA_BLOCK), lambda h: (h, 0, 0, 0)),
        out_shape=jax.ShapeDtypeStruct((A_HEADS, 3, MOBA_BLOCK, MOBA_BLOCK), F32),
        compiler_params=_cparams("parallel"),
        name="t5_bias_tiles",
    )(tab)


def _moba_prompt_kernel(q_ref, k_ref, v_ref, bias_ref, o_ref, kb_scr, vb_scr, *, n_blocks):
    nb = n_blocks
    blk = MOBA_BLOCK
    kb_scr[...] = k_ref[...].astype(BF16)
    vb_scr[...] = v_ref[...].astype(BF16)
    need_gate = nb - 1 > MOBA_TOP_K
    if need_gate:
        means = [jnp.mean(k_ref[n * blk:(n + 1) * blk, :], axis=0, keepdims=True) for n in range(nb)]
        kmean = jnp.concatenate(means + [jnp.zeros((128 - nb, A_HEAD_DIM), F32)], axis=0)
    ri = lax.broadcasted_iota(jnp.int32, (blk, blk), 0)
    ci = lax.broadcasted_iota(jnp.int32, (blk, blk), 1)
    causal = ci <= ri

    for i in range(nb):
        q = q_ref[i * blk:(i + 1) * blk, :] * (A_HEAD_DIM ** -0.5)
        qb = q.astype(BF16)
        sel = None
        if i > MOBA_TOP_K:
            gate = _dot_nt(q, kmean, precision=HIGHEST)
            cols = [gate[:, n:n + 1] for n in range(i)]
            sel = []
            for n in range(i):
                rank = jnp.zeros((blk, 1), F32)
                for j in range(i):
                    if j == n:
                        continue
                    beats = (cols[j] >= cols[n]) if j < n else (cols[j] > cols[n])
                    rank = rank + beats.astype(F32)
                sel.append(jnp.where(rank < MOBA_TOP_K, 0.0, NEG_INF))
        s = _dot_nt(qb, kb_scr[0:(i + 1) * blk, :])
        tiles = []
        m = None
        for n in range(i + 1):
            kind = 0 if n == i else (1 if n == i - 1 else 2)
            tl = s[:, n * blk:(n + 1) * blk] + bias_ref[0, kind]
            if n == i:
                tl = jnp.where(causal, tl, NEG_INF)
            elif sel is not None:
                tl = tl + sel[n]
            tiles.append(tl)
            tmax = jnp.max(tl, axis=1, keepdims=True)
            m = tmax if m is None else jnp.maximum(m, tmax)
        l = jnp.zeros((blk, 1), F32)
        acc = jnp.zeros((blk, A_HEAD_DIM), F32)
        for n in range(i + 1):
            p = jnp.exp(tiles[n] - m)
            l = l + jnp.sum(p, axis=1, keepdims=True)
            acc = acc + _dot(p.astype(BF16), vb_scr[n * blk:(n + 1) * blk, :])
        o_ref[i * blk:(i + 1) * blk, :] = (acc / l).astype(o_ref.dtype)


def _moba_prompt(aq, ak, av, bias_tiles, batch, seq):
    assert seq % MOBA_BLOCK == 0
    nb = seq // MOBA_BLOCK
    head = pl.BlockSpec((seq, A_HEAD_DIM), lambda b, h: (b, h))
    kern = functools.partial(_moba_prompt_kernel, n_blocks=nb)
    return pl.pallas_call(
        kern,
        grid=(batch, A_HEADS),
        in_specs=[head, head, head,
                  pl.BlockSpec((1, 3, MOBA_BLOCK, MOBA_BLOCK), lambda b, h: (h, 0, 0, 0))],
        out_specs=head,
        out_shape=jax.ShapeDtypeStruct((batch * seq, A_HEADS * A_HEAD_DIM), BF16),
        scratch_shapes=[pltpu.VMEM((seq, A_HEAD_DIM), BF16), pltpu.VMEM((seq, A_HEAD_DIM), BF16)],
        compiler_params=_cparams("parallel", "parallel"),
        name="moba_prompt",
    )(aq, ak, av, bias_tiles)


PAGES_PER_BLOCK = MOBA_BLOCK // PAGE_SIZE
PAST_BLOCKS_PER_STEP = 4


def _moba_past_kernel(pt_ref, q_ref, *refs, n_pages_step):
    del pt_ref
    k_refs = refs[:n_pages_step]
    v_refs = refs[n_pages_step:2 * n_pages_step]
    bias_ref, o_ref, m_ref, l_ref, km_ref = refs[2 * n_pages_step:]
    rows = PAGE_SIZE * A_HEADS
    qb = (q_ref[0] * (A_HEAD_DIM ** -0.5)).astype(BF16)
    hq = qb.shape[0]
    nq = hq // A_HEADS
    r = lax.broadcasted_iota(jnp.int32, (hq, PAGES_PER_BLOCK * rows), 0)
    c = lax.broadcasted_iota(jnp.int32, (hq, PAGES_PER_BLOCK * rows), 1)
    same_head = (c % A_HEADS) == (r // nq)
    n_blocks = n_pages_step // PAGES_PER_BLOCK
    is_last_step = pl.program_id(1) == pl.num_programs(1) - 1
    for g in range(n_blocks):
        pages = range(g * PAGES_PER_BLOCK, (g + 1) * PAGES_PER_BLOCK)
        k = jnp.concatenate([k_refs[p][0, 0].reshape(rows, A_HEAD_DIM) for p in pages], axis=0)
        v = jnp.concatenate([v_refs[p][0, 0].reshape(rows, A_HEAD_DIM) for p in pages], axis=0)
        ksum = sum(jnp.sum(k_refs[p][0, 0], axis=0) for p in pages)
        km_ref[0, g] = ksum / MOBA_BLOCK
        bias = bias_ref[jnp.where(is_last_step, 1, 0)] if g == n_blocks - 1 else bias_ref[0]
        lt = _dot_nt(qb, k.astype(BF16))
        lm = jnp.where(same_head, lt + bias, NEG_INF)
        m = jnp.max(lm, axis=1, keepdims=True)
        p_ = jnp.exp(lm - m)
        l = jnp.sum(p_, axis=1, keepdims=True)
        o_ref[0, g] = _dot(p_.astype(BF16), v.astype(BF16))
        m_ref[0, g] = jnp.broadcast_to(m, (hq, A_HEAD_DIM))
        l_ref[0, g] = jnp.broadcast_to(l, (hq, A_HEAD_DIM))


def _moba_past(page_table, q_hq, cache_k, cache_v, past_bias, layer):
    bs, n_pages = page_table.shape
    assert n_pages % PAGES_PER_BLOCK == 0
    nbp = n_pages // PAGES_PER_BLOCK
    gb = math.gcd(nbp, PAST_BLOCKS_PER_STEP)
    pps = gb * PAGES_PER_BLOCK
    hq = q_hq.shape[1]
    page = (1, 1, PAGE_SIZE, A_HEADS, A_HEAD_DIM)
    part = pl.BlockSpec((1, gb, hq, A_HEAD_DIM), lambda b, n, pt: (b, n, 0, 0))

    def page_spec(p):
        return pl.BlockSpec(page, lambda b, n, pt: (layer, pt[b, pps * n + p], 0, 0, 0))

    grid_spec = pltpu.PrefetchScalarGridSpec(
        num_scalar_prefetch=1,
        grid=(bs, nbp // gb),
        in_specs=[pl.BlockSpec((1, hq, A_HEAD_DIM), lambda b, n, pt: (b, 0, 0))]
        + [page_spec(p) for p in range(pps)] * 2
        + [pl.BlockSpec((2, hq, MOBA_BLOCK * A_HEADS), lambda b, n, pt: (0, 0, 0))],
        out_specs=[part, part, part,
                   pl.BlockSpec((1, gb, A_HEADS, A_HEAD_DIM), lambda b, n, pt: (b, n, 0, 0))],
    )
    pshape = jax.ShapeDtypeStruct((bs, nbp, hq, A_HEAD_DIM), F32)
    return pl.pallas_call(
        functools.partial(_moba_past_kernel, n_pages_step=pps),
        grid_spec=grid_spec,
        out_shape=[pshape, pshape, pshape, jax.ShapeDtypeStruct((bs, nbp, A_HEADS, A_HEAD_DIM), F32)],
        compiler_params=_cparams("parallel", "parallel"),
        name="moba_past_blocks",
    )(page_table, q_hq, *([cache_k] * pps), *([cache_v] * pps), past_bias)


def _moba_combine_kernel(q_ref, km_ref, kn_ref, vn_ref, ob_ref, op_ref, mp_ref, lp_ref, out_ref):
    nbp = km_ref.shape[2]
    q = q_ref[0] * (A_HEAD_DIM ** -0.5)
    hq = q.shape[0]
    nq = hq // A_HEADS
    gate = jnp.concatenate(
        [_dot_nt(q[h * nq:(h + 1) * nq], km_ref[0, h], precision=HIGHEST) for h in range(A_HEADS)], axis=0)
    lane = lax.broadcasted_iota(jnp.int32, (hq, nbp), 1)
    rank = jnp.zeros((hq, nbp), F32)
    for j in range(nbp):
        gj = gate[:, j:j + 1]
        beats = (gj > gate) | ((gj == gate) & (j < lane))
        rank = rank + beats.astype(F32)
    selneg = jnp.where(rank < MOBA_TOP_K, 0.0, NEG_INF)
    qb = q.astype(BF16)
    lt = _dot_nt(qb, kn_ref[0].astype(BF16))
    r = lax.broadcasted_iota(jnp.int32, lt.shape, 0)
    c = lax.broadcasted_iota(jnp.int32, lt.shape, 1)
    ok = ((c % A_HEADS) == (r // nq)) & ((c // A_HEADS) <= (r % nq))
    lo = jnp.where(ok, lt + ob_ref[...], NEG_INF)
    m = jnp.max(lo, axis=1, keepdims=True)
    for n in range(nbp):
        m = jnp.maximum(m, mp_ref[0, n][:, 0:1] + selneg[:, n:n + 1])
    p = jnp.exp(lo - m)
    l = jnp.sum(p, axis=1, keepdims=True)
    acc = _dot(p.astype(BF16), vn_ref[0].astype(BF16))
    for n in range(nbp):
        w = jnp.exp(mp_ref[0, n][:, 0:1] + selneg[:, n:n + 1] - m)
        l = l + w * lp_ref[0, n][:, 0:1]
        acc = acc + w * op_ref[0, n]
    out_ref[0] = acc / l


def _moba_combine(q_hq, kmean_hn, k_new, v_new, own_bias, o_part, m_part, l_part):
    bs, hq, _ = q_hq.shape
    nbp = o_part.shape[1]
    part = pl.BlockSpec((1, nbp, hq, A_HEAD_DIM), lambda b: (b, 0, 0, 0))
    tok = pl.BlockSpec((1, hq, A_HEAD_DIM), lambda b: (b, 0, 0))
    return pl.pallas_call(
        _moba_combine_kernel,
        grid=(bs,),
        in_specs=[tok,
                  pl.BlockSpec((1, A_HEADS, nbp, A_HEAD_DIM), lambda b: (b, 0, 0, 0)),
                  tok, tok,
                  pl.BlockSpec((hq, hq), lambda b: (0, 0)),
                  part, part, part],
        out_specs=tok,
        out_shape=jax.ShapeDtypeStruct((bs, hq, A_HEAD_DIM), F32),
        compiler_params=_cparams("parallel"),
        name="moba_combine",
    )(q_hq, kmean_hn, k_new, v_new, own_bias, o_part, m_part, l_part)


def _out_proj_kernel(h_ref, om_ref, oa_ref, wm_ref, wa_ref, o_ref, wm_scr, wa_scr):
    @pl.when(pl.program_id(1) == 0)
    def _():
        for r in range(0, wm_scr.shape[0], _CAST_ROWS):
            wm_scr[r:r + _CAST_ROWS, :] = wm_ref[r:r + _CAST_ROWS, :].astype(BF16)
            wa_scr[r:r + _CAST_ROWS, :] = wa_ref[r:r + _CAST_ROWS, :].astype(BF16)

    o_ref[...] = (h_ref[...] + _dot(om_ref[...].astype(BF16), wm_scr[...])
                  + _dot(oa_ref[...].astype(BF16), wa_scr[...]))


def _out_proj(h, out_m, out_a, w_out):
    m = h.shape[0]
    tm = min(512, m)
    tn = 1024
    half = M_HEADS * M_V_DIM
    return pl.pallas_call(
        _out_proj_kernel,
        grid=(D_MODEL // tn, m // tm),
        in_specs=[
            pl.BlockSpec((tm, tn), lambda j, i: (i, j)),
            pl.BlockSpec((tm, half), lambda j, i: (i, 0)),
            pl.BlockSpec((tm, half), lambda j, i: (i, 0)),
            pl.BlockSpec((half, tn), lambda j, i: (0, j)),
            pl.BlockSpec((half, tn), lambda j, i: (1, j)),
        ],
        out_specs=pl.BlockSpec((tm, tn), lambda j, i: (i, j)),
        out_shape=jax.ShapeDtypeStruct((m, D_MODEL), F32),
        scratch_shapes=[pltpu.VMEM((half, tn), BF16), pltpu.VMEM((half, tn), BF16)],
        compiler_params=_cparams("arbitrary", "arbitrary"),
        name="out_proj",
    )(h, out_m, out_a, w_out, w_out)


def _mlp_kernel(h_ref, g_ref, wu_ref, wd_ref, o_ref, xn_ref):
    f = pl.program_id(1)

    @pl.when(f == 0)
    def _():
        xn_ref[...] = _rms(h_ref[...], g_ref[...]).astype(BF16)
        o_ref[...] = h_ref[...]

    u = _dot(xn_ref[...], wu_ref[...])
    a = jnp.square(jnp.maximum(u, 0.0)).astype(BF16)
    o_ref[...] += _dot(a, wd_ref[...])


def _mlp(h, g, w_up, w_down):
    m = h.shape[0]
    tm = min(512, m)
    tf = 1024
    return pl.pallas_call(
        _mlp_kernel,
        grid=(m // tm, D_FF // tf),
        in_specs=[
            pl.BlockSpec((tm, D_MODEL), lambda i, f: (i, 0)),
            pl.BlockSpec((1, D_MODEL), lambda i, f: (0, 0)),
            pl.BlockSpec((D_MODEL, tf), lambda i, f: (0, f)),
            pl.BlockSpec((tf, D_MODEL), lambda i, f: (f, 0)),
        ],
        out_specs=pl.BlockSpec((tm, D_MODEL), lambda i, f: (i, 0)),
        out_shape=jax.ShapeDtypeStruct((m, D_MODEL), F32),
        scratch_shapes=[pltpu.VMEM((tm, D_MODEL), BF16)],
        compiler_params=_cparams("parallel", "arbitrary"),
        name="mlp",
    )(h, g, w_up, w_down)


def _ple_kernel(h_ref, pe_ref, g_ref, wg_ref, wp_ref, gf_ref, y_ref):
    h = h_ref[...]
    gate = jax.nn.sigmoid(_dot(_rms(h, g_ref[...]).astype(BF16), wg_ref[...]))
    proj = _dot(pe_ref[...].astype(BF16), wp_ref[...])
    y_ref[...] = _rms(h + gate * proj, gf_ref[...])


def _ple_final(h, pe, g_ple, w_gate, w_proj, g_final):
    m = h.shape[0]
    tm = min(256, m)
    vec = pl.BlockSpec((1, D_MODEL), lambda i: (0, 0))
    return pl.pallas_call(
        _ple_kernel,
        grid=(m // tm,),
        in_specs=[
            pl.BlockSpec((tm, D_MODEL), lambda i: (i, 0)),
            pl.BlockSpec((tm, PLE_DIM), lambda i: (i, 0)),
            vec,
            pl.BlockSpec((D_MODEL, D_MODEL), lambda i: (0, 0)),
            pl.BlockSpec((PLE_DIM, D_MODEL), lambda i: (0, 0)),
            vec,
        ],
        out_specs=pl.BlockSpec((tm, D_MODEL), lambda i: (i, 0)),
        out_shape=jax.ShapeDtypeStruct((m, D_MODEL), F32),
        compiler_params=_cparams("parallel"),
        name="ple_final_norm",
    )(h, pe, g_ple, w_gate, w_proj, g_final)


def _layer_weights(w_in, b_igate, b_fgate, g_mix, g_mhead, w_out, g_ffn, w_up, w_down,
                   g_ple, w_ple_gate, w_ple_proj):
    gate_bias = jnp.pad(jnp.concatenate([b_igate, b_fgate]), (0, GATE_PAD - 2 * M_HEADS)).reshape(1, GATE_PAD)
    return dict(
        w_in=w_in, gate_bias=gate_bias.astype(F32),
        g_mix=g_mix.reshape(1, D_MODEL), g_mhead=g_mhead, w_out=w_out,
        g_ffn=g_ffn.reshape(1, D_MODEL), w_up=w_up.astype(BF16), w_down=w_down.astype(BF16),
        g_ple=g_ple.reshape(1, D_MODEL), w_ple_gate=w_ple_gate.astype(BF16),
        w_ple_proj=w_ple_proj.astype(BF16))


def _tail(h, out_m, out_a, pe, w, g_final):
    h = _out_proj(h, out_m, out_a, w["w_out"])
    h = _mlp(h, w["g_ffn"], w["w_up"], w["w_down"])
    return _ple_final(h, pe, w["g_ple"], w["w_ple_gate"], w["w_ple_proj"], g_final)


def _split_state(c_aug, m_fin):
    return c_aug[..., :M_V_DIM], c_aug[..., M_V_DIM], m_fin[..., 0]


def kernel(x_prompt, x_sample, cache_k, cache_v, state_C, state_n, state_m, page_table, p_prompt, p_sample, rel_bias_table, w_in, b_igate, b_fgate, g_mix, g_mhead, w_out, g_ffn, w_up, w_down, g_ple, w_ple_gate, w_ple_proj, g_final):
    depth = w_in.shape[0]
    assert depth == 1, "one decoder layer per call"
    layer = 0
    bp, tp, _ = x_prompt.shape
    bs, ts, _ = x_sample.shape
    past_len = page_table.shape[1] * PAGE_SIZE
    assert past_len % MOBA_BLOCK == 0 and ts <= MOBA_BLOCK
    kv_shape = (A_HEADS, A_HEAD_DIM)
    g_fin = g_final.reshape(1, D_MODEL)

    w = _layer_weights(w_in[layer], b_igate[layer], b_fgate[layer], g_mix[layer], g_mhead[layer], w_out[layer],
                       g_ffn[layer], w_up[layer], w_down[layer], g_ple[layer], w_ple_gate[layer],
                       w_ple_proj[layer])
    bias_tiles = _bias_tiles(rel_bias_table)

    hp = x_prompt.reshape(bp * tp, D_MODEL)
    xn_p, gp = _norm_gate(hp, w["g_mix"], w["w_in"])
    zp, aq_p, ak_p, av_p = _in_proj(xn_p, w["w_in"])
    c0 = jnp.zeros((bp, M_HEADS, M_QK_DIM, 2 * M_V_DIM), F32)
    m0 = jnp.zeros((bp, M_HEADS, GATE_PAD), F32)
    om_p, c_p, m_p = _mlstm(zp, gp, w["gate_bias"], w["g_mhead"], c0, m0, bp, tp)
    oa_p = _moba_prompt(aq_p, ak_p, av_p, bias_tiles, bp, tp)
    y_p = _tail(hp, om_p, oa_p, p_prompt[layer].reshape(bp * tp, PLE_DIM), w, g_fin)
    k_p = ak_p.reshape((1, bp, tp) + kv_shape)
    v_p = av_p.reshape((1, bp, tp) + kv_shape)
    cp, np_, mp = _split_state(c_p, m_p)

    hs = x_sample.reshape(bs * ts, D_MODEL)
    xn_s, gs = _norm_gate(hs, w["g_mix"], w["w_in"])
    zs, aq_s, ak_s, av_s = _in_proj(xn_s, w["w_in"])
    c0s = jnp.concatenate([state_C[layer], state_n[layer][..., None],
                           jnp.zeros((bs, M_HEADS, M_QK_DIM, M_V_DIM - 1), F32)], axis=-1)
    m0s = jnp.broadcast_to(state_m[layer][..., None], (bs, M_HEADS, GATE_PAD))
    om_s, c_s, m_s = _mlstm(zs, gs, w["gate_bias"], w["g_mhead"], c0s, m0s, bs, ts)

    hq = A_HEADS * ts
    q_s = aq_s.reshape(bs, ts, A_HEADS, A_HEAD_DIM)
    q_hq = q_s.transpose(0, 2, 1, 3).reshape(bs, hq, A_HEAD_DIM)
    k_new = ak_s.reshape(bs, hq, A_HEAD_DIM)
    v_new = av_s.reshape(bs, hq, A_HEAD_DIM)
    rows = bias_tiles[:, :, :ts, :]
    expand = lambda a: jnp.repeat(a, A_HEADS, axis=-1).reshape(hq, -1)
    past_bias = jnp.stack([expand(rows[:, 2]), expand(rows[:, 1])])
    own_bias = expand(rows[:, 0, :, :ts])
    o_part, m_part, l_part, kmean = _moba_past(page_table, q_hq, cache_k, cache_v, past_bias, layer)
    oa_hq = _moba_combine(q_hq, kmean.transpose(0, 2, 1, 3), k_new, v_new, own_bias, o_part, m_part, l_part)
    oa_s = oa_hq.reshape(bs, A_HEADS, ts, A_HEAD_DIM).transpose(0, 2, 1, 3).reshape(bs * ts, A_HEADS * A_HEAD_DIM)
    y_s = _tail(hs, om_s, oa_s, p_sample[layer].reshape(bs * ts, PLE_DIM), w, g_fin)
    k_s = k_new.reshape((1, bs, ts) + kv_shape)
    v_s = v_new.reshape((1, bs, ts) + kv_shape)
    cs, ns, ms = _split_state(c_s, m_s)

    return (y_p.reshape(bp, tp, D_MODEL), y_s.reshape(bs, ts, D_MODEL),
            k_p, v_p, cp[None], np_[None], mp[None],
            k_s, v_s, cs[None], ns[None], ms[None])
```

```python
import functools
import math

import jax
import jax.numpy as jnp
from jax import lax
from jax.experimental import pallas as pl
from jax.experimental.pallas import tpu as pltpu

F32 = jnp.float32
BF16 = jnp.bfloat16
HIGHEST = lax.Precision.HIGHEST

D_MODEL = 2048
M_HEADS = 8
M_V_DIM = 128
M_QK_DIM = 64
M_CHUNK = 64
A_HEADS = 8
A_HEAD_DIM = 128
MOBA_BLOCK = 256
MOBA_TOP_K = 3
NUM_BUCKETS = 32
MAX_EXACT = NUM_BUCKETS // 2
MAX_DISTANCE = 128
D_FF = 4 * D_MODEL
PLE_DIM = 256
PAGE_SIZE = 128
EPS = 1e-6

_MQ, _MK, _MV, _MO = 0, 512, 1024, 2048
_GATES = 3072
_AQ = 3088
D_MAIN = 6144
GATE_PAD = 128

VMEM_LIMIT = 56 * 1024 * 1024
NEG_INF = float("-inf")


def _cparams(*sem):
    return pltpu.CompilerParams(dimension_semantics=sem, vmem_limit_bytes=VMEM_LIMIT)


def _rms(x, g):
    return x * lax.rsqrt(jnp.mean(x * x, axis=-1, keepdims=True) + EPS) * g


def _dot_nt(a, b, **kw):
    return lax.dot_general(a, b, (((1,), (1,)), ((), ())), preferred_element_type=F32, **kw)


def _dot_tn(a, b, **kw):
    return lax.dot_general(a, b, (((0,), (0,)), ((), ())), preferred_element_type=F32, **kw)


def _dot(a, b, **kw):
    return jnp.dot(a, b, preferred_element_type=F32, **kw)


_IN_TN = 1024
_N_IN_TILES = D_MAIN // _IN_TN
_N_MLSTM_TILES = _GATES // _IN_TN
_GATE_COLS = _AQ - _GATES
_CAST_ROWS = 256


def _norm_gate_kernel(x_ref, g_ref, wg_ref, xn_ref, gate_ref):
    xn = _rms(x_ref[...], g_ref[...]).astype(BF16)
    xn_ref[...] = xn
    wg = wg_ref[...]
    hi = wg.astype(BF16).astype(F32)
    row = lax.broadcasted_iota(jnp.int32, wg.shape, 0)
    w2 = jnp.where(row < _GATE_COLS, hi,
                   jnp.where(row < 2 * _GATE_COLS, pltpu.roll(wg - hi, _GATE_COLS, axis=0), 0.0))
    r = _dot_nt(xn, w2.astype(BF16))
    gate_ref[...] = r + pltpu.roll(r, GATE_PAD - _GATE_COLS, axis=1)


def _norm_gate(x, g, w_in_t):
    m = x.shape[0]
    tm = min(512, m)
    return pl.pallas_call(
        _norm_gate_kernel,
        grid=(m // tm,),
        in_specs=[
            pl.BlockSpec((tm, D_MODEL), lambda i: (i, 0)),
            pl.BlockSpec((1, D_MODEL), lambda i: (0, 0)),
            pl.BlockSpec((GATE_PAD, D_MODEL), lambda i: (_GATES // GATE_PAD, 0)),
        ],
        out_specs=[pl.BlockSpec((tm, D_MODEL), lambda i: (i, 0)),
                   pl.BlockSpec((tm, GATE_PAD), lambda i: (i, 0))],
        out_shape=[jax.ShapeDtypeStruct((m, D_MODEL), BF16), jax.ShapeDtypeStruct((m, GATE_PAD), F32)],
        compiler_params=_cparams("parallel"),
        name="norm_gate",
    )(x, g, w_in_t)


def _in_proj_kernel(xn_ref, wa_ref, wb_ref, zm_ref, aq_ref, ak_ref, av_ref, w_scr):
    j = pl.program_id(0)
    i = pl.program_id(1)

    @pl.when((i == 0) & (j < _N_MLSTM_TILES))
    def _():
        for r in range(0, _IN_TN, _CAST_ROWS):
            w_scr[r:r + _CAST_ROWS, :] = wa_ref[r:r + _CAST_ROWS, :].astype(BF16)

    @pl.when((i == 0) & (j >= _N_MLSTM_TILES))
    def _():
        body = _IN_TN - _GATE_COLS
        for r in range(0, body, _CAST_ROWS):
            n = min(_CAST_ROWS, body - r)
            w_scr[r:r + n, :] = wa_ref[_GATE_COLS + r:_GATE_COLS + r + n, :].astype(BF16)
        w_scr[body:, :] = wb_ref[:_GATE_COLS, :].astype(BF16)

    acc = _dot_nt(xn_ref[...], w_scr[...])

    @pl.when(j < _N_MLSTM_TILES)
    def _():
        zm_ref[...] = acc

    for t, ref in enumerate((aq_ref, ak_ref, av_ref)):
        @pl.when(j == _N_MLSTM_TILES + t)
        def _(ref=ref):
            ref[...] = acc


def _in_proj(xn, w_in_t):
    m = xn.shape[0]
    tm = min(512, m)
    tn = _IN_TN
    last = m // tm - 1

    def held(j, i, first_tile, last_tile):
        return jnp.where(j < first_tile, 0, jnp.where(j <= last_tile, i, last))

    def section(t):
        return pl.BlockSpec((tm, tn), lambda j, i: (held(j, i, t, t), 0))

    sec_shape = jax.ShapeDtypeStruct((m, tn), F32)
    nm = _N_MLSTM_TILES
    return pl.pallas_call(
        _in_proj_kernel,
        grid=(_N_IN_TILES, m // tm),
        in_specs=[
            pl.BlockSpec((tm, D_MODEL), lambda j, i: (i, 0)),
            pl.BlockSpec((tn, D_MODEL), lambda j, i: (j, 0)),
            pl.BlockSpec((GATE_PAD, D_MODEL), lambda j, i: ((j + 1) * (tn // GATE_PAD), 0)),
        ],
        out_specs=[
            pl.BlockSpec((tm, tn), lambda j, i: (held(j, i, 0, nm - 1), jnp.minimum(j, nm - 1))),
            section(nm), section(nm + 1), section(nm + 2),
        ],
        out_shape=[jax.ShapeDtypeStruct((m, _GATES), F32), sec_shape, sec_shape, sec_shape],
        scratch_shapes=[pltpu.VMEM((tn, D_MODEL), BF16)],
        compiler_params=_cparams("arbitrary", "arbitrary"),
        name="in_proj",
    )(xn, w_in_t, w_in_t)


def _log_sigmoid(x):
    return -(jnp.maximum(-x, 0.0) + jnp.log1p(jnp.exp(-jnp.abs(x))))


def _mlstm_kernel(q_ref, k_ref, v_ref, o_ref, g_ref, gb_ref, gh_ref, c0_ref, m0_ref,
                  out_ref, cfin_ref, mfin_ref, c_scr, m_scr, *, chunk, n_chunks):
    L = chunk
    tb = L * n_chunks
    t = pl.program_id(1)

    @pl.when(t == 0)
    def _():
        c_scr[...] = c0_ref[0]
        m_scr[...] = m0_ref[0]

    g = g_ref[...] + gb_ref[...]
    lane = lax.broadcasted_iota(jnp.int32, (tb, GATE_PAD), 1)
    gl = jnp.where((lane >= M_HEADS) & (lane < 2 * M_HEADS), _log_sigmoid(g), g)
    r = lax.broadcasted_iota(jnp.int32, (tb, tb), 0)
    c = lax.broadcasted_iota(jnp.int32, (tb, tb), 1)
    tri = ((r // L == c // L) & (c <= r)).astype(F32)
    cum = _dot(tri, gl, precision=HIGHEST)

    rl = lax.broadcasted_iota(jnp.int32, (L, L), 0)
    cl = lax.broadcasted_iota(jnp.int32, (L, L), 1)
    eye = rl == cl
    causal = cl <= rl
    ones_col = (lax.broadcasted_iota(jnp.int32, (L, M_V_DIM), 1) == 0).astype(BF16)

    for h in range(M_HEADS):
        m_prev = m_scr[h:h + 1, 0:1]
        for ci in range(n_chunks):
            rows = slice(ci * L, (ci + 1) * L)
            qb = q_ref[rows, h * M_QK_DIM:(h + 1) * M_QK_DIM].astype(BF16)
            kc = k_ref[rows, h * M_QK_DIM:(h + 1) * M_QK_DIM] * (M_QK_DIM ** -0.5)
            vb = v_ref[rows, h * M_V_DIM:(h + 1) * M_V_DIM].astype(BF16)
            vaug = jnp.concatenate([vb, ones_col], axis=1)
            li = gl[rows, h:h + 1]
            b = cum[rows, M_HEADS + h:M_HEADS + h + 1]
            w_row = jnp.sum(jnp.where(eye, li - b, 0.0), axis=0, keepdims=True)
            d = jnp.where(causal, b + w_row, NEG_INF)
            a = b + m_prev
            m_t = jnp.maximum(a, jnp.max(d, axis=1, keepdims=True))
            s = _dot_nt(qb, kc.astype(BF16)) * jnp.exp(d - m_t)
            inter = jnp.exp(a - m_t)
            c_aug = c_scr[h]
            nd = inter * _dot(qb, c_aug.astype(BF16)) + _dot(s.astype(BF16), vaug)
            num = nd[:, :M_V_DIM]
            den = nd[:, M_V_DIM:M_V_DIM + 1]
            hh = num / jnp.maximum(jnp.abs(den), jnp.exp(-m_t))
            m_l = m_t[L - 1:L, :]
            b_l = b[L - 1:L, :]
            w_prev = jnp.exp(b_l + m_prev - m_l)
            ws = jnp.exp(b_l - b + li - m_l)
            c_scr[h] = w_prev * c_aug + _dot_tn((kc * ws).astype(BF16), vaug)
            m_prev = m_l
            hn = _rms(hh, gh_ref[h:h + 1, :])
            og = jax.nn.sigmoid(o_ref[rows, h * M_V_DIM:(h + 1) * M_V_DIM])
            out_ref[rows, h * M_V_DIM:(h + 1) * M_V_DIM] = (og * hn).astype(out_ref.dtype)
        m_scr[h:h + 1, :] = jnp.broadcast_to(m_prev, (1, GATE_PAD))

    @pl.when(t == pl.num_programs(1) - 1)
    def _():
        cfin_ref[0] = c_scr[...]
        mfin_ref[0] = m_scr[...]


def _mlstm(z, gates, gate_bias, g_mhead, c0_aug, m0, batch, seq):
    L = math.gcd(seq, M_CHUNK)
    n_chunks = max(1, min(seq // L, 128 // L))
    tb = L * n_chunks
    nt = seq // tb
    row = lambda b, t: b * nt + t
    kern = functools.partial(_mlstm_kernel, chunk=L, n_chunks=n_chunks)
    return pl.pallas_call(
        kern,
        grid=(batch, nt),
        in_specs=[
            pl.BlockSpec((tb, 512), lambda b, t: (row(b, t), _MQ // 512)),
            pl.BlockSpec((tb, 512), lambda b, t: (row(b, t), _MK // 512)),
            pl.BlockSpec((tb, 1024), lambda b, t: (row(b, t), _MV // 1024)),
            pl.BlockSpec((tb, 1024), lambda b, t: (row(b, t), _MO // 1024)),
            pl.BlockSpec((tb, GATE_PAD), lambda b, t: (row(b, t), 0)),
            pl.BlockSpec((1, GATE_PAD), lambda b, t: (0, 0)),
            pl.BlockSpec((M_HEADS, M_V_DIM), lambda b, t: (0, 0)),
            pl.BlockSpec((1, M_HEADS, M_QK_DIM, 2 * M_V_DIM), lambda b, t: (b, 0, 0, 0)),
            pl.BlockSpec((1, M_HEADS, GATE_PAD), lambda b, t: (b, 0, 0)),
        ],
        out_specs=[
            pl.BlockSpec((tb, M_HEADS * M_V_DIM), lambda b, t: (row(b, t), 0)),
            pl.BlockSpec((1, M_HEADS, M_QK_DIM, 2 * M_V_DIM), lambda b, t: (b, 0, 0, 0)),
            pl.BlockSpec((1, M_HEADS, GATE_PAD), lambda b, t: (b, 0, 0)),
        ],
        out_shape=[
            jax.ShapeDtypeStruct((batch * seq, M_HEADS * M_V_DIM), BF16 if tb % 16 == 0 else F32),
            jax.ShapeDtypeStruct((batch, M_HEADS, M_QK_DIM, 2 * M_V_DIM), F32),
            jax.ShapeDtypeStruct((batch, M_HEADS, GATE_PAD), F32),
        ],
        scratch_shapes=[pltpu.VMEM((M_HEADS, M_QK_DIM, 2 * M_V_DIM), F32),
                        pltpu.VMEM((M_HEADS, GATE_PAD), F32)],
        compiler_params=_cparams("parallel", "arbitrary"),
        name="mlstm",
    )(z, z, z, z, gates, gate_bias, g_mhead, c0_aug, m0)


def _t5_bucket(rel):
    n = jnp.maximum(rel, 0)
    nf = jnp.maximum(n, 1).astype(F32)
    large = MAX_EXACT + (jnp.log(nf / MAX_EXACT) / math.log(MAX_DISTANCE / MAX_EXACT)
                         * (NUM_BUCKETS - MAX_EXACT)).astype(jnp.int32)
    large = jnp.minimum(large, NUM_BUCKETS - 1)
    return jnp.where(n < MAX_EXACT, n, large)


def _bias_kernel(tab_ref, out_ref):
    h = pl.program_id(0)
    i = lax.broadcasted_iota(jnp.int32, (MOBA_BLOCK, MOBA_BLOCK), 1)
    j = lax.broadcasted_iota(jnp.int32, (MOBA_BLOCK, MOBA_BLOCK), 0)
    for kind, rel in ((0, i - j), (1, MOBA_BLOCK + i - j)):
        bucket = _t5_bucket(rel)
        acc = jnp.zeros((MOBA_BLOCK, MOBA_BLOCK), F32)
        for b in range(NUM_BUCKETS):
            acc = jnp.where(bucket == b, tab_ref[h, b], acc)
        out_ref[0, kind] = acc
    out_ref[0, 2] = jnp.full((MOBA_BLOCK, MOBA_BLOCK), tab_ref[h, NUM_BUCKETS - 1], F32)


def _bias_tiles(rel_table):
    assert MOBA_BLOCK + 1 >= MAX_DISTANCE
    tab = rel_table.T.astype(F32)
    return pl.pallas_call(
        _bias_kernel,
        grid=(A_HEADS,),
        in_specs=[pl.BlockSpec(memory_space=pltpu.SMEM)],
        out_specs=pl.BlockSpec((1, 3, MOBA_BLOCK, MOBA_BLOCK), lambda h: (h, 0, 0, 0)),
        out_shape=jax.ShapeDtypeStruct((A_HEADS, 3, MOBA_BLOCK, MOBA_BLOCK), F32),
        compiler_params=_cparams("parallel"),
        name="t5_bias_tiles",
    )(tab)


def _moba_prompt_kernel(q_ref, k_ref, v_ref, bias_ref, wu_ref, wd_ref, o_ref, wub_ref, wdb_ref,
                        kb_scr, qt_scr, vt_scr, gt_scr, s_scr, *, n_blocks):
    nb = n_blocks
    blk = MOBA_BLOCK
    nbp = gt_scr.shape[0]
    wub_ref[...] = wu_ref[...].astype(BF16)
    wdb_ref[...] = wd_ref[...].astype(BF16)

    kb_scr[...] = k_ref[...].astype(BF16)
    for c in range(nb):
        cols = slice(c * blk, (c + 1) * blk)
        qt_scr[:, cols] = (q_ref[cols, :] * (A_HEAD_DIM ** -0.5)).T.astype(BF16)
        vt_scr[:, cols] = v_ref[cols, :].T.astype(BF16)
    need_gate = nb - 1 > MOBA_TOP_K
    if need_gate:
        means = [jnp.mean(k_ref[n * blk:(n + 1) * blk, :], axis=0, keepdims=True) for n in range(nb)]
        kmean = jnp.concatenate(means + [jnp.zeros((128 - nb, A_HEAD_DIM), F32)], axis=0)
        for c in range(MOBA_TOP_K + 1, nb):
            cols = slice(c * blk, (c + 1) * blk)
            gate = _dot_nt(q_ref[cols, :] * (A_HEAD_DIM ** -0.5), kmean, precision=HIGHEST)
            gt_scr[:, cols] = gate.T[:nbp, :]
    key = lax.broadcasted_iota(jnp.int32, (blk, blk), 0)
    qry = lax.broadcasted_iota(jnp.int32, (blk, blk), 1)
    causal = key <= qry
    blk_id = lax.broadcasted_iota(jnp.int32, (nbp, blk), 0)

    for i in range(nb):
        cols = slice(i * blk, (i + 1) * blk)
        selneg = None
        if i > MOBA_TOP_K:
            g = gt_scr[:, cols]
            rank = jnp.zeros((nbp, blk), F32)
            for j in range(i):
                gj = g[j:j + 1, :]
                rank = rank + ((gj > g) | ((gj == g) & (j < blk_id))).astype(F32)
            selneg = jnp.where(rank < MOBA_TOP_K, 0.0, NEG_INF)
        s_scr[0:(i + 1) * blk, :] = _dot(kb_scr[0:(i + 1) * blk, :], qt_scr[:, cols])
        m = jnp.full((1, blk), NEG_INF, F32)
        for n in range(i + 1):
            rows = slice(n * blk, (n + 1) * blk)
            kind = 0 if n == i else (1 if n == i - 1 else 2)
            st = s_scr[rows, :] + bias_ref[0, kind]
            if n == i:
                st = jnp.where(causal, st, NEG_INF)
            elif selneg is not None:
                st = st + selneg[n:n + 1, :]
            s_scr[rows, :] = st
            m = jnp.maximum(m, jnp.max(st, axis=0, keepdims=True))
        l = jnp.zeros((1, blk), F32)
        acc = jnp.zeros((A_HEAD_DIM, blk), F32)
        for n in range(i + 1):
            rows = slice(n * blk, (n + 1) * blk)
            p = jnp.exp(s_scr[rows, :] - m)
            l = l + jnp.sum(p, axis=0, keepdims=True)
            acc = acc + _dot(vt_scr[:, rows], p.astype(BF16))
        o_ref[cols, :] = (acc / l).T.astype(o_ref.dtype)


def _moba_prompt(aq, ak, av, bias_tiles, w_up, w_down, batch, seq):
    assert seq % MOBA_BLOCK == 0
    nb = seq // MOBA_BLOCK
    assert nb <= 128
    nbp = -(-nb // 8) * 8
    steps = batch * A_HEADS
    slab = D_FF // steps
    assert D_FF % steps == 0 and slab % 128 == 0
    head = pl.BlockSpec((seq, A_HEAD_DIM), lambda b, h: (b, h))
    up_slab = pl.BlockSpec((D_MODEL, slab), lambda b, h: (0, b * A_HEADS + h))
    down_slab = pl.BlockSpec((slab, D_MODEL), lambda b, h: (b * A_HEADS + h, 0))
    kern = functools.partial(_moba_prompt_kernel, n_blocks=nb)
    return pl.pallas_call(
        kern,
        grid=(batch, A_HEADS),
        in_specs=[head, head, head,
                  pl.BlockSpec((1, 3, MOBA_BLOCK, MOBA_BLOCK), lambda b, h: (h, 0, 0, 0)),
                  up_slab, down_slab],
        out_specs=[head, up_slab, down_slab],
        out_shape=[jax.ShapeDtypeStruct((batch * seq, A_HEADS * A_HEAD_DIM), BF16),
                   jax.ShapeDtypeStruct((D_MODEL, D_FF), BF16),
                   jax.ShapeDtypeStruct((D_FF, D_MODEL), BF16)],
        scratch_shapes=[pltpu.VMEM((seq, A_HEAD_DIM), BF16),
                        pltpu.VMEM((A_HEAD_DIM, seq), BF16),
                        pltpu.VMEM((A_HEAD_DIM, seq), BF16),
                        pltpu.VMEM((nbp, seq), F32),
                        pltpu.VMEM((seq, MOBA_BLOCK), F32)],
        compiler_params=_cparams("parallel", "parallel"),
        name="moba_prompt",
    )(aq, ak, av, bias_tiles, w_up, w_down)


PAGES_PER_BLOCK = MOBA_BLOCK // PAGE_SIZE
PAST_BLOCKS_PER_STEP = 4


def _moba_past_kernel(pt_ref, q_ref, *refs, n_pages_step):
    del pt_ref
    k_refs = refs[:n_pages_step]
    v_refs = refs[n_pages_step:2 * n_pages_step]
    bias_ref, o_ref, m_ref, l_ref, km_ref = refs[2 * n_pages_step:]
    rows = PAGE_SIZE * A_HEADS
    qb = (q_ref[0] * (A_HEAD_DIM ** -0.5)).astype(BF16)
    hq = qb.shape[0]
    nq = hq // A_HEADS
    r = lax.broadcasted_iota(jnp.int32, (hq, PAGES_PER_BLOCK * rows), 0)
    c = lax.broadcasted_iota(jnp.int32, (hq, PAGES_PER_BLOCK * rows), 1)
    same_head = (c % A_HEADS) == (r // nq)
    n_blocks = n_pages_step // PAGES_PER_BLOCK
    is_last_step = pl.program_id(1) == pl.num_programs(1) - 1
    for g in range(n_blocks):
        pages = range(g * PAGES_PER_BLOCK, (g + 1) * PAGES_PER_BLOCK)
        k = jnp.concatenate([k_refs[p][0, 0].reshape(rows, A_HEAD_DIM) for p in pages], axis=0)
        v = jnp.concatenate([v_refs[p][0, 0].reshape(rows, A_HEAD_DIM) for p in pages], axis=0)
        ksum = sum(jnp.sum(k_refs[p][0, 0], axis=0) for p in pages)
        km_ref[0, g] = ksum / MOBA_BLOCK
        bias = bias_ref[jnp.where(is_last_step, 1, 0)] if g == n_blocks - 1 else bias_ref[0]
        lt = _dot_nt(qb, k.astype(BF16))
        lm = jnp.where(same_head, lt + bias, NEG_INF)
        m = jnp.max(lm, axis=1, keepdims=True)
        p_ = jnp.exp(lm - m)
        l = jnp.sum(p_, axis=1, keepdims=True)
        o_ref[0, g] = _dot(p_.astype(BF16), v.astype(BF16))
        m_ref[0, g] = jnp.broadcast_to(m, (hq, A_HEAD_DIM))
        l_ref[0, g] = jnp.broadcast_to(l, (hq, A_HEAD_DIM))


def _moba_past(page_table, q_hq, cache_k, cache_v, past_bias, layer):
    bs, n_pages = page_table.shape
    assert n_pages % PAGES_PER_BLOCK == 0
    nbp = n_pages // PAGES_PER_BLOCK
    gb = math.gcd(nbp, PAST_BLOCKS_PER_STEP)
    pps = gb * PAGES_PER_BLOCK
    hq = q_hq.shape[1]
    page = (1, 1, PAGE_SIZE, A_HEADS, A_HEAD_DIM)
    part = pl.BlockSpec((1, gb, hq, A_HEAD_DIM), lambda b, n, pt: (b, n, 0, 0))

    def page_spec(p):
        return pl.BlockSpec(page, lambda b, n, pt: (layer, pt[b, pps * n + p], 0, 0, 0))

    grid_spec = pltpu.PrefetchScalarGridSpec(
        num_scalar_prefetch=1,
        grid=(bs, nbp // gb),
        in_specs=[pl.BlockSpec((1, hq, A_HEAD_DIM), lambda b, n, pt: (b, 0, 0))]
        + [page_spec(p) for p in range(pps)] * 2
        + [pl.BlockSpec((2, hq, MOBA_BLOCK * A_HEADS), lambda b, n, pt: (0, 0, 0))],
        out_specs=[part, part, part,
                   pl.BlockSpec((1, gb, A_HEADS, A_HEAD_DIM), lambda b, n, pt: (b, n, 0, 0))],
    )
    pshape = jax.ShapeDtypeStruct((bs, nbp, hq, A_HEAD_DIM), F32)
    return pl.pallas_call(
        functools.partial(_moba_past_kernel, n_pages_step=pps),
        grid_spec=grid_spec,
        out_shape=[pshape, pshape, pshape, jax.ShapeDtypeStruct((bs, nbp, A_HEADS, A_HEAD_DIM), F32)],
        compiler_params=_cparams("parallel", "parallel"),
        name="moba_past_blocks",
    )(page_table, q_hq, *([cache_k] * pps), *([cache_v] * pps), past_bias)


def _moba_combine_kernel(q_ref, km_ref, kn_ref, vn_ref, ob_ref, op_ref, mp_ref, lp_ref, out_ref):
    nbp = km_ref.shape[2]
    q = q_ref[0] * (A_HEAD_DIM ** -0.5)
    hq = q.shape[0]
    nq = hq // A_HEADS
    gate = jnp.concatenate(
        [_dot_nt(q[h * nq:(h + 1) * nq], km_ref[0, h], precision=HIGHEST) for h in range(A_HEADS)], axis=0)
    lane = lax.broadcasted_iota(jnp.int32, (hq, nbp), 1)
    rank = jnp.zeros((hq, nbp), F32)
    for j in range(nbp):
        gj = gate[:, j:j + 1]
        beats = (gj > gate) | ((gj == gate) & (j < lane))
        rank = rank + beats.astype(F32)
    selneg = jnp.where(rank < MOBA_TOP_K, 0.0, NEG_INF)
    qb = q.astype(BF16)
    lt = _dot_nt(qb, kn_ref[0].astype(BF16))
    r = lax.broadcasted_iota(jnp.int32, lt.shape, 0)
    c = lax.broadcasted_iota(jnp.int32, lt.shape, 1)
    ok = ((c % A_HEADS) == (r // nq)) & ((c // A_HEADS) <= (r % nq))
    lo = jnp.where(ok, lt + ob_ref[...], NEG_INF)
    m = jnp.max(lo, axis=1, keepdims=True)
    for n in range(nbp):
        m = jnp.maximum(m, mp_ref[0, n][:, 0:1] + selneg[:, n:n + 1])
    p = jnp.exp(lo - m)
    l = jnp.sum(p, axis=1, keepdims=True)
    acc = _dot(p.astype(BF16), vn_ref[0].astype(BF16))
    for n in range(nbp):
        w = jnp.exp(mp_ref[0, n][:, 0:1] + selneg[:, n:n + 1] - m)
        l = l + w * lp_ref[0, n][:, 0:1]
        acc = acc + w * op_ref[0, n]
    out_ref[0] = acc / l


def _moba_combine(q_hq, kmean_hn, k_new, v_new, own_bias, o_part, m_part, l_part):
    bs, hq, _ = q_hq.shape
    nbp = o_part.shape[1]
    part = pl.BlockSpec((1, nbp, hq, A_HEAD_DIM), lambda b: (b, 0, 0, 0))
    tok = pl.BlockSpec((1, hq, A_HEAD_DIM), lambda b: (b, 0, 0))
    return pl.pallas_call(
        _moba_combine_kernel,
        grid=(bs,),
        in_specs=[tok,
                  pl.BlockSpec((1, A_HEADS, nbp, A_HEAD_DIM), lambda b: (b, 0, 0, 0)),
                  tok, tok,
                  pl.BlockSpec((hq, hq), lambda b: (0, 0)),
                  part, part, part],
        out_specs=tok,
        out_shape=jax.ShapeDtypeStruct((bs, hq, A_HEAD_DIM), F32),
        compiler_params=_cparams("parallel"),
        name="moba_combine",
    )(q_hq, kmean_hn, k_new, v_new, own_bias, o_part, m_part, l_part)


def _out_proj_kernel(h_ref, om_ref, oa_ref, wm_ref, wa_ref, o_ref, wm_scr, wa_scr):
    @pl.when(pl.program_id(1) == 0)
    def _():
        for r in range(0, wm_scr.shape[0], _CAST_ROWS):
            wm_scr[r:r + _CAST_ROWS, :] = wm_ref[r:r + _CAST_ROWS, :].astype(BF16)
            wa_scr[r:r + _CAST_ROWS, :] = wa_ref[r:r + _CAST_ROWS, :].astype(BF16)

    o_ref[...] = (h_ref[...] + _dot(om_ref[...].astype(BF16), wm_scr[...])
                  + _dot(oa_ref[...].astype(BF16), wa_scr[...]))


def _out_proj(h, out_m, out_a, w_out):
    m = h.shape[0]
    tm = min(512, m)
    tn = 1024
    half = M_HEADS * M_V_DIM
    return pl.pallas_call(
        _out_proj_kernel,
        grid=(D_MODEL // tn, m // tm),
        in_specs=[
            pl.BlockSpec((tm, tn), lambda j, i: (i, j)),
            pl.BlockSpec((tm, half), lambda j, i: (i, 0)),
            pl.BlockSpec((tm, half), lambda j, i: (i, 0)),
            pl.BlockSpec((half, tn), lambda j, i: (0, j)),
            pl.BlockSpec((half, tn), lambda j, i: (1, j)),
        ],
        out_specs=pl.BlockSpec((tm, tn), lambda j, i: (i, j)),
        out_shape=jax.ShapeDtypeStruct((m, D_MODEL), F32),
        scratch_shapes=[pltpu.VMEM((half, tn), BF16), pltpu.VMEM((half, tn), BF16)],
        compiler_params=_cparams("arbitrary", "arbitrary"),
        name="out_proj",
    )(h, out_m, out_a, w_out, w_out)


def _mlp_kernel(h_ref, g_ref, wu_ref, wd_ref, o_ref, xn_ref):
    f = pl.program_id(1)

    @pl.when(f == 0)
    def _():
        xn_ref[...] = _rms(h_ref[...], g_ref[...]).astype(BF16)
        o_ref[...] = h_ref[...]

    u = _dot(xn_ref[...], wu_ref[...])
    a = jnp.square(jnp.maximum(u, 0.0)).astype(BF16)
    o_ref[...] += _dot(a, wd_ref[...])


def _mlp(h, g, w_up, w_down):
    m = h.shape[0]
    tm = min(512, m)
    tf = 1024
    return pl.pallas_call(
        _mlp_kernel,
        grid=(m // tm, D_FF // tf),
        in_specs=[
            pl.BlockSpec((tm, D_MODEL), lambda i, f: (i, 0)),
            pl.BlockSpec((1, D_MODEL), lambda i, f: (0, 0)),
            pl.BlockSpec((D_MODEL, tf), lambda i, f: (0, f)),
            pl.BlockSpec((tf, D_MODEL), lambda i, f: (f, 0)),
        ],
        out_specs=pl.BlockSpec((tm, D_MODEL), lambda i, f: (i, 0)),
        out_shape=jax.ShapeDtypeStruct((m, D_MODEL), F32),
        scratch_shapes=[pltpu.VMEM((tm, D_MODEL), BF16)],
        compiler_params=_cparams("parallel", "arbitrary"),
        name="mlp",
    )(h, g, w_up, w_down)


def _ple_kernel(h_ref, pe_ref, g_ref, wg_ref, wp_ref, gf_ref, y_ref):
    h = h_ref[...]
    gate = jax.nn.sigmoid(_dot(_rms(h, g_ref[...]).astype(BF16), wg_ref[...]))
    proj = _dot(pe_ref[...].astype(BF16), wp_ref[...])
    y_ref[...] = _rms(h + gate * proj, gf_ref[...])


def _ple_final(h, pe, g_ple, w_gate, w_proj, g_final):
    m = h.shape[0]
    tm = min(256, m)
    vec = pl.BlockSpec((1, D_MODEL), lambda i: (0, 0))
    return pl.pallas_call(
        _ple_kernel,
        grid=(m // tm,),
        in_specs=[
            pl.BlockSpec((tm, D_MODEL), lambda i: (i, 0)),
            pl.BlockSpec((tm, PLE_DIM), lambda i: (i, 0)),
            vec,
            pl.BlockSpec((D_MODEL, D_MODEL), lambda i: (0, 0)),
            pl.BlockSpec((PLE_DIM, D_MODEL), lambda i: (0, 0)),
            vec,
        ],
        out_specs=pl.BlockSpec((tm, D_MODEL), lambda i: (i, 0)),
        out_shape=jax.ShapeDtypeStruct((m, D_MODEL), F32),
        compiler_params=_cparams("parallel"),
        name="ple_final_norm",
    )(h, pe, g_ple, w_gate, w_proj, g_final)


def _layer_weights(w_in, b_igate, b_fgate, g_mix, g_mhead, w_out, g_ffn, w_up, w_down,
                   g_ple, w_ple_gate, w_ple_proj):
    gate_bias = jnp.pad(jnp.concatenate([b_igate, b_fgate]), (0, GATE_PAD - 2 * M_HEADS)).reshape(1, GATE_PAD)
    return dict(
        w_in=w_in.T, gate_bias=gate_bias.astype(F32),
        g_mix=g_mix.reshape(1, D_MODEL), g_mhead=g_mhead, w_out=w_out,
        g_ffn=g_ffn.reshape(1, D_MODEL), w_up=w_up, w_down=w_down,
        g_ple=g_ple.reshape(1, D_MODEL), w_ple_gate=w_ple_gate.astype(BF16),
        w_ple_proj=w_ple_proj.astype(BF16))


def _tail(h, out_m, out_a, pe, w, g_final):
    h = _out_proj(h, out_m, out_a, w["w_out"])
    h = _mlp(h, w["g_ffn"], w["w_up_bf16"], w["w_down_bf16"])
    return _ple_final(h, pe, w["g_ple"], w["w_ple_gate"], w["w_ple_proj"], g_final)


def _split_state(c_aug, m_fin):
    return c_aug[..., :M_V_DIM], c_aug[..., M_V_DIM], m_fin[..., 0]


def kernel(x_prompt, x_sample, cache_k, cache_v, state_C, state_n, state_m, page_table, p_prompt, p_sample, rel_bias_table, w_in, b_igate, b_fgate, g_mix, g_mhead, w_out, g_ffn, w_up, w_down, g_ple, w_ple_gate, w_ple_proj, g_final):
    depth = w_in.shape[0]
    assert depth == 1, "one decoder layer per call"
    layer = 0
    bp, tp, _ = x_prompt.shape
    bs, ts, _ = x_sample.shape
    past_len = page_table.shape[1] * PAGE_SIZE
    assert past_len % MOBA_BLOCK == 0 and ts <= MOBA_BLOCK
    kv_shape = (A_HEADS, A_HEAD_DIM)
    g_fin = g_final.reshape(1, D_MODEL)

    w = _layer_weights(w_in[layer], b_igate[layer], b_fgate[layer], g_mix[layer], g_mhead[layer], w_out[layer],
                       g_ffn[layer], w_up[layer], w_down[layer], g_ple[layer], w_ple_gate[layer],
                       w_ple_proj[layer])
    bias_tiles = _bias_tiles(rel_bias_table)

    hp = x_prompt.reshape(bp * tp, D_MODEL)
    xn_p, gp = _norm_gate(hp, w["g_mix"], w["w_in"])
    zp, aq_p, ak_p, av_p = _in_proj(xn_p, w["w_in"])
    c0 = jnp.zeros((bp, M_HEADS, M_QK_DIM, 2 * M_V_DIM), F32)
    m0 = jnp.zeros((bp, M_HEADS, GATE_PAD), F32)
    om_p, c_p, m_p = _mlstm(zp, gp, w["gate_bias"], w["g_mhead"], c0, m0, bp, tp)
    oa_p, w["w_up_bf16"], w["w_down_bf16"] = _moba_prompt(aq_p, ak_p, av_p, bias_tiles, w["w_up"], w["w_down"],
                                                          bp, tp)
    y_p = _tail(hp, om_p, oa_p, p_prompt[layer].reshape(bp * tp, PLE_DIM), w, g_fin)
    k_p = ak_p.reshape((1, bp, tp) + kv_shape)
    v_p = av_p.reshape((1, bp, tp) + kv_shape)
    cp, np_, mp = _split_state(c_p, m_p)

    hs = x_sample.reshape(bs * ts, D_MODEL)
    xn_s, gs = _norm_gate(hs, w["g_mix"], w["w_in"])
    zs, aq_s, ak_s, av_s = _in_proj(xn_s, w["w_in"])
    c0s = jnp.concatenate([state_C[layer], state_n[layer][..., None],
                           jnp.zeros((bs, M_HEADS, M_QK_DIM, M_V_DIM - 1), F32)], axis=-1)
    m0s = jnp.broadcast_to(state_m[layer][..., None], (bs, M_HEADS, GATE_PAD))
    om_s, c_s, m_s = _mlstm(zs, gs, w["gate_bias"], w["g_mhead"], c0s, m0s, bs, ts)

    hq = A_HEADS * ts
    q_s = aq_s.reshape(bs, ts, A_HEADS, A_HEAD_DIM)
    q_hq = q_s.transpose(0, 2, 1, 3).reshape(bs, hq, A_HEAD_DIM)
    k_new = ak_s.reshape(bs, hq, A_HEAD_DIM)
    v_new = av_s.reshape(bs, hq, A_HEAD_DIM)
    rows = bias_tiles[:, :, :, :ts].transpose(0, 1, 3, 2)
    expand = lambda a: jnp.repeat(a, A_HEADS, axis=-1).reshape(hq, -1)
    past_bias = jnp.stack([expand(rows[:, 2]), expand(rows[:, 1])])
    own_bias = expand(rows[:, 0, :, :ts])
    o_part, m_part, l_part, kmean = _moba_past(page_table, q_hq, cache_k, cache_v, past_bias, layer)
    oa_hq = _moba_combine(q_hq, kmean.transpose(0, 2, 1, 3), k_new, v_new, own_bias, o_part, m_part, l_part)
    oa_s = oa_hq.reshape(bs, A_HEADS, ts, A_HEAD_DIM).transpose(0, 2, 1, 3).reshape(bs * ts, A_HEADS * A_HEAD_DIM)
    y_s = _tail(hs, om_s, oa_s, p_sample[layer].reshape(bs * ts, PLE_DIM), w, g_fin)
    k_s = k_new.reshape((1, bs, ts) + kv_shape)
    v_s = v_new.reshape((1, bs, ts) + kv_shape)
    cs, ns, ms = _split_state(c_s, m_s)

    return (y_p.reshape(bp, tp, D_MODEL), y_s.reshape(bs, ts, D_MODEL),
            k_p, v_p, cp[None], np_[None], mp[None],
            k_s, v_s, cs[None], ns[None], ms[None])
```

```python
import functools
import math

import jax
import jax.numpy as jnp
from jax import lax
from jax.experimental import pallas as pl
from jax.experimental.pallas import tpu as pltpu

F32 = jnp.float32
BF16 = jnp.bfloat16
HIGHEST = lax.Precision.HIGHEST

D_MODEL = 2048
M_HEADS = 8
M_V_DIM = 128
M_QK_DIM = 64
M_CHUNK = 64
A_HEADS = 8
A_HEAD_DIM = 128
MOBA_BLOCK = 256
MOBA_TOP_K = 3
NUM_BUCKETS = 32
MAX_EXACT = NUM_BUCKETS // 2
MAX_DISTANCE = 128
D_FF = 4 * D_MODEL
PLE_DIM = 256
PAGE_SIZE = 128
EPS = 1e-6

_MQ, _MK, _MV, _MO = 0, 512, 1024, 2048
_GATES = 3072
_AQ = 3088
D_MAIN = 6144
GATE_PAD = 128

VMEM_LIMIT = 56 * 1024 * 1024
NEG_INF = float("-inf")


def _cparams(*sem):
    return pltpu.CompilerParams(dimension_semantics=sem, vmem_limit_bytes=VMEM_LIMIT)


def _rms(x, g):
    return x * lax.rsqrt(jnp.mean(x * x, axis=-1, keepdims=True) + EPS) * g


def _dot_nt(a, b, **kw):
    return lax.dot_general(a, b, (((1,), (1,)), ((), ())), preferred_element_type=F32, **kw)


def _dot_tn(a, b, **kw):
    return lax.dot_general(a, b, (((0,), (0,)), ((), ())), preferred_element_type=F32, **kw)


def _dot(a, b, **kw):
    return jnp.dot(a, b, preferred_element_type=F32, **kw)


_IN_TN = 1024
_N_IN_TILES = D_MAIN // _IN_TN
_N_MLSTM_TILES = _GATES // _IN_TN
_GATE_COLS = _AQ - _GATES
_CAST_ROWS = 256


def _norm_gate_kernel(x_ref, g_ref, wg_ref, xn_ref, gate_ref):
    xn = _rms(x_ref[...], g_ref[...]).astype(BF16)
    xn_ref[...] = xn
    wg = wg_ref[...]
    hi = wg.astype(BF16).astype(F32)
    row = lax.broadcasted_iota(jnp.int32, wg.shape, 0)
    w2 = jnp.where(row < _GATE_COLS, hi,
                   jnp.where(row < 2 * _GATE_COLS, pltpu.roll(wg - hi, _GATE_COLS, axis=0), 0.0))
    r = _dot_nt(xn, w2.astype(BF16))
    gate_ref[...] = r + pltpu.roll(r, GATE_PAD - _GATE_COLS, axis=1)


def _norm_gate(x, g, w_in_t):
    m = x.shape[0]
    tm = min(512, m)
    return pl.pallas_call(
        _norm_gate_kernel,
        grid=(m // tm,),
        in_specs=[
            pl.BlockSpec((tm, D_MODEL), lambda i: (i, 0)),
            pl.BlockSpec((1, D_MODEL), lambda i: (0, 0)),
            pl.BlockSpec((GATE_PAD, D_MODEL), lambda i: (_GATES // GATE_PAD, 0)),
        ],
        out_specs=[pl.BlockSpec((tm, D_MODEL), lambda i: (i, 0)),
                   pl.BlockSpec((tm, GATE_PAD), lambda i: (i, 0))],
        out_shape=[jax.ShapeDtypeStruct((m, D_MODEL), BF16), jax.ShapeDtypeStruct((m, GATE_PAD), F32)],
        compiler_params=_cparams("parallel"),
        name="norm_gate",
    )(x, g, w_in_t)


def _in_proj_kernel(xn_ref, wa_ref, wb_ref, zm_ref, aq_ref, ak_ref, av_ref, w_scr):
    j = pl.program_id(0)
    i = pl.program_id(1)

    @pl.when((i == 0) & (j < _N_MLSTM_TILES))
    def _():
        for r in range(0, _IN_TN, _CAST_ROWS):
            w_scr[r:r + _CAST_ROWS, :] = wa_ref[r:r + _CAST_ROWS, :].astype(BF16)

    @pl.when((i == 0) & (j >= _N_MLSTM_TILES))
    def _():
        body = _IN_TN - _GATE_COLS
        for r in range(0, body, _CAST_ROWS):
            n = min(_CAST_ROWS, body - r)
            w_scr[r:r + n, :] = wa_ref[_GATE_COLS + r:_GATE_COLS + r + n, :].astype(BF16)
        w_scr[body:, :] = wb_ref[:_GATE_COLS, :].astype(BF16)

    acc = _dot_nt(xn_ref[...], w_scr[...])

    @pl.when(j < _N_MLSTM_TILES)
    def _():
        zm_ref[...] = acc

    for t, ref in enumerate((aq_ref, ak_ref, av_ref)):
        @pl.when(j == _N_MLSTM_TILES + t)
        def _(ref=ref):
            ref[...] = acc


def _in_proj(xn, w_in_t):
    m = xn.shape[0]
    tm = min(512, m)
    tn = _IN_TN
    last = m // tm - 1

    def held(j, i, first_tile, last_tile):
        return jnp.where(j < first_tile, 0, jnp.where(j <= last_tile, i, last))

    def section(t):
        return pl.BlockSpec((tm, tn), lambda j, i: (held(j, i, t, t), 0))

    sec_shape = jax.ShapeDtypeStruct((m, tn), F32)
    nm = _N_MLSTM_TILES
    return pl.pallas_call(
        _in_proj_kernel,
        grid=(_N_IN_TILES, m // tm),
        in_specs=[
            pl.BlockSpec((tm, D_MODEL), lambda j, i: (i, 0)),
            pl.BlockSpec((tn, D_MODEL), lambda j, i: (j, 0)),
            pl.BlockSpec((GATE_PAD, D_MODEL), lambda j, i: ((j + 1) * (tn // GATE_PAD), 0)),
        ],
        out_specs=[
            pl.BlockSpec((tm, tn), lambda j, i: (held(j, i, 0, nm - 1), jnp.minimum(j, nm - 1))),
            section(nm), section(nm + 1), section(nm + 2),
        ],
        out_shape=[jax.ShapeDtypeStruct((m, _GATES), F32), sec_shape, sec_shape, sec_shape],
        scratch_shapes=[pltpu.VMEM((tn, D_MODEL), BF16)],
        compiler_params=_cparams("arbitrary", "arbitrary"),
        name="in_proj",
    )(xn, w_in_t, w_in_t)


def _log_sigmoid(x):
    return -(jnp.maximum(-x, 0.0) + jnp.log1p(jnp.exp(-jnp.abs(x))))


def _bdot(a, b):
    return lax.dot_general(a, b, (((2,), (1,)), ((0,), (0,))), preferred_element_type=F32)


def _bdot_nt(a, b):
    return lax.dot_general(a, b, (((2,), (2,)), ((0,), (0,))), preferred_element_type=F32)


def _bdot_tn(a, b):
    return lax.dot_general(a, b, (((1,), (1,)), ((0,), (0,))), preferred_element_type=F32)


def _mlstm_kernel(q_ref, k_ref, v_ref, o_ref, g_ref, gb_ref, gh_ref, c0_ref, m0_ref,
                  out_ref, cfin_ref, mfin_ref, c_scr, m_scr, *, chunk, n_chunks):
    L = chunk
    H = M_HEADS
    tb = L * n_chunks
    t = pl.program_id(1)

    @pl.when(t == 0)
    def _():
        c_scr[...] = c0_ref[0]
        m_scr[...] = m0_ref[0]

    g = g_ref[...] + gb_ref[...]
    lane = lax.broadcasted_iota(jnp.int32, (tb, GATE_PAD), 1)
    gl = jnp.where((lane >= H) & (lane < 2 * H), _log_sigmoid(g), g)
    r = lax.broadcasted_iota(jnp.int32, (tb, tb), 0)
    c = lax.broadcasted_iota(jnp.int32, (tb, tb), 1)
    tri = ((r // L == c // L) & (c <= r)).astype(F32)
    cum = _dot(tri, gl, precision=HIGHEST)

    def rows(ci):
        return slice(ci * L, (ci + 1) * L)

    def stack(f):
        return jnp.stack([f(ci, h) for ci in range(n_chunks) for h in range(H)])

    ones_col = (lax.broadcasted_iota(jnp.int32, (L, M_V_DIM), 1) == 0).astype(BF16)
    li = stack(lambda ci, h: gl[rows(ci), h:h + 1])
    b = stack(lambda ci, h: cum[rows(ci), H + h:H + h + 1])
    qb = stack(lambda ci, h: q_ref[rows(ci), h * M_QK_DIM:(h + 1) * M_QK_DIM]).astype(BF16)
    kc = stack(lambda ci, h: k_ref[rows(ci), h * M_QK_DIM:(h + 1) * M_QK_DIM]) * (M_QK_DIM ** -0.5)
    vaug = stack(lambda ci, h: jnp.concatenate(
        [v_ref[rows(ci), h * M_V_DIM:(h + 1) * M_V_DIM].astype(BF16), ones_col], axis=1))

    rl = lax.broadcasted_iota(jnp.int32, (L, L), 0)
    cl = lax.broadcasted_iota(jnp.int32, (L, L), 1)
    w_row = jnp.sum(jnp.where(rl == cl, li - b, 0.0), axis=1, keepdims=True)
    d = jnp.where(cl <= rl, b + w_row, NEG_INF)
    m_loc = jnp.max(d, axis=2, keepdims=True)
    s = _bdot_nt(qb, kc.astype(BF16)) * jnp.exp(d - m_loc)
    nd_loc = _bdot(s.astype(BF16), vaug)
    g_last = m_loc[:, L - 1:L, :]
    b_last = b[:, L - 1:L, :]
    ws = jnp.exp(b_last - b + li - g_last)
    dc_loc = _bdot_tn((kc * ws).astype(BF16), vaug)

    m_prev = m_scr[:, :, 0:1]
    c_aug = c_scr[...]
    for ci in range(n_chunks):
        grp = slice(ci * H, (ci + 1) * H)
        a = b[grp] + m_prev
        m_t = jnp.maximum(a, m_loc[grp])
        nd = (jnp.exp(a - m_t) * _bdot(qb[grp], c_aug.astype(BF16))
              + jnp.exp(m_loc[grp] - m_t) * nd_loc[grp])
        hh = nd[:, :, :M_V_DIM] / jnp.maximum(jnp.abs(nd[:, :, M_V_DIM:M_V_DIM + 1]), jnp.exp(-m_t))
        hn = _rms(hh, gh_ref[...])
        for h in range(H):
            og = jax.nn.sigmoid(o_ref[rows(ci), h * M_V_DIM:(h + 1) * M_V_DIM])
            out_ref[rows(ci), h * M_V_DIM:(h + 1) * M_V_DIM] = (og * hn[h]).astype(out_ref.dtype)
        m_new = jnp.maximum(b_last[grp] + m_prev, g_last[grp])
        c_aug = (jnp.exp(b_last[grp] + m_prev - m_new) * c_aug
                 + jnp.exp(g_last[grp] - m_new) * dc_loc[grp])
        m_prev = m_new
    c_scr[...] = c_aug
    m_scr[...] = jnp.broadcast_to(m_prev, m_scr.shape)

    @pl.when(t == pl.num_programs(1) - 1)
    def _():
        cfin_ref[0] = c_scr[...]
        mfin_ref[0] = m_scr[...]


def _mlstm(z, gates, gate_bias, g_mhead, c0_aug, m0, batch, seq):
    L = math.gcd(seq, M_CHUNK)
    n_chunks = max(1, min(seq // L, 256 // L))
    tb = L * n_chunks
    nt = seq // tb
    row = lambda b, t: b * nt + t
    kern = functools.partial(_mlstm_kernel, chunk=L, n_chunks=n_chunks)
    return pl.pallas_call(
        kern,
        grid=(batch, nt),
        in_specs=[
            pl.BlockSpec((tb, 512), lambda b, t: (row(b, t), _MQ // 512)),
            pl.BlockSpec((tb, 512), lambda b, t: (row(b, t), _MK // 512)),
            pl.BlockSpec((tb, 1024), lambda b, t: (row(b, t), _MV // 1024)),
            pl.BlockSpec((tb, 1024), lambda b, t: (row(b, t), _MO // 1024)),
            pl.BlockSpec((tb, GATE_PAD), lambda b, t: (row(b, t), 0)),
            pl.BlockSpec((1, GATE_PAD), lambda b, t: (0, 0)),
            pl.BlockSpec((M_HEADS, 1, M_V_DIM), lambda b, t: (0, 0, 0)),
            pl.BlockSpec((1, M_HEADS, M_QK_DIM, 2 * M_V_DIM), lambda b, t: (b, 0, 0, 0)),
            pl.BlockSpec((1, M_HEADS, 1, GATE_PAD), lambda b, t: (b, 0, 0, 0)),
        ],
        out_specs=[
            pl.BlockSpec((tb, M_HEADS * M_V_DIM), lambda b, t: (row(b, t), 0)),
            pl.BlockSpec((1, M_HEADS, M_QK_DIM, 2 * M_V_DIM), lambda b, t: (b, 0, 0, 0)),
            pl.BlockSpec((1, M_HEADS, 1, GATE_PAD), lambda b, t: (b, 0, 0, 0)),
        ],
        out_shape=[
            jax.ShapeDtypeStruct((batch * seq, M_HEADS * M_V_DIM), BF16 if tb % 16 == 0 else F32),
            jax.ShapeDtypeStruct((batch, M_HEADS, M_QK_DIM, 2 * M_V_DIM), F32),
            jax.ShapeDtypeStruct((batch, M_HEADS, 1, GATE_PAD), F32),
        ],
        scratch_shapes=[pltpu.VMEM((M_HEADS, M_QK_DIM, 2 * M_V_DIM), F32),
                        pltpu.VMEM((M_HEADS, 1, GATE_PAD), F32)],
        compiler_params=_cparams("parallel", "arbitrary"),
        name="mlstm",
    )(z, z, z, z, gates, gate_bias, g_mhead.reshape(M_HEADS, 1, M_V_DIM), c0_aug, m0)


def _t5_bucket(rel):
    n = jnp.maximum(rel, 0)
    nf = jnp.maximum(n, 1).astype(F32)
    large = MAX_EXACT + (jnp.log(nf / MAX_EXACT) / math.log(MAX_DISTANCE / MAX_EXACT)
                         * (NUM_BUCKETS - MAX_EXACT)).astype(jnp.int32)
    large = jnp.minimum(large, NUM_BUCKETS - 1)
    return jnp.where(n < MAX_EXACT, n, large)


def _bias_kernel(tab_ref, out_ref):
    h = pl.program_id(0)
    i = lax.broadcasted_iota(jnp.int32, (MOBA_BLOCK, MOBA_BLOCK), 1)
    j = lax.broadcasted_iota(jnp.int32, (MOBA_BLOCK, MOBA_BLOCK), 0)
    for kind, rel in ((0, i - j), (1, MOBA_BLOCK + i - j)):
        bucket = _t5_bucket(rel)
        acc = jnp.zeros((MOBA_BLOCK, MOBA_BLOCK), F32)
        for b in range(NUM_BUCKETS):
            acc = jnp.where(bucket == b, tab_ref[h, b], acc)
        out_ref[0, kind] = acc
    out_ref[0, 2] = jnp.full((MOBA_BLOCK, MOBA_BLOCK), tab_ref[h, NUM_BUCKETS - 1], F32)


def _bias_tiles(rel_table):
    assert MOBA_BLOCK + 1 >= MAX_DISTANCE
    tab = rel_table.T.astype(F32)
    return pl.pallas_call(
        _bias_kernel,
        grid=(A_HEADS,),
        in_specs=[pl.BlockSpec(memory_space=pltpu.SMEM)],
        out_specs=pl.BlockSpec((1, 3, MOBA_BLOCK, MOBA_BLOCK), lambda h: (h, 0, 0, 0)),
        out_shape=jax.ShapeDtypeStruct((A_HEADS, 3, MOBA_BLOCK, MOBA_BLOCK), F32),
        compiler_params=_cparams("parallel"),
        name="t5_bias_tiles",
    )(tab)


def _moba_prompt_kernel(q_ref, k_ref, v_ref, bias_ref, wu_ref, wd_ref, o_ref, wub_ref, wdb_ref,
                        kb_scr, qt_scr, vt_scr, gt_scr, s_scr, *, n_blocks):
    nb = n_blocks
    blk = MOBA_BLOCK
    nbp = gt_scr.shape[0]
    wub_ref[...] = wu_ref[...].astype(BF16)
    wdb_ref[...] = wd_ref[...].astype(BF16)

    kb_scr[...] = k_ref[...].astype(BF16)
    for c in range(nb):
        cols = slice(c * blk, (c + 1) * blk)
        qt_scr[:, cols] = (q_ref[cols, :] * (A_HEAD_DIM ** -0.5)).T.astype(BF16)
        vt_scr[:, cols] = v_ref[cols, :].T.astype(BF16)
    need_gate = nb - 1 > MOBA_TOP_K
    if need_gate:
        means = [jnp.mean(k_ref[n * blk:(n + 1) * blk, :], axis=0, keepdims=True) for n in range(nb)]
        kmean = jnp.concatenate(means + [jnp.zeros((128 - nb, A_HEAD_DIM), F32)], axis=0)
        for c in range(MOBA_TOP_K + 1, nb):
            cols = slice(c * blk, (c + 1) * blk)
            gate = _dot_nt(q_ref[cols, :] * (A_HEAD_DIM ** -0.5), kmean, precision=HIGHEST)
            gt_scr[:, cols] = gate.T[:nbp, :]
    key = lax.broadcasted_iota(jnp.int32, (blk, blk), 0)
    qry = lax.broadcasted_iota(jnp.int32, (blk, blk), 1)
    causal = key <= qry
    blk_id = lax.broadcasted_iota(jnp.int32, (nbp, blk), 0)

    for i in range(nb):
        cols = slice(i * blk, (i + 1) * blk)
        selneg = None
        if i > MOBA_TOP_K:
            g = gt_scr[:, cols]
            rank = jnp.zeros((nbp, blk), F32)
            for j in range(i):
                gj = g[j:j + 1, :]
                rank = rank + ((gj > g) | ((gj == g) & (j < blk_id))).astype(F32)
            selneg = jnp.where(rank < MOBA_TOP_K, 0.0, NEG_INF)
        s_scr[0:(i + 1) * blk, :] = _dot(kb_scr[0:(i + 1) * blk, :], qt_scr[:, cols])
        m = jnp.full((1, blk), NEG_INF, F32)
        for n in range(i + 1):
            rows = slice(n * blk, (n + 1) * blk)
            kind = 0 if n == i else (1 if n == i - 1 else 2)
            st = s_scr[rows, :] + bias_ref[0, kind]
            if n == i:
                st = jnp.where(causal, st, NEG_INF)
            elif selneg is not None:
                st = st + selneg[n:n + 1, :]
            s_scr[rows, :] = st
            m = jnp.maximum(m, jnp.max(st, axis=0, keepdims=True))
        l = jnp.zeros((1, blk), F32)
        acc = jnp.zeros((A_HEAD_DIM, blk), F32)
        for n in range(i + 1):
            rows = slice(n * blk, (n + 1) * blk)
            p = jnp.exp(s_scr[rows, :] - m)
            l = l + jnp.sum(p, axis=0, keepdims=True)
            acc = acc + _dot(vt_scr[:, rows], p.astype(BF16))
        o_ref[cols, :] = (acc / l).T.astype(o_ref.dtype)


def _moba_prompt(aq, ak, av, bias_tiles, w_up, w_down, batch, seq):
    assert seq % MOBA_BLOCK == 0
    nb = seq // MOBA_BLOCK
    assert nb <= 128
    nbp = -(-nb // 8) * 8
    steps = batch * A_HEADS
    slab = D_FF // steps
    assert D_FF % steps == 0 and slab % 128 == 0
    head = pl.BlockSpec((seq, A_HEAD_DIM), lambda b, h: (b, h))
    up_slab = pl.BlockSpec((D_MODEL, slab), lambda b, h: (0, b * A_HEADS + h))
    down_slab = pl.BlockSpec((slab, D_MODEL), lambda b, h: (b * A_HEADS + h, 0))
    kern = functools.partial(_moba_prompt_kernel, n_blocks=nb)
    return pl.pallas_call(
        kern,
        grid=(batch, A_HEADS),
        in_specs=[head, head, head,
                  pl.BlockSpec((1, 3, MOBA_BLOCK, MOBA_BLOCK), lambda b, h: (h, 0, 0, 0)),
                  up_slab, down_slab],
        out_specs=[head, up_slab, down_slab],
        out_shape=[jax.ShapeDtypeStruct((batch * seq, A_HEADS * A_HEAD_DIM), BF16),
                   jax.ShapeDtypeStruct((D_MODEL, D_FF), BF16),
                   jax.ShapeDtypeStruct((D_FF, D_MODEL), BF16)],
        scratch_shapes=[pltpu.VMEM((seq, A_HEAD_DIM), BF16),
                        pltpu.VMEM((A_HEAD_DIM, seq), BF16),
                        pltpu.VMEM((A_HEAD_DIM, seq), BF16),
                        pltpu.VMEM((nbp, seq), F32),
                        pltpu.VMEM((seq, MOBA_BLOCK), F32)],
        compiler_params=_cparams("parallel", "parallel"),
        name="moba_prompt",
    )(aq, ak, av, bias_tiles, w_up, w_down)


PAGES_PER_BLOCK = MOBA_BLOCK // PAGE_SIZE
PAST_BLOCKS_PER_STEP = 4


def _moba_past_kernel(pt_ref, q_ref, *refs, n_pages_step):
    del pt_ref
    k_refs = refs[:n_pages_step]
    v_refs = refs[n_pages_step:2 * n_pages_step]
    bias_ref, o_ref, m_ref, l_ref, km_ref = refs[2 * n_pages_step:]
    rows = PAGE_SIZE * A_HEADS
    qb = (q_ref[0] * (A_HEAD_DIM ** -0.5)).astype(BF16)
    hq = qb.shape[0]
    nq = hq // A_HEADS
    r = lax.broadcasted_iota(jnp.int32, (hq, PAGES_PER_BLOCK * rows), 0)
    c = lax.broadcasted_iota(jnp.int32, (hq, PAGES_PER_BLOCK * rows), 1)
    same_head = (c % A_HEADS) == (r // nq)
    n_blocks = n_pages_step // PAGES_PER_BLOCK
    is_last_step = pl.program_id(1) == pl.num_programs(1) - 1
    for g in range(n_blocks):
        pages = range(g * PAGES_PER_BLOCK, (g + 1) * PAGES_PER_BLOCK)
        k = jnp.concatenate([k_refs[p][0, 0].reshape(rows, A_HEAD_DIM) for p in pages], axis=0)
        v = jnp.concatenate([v_refs[p][0, 0].reshape(rows, A_HEAD_DIM) for p in pages], axis=0)
        ksum = sum(jnp.sum(k_refs[p][0, 0], axis=0) for p in pages)
        km_ref[0, g] = ksum / MOBA_BLOCK
        bias = bias_ref[jnp.where(is_last_step, 1, 0)] if g == n_blocks - 1 else bias_ref[0]
        lt = _dot_nt(qb, k.astype(BF16))
        lm = jnp.where(same_head, lt + bias, NEG_INF)
        m = jnp.max(lm, axis=1, keepdims=True)
        p_ = jnp.exp(lm - m)
        l = jnp.sum(p_, axis=1, keepdims=True)
        o_ref[0, g] = _dot(p_.astype(BF16), v.astype(BF16))
        m_ref[0, g] = jnp.broadcast_to(m, (hq, A_HEAD_DIM))
        l_ref[0, g] = jnp.broadcast_to(l, (hq, A_HEAD_DIM))


def _moba_past(page_table, q_hq, cache_k, cache_v, past_bias, layer):
    bs, n_pages = page_table.shape
    assert n_pages % PAGES_PER_BLOCK == 0
    nbp = n_pages // PAGES_PER_BLOCK
    gb = math.gcd(nbp, PAST_BLOCKS_PER_STEP)
    pps = gb * PAGES_PER_BLOCK
    hq = q_hq.shape[1]
    page = (1, 1, PAGE_SIZE, A_HEADS, A_HEAD_DIM)
    part = pl.BlockSpec((1, gb, hq, A_HEAD_DIM), lambda b, n, pt: (b, n, 0, 0))

    def page_spec(p):
        return pl.BlockSpec(page, lambda b, n, pt: (layer, pt[b, pps * n + p], 0, 0, 0))

    grid_spec = pltpu.PrefetchScalarGridSpec(
        num_scalar_prefetch=1,
        grid=(bs, nbp // gb),
        in_specs=[pl.BlockSpec((1, hq, A_HEAD_DIM), lambda b, n, pt: (b, 0, 0))]
        + [page_spec(p) for p in range(pps)] * 2
        + [pl.BlockSpec((2, hq, MOBA_BLOCK * A_HEADS), lambda b, n, pt: (0, 0, 0))],
        out_specs=[part, part, part,
                   pl.BlockSpec((1, gb, A_HEADS, A_HEAD_DIM), lambda b, n, pt: (b, n, 0, 0))],
    )
    pshape = jax.ShapeDtypeStruct((bs, nbp, hq, A_HEAD_DIM), F32)
    return pl.pallas_call(
        functools.partial(_moba_past_kernel, n_pages_step=pps),
        grid_spec=grid_spec,
        out_shape=[pshape, pshape, pshape, jax.ShapeDtypeStruct((bs, nbp, A_HEADS, A_HEAD_DIM), F32)],
        compiler_params=_cparams("parallel", "parallel"),
        name="moba_past_blocks",
    )(page_table, q_hq, *([cache_k] * pps), *([cache_v] * pps), past_bias)


def _moba_combine_kernel(q_ref, km_ref, kn_ref, vn_ref, ob_ref, op_ref, mp_ref, lp_ref, out_ref):
    nbp = km_ref.shape[2]
    q = q_ref[0] * (A_HEAD_DIM ** -0.5)
    hq = q.shape[0]
    nq = hq // A_HEADS
    gate = jnp.concatenate(
        [_dot_nt(q[h * nq:(h + 1) * nq], km_ref[0, h], precision=HIGHEST) for h in range(A_HEADS)], axis=0)
    lane = lax.broadcasted_iota(jnp.int32, (hq, nbp), 1)
    rank = jnp.zeros((hq, nbp), F32)
    for j in range(nbp):
        gj = gate[:, j:j + 1]
        beats = (gj > gate) | ((gj == gate) & (j < lane))
        rank = rank + beats.astype(F32)
    selneg = jnp.where(rank < MOBA_TOP_K, 0.0, NEG_INF)
    qb = q.astype(BF16)
    lt = _dot_nt(qb, kn_ref[0].astype(BF16))
    r = lax.broadcasted_iota(jnp.int32, lt.shape, 0)
    c = lax.broadcasted_iota(jnp.int32, lt.shape, 1)
    ok = ((c % A_HEADS) == (r // nq)) & ((c // A_HEADS) <= (r % nq))
    lo = jnp.where(ok, lt + ob_ref[...], NEG_INF)
    m = jnp.max(lo, axis=1, keepdims=True)
    for n in range(nbp):
        m = jnp.maximum(m, mp_ref[0, n][:, 0:1] + selneg[:, n:n + 1])
    p = jnp.exp(lo - m)
    l = jnp.sum(p, axis=1, keepdims=True)
    acc = _dot(p.astype(BF16), vn_ref[0].astype(BF16))
    for n in range(nbp):
        w = jnp.exp(mp_ref[0, n][:, 0:1] + selneg[:, n:n + 1] - m)
        l = l + w * lp_ref[0, n][:, 0:1]
        acc = acc + w * op_ref[0, n]
    out_ref[0] = acc / l


def _moba_combine(q_hq, kmean_hn, k_new, v_new, own_bias, o_part, m_part, l_part):
    bs, hq, _ = q_hq.shape
    nbp = o_part.shape[1]
    part = pl.BlockSpec((1, nbp, hq, A_HEAD_DIM), lambda b: (b, 0, 0, 0))
    tok = pl.BlockSpec((1, hq, A_HEAD_DIM), lambda b: (b, 0, 0))
    return pl.pallas_call(
        _moba_combine_kernel,
        grid=(bs,),
        in_specs=[tok,
                  pl.BlockSpec((1, A_HEADS, nbp, A_HEAD_DIM), lambda b: (b, 0, 0, 0)),
                  tok, tok,
                  pl.BlockSpec((hq, hq), lambda b: (0, 0)),
                  part, part, part],
        out_specs=tok,
        out_shape=jax.ShapeDtypeStruct((bs, hq, A_HEAD_DIM), F32),
        compiler_params=_cparams("parallel"),
        name="moba_combine",
    )(q_hq, kmean_hn, k_new, v_new, own_bias, o_part, m_part, l_part)


def _out_proj_kernel(h_ref, om_ref, oa_ref, wm_ref, wa_ref, o_ref, wm_scr, wa_scr):
    @pl.when(pl.program_id(1) == 0)
    def _():
        for r in range(0, wm_scr.shape[0], _CAST_ROWS):
            wm_scr[r:r + _CAST_ROWS, :] = wm_ref[r:r + _CAST_ROWS, :].astype(BF16)
            wa_scr[r:r + _CAST_ROWS, :] = wa_ref[r:r + _CAST_ROWS, :].astype(BF16)

    o_ref[...] = (h_ref[...] + _dot(om_ref[...].astype(BF16), wm_scr[...])
                  + _dot(oa_ref[...].astype(BF16), wa_scr[...]))


def _out_proj(h, out_m, out_a, w_out):
    m = h.shape[0]
    tm = min(512, m)
    tn = 1024
    half = M_HEADS * M_V_DIM
    return pl.pallas_call(
        _out_proj_kernel,
        grid=(D_MODEL // tn, m // tm),
        in_specs=[
            pl.BlockSpec((tm, tn), lambda j, i: (i, j)),
            pl.BlockSpec((tm, half), lambda j, i: (i, 0)),
            pl.BlockSpec((tm, half), lambda j, i: (i, 0)),
            pl.BlockSpec((half, tn), lambda j, i: (0, j)),
            pl.BlockSpec((half, tn), lambda j, i: (1, j)),
        ],
        out_specs=pl.BlockSpec((tm, tn), lambda j, i: (i, j)),
        out_shape=jax.ShapeDtypeStruct((m, D_MODEL), F32),
        scratch_shapes=[pltpu.VMEM((half, tn), BF16), pltpu.VMEM((half, tn), BF16)],
        compiler_params=_cparams("arbitrary", "arbitrary"),
        name="out_proj",
    )(h, out_m, out_a, w_out, w_out)


def _mlp_kernel(h_ref, g_ref, wu_ref, wd_ref, o_ref, xn_ref):
    f = pl.program_id(1)

    @pl.when(f == 0)
    def _():
        xn_ref[...] = _rms(h_ref[...], g_ref[...]).astype(BF16)
        o_ref[...] = h_ref[...]

    u = _dot(xn_ref[...], wu_ref[...])
    a = jnp.square(jnp.maximum(u, 0.0)).astype(BF16)
    o_ref[...] += _dot(a, wd_ref[...])


def _mlp(h, g, w_up, w_down):
    m = h.shape[0]
    tm = min(512, m)
    tf = 1024
    return pl.pallas_call(
        _mlp_kernel,
        grid=(m // tm, D_FF // tf),
        in_specs=[
            pl.BlockSpec((tm, D_MODEL), lambda i, f: (i, 0)),
            pl.BlockSpec((1, D_MODEL), lambda i, f: (0, 0)),
            pl.BlockSpec((D_MODEL, tf), lambda i, f: (0, f)),
            pl.BlockSpec((tf, D_MODEL), lambda i, f: (f, 0)),
        ],
        out_specs=pl.BlockSpec((tm, D_MODEL), lambda i, f: (i, 0)),
        out_shape=jax.ShapeDtypeStruct((m, D_MODEL), F32),
        scratch_shapes=[pltpu.VMEM((tm, D_MODEL), BF16)],
        compiler_params=_cparams("parallel", "arbitrary"),
        name="mlp",
    )(h, g, w_up, w_down)


def _ple_kernel(h_ref, pe_ref, g_ref, wg_ref, wp_ref, gf_ref, y_ref):
    h = h_ref[...]
    gate = jax.nn.sigmoid(_dot(_rms(h, g_ref[...]).astype(BF16), wg_ref[...]))
    proj = _dot(pe_ref[...].astype(BF16), wp_ref[...])
    y_ref[...] = _rms(h + gate * proj, gf_ref[...])


def _ple_final(h, pe, g_ple, w_gate, w_proj, g_final):
    m = h.shape[0]
    tm = min(256, m)
    vec = pl.BlockSpec((1, D_MODEL), lambda i: (0, 0))
    return pl.pallas_call(
        _ple_kernel,
        grid=(m // tm,),
        in_specs=[
            pl.BlockSpec((tm, D_MODEL), lambda i: (i, 0)),
            pl.BlockSpec((tm, PLE_DIM), lambda i: (i, 0)),
            vec,
            pl.BlockSpec((D_MODEL, D_MODEL), lambda i: (0, 0)),
            pl.BlockSpec((PLE_DIM, D_MODEL), lambda i: (0, 0)),
            vec,
        ],
        out_specs=pl.BlockSpec((tm, D_MODEL), lambda i: (i, 0)),
        out_shape=jax.ShapeDtypeStruct((m, D_MODEL), F32),
        compiler_params=_cparams("parallel"),
        name="ple_final_norm",
    )(h, pe, g_ple, w_gate, w_proj, g_final)


def _layer_weights(w_in, b_igate, b_fgate, g_mix, g_mhead, w_out, g_ffn, w_up, w_down,
                   g_ple, w_ple_gate, w_ple_proj):
    gate_bias = jnp.pad(jnp.concatenate([b_igate, b_fgate]), (0, GATE_PAD - 2 * M_HEADS)).reshape(1, GATE_PAD)
    return dict(
        w_in=w_in.T, gate_bias=gate_bias.astype(F32),
        g_mix=g_mix.reshape(1, D_MODEL), g_mhead=g_mhead, w_out=w_out,
        g_ffn=g_ffn.reshape(1, D_MODEL), w_up=w_up, w_down=w_down,
        g_ple=g_ple.reshape(1, D_MODEL), w_ple_gate=w_ple_gate.astype(BF16),
        w_ple_proj=w_ple_proj.astype(BF16))


def _tail(h, out_m, out_a, pe, w, g_final):
    h = _out_proj(h, out_m, out_a, w["w_out"])
    h = _mlp(h, w["g_ffn"], w["w_up_bf16"], w["w_down_bf16"])
    return _ple_final(h, pe, w["g_ple"], w["w_ple_gate"], w["w_ple_proj"], g_final)


def _split_state(c_aug, m_fin):
    return c_aug[..., :M_V_DIM], c_aug[..., M_V_DIM], m_fin[..., 0, 0]


def kernel(x_prompt, x_sample, cache_k, cache_v, state_C, state_n, state_m, page_table, p_prompt, p_sample, rel_bias_table, w_in, b_igate, b_fgate, g_mix, g_mhead, w_out, g_ffn, w_up, w_down, g_ple, w_ple_gate, w_ple_proj, g_final):
    depth = w_in.shape[0]
    assert depth == 1, "one decoder layer per call"
    layer = 0
    bp, tp, _ = x_prompt.shape
    bs, ts, _ = x_sample.shape
    past_len = page_table.shape[1] * PAGE_SIZE
    assert past_len % MOBA_BLOCK == 0 and ts <= MOBA_BLOCK
    kv_shape = (A_HEADS, A_HEAD_DIM)
    g_fin = g_final.reshape(1, D_MODEL)

    w = _layer_weights(w_in[layer], b_igate[layer], b_fgate[layer], g_mix[layer], g_mhead[layer], w_out[layer],
                       g_ffn[layer], w_up[layer], w_down[layer], g_ple[layer], w_ple_gate[layer],
                       w_ple_proj[layer])
    bias_tiles = _bias_tiles(rel_bias_table)

    hp = x_prompt.reshape(bp * tp, D_MODEL)
    xn_p, gp = _norm_gate(hp, w["g_mix"], w["w_in"])
    zp, aq_p, ak_p, av_p = _in_proj(xn_p, w["w_in"])
    c0 = jnp.zeros((bp, M_HEADS, M_QK_DIM, 2 * M_V_DIM), F32)
    m0 = jnp.zeros((bp, M_HEADS, 1, GATE_PAD), F32)
    om_p, c_p, m_p = _mlstm(zp, gp, w["gate_bias"], w["g_mhead"], c0, m0, bp, tp)
    oa_p, w["w_up_bf16"], w["w_down_bf16"] = _moba_prompt(aq_p, ak_p, av_p, bias_tiles, w["w_up"], w["w_down"],
                                                          bp, tp)
    y_p = _tail(hp, om_p, oa_p, p_prompt[layer].reshape(bp * tp, PLE_DIM), w, g_fin)
    k_p = ak_p.reshape((1, bp, tp) + kv_shape)
    v_p = av_p.reshape((1, bp, tp) + kv_shape)
    cp, np_, mp = _split_state(c_p, m_p)

    hs = x_sample.reshape(bs * ts, D_MODEL)
    xn_s, gs = _norm_gate(hs, w["g_mix"], w["w_in"])
    zs, aq_s, ak_s, av_s = _in_proj(xn_s, w["w_in"])
    c0s = jnp.concatenate([state_C[layer], state_n[layer][..., None],
                           jnp.zeros((bs, M_HEADS, M_QK_DIM, M_V_DIM - 1), F32)], axis=-1)
    m0s = jnp.broadcast_to(state_m[layer][..., None, None], (bs, M_HEADS, 1, GATE_PAD))
    om_s, c_s, m_s = _mlstm(zs, gs, w["gate_bias"], w["g_mhead"], c0s, m0s, bs, ts)

    hq = A_HEADS * ts
    q_s = aq_s.reshape(bs, ts, A_HEADS, A_HEAD_DIM)
    q_hq = q_s.transpose(0, 2, 1, 3).reshape(bs, hq, A_HEAD_DIM)
    k_new = ak_s.reshape(bs, hq, A_HEAD_DIM)
    v_new = av_s.reshape(bs, hq, A_HEAD_DIM)
    rows = bias_tiles[:, :, :, :ts].transpose(0, 1, 3, 2)
    expand = lambda a: jnp.repeat(a, A_HEADS, axis=-1).reshape(hq, -1)
    past_bias = jnp.stack([expand(rows[:, 2]), expand(rows[:, 1])])
    own_bias = expand(rows[:, 0, :, :ts])
    o_part, m_part, l_part, kmean = _moba_past(page_table, q_hq, cache_k, cache_v, past_bias, layer)
    oa_hq = _moba_combine(q_hq, kmean.transpose(0, 2, 1, 3), k_new, v_new, own_bias, o_part, m_part, l_part)
    oa_s = oa_hq.reshape(bs, A_HEADS, ts, A_HEAD_DIM).transpose(0, 2, 1, 3).reshape(bs * ts, A_HEADS * A_HEAD_DIM)
    y_s = _tail(hs, om_s, oa_s, p_sample[layer].reshape(bs * ts, PLE_DIM), w, g_fin)
    k_s = k_new.reshape((1, bs, ts) + kv_shape)
    v_s = v_new.reshape((1, bs, ts) + kv_shape)
    cs, ns, ms = _split_state(c_s, m_s)

    return (y_p.reshape(bp, tp, D_MODEL), y_s.reshape(bs, ts, D_MODEL),
            k_p, v_p, cp[None], np_[None], mp[None],
            k_s, v_s, cs[None], ns[None], ms[None])
```

```python
import functools
import math

import jax
import jax.numpy as jnp
from jax import lax
from jax.experimental import pallas as pl
from jax.experimental.pallas import tpu as pltpu

F32 = jnp.float32
BF16 = jnp.bfloat16
HIGHEST = lax.Precision.HIGHEST

D_MODEL = 2048
M_HEADS = 8
M_V_DIM = 128
M_QK_DIM = 64
M_CHUNK = 64
A_HEADS = 8
A_HEAD_DIM = 128
MOBA_BLOCK = 256
MOBA_TOP_K = 3
NUM_BUCKETS = 32
MAX_EXACT = NUM_BUCKETS // 2
MAX_DISTANCE = 128
D_FF = 4 * D_MODEL
PLE_DIM = 256
PAGE_SIZE = 128
EPS = 1e-6

_MQ, _MK, _MV, _MO = 0, 512, 1024, 2048
_GATES = 3072
_AQ = 3088
D_MAIN = 6144
GATE_PAD = 128

VMEM_LIMIT = 56 * 1024 * 1024
NEG_INF = float("-inf")


def _cparams(*sem):
    return pltpu.CompilerParams(dimension_semantics=sem, vmem_limit_bytes=VMEM_LIMIT)


def _rms(x, g):
    return x * lax.rsqrt(jnp.mean(x * x, axis=-1, keepdims=True) + EPS) * g


def _dot_nt(a, b, **kw):
    return lax.dot_general(a, b, (((1,), (1,)), ((), ())), preferred_element_type=F32, **kw)


def _dot_tn(a, b, **kw):
    return lax.dot_general(a, b, (((0,), (0,)), ((), ())), preferred_element_type=F32, **kw)


def _dot(a, b, **kw):
    return jnp.dot(a, b, preferred_element_type=F32, **kw)


_IN_TN = 1024
_N_IN_TILES = D_MAIN // _IN_TN
_N_MLSTM_TILES = _GATES // _IN_TN
_GATE_COLS = _AQ - _GATES
_CAST_ROWS = 256


def _norm_gate_kernel(x_ref, g_ref, wg_ref, xn_ref, gate_ref):
    xn = _rms(x_ref[...], g_ref[...]).astype(BF16)
    xn_ref[...] = xn
    wg = wg_ref[...]
    hi = wg.astype(BF16).astype(F32)
    row = lax.broadcasted_iota(jnp.int32, wg.shape, 0)
    w2 = jnp.where(row < _GATE_COLS, hi,
                   jnp.where(row < 2 * _GATE_COLS, pltpu.roll(wg - hi, _GATE_COLS, axis=0), 0.0))
    r = _dot_nt(xn, w2.astype(BF16))
    gate_ref[...] = r + pltpu.roll(r, GATE_PAD - _GATE_COLS, axis=1)


def _norm_gate(x, g, w_in_t):
    m = x.shape[0]
    tm = min(512, m)
    return pl.pallas_call(
        _norm_gate_kernel,
        grid=(m // tm,),
        in_specs=[
            pl.BlockSpec((tm, D_MODEL), lambda i: (i, 0)),
            pl.BlockSpec((1, D_MODEL), lambda i: (0, 0)),
            pl.BlockSpec((GATE_PAD, D_MODEL), lambda i: (_GATES // GATE_PAD, 0)),
        ],
        out_specs=[pl.BlockSpec((tm, D_MODEL), lambda i: (i, 0)),
                   pl.BlockSpec((tm, GATE_PAD), lambda i: (i, 0))],
        out_shape=[jax.ShapeDtypeStruct((m, D_MODEL), BF16), jax.ShapeDtypeStruct((m, GATE_PAD), F32)],
        compiler_params=_cparams("parallel"),
        name="norm_gate",
    )(x, g, w_in_t)


def _in_proj_kernel(xn_ref, wa_ref, wb_ref, zm_ref, aq_ref, ak_ref, av_ref, w_scr):
    j = pl.program_id(0)
    i = pl.program_id(1)

    @pl.when((i == 0) & (j < _N_MLSTM_TILES))
    def _():
        for r in range(0, _IN_TN, _CAST_ROWS):
            w_scr[r:r + _CAST_ROWS, :] = wa_ref[r:r + _CAST_ROWS, :].astype(BF16)

    @pl.when((i == 0) & (j >= _N_MLSTM_TILES))
    def _():
        body = _IN_TN - _GATE_COLS
        for r in range(0, body, _CAST_ROWS):
            n = min(_CAST_ROWS, body - r)
            w_scr[r:r + n, :] = wa_ref[_GATE_COLS + r:_GATE_COLS + r + n, :].astype(BF16)
        w_scr[body:, :] = wb_ref[:_GATE_COLS, :].astype(BF16)

    @pl.when(j < _N_MLSTM_TILES)
    def _():
        zm_ref[...] = _dot_nt(xn_ref[...], w_scr[...])

    for t, ref in enumerate((aq_ref, ak_ref, av_ref)):
        @pl.when(j == _N_MLSTM_TILES + t)
        def _(ref=ref):
            ref[...] = _dot_nt(xn_ref[...], w_scr[...])


def _in_proj(xn, w_in_t):
    m = xn.shape[0]
    tm = min(512, m)
    tn = _IN_TN
    last = m // tm - 1

    def held(j, i, first_tile, last_tile):
        return jnp.where(j < first_tile, 0, jnp.where(j <= last_tile, i, last))

    def section(t):
        return pl.BlockSpec((tm, tn), lambda j, i: (held(j, i, t, t), 0))

    sec_shape = jax.ShapeDtypeStruct((m, tn), F32)
    nm = _N_MLSTM_TILES
    return pl.pallas_call(
        _in_proj_kernel,
        grid=(_N_IN_TILES, m // tm),
        in_specs=[
            pl.BlockSpec((tm, D_MODEL), lambda j, i: (i, 0)),
            pl.BlockSpec((tn, D_MODEL), lambda j, i: (j, 0)),
            pl.BlockSpec((GATE_PAD, D_MODEL), lambda j, i: ((j + 1) * (tn // GATE_PAD), 0)),
        ],
        out_specs=[
            pl.BlockSpec((tm, tn), lambda j, i: (held(j, i, 0, nm - 1), jnp.minimum(j, nm - 1))),
            section(nm), section(nm + 1), section(nm + 2),
        ],
        out_shape=[jax.ShapeDtypeStruct((m, _GATES), F32), sec_shape, sec_shape, sec_shape],
        scratch_shapes=[pltpu.VMEM((tn, D_MODEL), BF16)],
        compiler_params=_cparams("arbitrary", "arbitrary"),
        name="in_proj",
    )(xn, w_in_t, w_in_t)


def _log_sigmoid(x):
    return -(jnp.maximum(-x, 0.0) + jnp.log1p(jnp.exp(-jnp.abs(x))))


def _bdot(a, b):
    return lax.dot_general(a, b, (((2,), (1,)), ((0,), (0,))), preferred_element_type=F32)


def _bdot_nt(a, b):
    return lax.dot_general(a, b, (((2,), (2,)), ((0,), (0,))), preferred_element_type=F32)


def _bdot_tn(a, b):
    return lax.dot_general(a, b, (((1,), (1,)), ((0,), (0,))), preferred_element_type=F32)


def _mlstm_kernel(q_ref, k_ref, v_ref, o_ref, g_ref, gb_ref, gh_ref, c0_ref, m0_ref,
                  out_ref, cfin_ref, mfin_ref, c_scr, m_scr, *, chunk, n_chunks):
    L = chunk
    H = M_HEADS
    tb = L * n_chunks
    t = pl.program_id(1)

    @pl.when(t == 0)
    def _():
        c_scr[...] = c0_ref[0]
        m_scr[...] = m0_ref[0]

    g = g_ref[...] + gb_ref[...]
    lane = lax.broadcasted_iota(jnp.int32, (tb, GATE_PAD), 1)
    gl = jnp.where((lane >= H) & (lane < 2 * H), _log_sigmoid(g), g)
    r = lax.broadcasted_iota(jnp.int32, (tb, tb), 0)
    c = lax.broadcasted_iota(jnp.int32, (tb, tb), 1)
    tri = ((r // L == c // L) & (c <= r)).astype(F32)
    cum = _dot(tri, gl, precision=HIGHEST)

    def rows(ci):
        return slice(ci * L, (ci + 1) * L)

    def stack(f):
        return jnp.stack([f(ci, h) for ci in range(n_chunks) for h in range(H)])

    ones_col = (lax.broadcasted_iota(jnp.int32, (L, M_V_DIM), 1) == 0).astype(BF16)
    li = stack(lambda ci, h: gl[rows(ci), h:h + 1])
    b = stack(lambda ci, h: cum[rows(ci), H + h:H + h + 1])
    qb = stack(lambda ci, h: q_ref[rows(ci), h * M_QK_DIM:(h + 1) * M_QK_DIM]).astype(BF16)
    kc = stack(lambda ci, h: k_ref[rows(ci), h * M_QK_DIM:(h + 1) * M_QK_DIM]) * (M_QK_DIM ** -0.5)
    vaug = stack(lambda ci, h: jnp.concatenate(
        [v_ref[rows(ci), h * M_V_DIM:(h + 1) * M_V_DIM].astype(BF16), ones_col], axis=1))

    rl = lax.broadcasted_iota(jnp.int32, (L, L), 0)
    cl = lax.broadcasted_iota(jnp.int32, (L, L), 1)
    w_row = jnp.sum(jnp.where(rl == cl, li - b, 0.0), axis=1, keepdims=True)
    d = jnp.where(cl <= rl, b + w_row, NEG_INF)
    m_loc = jnp.max(d, axis=2, keepdims=True)
    s = _bdot_nt(qb, kc.astype(BF16)) * jnp.exp(d - m_loc)
    nd_loc = _bdot(s.astype(BF16), vaug)
    g_last = m_loc[:, L - 1:L, :]
    b_last = b[:, L - 1:L, :]
    ws = jnp.exp(b_last - b + li - g_last)
    dc_loc = _bdot_tn((kc * ws).astype(BF16), vaug)

    m_prev = m_scr[:, :, 0:1]
    c_aug = c_scr[...]
    for ci in range(n_chunks):
        grp = slice(ci * H, (ci + 1) * H)
        a = b[grp] + m_prev
        m_t = jnp.maximum(a, m_loc[grp])
        nd = (jnp.exp(a - m_t) * _bdot(qb[grp], c_aug.astype(BF16))
              + jnp.exp(m_loc[grp] - m_t) * nd_loc[grp])
        hh = nd[:, :, :M_V_DIM] / jnp.maximum(jnp.abs(nd[:, :, M_V_DIM:M_V_DIM + 1]), jnp.exp(-m_t))
        hn = _rms(hh, gh_ref[...])
        for h in range(H):
            og = jax.nn.sigmoid(o_ref[rows(ci), h * M_V_DIM:(h + 1) * M_V_DIM])
            out_ref[rows(ci), h * M_V_DIM:(h + 1) * M_V_DIM] = (og * hn[h]).astype(out_ref.dtype)
        m_new = jnp.maximum(b_last[grp] + m_prev, g_last[grp])
        c_aug = (jnp.exp(b_last[grp] + m_prev - m_new) * c_aug
                 + jnp.exp(g_last[grp] - m_new) * dc_loc[grp])
        m_prev = m_new
    c_scr[...] = c_aug
    m_scr[...] = jnp.broadcast_to(m_prev, m_scr.shape)

    @pl.when(t == pl.num_programs(1) - 1)
    def _():
        cfin_ref[0] = c_scr[...]
        mfin_ref[0] = m_scr[...]


def _mlstm(z, gates, gate_bias, g_mhead, c0_aug, m0, batch, seq):
    L = math.gcd(seq, M_CHUNK)
    n_chunks = max(1, min(seq // L, 256 // L))
    tb = L * n_chunks
    nt = seq // tb
    row = lambda b, t: b * nt + t
    kern = functools.partial(_mlstm_kernel, chunk=L, n_chunks=n_chunks)
    return pl.pallas_call(
        kern,
        grid=(batch, nt),
        in_specs=[
            pl.BlockSpec((tb, 512), lambda b, t: (row(b, t), _MQ // 512)),
            pl.BlockSpec((tb, 512), lambda b, t: (row(b, t), _MK // 512)),
            pl.BlockSpec((tb, 1024), lambda b, t: (row(b, t), _MV // 1024)),
            pl.BlockSpec((tb, 1024), lambda b, t: (row(b, t), _MO // 1024)),
            pl.BlockSpec((tb, GATE_PAD), lambda b, t: (row(b, t), 0)),
            pl.BlockSpec((1, GATE_PAD), lambda b, t: (0, 0)),
            pl.BlockSpec((M_HEADS, 1, M_V_DIM), lambda b, t: (0, 0, 0)),
            pl.BlockSpec((1, M_HEADS, M_QK_DIM, 2 * M_V_DIM), lambda b, t: (b, 0, 0, 0)),
            pl.BlockSpec((1, M_HEADS, 1, GATE_PAD), lambda b, t: (b, 0, 0, 0)),
        ],
        out_specs=[
            pl.BlockSpec((tb, M_HEADS * M_V_DIM), lambda b, t: (row(b, t), 0)),
            pl.BlockSpec((1, M_HEADS, M_QK_DIM, 2 * M_V_DIM), lambda b, t: (b, 0, 0, 0)),
            pl.BlockSpec((1, M_HEADS, 1, GATE_PAD), lambda b, t: (b, 0, 0, 0)),
        ],
        out_shape=[
            jax.ShapeDtypeStruct((batch * seq, M_HEADS * M_V_DIM), BF16 if tb % 16 == 0 else F32),
            jax.ShapeDtypeStruct((batch, M_HEADS, M_QK_DIM, 2 * M_V_DIM), F32),
            jax.ShapeDtypeStruct((batch, M_HEADS, 1, GATE_PAD), F32),
        ],
        scratch_shapes=[pltpu.VMEM((M_HEADS, M_QK_DIM, 2 * M_V_DIM), F32),
                        pltpu.VMEM((M_HEADS, 1, GATE_PAD), F32)],
        compiler_params=_cparams("parallel", "arbitrary"),
        name="mlstm",
    )(z, z, z, z, gates, gate_bias, g_mhead.reshape(M_HEADS, 1, M_V_DIM), c0_aug, m0)


def _t5_bucket(rel):
    n = jnp.maximum(rel, 0)
    nf = jnp.maximum(n, 1).astype(F32)
    large = MAX_EXACT + (jnp.log(nf / MAX_EXACT) / math.log(MAX_DISTANCE / MAX_EXACT)
                         * (NUM_BUCKETS - MAX_EXACT)).astype(jnp.int32)
    large = jnp.minimum(large, NUM_BUCKETS - 1)
    return jnp.where(n < MAX_EXACT, n, large)


def _bias_kernel(tab_ref, out_ref):
    h = pl.program_id(0)
    i = lax.broadcasted_iota(jnp.int32, (MOBA_BLOCK, MOBA_BLOCK), 1)
    j = lax.broadcasted_iota(jnp.int32, (MOBA_BLOCK, MOBA_BLOCK), 0)
    for kind, rel in ((0, i - j), (1, MOBA_BLOCK + i - j)):
        bucket = _t5_bucket(rel)
        acc = jnp.zeros((MOBA_BLOCK, MOBA_BLOCK), F32)
        for b in range(NUM_BUCKETS):
            acc = jnp.where(bucket == b, tab_ref[h, b], acc)
        out_ref[0, kind] = acc
    out_ref[0, 2] = jnp.full((MOBA_BLOCK, MOBA_BLOCK), tab_ref[h, NUM_BUCKETS - 1], F32)


def _bias_tiles(rel_table):
    assert MOBA_BLOCK + 1 >= MAX_DISTANCE
    tab = rel_table.T.astype(F32)
    return pl.pallas_call(
        _bias_kernel,
        grid=(A_HEADS,),
        in_specs=[pl.BlockSpec(memory_space=pltpu.SMEM)],
        out_specs=pl.BlockSpec((1, 3, MOBA_BLOCK, MOBA_BLOCK), lambda h: (h, 0, 0, 0)),
        out_shape=jax.ShapeDtypeStruct((A_HEADS, 3, MOBA_BLOCK, MOBA_BLOCK), F32),
        compiler_params=_cparams("parallel"),
        name="t5_bias_tiles",
    )(tab)


def _moba_prompt_kernel(q_ref, k_ref, v_ref, bias_ref, wu_ref, wd_ref, o_ref, wub_ref, wdb_ref,
                        kb_scr, qt_scr, vt_scr, gt_scr, s_scr, *, n_blocks):
    nb = n_blocks
    blk = MOBA_BLOCK
    nbp = gt_scr.shape[0]
    wub_ref[...] = wu_ref[...].astype(BF16)
    wdb_ref[...] = wd_ref[...].astype(BF16)

    kb_scr[...] = k_ref[...].astype(BF16)
    for c in range(nb):
        cols = slice(c * blk, (c + 1) * blk)
        qt_scr[:, cols] = (q_ref[cols, :] * (A_HEAD_DIM ** -0.5)).T.astype(BF16)
        vt_scr[:, cols] = v_ref[cols, :].T.astype(BF16)
    need_gate = nb - 1 > MOBA_TOP_K
    if need_gate:
        means = [jnp.mean(k_ref[n * blk:(n + 1) * blk, :], axis=0, keepdims=True) for n in range(nb)]
        kmean = jnp.concatenate(means + [jnp.zeros((128 - nb, A_HEAD_DIM), F32)], axis=0)
        for c in range(MOBA_TOP_K + 1, nb):
            cols = slice(c * blk, (c + 1) * blk)
            gate = _dot_nt(q_ref[cols, :] * (A_HEAD_DIM ** -0.5), kmean, precision=HIGHEST)
            gt_scr[:, cols] = gate.T[:nbp, :]
    key = lax.broadcasted_iota(jnp.int32, (blk, blk), 0)
    qry = lax.broadcasted_iota(jnp.int32, (blk, blk), 1)
    causal = key <= qry
    blk_id = lax.broadcasted_iota(jnp.int32, (nbp, blk), 0)

    for i in range(nb):
        cols = slice(i * blk, (i + 1) * blk)
        selneg = None
        if i > MOBA_TOP_K:
            g = gt_scr[:, cols]
            rank = jnp.zeros((nbp, blk), F32)
            for j in range(i):
                gj = g[j:j + 1, :]
                rank = rank + ((gj > g) | ((gj == g) & (j < blk_id))).astype(F32)
            selneg = jnp.where(rank < MOBA_TOP_K, 0.0, NEG_INF)
        s_scr[0:(i + 1) * blk, :] = _dot(kb_scr[0:(i + 1) * blk, :], qt_scr[:, cols])
        m = jnp.full((1, blk), NEG_INF, F32)
        for n in range(i + 1):
            rows = slice(n * blk, (n + 1) * blk)
            kind = 0 if n == i else (1 if n == i - 1 else 2)
            st = s_scr[rows, :] + bias_ref[0, kind]
            if n == i:
                st = jnp.where(causal, st, NEG_INF)
            elif selneg is not None:
                st = st + selneg[n:n + 1, :]
            s_scr[rows, :] = st
            m = jnp.maximum(m, jnp.max(st, axis=0, keepdims=True))
        l = jnp.zeros((1, blk), F32)
        acc = jnp.zeros((A_HEAD_DIM, blk), F32)
        for n in range(i + 1):
            rows = slice(n * blk, (n + 1) * blk)
            p = jnp.exp(s_scr[rows, :] - m)
            l = l + jnp.sum(p, axis=0, keepdims=True)
            acc = acc + _dot(vt_scr[:, rows], p.astype(BF16))
        o_ref[cols, :] = (acc / l).T.astype(o_ref.dtype)


def _moba_prompt(aq, ak, av, bias_tiles, w_up, w_down, batch, seq):
    assert seq % MOBA_BLOCK == 0
    nb = seq // MOBA_BLOCK
    assert nb <= 128
    nbp = -(-nb // 8) * 8
    steps = batch * A_HEADS
    slab = D_FF // steps
    assert D_FF % steps == 0 and slab % 128 == 0
    head = pl.BlockSpec((seq, A_HEAD_DIM), lambda b, h: (b, h))
    up_slab = pl.BlockSpec((D_MODEL, slab), lambda b, h: (0, b * A_HEADS + h))
    down_slab = pl.BlockSpec((slab, D_MODEL), lambda b, h: (b * A_HEADS + h, 0))
    kern = functools.partial(_moba_prompt_kernel, n_blocks=nb)
    return pl.pallas_call(
        kern,
        grid=(batch, A_HEADS),
        in_specs=[head, head, head,
                  pl.BlockSpec((1, 3, MOBA_BLOCK, MOBA_BLOCK), lambda b, h: (h, 0, 0, 0)),
                  up_slab, down_slab],
        out_specs=[head, up_slab, down_slab],
        out_shape=[jax.ShapeDtypeStruct((batch * seq, A_HEADS * A_HEAD_DIM), BF16),
                   jax.ShapeDtypeStruct((D_MODEL, D_FF), BF16),
                   jax.ShapeDtypeStruct((D_FF, D_MODEL), BF16)],
        scratch_shapes=[pltpu.VMEM((seq, A_HEAD_DIM), BF16),
                        pltpu.VMEM((A_HEAD_DIM, seq), BF16),
                        pltpu.VMEM((A_HEAD_DIM, seq), BF16),
                        pltpu.VMEM((nbp, seq), F32),
                        pltpu.VMEM((seq, MOBA_BLOCK), F32)],
        compiler_params=_cparams("parallel", "parallel"),
        name="moba_prompt",
    )(aq, ak, av, bias_tiles, w_up, w_down)


PAGES_PER_BLOCK = MOBA_BLOCK // PAGE_SIZE
PAST_BLOCKS_PER_STEP = 4


def _moba_past_kernel(pt_ref, q_ref, *refs, n_pages_step):
    del pt_ref
    k_refs = refs[:n_pages_step]
    v_refs = refs[n_pages_step:2 * n_pages_step]
    bias_ref, o_ref, m_ref, l_ref, km_ref = refs[2 * n_pages_step:]
    hq = q_ref.shape[1]
    nq = hq // A_HEADS
    qb = (q_ref[0] * (A_HEAD_DIM ** -0.5)).reshape(A_HEADS, nq, A_HEAD_DIM).astype(BF16)
    n_blocks = n_pages_step // PAGES_PER_BLOCK
    is_last_step = pl.program_id(1) == pl.num_programs(1) - 1

    def head_rows(ref, h):
        return ref[0, 0, pl.ds(h, PAGE_SIZE, stride=A_HEADS), :]

    for g in range(n_blocks):
        pages = range(g * PAGES_PER_BLOCK, (g + 1) * PAGES_PER_BLOCK)
        ksum = sum(jnp.sum(k_refs[p][0, 0].reshape(PAGE_SIZE, A_HEADS, A_HEAD_DIM), axis=0) for p in pages)
        km_ref[0, g] = ksum / MOBA_BLOCK
        kind = jnp.where(is_last_step, 1, 0) if g == n_blocks - 1 else 0
        kh = jnp.stack([jnp.concatenate([head_rows(k_refs[p], h) for p in pages], axis=0)
                        for h in range(A_HEADS)]).astype(BF16)
        vh = jnp.stack([jnp.concatenate([head_rows(v_refs[p], h) for p in pages], axis=0)
                        for h in range(A_HEADS)]).astype(BF16)
        lt = _bdot_nt(qb, kh) + bias_ref[kind]
        m = jnp.max(lt, axis=2, keepdims=True)
        p_ = jnp.exp(lt - m)
        l = jnp.sum(p_, axis=2, keepdims=True)
        o_ref[0, g] = _bdot(p_.astype(BF16), vh).reshape(hq, A_HEAD_DIM)
        m_ref[0, g] = jnp.broadcast_to(m, (A_HEADS, nq, A_HEAD_DIM)).reshape(hq, A_HEAD_DIM)
        l_ref[0, g] = jnp.broadcast_to(l, (A_HEADS, nq, A_HEAD_DIM)).reshape(hq, A_HEAD_DIM)


def _moba_past(page_table, q_hq, cache_k, cache_v, past_bias, layer):
    bs, n_pages = page_table.shape
    assert n_pages % PAGES_PER_BLOCK == 0
    nbp = n_pages // PAGES_PER_BLOCK
    gb = math.gcd(nbp, PAST_BLOCKS_PER_STEP)
    pps = gb * PAGES_PER_BLOCK
    hq = q_hq.shape[1]
    page = (1, 1, PAGE_SIZE * A_HEADS, A_HEAD_DIM)
    cache_k, cache_v = (c.reshape(c.shape[:2] + page[2:]) for c in (cache_k, cache_v))
    part = pl.BlockSpec((1, gb, hq, A_HEAD_DIM), lambda b, n, pt: (b, n, 0, 0))

    def page_spec(p):
        return pl.BlockSpec(page, lambda b, n, pt: (layer, pt[b, pps * n + p], 0, 0))

    grid_spec = pltpu.PrefetchScalarGridSpec(
        num_scalar_prefetch=1,
        grid=(bs, nbp // gb),
        in_specs=[pl.BlockSpec((1, hq, A_HEAD_DIM), lambda b, n, pt: (b, 0, 0))]
        + [page_spec(p) for p in range(pps)] * 2
        + [pl.BlockSpec((2, A_HEADS, hq // A_HEADS, MOBA_BLOCK), lambda b, n, pt: (0, 0, 0, 0))],
        out_specs=[part, part, part,
                   pl.BlockSpec((1, gb, A_HEADS, A_HEAD_DIM), lambda b, n, pt: (b, n, 0, 0))],
    )
    pshape = jax.ShapeDtypeStruct((bs, nbp, hq, A_HEAD_DIM), F32)
    return pl.pallas_call(
        functools.partial(_moba_past_kernel, n_pages_step=pps),
        grid_spec=grid_spec,
        out_shape=[pshape, pshape, pshape, jax.ShapeDtypeStruct((bs, nbp, A_HEADS, A_HEAD_DIM), F32)],
        compiler_params=_cparams("parallel", "parallel"),
        name="moba_past_blocks",
    )(page_table, q_hq, *([cache_k] * pps), *([cache_v] * pps), past_bias)


def _moba_combine_kernel(q_ref, km_ref, kn_ref, vn_ref, ob_ref, op_ref, mp_ref, lp_ref, out_ref):
    nbp = km_ref.shape[2]
    q = q_ref[0] * (A_HEAD_DIM ** -0.5)
    hq = q.shape[0]
    nq = hq // A_HEADS
    gate = jnp.concatenate(
        [_dot_nt(q[h * nq:(h + 1) * nq], km_ref[0, h], precision=HIGHEST) for h in range(A_HEADS)], axis=0)
    lane = lax.broadcasted_iota(jnp.int32, (hq, nbp), 1)
    rank = jnp.zeros((hq, nbp), F32)
    for j in range(nbp):
        gj = gate[:, j:j + 1]
        beats = (gj > gate) | ((gj == gate) & (j < lane))
        rank = rank + beats.astype(F32)
    selneg = jnp.where(rank < MOBA_TOP_K, 0.0, NEG_INF)
    qb = q.astype(BF16)
    lt = _dot_nt(qb, kn_ref[0].astype(BF16))
    r = lax.broadcasted_iota(jnp.int32, lt.shape, 0)
    c = lax.broadcasted_iota(jnp.int32, lt.shape, 1)
    ok = ((c % A_HEADS) == (r // nq)) & ((c // A_HEADS) <= (r % nq))
    lo = jnp.where(ok, lt + ob_ref[...], NEG_INF)
    m = jnp.max(lo, axis=1, keepdims=True)
    for n in range(nbp):
        m = jnp.maximum(m, mp_ref[0, n][:, 0:1] + selneg[:, n:n + 1])
    p = jnp.exp(lo - m)
    l = jnp.sum(p, axis=1, keepdims=True)
    acc = _dot(p.astype(BF16), vn_ref[0].astype(BF16))
    for n in range(nbp):
        w = jnp.exp(mp_ref[0, n][:, 0:1] + selneg[:, n:n + 1] - m)
        l = l + w * lp_ref[0, n][:, 0:1]
        acc = acc + w * op_ref[0, n]
    out_ref[0] = acc / l


def _moba_combine(q_hq, kmean_hn, k_new, v_new, own_bias, o_part, m_part, l_part):
    bs, hq, _ = q_hq.shape
    nbp = o_part.shape[1]
    part = pl.BlockSpec((1, nbp, hq, A_HEAD_DIM), lambda b: (b, 0, 0, 0))
    tok = pl.BlockSpec((1, hq, A_HEAD_DIM), lambda b: (b, 0, 0))
    return pl.pallas_call(
        _moba_combine_kernel,
        grid=(bs,),
        in_specs=[tok,
                  pl.BlockSpec((1, A_HEADS, nbp, A_HEAD_DIM), lambda b: (b, 0, 0, 0)),
                  tok, tok,
                  pl.BlockSpec((hq, hq), lambda b: (0, 0)),
                  part, part, part],
        out_specs=tok,
        out_shape=jax.ShapeDtypeStruct((bs, hq, A_HEAD_DIM), F32),
        compiler_params=_cparams("parallel"),
        name="moba_combine",
    )(q_hq, kmean_hn, k_new, v_new, own_bias, o_part, m_part, l_part)


def _out_proj_kernel(h_ref, om_ref, oa_ref, wm_ref, wa_ref, o_ref, wm_scr, wa_scr):
    @pl.when(pl.program_id(1) == 0)
    def _():
        for r in range(0, wm_scr.shape[0], _CAST_ROWS):
            wm_scr[r:r + _CAST_ROWS, :] = wm_ref[r:r + _CAST_ROWS, :].astype(BF16)
            wa_scr[r:r + _CAST_ROWS, :] = wa_ref[r:r + _CAST_ROWS, :].astype(BF16)

    o_ref[...] = (h_ref[...] + _dot(om_ref[...].astype(BF16), wm_scr[...])
                  + _dot(oa_ref[...].astype(BF16), wa_scr[...]))


def _out_proj(h, out_m, out_a, w_out):
    m = h.shape[0]
    tm = min(512, m)
    tn = 1024
    half = M_HEADS * M_V_DIM
    return pl.pallas_call(
        _out_proj_kernel,
        grid=(D_MODEL // tn, m // tm),
        in_specs=[
            pl.BlockSpec((tm, tn), lambda j, i: (i, j)),
            pl.BlockSpec((tm, half), lambda j, i: (i, 0)),
            pl.BlockSpec((tm, half), lambda j, i: (i, 0)),
            pl.BlockSpec((half, tn), lambda j, i: (0, j)),
            pl.BlockSpec((half, tn), lambda j, i: (1, j)),
        ],
        out_specs=pl.BlockSpec((tm, tn), lambda j, i: (i, j)),
        out_shape=jax.ShapeDtypeStruct((m, D_MODEL), F32),
        scratch_shapes=[pltpu.VMEM((half, tn), BF16), pltpu.VMEM((half, tn), BF16)],
        compiler_params=_cparams("arbitrary", "arbitrary"),
        name="out_proj",
    )(h, out_m, out_a, w_out, w_out)


def _mlp_kernel(h_ref, g_ref, wu_ref, wd_ref, o_ref, xn_ref):
    f = pl.program_id(1)

    @pl.when(f == 0)
    def _():
        xn_ref[...] = _rms(h_ref[...], g_ref[...]).astype(BF16)
        o_ref[...] = h_ref[...]

    u = _dot(xn_ref[...], wu_ref[...])
    a = jnp.square(jnp.maximum(u, 0.0)).astype(BF16)
    o_ref[...] += _dot(a, wd_ref[...])


def _mlp(h, g, w_up, w_down):
    m = h.shape[0]
    tm = min(512, m)
    tf = 1024
    return pl.pallas_call(
        _mlp_kernel,
        grid=(m // tm, D_FF // tf),
        in_specs=[
            pl.BlockSpec((tm, D_MODEL), lambda i, f: (i, 0)),
            pl.BlockSpec((1, D_MODEL), lambda i, f: (0, 0)),
            pl.BlockSpec((D_MODEL, tf), lambda i, f: (0, f)),
            pl.BlockSpec((tf, D_MODEL), lambda i, f: (f, 0)),
        ],
        out_specs=pl.BlockSpec((tm, D_MODEL), lambda i, f: (i, 0)),
        out_shape=jax.ShapeDtypeStruct((m, D_MODEL), F32),
        scratch_shapes=[pltpu.VMEM((tm, D_MODEL), BF16)],
        compiler_params=_cparams("parallel", "arbitrary"),
        name="mlp",
    )(h, g, w_up, w_down)


def _ple_kernel(h_ref, pe_ref, g_ref, wg_ref, wp_ref, gf_ref, y_ref):
    h = h_ref[...]
    gate = jax.nn.sigmoid(_dot(_rms(h, g_ref[...]).astype(BF16), wg_ref[...]))
    proj = _dot(pe_ref[...].astype(BF16), wp_ref[...])
    y_ref[...] = _rms(h + gate * proj, gf_ref[...])


def _ple_final(h, pe, g_ple, w_gate, w_proj, g_final):
    m = h.shape[0]
    tm = min(512, m)
    vec = pl.BlockSpec((1, D_MODEL), lambda i: (0, 0))
    return pl.pallas_call(
        _ple_kernel,
        grid=(m // tm,),
        in_specs=[
            pl.BlockSpec((tm, D_MODEL), lambda i: (i, 0)),
            pl.BlockSpec((tm, PLE_DIM), lambda i: (i, 0)),
            vec,
            pl.BlockSpec((D_MODEL, D_MODEL), lambda i: (0, 0)),
            pl.BlockSpec((PLE_DIM, D_MODEL), lambda i: (0, 0)),
            vec,
        ],
        out_specs=pl.BlockSpec((tm, D_MODEL), lambda i: (i, 0)),
        out_shape=jax.ShapeDtypeStruct((m, D_MODEL), F32),
        compiler_params=_cparams("parallel"),
        name="ple_final_norm",
    )(h, pe, g_ple, w_gate, w_proj, g_final)


def _layer_weights(w_in, b_igate, b_fgate, g_mix, g_mhead, w_out, g_ffn, w_up, w_down,
                   g_ple, w_ple_gate, w_ple_proj):
    gate_bias = jnp.pad(jnp.concatenate([b_igate, b_fgate]), (0, GATE_PAD - 2 * M_HEADS)).reshape(1, GATE_PAD)
    return dict(
        w_in=w_in.T, gate_bias=gate_bias.astype(F32),
        g_mix=g_mix.reshape(1, D_MODEL), g_mhead=g_mhead, w_out=w_out,
        g_ffn=g_ffn.reshape(1, D_MODEL), w_up=w_up, w_down=w_down,
        g_ple=g_ple.reshape(1, D_MODEL), w_ple_gate=w_ple_gate.astype(BF16),
        w_ple_proj=w_ple_proj.astype(BF16))


def _tail(h, out_m, out_a, pe, w, g_final):
    h = _out_proj(h, out_m, out_a, w["w_out"])
    h = _mlp(h, w["g_ffn"], w["w_up_bf16"], w["w_down_bf16"])
    return _ple_final(h, pe, w["g_ple"], w["w_ple_gate"], w["w_ple_proj"], g_final)


def _split_state(c_aug, m_fin):
    return c_aug[..., :M_V_DIM], c_aug[..., M_V_DIM], m_fin[..., 0, 0]


def kernel(x_prompt, x_sample, cache_k, cache_v, state_C, state_n, state_m, page_table, p_prompt, p_sample, rel_bias_table, w_in, b_igate, b_fgate, g_mix, g_mhead, w_out, g_ffn, w_up, w_down, g_ple, w_ple_gate, w_ple_proj, g_final):
    depth = w_in.shape[0]
    assert depth == 1, "one decoder layer per call"
    layer = 0
    bp, tp, _ = x_prompt.shape
    bs, ts, _ = x_sample.shape
    past_len = page_table.shape[1] * PAGE_SIZE
    assert past_len % MOBA_BLOCK == 0 and ts <= MOBA_BLOCK
    kv_shape = (A_HEADS, A_HEAD_DIM)
    g_fin = g_final.reshape(1, D_MODEL)

    w = _layer_weights(w_in[layer], b_igate[layer], b_fgate[layer], g_mix[layer], g_mhead[layer], w_out[layer],
                       g_ffn[layer], w_up[layer], w_down[layer], g_ple[layer], w_ple_gate[layer],
                       w_ple_proj[layer])
    bias_tiles = _bias_tiles(rel_bias_table)

    hp = x_prompt.reshape(bp * tp, D_MODEL)
    xn_p, gp = _norm_gate(hp, w["g_mix"], w["w_in"])
    zp, aq_p, ak_p, av_p = _in_proj(xn_p, w["w_in"])
    c0 = jnp.zeros((bp, M_HEADS, M_QK_DIM, 2 * M_V_DIM), F32)
    m0 = jnp.zeros((bp, M_HEADS, 1, GATE_PAD), F32)
    om_p, c_p, m_p = _mlstm(zp, gp, w["gate_bias"], w["g_mhead"], c0, m0, bp, tp)
    oa_p, w["w_up_bf16"], w["w_down_bf16"] = _moba_prompt(aq_p, ak_p, av_p, bias_tiles, w["w_up"], w["w_down"],
                                                          bp, tp)
    y_p = _tail(hp, om_p, oa_p, p_prompt[layer].reshape(bp * tp, PLE_DIM), w, g_fin)
    k_p = ak_p.reshape((1, bp, tp) + kv_shape)
    v_p = av_p.reshape((1, bp, tp) + kv_shape)
    cp, np_, mp = _split_state(c_p, m_p)

    hs = x_sample.reshape(bs * ts, D_MODEL)
    xn_s, gs = _norm_gate(hs, w["g_mix"], w["w_in"])
    zs, aq_s, ak_s, av_s = _in_proj(xn_s, w["w_in"])
    c0s = jnp.concatenate([state_C[layer], state_n[layer][..., None],
                           jnp.zeros((bs, M_HEADS, M_QK_DIM, M_V_DIM - 1), F32)], axis=-1)
    m0s = jnp.broadcast_to(state_m[layer][..., None, None], (bs, M_HEADS, 1, GATE_PAD))
    om_s, c_s, m_s = _mlstm(zs, gs, w["gate_bias"], w["g_mhead"], c0s, m0s, bs, ts)

    hq = A_HEADS * ts
    q_s = aq_s.reshape(bs, ts, A_HEADS, A_HEAD_DIM)
    q_hq = q_s.transpose(0, 2, 1, 3).reshape(bs, hq, A_HEAD_DIM)
    k_new = ak_s.reshape(bs, hq, A_HEAD_DIM)
    v_new = av_s.reshape(bs, hq, A_HEAD_DIM)
    rows = bias_tiles[:, :, :, :ts].transpose(0, 1, 3, 2)
    expand = lambda a: jnp.repeat(a, A_HEADS, axis=-1).reshape(hq, -1)
    past_bias = jnp.stack([rows[:, 2], rows[:, 1]])
    own_bias = expand(rows[:, 0, :, :ts])
    o_part, m_part, l_part, kmean = _moba_past(page_table, q_hq, cache_k, cache_v, past_bias, layer)
    oa_hq = _moba_combine(q_hq, kmean.transpose(0, 2, 1, 3), k_new, v_new, own_bias, o_part, m_part, l_part)
    oa_s = oa_hq.reshape(bs, A_HEADS, ts, A_HEAD_DIM).transpose(0, 2, 1, 3).reshape(bs * ts, A_HEADS * A_HEAD_DIM)
    y_s = _tail(hs, om_s, oa_s, p_sample[layer].reshape(bs * ts, PLE_DIM), w, g_fin)
    k_s = k_new.reshape((1, bs, ts) + kv_shape)
    v_s = v_new.reshape((1, bs, ts) + kv_shape)
    cs, ns, ms = _split_state(c_s, m_s)

    return (y_p.reshape(bp, tp, D_MODEL), y_s.reshape(bs, ts, D_MODEL),
            k_p, v_p, cp[None], np_[None], mp[None],
            k_s, v_s, cs[None], ns[None], ms[None])
```

```python
import functools
import math

import jax
import jax.numpy as jnp
from jax import lax
from jax.experimental import pallas as pl
from jax.experimental.pallas import tpu as pltpu

F32 = jnp.float32
BF16 = jnp.bfloat16
HIGHEST = lax.Precision.HIGHEST

D_MODEL = 2048
M_HEADS = 8
M_V_DIM = 128
M_QK_DIM = 64
M_CHUNK = 64
A_HEADS = 8
A_HEAD_DIM = 128
MOBA_BLOCK = 256
MOBA_TOP_K = 3
NUM_BUCKETS = 32
MAX_EXACT = NUM_BUCKETS // 2
MAX_DISTANCE = 128
D_FF = 4 * D_MODEL
PLE_DIM = 256
PAGE_SIZE = 128
EPS = 1e-6

_MQ, _MK, _MV, _MO = 0, 512, 1024, 2048
_GATES = 3072
_AQ = 3088
D_MAIN = 6144
GATE_PAD = 128

VMEM_LIMIT = 56 * 1024 * 1024
NEG_INF = float("-inf")


def _cparams(*sem):
    return pltpu.CompilerParams(dimension_semantics=sem, vmem_limit_bytes=VMEM_LIMIT)


def _rms(x, g):
    return x * lax.rsqrt(jnp.mean(x * x, axis=-1, keepdims=True) + EPS) * g


def _dot_nt(a, b, **kw):
    return lax.dot_general(a, b, (((1,), (1,)), ((), ())), preferred_element_type=F32, **kw)


def _dot_tn(a, b, **kw):
    return lax.dot_general(a, b, (((0,), (0,)), ((), ())), preferred_element_type=F32, **kw)


def _dot(a, b, **kw):
    return jnp.dot(a, b, preferred_element_type=F32, **kw)


_IN_TN = 1024
_N_IN_TILES = D_MAIN // _IN_TN
_N_MLSTM_TILES = _GATES // _IN_TN
_GATE_COLS = _AQ - _GATES
_CAST_ROWS = 256


def _norm_gate_kernel(x_ref, xs_ref, g_ref, wg_ref, xn_ref, gate_ref, xns_ref, gates_ref):
    wg = wg_ref[...]
    hi = wg.astype(BF16).astype(F32)
    row = lax.broadcasted_iota(jnp.int32, wg.shape, 0)
    w2 = jnp.where(row < _GATE_COLS, hi,
                   jnp.where(row < 2 * _GATE_COLS, pltpu.roll(wg - hi, _GATE_COLS, axis=0), 0.0))
    w2 = w2.astype(BF16)

    def norm_and_gate(src_ref, dst_ref, gate_dst_ref):
        xn = _rms(src_ref[...], g_ref[...]).astype(BF16)
        dst_ref[...] = xn
        r = _dot_nt(xn, w2)
        gate_dst_ref[...] = r + pltpu.roll(r, GATE_PAD - _GATE_COLS, axis=1)

    norm_and_gate(x_ref, xn_ref, gate_ref)

    @pl.when(pl.program_id(0) == 0)
    def _():
        norm_and_gate(xs_ref, xns_ref, gates_ref)


def _norm_gate(x, xs, g, w_in_t):
    m, ms = x.shape[0], xs.shape[0]
    tm = min(512, m)
    return pl.pallas_call(
        _norm_gate_kernel,
        grid=(m // tm,),
        in_specs=[
            pl.BlockSpec((tm, D_MODEL), lambda i: (i, 0)),
            pl.BlockSpec((ms, D_MODEL), lambda i: (0, 0)),
            pl.BlockSpec((1, D_MODEL), lambda i: (0, 0)),
            pl.BlockSpec((GATE_PAD, D_MODEL), lambda i: (_GATES // GATE_PAD, 0)),
        ],
        out_specs=[pl.BlockSpec((tm, D_MODEL), lambda i: (i, 0)),
                   pl.BlockSpec((tm, GATE_PAD), lambda i: (i, 0)),
                   pl.BlockSpec((ms, D_MODEL), lambda i: (0, 0)),
                   pl.BlockSpec((ms, GATE_PAD), lambda i: (0, 0))],
        out_shape=[jax.ShapeDtypeStruct((m, D_MODEL), BF16), jax.ShapeDtypeStruct((m, GATE_PAD), F32),
                   jax.ShapeDtypeStruct((ms, D_MODEL), BF16), jax.ShapeDtypeStruct((ms, GATE_PAD), F32)],
        compiler_params=_cparams("arbitrary"),
        name="norm_gate",
    )(x, xs, g, w_in_t)


def _in_proj_kernel(xn_ref, xs_ref, wa_ref, wb_ref, zm_ref, aq_ref, ak_ref, av_ref,
                    zms_ref, aqs_ref, aks_ref, avs_ref, w_scr):
    j = pl.program_id(0)
    i = pl.program_id(1)

    @pl.when((i == 0) & (j < _N_MLSTM_TILES))
    def _():
        for r in range(0, _IN_TN, _CAST_ROWS):
            w_scr[r:r + _CAST_ROWS, :] = wa_ref[r:r + _CAST_ROWS, :].astype(BF16)

    @pl.when((i == 0) & (j >= _N_MLSTM_TILES))
    def _():
        body = _IN_TN - _GATE_COLS
        for r in range(0, body, _CAST_ROWS):
            n = min(_CAST_ROWS, body - r)
            w_scr[r:r + n, :] = wa_ref[_GATE_COLS + r:_GATE_COLS + r + n, :].astype(BF16)
        w_scr[body:, :] = wb_ref[:_GATE_COLS, :].astype(BF16)

    def project(dst_ref, dst_s_ref):
        dst_ref[...] = _dot_nt(xn_ref[...], w_scr[...])

        @pl.when(i == 0)
        def _():
            dst_s_ref[...] = _dot_nt(xs_ref[...], w_scr[...])

    @pl.when(j < _N_MLSTM_TILES)
    def _():
        project(zm_ref, zms_ref)

    for t, refs in enumerate(((aq_ref, aqs_ref), (ak_ref, aks_ref), (av_ref, avs_ref))):
        @pl.when(j == _N_MLSTM_TILES + t)
        def _(refs=refs):
            project(*refs)


def _in_proj(xn, xs, w_in_t):
    m, ms = xn.shape[0], xs.shape[0]
    tm = min(512, m)
    tn = _IN_TN
    last = m // tm - 1

    def held(j, i, first_tile, last_tile):
        return jnp.where(j < first_tile, 0, jnp.where(j <= last_tile, i, last))

    def section(t):
        return pl.BlockSpec((tm, tn), lambda j, i: (held(j, i, t, t), 0))

    sec_shape = jax.ShapeDtypeStruct((m, tn), F32)
    sec_s = pl.BlockSpec((ms, tn), lambda j, i: (0, 0))
    sec_s_shape = jax.ShapeDtypeStruct((ms, tn), F32)
    nm = _N_MLSTM_TILES
    return pl.pallas_call(
        _in_proj_kernel,
        grid=(_N_IN_TILES, m // tm),
        in_specs=[
            pl.BlockSpec((tm, D_MODEL), lambda j, i: (i, 0)),
            pl.BlockSpec((ms, D_MODEL), lambda j, i: (0, 0)),
            pl.BlockSpec((tn, D_MODEL), lambda j, i: (j, 0)),
            pl.BlockSpec((GATE_PAD, D_MODEL), lambda j, i: ((j + 1) * (tn // GATE_PAD), 0)),
        ],
        out_specs=[
            pl.BlockSpec((tm, tn), lambda j, i: (held(j, i, 0, nm - 1), jnp.minimum(j, nm - 1))),
            section(nm), section(nm + 1), section(nm + 2),
            pl.BlockSpec((ms, tn), lambda j, i: (0, jnp.minimum(j, nm - 1))),
            sec_s, sec_s, sec_s,
        ],
        out_shape=[jax.ShapeDtypeStruct((m, _GATES), F32), sec_shape, sec_shape, sec_shape,
                   jax.ShapeDtypeStruct((ms, _GATES), F32), sec_s_shape, sec_s_shape, sec_s_shape],
        scratch_shapes=[pltpu.VMEM((tn, D_MODEL), BF16)],
        compiler_params=_cparams("arbitrary", "arbitrary"),
        name="in_proj",
    )(xn, xs, w_in_t, w_in_t)


def _log_sigmoid(x):
    return -(jnp.maximum(-x, 0.0) + jnp.log1p(jnp.exp(-jnp.abs(x))))


def _bdot(a, b):
    return lax.dot_general(a, b, (((2,), (1,)), ((0,), (0,))), preferred_element_type=F32)


def _bdot_nt(a, b):
    return lax.dot_general(a, b, (((2,), (2,)), ((0,), (0,))), preferred_element_type=F32)


def _bdot_tn(a, b):
    return lax.dot_general(a, b, (((1,), (1,)), ((0,), (0,))), preferred_element_type=F32)


def _mlstm_kernel(q_ref, k_ref, v_ref, o_ref, g_ref, gb_ref, gh_ref, c0_ref, m0_ref,
                  out_ref, cfin_ref, mfin_ref, c_scr, m_scr, *, chunk, n_chunks):
    L = chunk
    H = M_HEADS
    tb = L * n_chunks
    t = pl.program_id(1)

    @pl.when(t == 0)
    def _():
        c_scr[...] = c0_ref[0]
        m_scr[...] = m0_ref[0]

    g = g_ref[...] + gb_ref[...]
    lane = lax.broadcasted_iota(jnp.int32, (tb, GATE_PAD), 1)
    gl = jnp.where((lane >= H) & (lane < 2 * H), _log_sigmoid(g), g)
    r = lax.broadcasted_iota(jnp.int32, (tb, tb), 0)
    c = lax.broadcasted_iota(jnp.int32, (tb, tb), 1)
    tri = ((r // L == c // L) & (c <= r)).astype(F32)
    cum = _dot(tri, gl, precision=HIGHEST)

    def rows(ci):
        return slice(ci * L, (ci + 1) * L)

    def stack(f):
        return jnp.stack([f(ci, h) for ci in range(n_chunks) for h in range(H)])

    ones_col = (lax.broadcasted_iota(jnp.int32, (L, M_V_DIM), 1) == 0).astype(BF16)
    li = stack(lambda ci, h: gl[rows(ci), h:h + 1])
    b = stack(lambda ci, h: cum[rows(ci), H + h:H + h + 1])
    qb = stack(lambda ci, h: q_ref[rows(ci), h * M_QK_DIM:(h + 1) * M_QK_DIM]).astype(BF16)
    kc = stack(lambda ci, h: k_ref[rows(ci), h * M_QK_DIM:(h + 1) * M_QK_DIM]) * (M_QK_DIM ** -0.5)
    vaug = stack(lambda ci, h: jnp.concatenate(
        [v_ref[rows(ci), h * M_V_DIM:(h + 1) * M_V_DIM].astype(BF16), ones_col], axis=1))

    rl = lax.broadcasted_iota(jnp.int32, (L, L), 0)
    cl = lax.broadcasted_iota(jnp.int32, (L, L), 1)
    w_row = jnp.sum(jnp.where(rl == cl, li - b, 0.0), axis=1, keepdims=True)
    d = jnp.where(cl <= rl, b + w_row, NEG_INF)
    m_loc = jnp.max(d, axis=2, keepdims=True)
    s = _bdot_nt(qb, kc.astype(BF16)) * jnp.exp(d - m_loc)
    nd_loc = _bdot(s.astype(BF16), vaug)
    g_last = m_loc[:, L - 1:L, :]
    b_last = b[:, L - 1:L, :]
    ws = jnp.exp(b_last - b + li - g_last)
    dc_loc = _bdot_tn((kc * ws).astype(BF16), vaug)

    m_prev = m_scr[:, :, 0:1]
    c_aug = c_scr[...]
    for ci in range(n_chunks):
        grp = slice(ci * H, (ci + 1) * H)
        a = b[grp] + m_prev
        m_t = jnp.maximum(a, m_loc[grp])
        nd = (jnp.exp(a - m_t) * _bdot(qb[grp], c_aug.astype(BF16))
              + jnp.exp(m_loc[grp] - m_t) * nd_loc[grp])
        hh = nd[:, :, :M_V_DIM] / jnp.maximum(jnp.abs(nd[:, :, M_V_DIM:M_V_DIM + 1]), jnp.exp(-m_t))
        hn = _rms(hh, gh_ref[...])
        for h in range(H):
            og = jax.nn.sigmoid(o_ref[rows(ci), h * M_V_DIM:(h + 1) * M_V_DIM])
            out_ref[rows(ci), h * M_V_DIM:(h + 1) * M_V_DIM] = (og * hn[h]).astype(out_ref.dtype)
        m_new = jnp.maximum(b_last[grp] + m_prev, g_last[grp])
        c_aug = (jnp.exp(b_last[grp] + m_prev - m_new) * c_aug
                 + jnp.exp(g_last[grp] - m_new) * dc_loc[grp])
        m_prev = m_new
    c_scr[...] = c_aug
    m_scr[...] = jnp.broadcast_to(m_prev, m_scr.shape)

    @pl.when(t == pl.num_programs(1) - 1)
    def _():
        cfin_ref[0] = c_scr[...]
        mfin_ref[0] = m_scr[...]


def _mlstm(z, gates, gate_bias, g_mhead, c0_aug, m0, batch, seq):
    L = math.gcd(seq, M_CHUNK)
    n_chunks = max(1, min(seq // L, 256 // L))
    tb = L * n_chunks
    nt = seq // tb
    row = lambda b, t: b * nt + t
    kern = functools.partial(_mlstm_kernel, chunk=L, n_chunks=n_chunks)
    return pl.pallas_call(
        kern,
        grid=(batch, nt),
        in_specs=[
            pl.BlockSpec((tb, 512), lambda b, t: (row(b, t), _MQ // 512)),
            pl.BlockSpec((tb, 512), lambda b, t: (row(b, t), _MK // 512)),
            pl.BlockSpec((tb, 1024), lambda b, t: (row(b, t), _MV // 1024)),
            pl.BlockSpec((tb, 1024), lambda b, t: (row(b, t), _MO // 1024)),
            pl.BlockSpec((tb, GATE_PAD), lambda b, t: (row(b, t), 0)),
            pl.BlockSpec((1, GATE_PAD), lambda b, t: (0, 0)),
            pl.BlockSpec((M_HEADS, 1, M_V_DIM), lambda b, t: (0, 0, 0)),
            pl.BlockSpec((1, M_HEADS, M_QK_DIM, 2 * M_V_DIM), lambda b, t: (b, 0, 0, 0)),
            pl.BlockSpec((1, M_HEADS, 1, GATE_PAD), lambda b, t: (b, 0, 0, 0)),
        ],
        out_specs=[
            pl.BlockSpec((tb, M_HEADS * M_V_DIM), lambda b, t: (row(b, t), 0)),
            pl.BlockSpec((1, M_HEADS, M_QK_DIM, 2 * M_V_DIM), lambda b, t: (b, 0, 0, 0)),
            pl.BlockSpec((1, M_HEADS, 1, GATE_PAD), lambda b, t: (b, 0, 0, 0)),
        ],
        out_shape=[
            jax.ShapeDtypeStruct((batch * seq, M_HEADS * M_V_DIM), BF16 if tb % 16 == 0 else F32),
            jax.ShapeDtypeStruct((batch, M_HEADS, M_QK_DIM, 2 * M_V_DIM), F32),
            jax.ShapeDtypeStruct((batch, M_HEADS, 1, GATE_PAD), F32),
        ],
        scratch_shapes=[pltpu.VMEM((M_HEADS, M_QK_DIM, 2 * M_V_DIM), F32),
                        pltpu.VMEM((M_HEADS, 1, GATE_PAD), F32)],
        compiler_params=_cparams("parallel", "arbitrary"),
        name="mlstm",
    )(z, z, z, z, gates, gate_bias, g_mhead.reshape(M_HEADS, 1, M_V_DIM), c0_aug, m0)


def _t5_bucket(rel):
    n = jnp.maximum(rel, 0)
    nf = jnp.maximum(n, 1).astype(F32)
    large = MAX_EXACT + (jnp.log(nf / MAX_EXACT) / math.log(MAX_DISTANCE / MAX_EXACT)
                         * (NUM_BUCKETS - MAX_EXACT)).astype(jnp.int32)
    large = jnp.minimum(large, NUM_BUCKETS - 1)
    return jnp.where(n < MAX_EXACT, n, large)


def _bias_kernel(tab_ref, out_ref):
    h = pl.program_id(0)
    i = lax.broadcasted_iota(jnp.int32, (MOBA_BLOCK, MOBA_BLOCK), 1)
    j = lax.broadcasted_iota(jnp.int32, (MOBA_BLOCK, MOBA_BLOCK), 0)
    for kind, rel in ((0, i - j), (1, MOBA_BLOCK + i - j)):
        bucket = _t5_bucket(rel)
        acc = jnp.zeros((MOBA_BLOCK, MOBA_BLOCK), F32)
        for b in range(NUM_BUCKETS):
            acc = jnp.where(bucket == b, tab_ref[h, b], acc)
        out_ref[0, kind] = acc
    out_ref[0, 2] = jnp.full((MOBA_BLOCK, MOBA_BLOCK), tab_ref[h, NUM_BUCKETS - 1], F32)


def _bias_tiles(rel_table):
    assert MOBA_BLOCK + 1 >= MAX_DISTANCE
    tab = rel_table.T.astype(F32)
    return pl.pallas_call(
        _bias_kernel,
        grid=(A_HEADS,),
        in_specs=[pl.BlockSpec(memory_space=pltpu.SMEM)],
        out_specs=pl.BlockSpec((1, 3, MOBA_BLOCK, MOBA_BLOCK), lambda h: (h, 0, 0, 0)),
        out_shape=jax.ShapeDtypeStruct((A_HEADS, 3, MOBA_BLOCK, MOBA_BLOCK), F32),
        compiler_params=_cparams("parallel"),
        name="t5_bias_tiles",
    )(tab)


def _moba_prompt_kernel(q_ref, k_ref, v_ref, bias_ref, wu_ref, wd_ref, o_ref, wub_ref, wdb_ref,
                        kb_scr, qt_scr, vt_scr, gt_scr, s_scr, *, n_blocks):
    nb = n_blocks
    blk = MOBA_BLOCK
    nbp = gt_scr.shape[0]
    wub_ref[...] = wu_ref[...].astype(BF16)
    wdb_ref[...] = wd_ref[...].astype(BF16)

    kb_scr[...] = k_ref[...].astype(BF16)
    for c in range(nb):
        cols = slice(c * blk, (c + 1) * blk)
        qt_scr[:, cols] = (q_ref[cols, :] * (A_HEAD_DIM ** -0.5)).T.astype(BF16)
        vt_scr[:, cols] = v_ref[cols, :].T.astype(BF16)
    need_gate = nb - 1 > MOBA_TOP_K
    if need_gate:
        means = [jnp.mean(k_ref[n * blk:(n + 1) * blk, :], axis=0, keepdims=True) for n in range(nb)]
        kmean = jnp.concatenate(means + [jnp.zeros((128 - nb, A_HEAD_DIM), F32)], axis=0)
        for c in range(MOBA_TOP_K + 1, nb):
            cols = slice(c * blk, (c + 1) * blk)
            gate = _dot_nt(q_ref[cols, :] * (A_HEAD_DIM ** -0.5), kmean, precision=HIGHEST)
            gt_scr[:, cols] = gate.T[:nbp, :]
    key = lax.broadcasted_iota(jnp.int32, (blk, blk), 0)
    qry = lax.broadcasted_iota(jnp.int32, (blk, blk), 1)
    causal = key <= qry
    blk_id = lax.broadcasted_iota(jnp.int32, (nbp, blk), 0)

    for i in range(nb):
        cols = slice(i * blk, (i + 1) * blk)
        selneg = None
        if i > MOBA_TOP_K:
            g = gt_scr[:, cols]
            rank = jnp.zeros((nbp, blk), F32)
            for j in range(i):
                gj = g[j:j + 1, :]
                rank = rank + ((gj > g) | ((gj == g) & (j < blk_id))).astype(F32)
            selneg = jnp.where(rank < MOBA_TOP_K, 0.0, NEG_INF)
        s_scr[0:(i + 1) * blk, :] = _dot(kb_scr[0:(i + 1) * blk, :], qt_scr[:, cols])
        m = jnp.full((1, blk), NEG_INF, F32)
        for n in range(i + 1):
            rows = slice(n * blk, (n + 1) * blk)
            kind = 0 if n == i else (1 if n == i - 1 else 2)
            st = s_scr[rows, :] + bias_ref[0, kind]
            if n == i:
                st = jnp.where(causal, st, NEG_INF)
            elif selneg is not None:
                st = st + selneg[n:n + 1, :]
            s_scr[rows, :] = st
            m = jnp.maximum(m, jnp.max(st, axis=0, keepdims=True))
        l = jnp.zeros((1, blk), F32)
        acc = jnp.zeros((A_HEAD_DIM, blk), F32)
        for n in range(i + 1):
            rows = slice(n * blk, (n + 1) * blk)
            p = jnp.exp(s_scr[rows, :] - m)
            l = l + jnp.sum(p, axis=0, keepdims=True)
            acc = acc + _dot(vt_scr[:, rows], p.astype(BF16))
        o_ref[cols, :] = (acc / l).T.astype(o_ref.dtype)


def _moba_prompt(aq, ak, av, bias_tiles, w_up, w_down, batch, seq):
    assert seq % MOBA_BLOCK == 0
    nb = seq // MOBA_BLOCK
    assert nb <= 128
    nbp = -(-nb // 8) * 8
    steps = batch * A_HEADS
    slab = D_FF // steps
    assert D_FF % steps == 0 and slab % 128 == 0
    head = pl.BlockSpec((seq, A_HEAD_DIM), lambda b, h: (b, h))
    up_slab = pl.BlockSpec((D_MODEL, slab), lambda b, h: (0, b * A_HEADS + h))
    down_slab = pl.BlockSpec((slab, D_MODEL), lambda b, h: (b * A_HEADS + h, 0))
    kern = functools.partial(_moba_prompt_kernel, n_blocks=nb)
    return pl.pallas_call(
        kern,
        grid=(batch, A_HEADS),
        in_specs=[head, head, head,
                  pl.BlockSpec((1, 3, MOBA_BLOCK, MOBA_BLOCK), lambda b, h: (h, 0, 0, 0)),
                  up_slab, down_slab],
        out_specs=[head, up_slab, down_slab],
        out_shape=[jax.ShapeDtypeStruct((batch * seq, A_HEADS * A_HEAD_DIM), BF16),
                   jax.ShapeDtypeStruct((D_MODEL, D_FF), BF16),
                   jax.ShapeDtypeStruct((D_FF, D_MODEL), BF16)],
        scratch_shapes=[pltpu.VMEM((seq, A_HEAD_DIM), BF16),
                        pltpu.VMEM((A_HEAD_DIM, seq), BF16),
                        pltpu.VMEM((A_HEAD_DIM, seq), BF16),
                        pltpu.VMEM((nbp, seq), F32),
                        pltpu.VMEM((seq, MOBA_BLOCK), F32)],
        compiler_params=_cparams("parallel", "parallel"),
        name="moba_prompt",
    )(aq, ak, av, bias_tiles, w_up, w_down)


PAGES_PER_BLOCK = MOBA_BLOCK // PAGE_SIZE
PAST_BLOCKS_PER_STEP = 8


def _moba_past_kernel(pt_ref, q_ref, *refs, n_pages_step):
    del pt_ref
    k_refs = refs[:n_pages_step]
    v_refs = refs[n_pages_step:2 * n_pages_step]
    bias_ref, o_ref, m_ref, l_ref, km_ref = refs[2 * n_pages_step:]
    hq = q_ref.shape[1]
    nq = hq // A_HEADS
    qb = (q_ref[0] * (A_HEAD_DIM ** -0.5)).reshape(A_HEADS, nq, A_HEAD_DIM).astype(BF16)
    n_blocks = n_pages_step // PAGES_PER_BLOCK
    is_last_step = pl.program_id(1) == pl.num_programs(1) - 1

    def head_rows(ref, h):
        return ref[0, 0, pl.ds(h, PAGE_SIZE, stride=A_HEADS), :]

    for g in range(n_blocks):
        pages = range(g * PAGES_PER_BLOCK, (g + 1) * PAGES_PER_BLOCK)
        ksum = sum(jnp.sum(k_refs[p][0, 0].reshape(PAGE_SIZE, A_HEADS, A_HEAD_DIM), axis=0) for p in pages)
        km_ref[0, g] = ksum / MOBA_BLOCK
        kind = jnp.where(is_last_step, 1, 0) if g == n_blocks - 1 else 0
        kh = jnp.stack([jnp.concatenate([head_rows(k_refs[p], h) for p in pages], axis=0)
                        for h in range(A_HEADS)]).astype(BF16)
        vh = jnp.stack([jnp.concatenate([head_rows(v_refs[p], h) for p in pages], axis=0)
                        for h in range(A_HEADS)]).astype(BF16)
        lt = _bdot_nt(qb, kh) + bias_ref[kind]
        m = jnp.max(lt, axis=2, keepdims=True)
        p_ = jnp.exp(lt - m)
        l = jnp.sum(p_, axis=2, keepdims=True)
        o_ref[0, g] = _bdot(p_.astype(BF16), vh).reshape(hq, A_HEAD_DIM)
        m_ref[0, g] = jnp.broadcast_to(m, (A_HEADS, nq, A_HEAD_DIM)).reshape(hq, A_HEAD_DIM)
        l_ref[0, g] = jnp.broadcast_to(l, (A_HEADS, nq, A_HEAD_DIM)).reshape(hq, A_HEAD_DIM)


def _moba_past(page_table, q_hq, cache_k, cache_v, past_bias, layer):
    bs, n_pages = page_table.shape
    assert n_pages % PAGES_PER_BLOCK == 0
    nbp = n_pages // PAGES_PER_BLOCK
    gb = math.gcd(nbp, PAST_BLOCKS_PER_STEP)
    pps = gb * PAGES_PER_BLOCK
    hq = q_hq.shape[1]
    page = (1, 1, PAGE_SIZE * A_HEADS, A_HEAD_DIM)
    cache_k, cache_v = (c.reshape(c.shape[:2] + page[2:]) for c in (cache_k, cache_v))
    part = pl.BlockSpec((1, gb, hq, A_HEAD_DIM), lambda b, n, pt: (b, n, 0, 0))

    def page_spec(p):
        return pl.BlockSpec(page, lambda b, n, pt: (layer, pt[b, pps * n + p], 0, 0))

    grid_spec = pltpu.PrefetchScalarGridSpec(
        num_scalar_prefetch=1,
        grid=(bs, nbp // gb),
        in_specs=[pl.BlockSpec((1, hq, A_HEAD_DIM), lambda b, n, pt: (b, 0, 0))]
        + [page_spec(p) for p in range(pps)] * 2
        + [pl.BlockSpec((2, A_HEADS, hq // A_HEADS, MOBA_BLOCK), lambda b, n, pt: (0, 0, 0, 0))],
        out_specs=[part, part, part,
                   pl.BlockSpec((1, gb, A_HEADS, A_HEAD_DIM), lambda b, n, pt: (b, n, 0, 0))],
    )
    pshape = jax.ShapeDtypeStruct((bs, nbp, hq, A_HEAD_DIM), F32)
    return pl.pallas_call(
        functools.partial(_moba_past_kernel, n_pages_step=pps),
        grid_spec=grid_spec,
        out_shape=[pshape, pshape, pshape, jax.ShapeDtypeStruct((bs, nbp, A_HEADS, A_HEAD_DIM), F32)],
        compiler_params=_cparams("parallel", "parallel"),
        name="moba_past_blocks",
    )(page_table, q_hq, *([cache_k] * pps), *([cache_v] * pps), past_bias)


def _moba_combine_kernel(q_ref, km_ref, kn_ref, vn_ref, ob_ref, op_ref, mp_ref, lp_ref, out_ref):
    nbp = km_ref.shape[2]
    q = q_ref[0] * (A_HEAD_DIM ** -0.5)
    hq = q.shape[0]
    nq = hq // A_HEADS
    gate = jnp.concatenate(
        [_dot_nt(q[h * nq:(h + 1) * nq], km_ref[0, h], precision=HIGHEST) for h in range(A_HEADS)], axis=0)
    lane = lax.broadcasted_iota(jnp.int32, (hq, nbp), 1)
    rank = jnp.zeros((hq, nbp), F32)
    for j in range(nbp):
        gj = gate[:, j:j + 1]
        beats = (gj > gate) | ((gj == gate) & (j < lane))
        rank = rank + beats.astype(F32)
    selneg = jnp.where(rank < MOBA_TOP_K, 0.0, NEG_INF)
    qb = q.astype(BF16)
    lt = _dot_nt(qb, kn_ref[0].astype(BF16))
    r = lax.broadcasted_iota(jnp.int32, lt.shape, 0)
    c = lax.broadcasted_iota(jnp.int32, lt.shape, 1)
    ok = ((c % A_HEADS) == (r // nq)) & ((c // A_HEADS) <= (r % nq))
    lo = jnp.where(ok, lt + ob_ref[...], NEG_INF)
    m = jnp.max(lo, axis=1, keepdims=True)
    for n in range(nbp):
        m = jnp.maximum(m, mp_ref[0, n][:, 0:1] + selneg[:, n:n + 1])
    p = jnp.exp(lo - m)
    l = jnp.sum(p, axis=1, keepdims=True)
    acc = _dot(p.astype(BF16), vn_ref[0].astype(BF16))
    for n in range(nbp):
        w = jnp.exp(mp_ref[0, n][:, 0:1] + selneg[:, n:n + 1] - m)
        l = l + w * lp_ref[0, n][:, 0:1]
        acc = acc + w * op_ref[0, n]
    out_ref[0] = acc / l


def _moba_combine(q_hq, kmean_hn, k_new, v_new, own_bias, o_part, m_part, l_part):
    bs, hq, _ = q_hq.shape
    nbp = o_part.shape[1]
    part = pl.BlockSpec((1, nbp, hq, A_HEAD_DIM), lambda b: (b, 0, 0, 0))
    tok = pl.BlockSpec((1, hq, A_HEAD_DIM), lambda b: (b, 0, 0))
    return pl.pallas_call(
        _moba_combine_kernel,
        grid=(bs,),
        in_specs=[tok,
                  pl.BlockSpec((1, A_HEADS, nbp, A_HEAD_DIM), lambda b: (b, 0, 0, 0)),
                  tok, tok,
                  pl.BlockSpec((hq, hq), lambda b: (0, 0)),
                  part, part, part],
        out_specs=tok,
        out_shape=jax.ShapeDtypeStruct((bs, hq, A_HEAD_DIM), F32),
        compiler_params=_cparams("parallel"),
        name="moba_combine",
    )(q_hq, kmean_hn, k_new, v_new, own_bias, o_part, m_part, l_part)


def _out_proj_kernel(h_ref, om_ref, oa_ref, hs_ref, oms_ref, oas_ref, wm_ref, wa_ref, o_ref, os_ref,
                     wm_scr, wa_scr):
    def project(h, om, oa, dst):
        dst[...] = (h[...] + _dot(om[...].astype(BF16), wm_scr[...])
                    + _dot(oa[...].astype(BF16), wa_scr[...]))

    @pl.when(pl.program_id(1) == 0)
    def _():
        for r in range(0, wm_scr.shape[0], _CAST_ROWS):
            wm_scr[r:r + _CAST_ROWS, :] = wm_ref[r:r + _CAST_ROWS, :].astype(BF16)
            wa_scr[r:r + _CAST_ROWS, :] = wa_ref[r:r + _CAST_ROWS, :].astype(BF16)
        project(hs_ref, oms_ref, oas_ref, os_ref)

    project(h_ref, om_ref, oa_ref, o_ref)


def _out_proj(h, out_m, out_a, hs, out_ms, out_as, w_out):
    m, ms = h.shape[0], hs.shape[0]
    tm = min(512, m)
    tn = 1024
    half = M_HEADS * M_V_DIM
    return pl.pallas_call(
        _out_proj_kernel,
        grid=(D_MODEL // tn, m // tm),
        in_specs=[
            pl.BlockSpec((tm, tn), lambda j, i: (i, j)),
            pl.BlockSpec((tm, half), lambda j, i: (i, 0)),
            pl.BlockSpec((tm, half), lambda j, i: (i, 0)),
            pl.BlockSpec((ms, tn), lambda j, i: (0, j)),
            pl.BlockSpec((ms, half), lambda j, i: (0, 0)),
            pl.BlockSpec((ms, half), lambda j, i: (0, 0)),
            pl.BlockSpec((half, tn), lambda j, i: (0, j)),
            pl.BlockSpec((half, tn), lambda j, i: (1, j)),
        ],
        out_specs=[pl.BlockSpec((tm, tn), lambda j, i: (i, j)),
                   pl.BlockSpec((ms, tn), lambda j, i: (0, j))],
        out_shape=[jax.ShapeDtypeStruct((m, D_MODEL), F32), jax.ShapeDtypeStruct((ms, D_MODEL), F32)],
        scratch_shapes=[pltpu.VMEM((half, tn), BF16), pltpu.VMEM((half, tn), BF16)],
        compiler_params=_cparams("arbitrary", "arbitrary"),
        name="out_proj",
    )(h, out_m, out_a, hs, out_ms, out_as, w_out, w_out)


def _mlp_kernel(h_ref, hs_ref, g_ref, wu_ref, wd_ref, o_ref, os_ref, xn_ref, xns_ref):
    f = pl.program_id(1)

    def accumulate(src, dst, xn):
        @pl.when(f == 0)
        def _():
            xn[...] = _rms(src[...], g_ref[...]).astype(BF16)
            dst[...] = src[...]

        u = _dot(xn[...], wu_ref[...])
        a = jnp.square(jnp.maximum(u, 0.0)).astype(BF16)
        dst[...] += _dot(a, wd_ref[...])

    accumulate(h_ref, o_ref, xn_ref)

    @pl.when(pl.program_id(0) == 0)
    def _():
        accumulate(hs_ref, os_ref, xns_ref)


def _mlp(h, hs, g, w_up, w_down):
    m, ms = h.shape[0], hs.shape[0]
    tm = min(512, m)
    tf = 1024
    return pl.pallas_call(
        _mlp_kernel,
        grid=(m // tm, D_FF // tf),
        in_specs=[
            pl.BlockSpec((tm, D_MODEL), lambda i, f: (i, 0)),
            pl.BlockSpec((ms, D_MODEL), lambda i, f: (0, 0)),
            pl.BlockSpec((1, D_MODEL), lambda i, f: (0, 0)),
            pl.BlockSpec((D_MODEL, tf), lambda i, f: (0, f)),
            pl.BlockSpec((tf, D_MODEL), lambda i, f: (f, 0)),
        ],
        out_specs=[pl.BlockSpec((tm, D_MODEL), lambda i, f: (i, 0)),
                   pl.BlockSpec((ms, D_MODEL), lambda i, f: (0, 0))],
        out_shape=[jax.ShapeDtypeStruct((m, D_MODEL), F32), jax.ShapeDtypeStruct((ms, D_MODEL), F32)],
        scratch_shapes=[pltpu.VMEM((tm, D_MODEL), BF16), pltpu.VMEM((ms, D_MODEL), BF16)],
        compiler_params=_cparams("arbitrary", "arbitrary"),
        name="mlp",
    )(h, hs, g, w_up, w_down)


def _ple_kernel(h_ref, pe_ref, hs_ref, pes_ref, g_ref, wg_ref, wp_ref, gf_ref, y_ref, ys_ref):
    def gated_embedding(src, pe, dst):
        h = src[...]
        gate = jax.nn.sigmoid(_dot(_rms(h, g_ref[...]).astype(BF16), wg_ref[...]))
        proj = _dot(pe[...].astype(BF16), wp_ref[...])
        dst[...] = _rms(h + gate * proj, gf_ref[...])

    gated_embedding(h_ref, pe_ref, y_ref)

    @pl.when(pl.program_id(0) == 0)
    def _():
        gated_embedding(hs_ref, pes_ref, ys_ref)


def _ple_final(h, pe, hs, pes, g_ple, w_gate, w_proj, g_final):
    m, ms = h.shape[0], hs.shape[0]
    tm = min(512, m)
    vec = pl.BlockSpec((1, D_MODEL), lambda i: (0, 0))
    return pl.pallas_call(
        _ple_kernel,
        grid=(m // tm,),
        in_specs=[
            pl.BlockSpec((tm, D_MODEL), lambda i: (i, 0)),
            pl.BlockSpec((tm, PLE_DIM), lambda i: (i, 0)),
            pl.BlockSpec((ms, D_MODEL), lambda i: (0, 0)),
            pl.BlockSpec((ms, PLE_DIM), lambda i: (0, 0)),
            vec,
            pl.BlockSpec((D_MODEL, D_MODEL), lambda i: (0, 0)),
            pl.BlockSpec((PLE_DIM, D_MODEL), lambda i: (0, 0)),
            vec,
        ],
        out_specs=[pl.BlockSpec((tm, D_MODEL), lambda i: (i, 0)),
                   pl.BlockSpec((ms, D_MODEL), lambda i: (0, 0))],
        out_shape=[jax.ShapeDtypeStruct((m, D_MODEL), F32), jax.ShapeDtypeStruct((ms, D_MODEL), F32)],
        compiler_params=_cparams("arbitrary"),
        name="ple_final_norm",
    )(h, pe, hs, pes, g_ple, w_gate, w_proj, g_final)


def _layer_weights(w_in, b_igate, b_fgate, g_mix, g_mhead, w_out, g_ffn, w_up, w_down,
                   g_ple, w_ple_gate, w_ple_proj):
    gate_bias = jnp.pad(jnp.concatenate([b_igate, b_fgate]), (0, GATE_PAD - 2 * M_HEADS)).reshape(1, GATE_PAD)
    return dict(
        w_in=w_in.T, gate_bias=gate_bias.astype(F32),
        g_mix=g_mix.reshape(1, D_MODEL), g_mhead=g_mhead, w_out=w_out,
        g_ffn=g_ffn.reshape(1, D_MODEL), w_up=w_up, w_down=w_down,
        g_ple=g_ple.reshape(1, D_MODEL), w_ple_gate=w_ple_gate.astype(BF16),
        w_ple_proj=w_ple_proj.astype(BF16))


def _tail(h, out_m, out_a, pe, hs, out_ms, out_as, pes, w, g_final):
    h, hs = _out_proj(h, out_m, out_a, hs, out_ms, out_as, w["w_out"])
    h, hs = _mlp(h, hs, w["g_ffn"], w["w_up_bf16"], w["w_down_bf16"])
    return _ple_final(h, pe, hs, pes, w["g_ple"], w["w_ple_gate"], w["w_ple_proj"], g_final)


def _split_state(c_aug, m_fin):
    return c_aug[..., :M_V_DIM], c_aug[..., M_V_DIM], m_fin[..., 0, 0]


def kernel(x_prompt, x_sample, cache_k, cache_v, state_C, state_n, state_m, page_table, p_prompt, p_sample, rel_bias_table, w_in, b_igate, b_fgate, g_mix, g_mhead, w_out, g_ffn, w_up, w_down, g_ple, w_ple_gate, w_ple_proj, g_final):
    depth = w_in.shape[0]
    assert depth == 1, "one decoder layer per call"
    layer = 0
    bp, tp, _ = x_prompt.shape
    bs, ts, _ = x_sample.shape
    past_len = page_table.shape[1] * PAGE_SIZE
    assert past_len % MOBA_BLOCK == 0 and ts <= MOBA_BLOCK
    kv_shape = (A_HEADS, A_HEAD_DIM)
    g_fin = g_final.reshape(1, D_MODEL)

    w = _layer_weights(w_in[layer], b_igate[layer], b_fgate[layer], g_mix[layer], g_mhead[layer], w_out[layer],
                       g_ffn[layer], w_up[layer], w_down[layer], g_ple[layer], w_ple_gate[layer],
                       w_ple_proj[layer])
    bias_tiles = _bias_tiles(rel_bias_table)

    hp = x_prompt.reshape(bp * tp, D_MODEL)
    hs = x_sample.reshape(bs * ts, D_MODEL)
    xn_p, gp, xn_s, gs = _norm_gate(hp, hs, w["g_mix"], w["w_in"])
    zp, aq_p, ak_p, av_p, zs, aq_s, ak_s, av_s = _in_proj(xn_p, xn_s, w["w_in"])
    c0 = jnp.zeros((bp, M_HEADS, M_QK_DIM, 2 * M_V_DIM), F32)
    m0 = jnp.zeros((bp, M_HEADS, 1, GATE_PAD), F32)
    om_p, c_p, m_p = _mlstm(zp, gp, w["gate_bias"], w["g_mhead"], c0, m0, bp, tp)
    oa_p, w["w_up_bf16"], w["w_down_bf16"] = _moba_prompt(aq_p, ak_p, av_p, bias_tiles, w["w_up"], w["w_down"],
                                                          bp, tp)
    k_p = ak_p.reshape((1, bp, tp) + kv_shape)
    v_p = av_p.reshape((1, bp, tp) + kv_shape)
    cp, np_, mp = _split_state(c_p, m_p)

    c0s = jnp.concatenate([state_C[layer], state_n[layer][..., None],
                           jnp.zeros((bs, M_HEADS, M_QK_DIM, M_V_DIM - 1), F32)], axis=-1)
    m0s = jnp.broadcast_to(state_m[layer][..., None, None], (bs, M_HEADS, 1, GATE_PAD))
    om_s, c_s, m_s = _mlstm(zs, gs, w["gate_bias"], w["g_mhead"], c0s, m0s, bs, ts)

    hq = A_HEADS * ts
    q_s = aq_s.reshape(bs, ts, A_HEADS, A_HEAD_DIM)
    q_hq = q_s.transpose(0, 2, 1, 3).reshape(bs, hq, A_HEAD_DIM)
    k_new = ak_s.reshape(bs, hq, A_HEAD_DIM)
    v_new = av_s.reshape(bs, hq, A_HEAD_DIM)
    rows = bias_tiles[:, :, :, :ts].transpose(0, 1, 3, 2)
    expand = lambda a: jnp.repeat(a, A_HEADS, axis=-1).reshape(hq, -1)
    past_bias = jnp.stack([rows[:, 2], rows[:, 1]])
    own_bias = expand(rows[:, 0, :, :ts])
    o_part, m_part, l_part, kmean = _moba_past(page_table, q_hq, cache_k, cache_v, past_bias, layer)
    oa_hq = _moba_combine(q_hq, kmean.transpose(0, 2, 1, 3), k_new, v_new, own_bias, o_part, m_part, l_part)
    oa_s = oa_hq.reshape(bs, A_HEADS, ts, A_HEAD_DIM).transpose(0, 2, 1, 3).reshape(bs * ts, A_HEADS * A_HEAD_DIM)
    y_p, y_s = _tail(hp, om_p, oa_p, p_prompt[layer].reshape(bp * tp, PLE_DIM),
                     hs, om_s, oa_s, p_sample[layer].reshape(bs * ts, PLE_DIM), w, g_fin)
    k_s = k_new.reshape((1, bs, ts) + kv_shape)
    v_s = v_new.reshape((1, bs, ts) + kv_shape)
    cs, ns, ms = _split_state(c_s, m_s)

    return (y_p.reshape(bp, tp, D_MODEL), y_s.reshape(bs, ts, D_MODEL),
            k_p, v_p, cp[None], np_[None], mp[None],
            k_s, v_s, cs[None], ns[None], ms[None])
```

```python
import functools
import math

import jax
import jax.numpy as jnp
from jax import lax
from jax.experimental import pallas as pl
from jax.experimental.pallas import tpu as pltpu

F32 = jnp.float32
BF16 = jnp.bfloat16
HIGHEST = lax.Precision.HIGHEST

D_MODEL = 2048
M_HEADS = 8
M_V_DIM = 128
M_QK_DIM = 64
M_CHUNK = 64
A_HEADS = 8
A_HEAD_DIM = 128
MOBA_BLOCK = 256
MOBA_TOP_K = 3
NUM_BUCKETS = 32
MAX_EXACT = NUM_BUCKETS // 2
MAX_DISTANCE = 128
D_FF = 4 * D_MODEL
PLE_DIM = 256
PAGE_SIZE = 128
EPS = 1e-6

_MQ, _MK, _MV, _MO = 0, 512, 1024, 2048
_GATES = 3072
_AQ = 3088
D_MAIN = 6144
GATE_PAD = 128

VMEM_LIMIT = 56 * 1024 * 1024
NEG_INF = float("-inf")


def _cparams(*sem):
    return pltpu.CompilerParams(dimension_semantics=sem, vmem_limit_bytes=VMEM_LIMIT)


def _rms(x, g):
    return x * lax.rsqrt(jnp.mean(x * x, axis=-1, keepdims=True) + EPS) * g


def _dot_nt(a, b, **kw):
    return lax.dot_general(a, b, (((1,), (1,)), ((), ())), preferred_element_type=F32, **kw)


def _dot_tn(a, b, **kw):
    return lax.dot_general(a, b, (((0,), (0,)), ((), ())), preferred_element_type=F32, **kw)


def _dot(a, b, **kw):
    return jnp.dot(a, b, preferred_element_type=F32, **kw)


_IN_TN = 1024
_N_IN_TILES = D_MAIN // _IN_TN
_N_MLSTM_TILES = _GATES // _IN_TN
_GATE_COLS = _AQ - _GATES
_CAST_ROWS = 256


def _norm_gate_kernel(x_ref, xs_ref, g_ref, wg_ref, xn_ref, gate_ref, xns_ref, gates_ref):
    wg = wg_ref[...]
    hi = wg.astype(BF16).astype(F32)
    row = lax.broadcasted_iota(jnp.int32, wg.shape, 0)
    w2 = jnp.where(row < _GATE_COLS, hi,
                   jnp.where(row < 2 * _GATE_COLS, pltpu.roll(wg - hi, _GATE_COLS, axis=0), 0.0))
    w2 = w2.astype(BF16)

    def norm_and_gate(src_ref, dst_ref, gate_dst_ref):
        xn = _rms(src_ref[...], g_ref[...]).astype(BF16)
        dst_ref[...] = xn
        r = _dot_nt(xn, w2)
        gate_dst_ref[...] = r + pltpu.roll(r, GATE_PAD - _GATE_COLS, axis=1)

    norm_and_gate(x_ref, xn_ref, gate_ref)

    @pl.when(pl.program_id(0) == 0)
    def _():
        norm_and_gate(xs_ref, xns_ref, gates_ref)


def _norm_gate(x, xs, g, w_in_t):
    m, ms = x.shape[0], xs.shape[0]
    tm = min(512, m)
    return pl.pallas_call(
        _norm_gate_kernel,
        grid=(m // tm,),
        in_specs=[
            pl.BlockSpec((tm, D_MODEL), lambda i: (i, 0)),
            pl.BlockSpec((ms, D_MODEL), lambda i: (0, 0)),
            pl.BlockSpec((1, D_MODEL), lambda i: (0, 0)),
            pl.BlockSpec((GATE_PAD, D_MODEL), lambda i: (_GATES // GATE_PAD, 0)),
        ],
        out_specs=[pl.BlockSpec((tm, D_MODEL), lambda i: (i, 0)),
                   pl.BlockSpec((tm, GATE_PAD), lambda i: (i, 0)),
                   pl.BlockSpec((ms, D_MODEL), lambda i: (0, 0)),
                   pl.BlockSpec((ms, GATE_PAD), lambda i: (0, 0))],
        out_shape=[jax.ShapeDtypeStruct((m, D_MODEL), BF16), jax.ShapeDtypeStruct((m, GATE_PAD), F32),
                   jax.ShapeDtypeStruct((ms, D_MODEL), BF16), jax.ShapeDtypeStruct((ms, GATE_PAD), F32)],
        compiler_params=_cparams("arbitrary"),
        name="norm_gate",
    )(x, xs, g, w_in_t)


def _in_proj_kernel(xn_ref, xs_ref, wa_ref, wb_ref, zm_ref, aq_ref, ak_ref, av_ref,
                    zms_ref, aqs_ref, aks_ref, avs_ref, w_scr):
    j = pl.program_id(0)
    i = pl.program_id(1)

    @pl.when((i == 0) & (j < _N_MLSTM_TILES))
    def _():
        for r in range(0, _IN_TN, _CAST_ROWS):
            w_scr[r:r + _CAST_ROWS, :] = wa_ref[r:r + _CAST_ROWS, :].astype(BF16)

    @pl.when((i == 0) & (j >= _N_MLSTM_TILES))
    def _():
        body = _IN_TN - _GATE_COLS
        for r in range(0, body, _CAST_ROWS):
            n = min(_CAST_ROWS, body - r)
            w_scr[r:r + n, :] = wa_ref[_GATE_COLS + r:_GATE_COLS + r + n, :].astype(BF16)
        w_scr[body:, :] = wb_ref[:_GATE_COLS, :].astype(BF16)

    def project(dst_ref, dst_s_ref):
        dst_ref[...] = _dot_nt(xn_ref[...], w_scr[...])

        @pl.when(i == 0)
        def _():
            dst_s_ref[...] = _dot_nt(xs_ref[...], w_scr[...])

    @pl.when(j < _N_MLSTM_TILES)
    def _():
        project(zm_ref, zms_ref)

    for t, refs in enumerate(((aq_ref, aqs_ref), (ak_ref, aks_ref), (av_ref, avs_ref))):
        @pl.when(j == _N_MLSTM_TILES + t)
        def _(refs=refs):
            project(*refs)


def _in_proj(xn, xs, w_in_t):
    m, ms = xn.shape[0], xs.shape[0]
    tm = min(512, m)
    tn = _IN_TN
    last = m // tm - 1

    def held(j, i, first_tile, last_tile):
        return jnp.where(j < first_tile, 0, jnp.where(j <= last_tile, i, last))

    def section(t):
        return pl.BlockSpec((tm, tn), lambda j, i: (held(j, i, t, t), 0))

    sec_shape = jax.ShapeDtypeStruct((m, tn), F32)
    sec_s = pl.BlockSpec((ms, tn), lambda j, i: (0, 0))
    sec_s_shape = jax.ShapeDtypeStruct((ms, tn), F32)
    nm = _N_MLSTM_TILES
    return pl.pallas_call(
        _in_proj_kernel,
        grid=(_N_IN_TILES, m // tm),
        in_specs=[
            pl.BlockSpec((tm, D_MODEL), lambda j, i: (i, 0)),
            pl.BlockSpec((ms, D_MODEL), lambda j, i: (0, 0)),
            pl.BlockSpec((tn, D_MODEL), lambda j, i: (j, 0)),
            pl.BlockSpec((GATE_PAD, D_MODEL), lambda j, i: ((j + 1) * (tn // GATE_PAD), 0)),
        ],
        out_specs=[
            pl.BlockSpec((tm, tn), lambda j, i: (held(j, i, 0, nm - 1), jnp.minimum(j, nm - 1))),
            section(nm), section(nm + 1), section(nm + 2),
            pl.BlockSpec((ms, tn), lambda j, i: (0, jnp.minimum(j, nm - 1))),
            sec_s, sec_s, sec_s,
        ],
        out_shape=[jax.ShapeDtypeStruct((m, _GATES), F32), sec_shape, sec_shape, sec_shape,
                   jax.ShapeDtypeStruct((ms, _GATES), F32), sec_s_shape, sec_s_shape, sec_s_shape],
        scratch_shapes=[pltpu.VMEM((tn, D_MODEL), BF16)],
        compiler_params=_cparams("arbitrary", "arbitrary"),
        name="in_proj",
    )(xn, xs, w_in_t, w_in_t)


def _log_sigmoid(x):
    return -(jnp.maximum(-x, 0.0) + jnp.log1p(jnp.exp(-jnp.abs(x))))


def _bdot(a, b):
    return lax.dot_general(a, b, (((2,), (1,)), ((0,), (0,))), preferred_element_type=F32)


def _bdot_nt(a, b):
    return lax.dot_general(a, b, (((2,), (2,)), ((0,), (0,))), preferred_element_type=F32)


def _bdot_tn(a, b):
    return lax.dot_general(a, b, (((1,), (1,)), ((0,), (0,))), preferred_element_type=F32)


def _mlstm_kernel(q_ref, k_ref, v_ref, o_ref, g_ref, gb_ref, gh_ref, c0_ref, m0_ref,
                  out_ref, cfin_ref, mfin_ref, c_scr, m_scr, *, chunk, n_chunks):
    L = chunk
    H = M_HEADS
    tb = L * n_chunks
    t = pl.program_id(1)

    @pl.when(t == 0)
    def _():
        c_scr[...] = c0_ref[0]
        m_scr[...] = m0_ref[0]

    g = g_ref[...] + gb_ref[...]
    lane = lax.broadcasted_iota(jnp.int32, (tb, GATE_PAD), 1)
    gl = jnp.where((lane >= H) & (lane < 2 * H), _log_sigmoid(g), g)
    r = lax.broadcasted_iota(jnp.int32, (tb, tb), 0)
    c = lax.broadcasted_iota(jnp.int32, (tb, tb), 1)
    tri = ((r // L == c // L) & (c <= r)).astype(F32)
    cum = _dot(tri, gl, precision=HIGHEST)

    def rows(ci):
        return slice(ci * L, (ci + 1) * L)

    def stack(f):
        return jnp.stack([f(ci, h) for ci in range(n_chunks) for h in range(H)])

    ones_col = (lax.broadcasted_iota(jnp.int32, (L, M_V_DIM), 1) == 0).astype(BF16)
    li = stack(lambda ci, h: gl[rows(ci), h:h + 1])
    b = stack(lambda ci, h: cum[rows(ci), H + h:H + h + 1])
    qb = stack(lambda ci, h: q_ref[rows(ci), h * M_QK_DIM:(h + 1) * M_QK_DIM]).astype(BF16)
    kc = stack(lambda ci, h: k_ref[rows(ci), h * M_QK_DIM:(h + 1) * M_QK_DIM]) * (M_QK_DIM ** -0.5)
    vaug = stack(lambda ci, h: jnp.concatenate(
        [v_ref[rows(ci), h * M_V_DIM:(h + 1) * M_V_DIM].astype(BF16), ones_col], axis=1))

    rl = lax.broadcasted_iota(jnp.int32, (L, L), 0)
    cl = lax.broadcasted_iota(jnp.int32, (L, L), 1)
    w_row = jnp.sum(jnp.where(rl == cl, li - b, 0.0), axis=1, keepdims=True)
    d = jnp.where(cl <= rl, b + w_row, NEG_INF)
    m_loc = jnp.max(d, axis=2, keepdims=True)
    s = _bdot_nt(qb, kc.astype(BF16)) * jnp.exp(d - m_loc)
    nd_loc = _bdot(s.astype(BF16), vaug)
    g_last = m_loc[:, L - 1:L, :]
    b_last = b[:, L - 1:L, :]
    ws = jnp.exp(b_last - b + li - g_last)
    dc_loc = _bdot_tn((kc * ws).astype(BF16), vaug)

    m_prev = m_scr[:, :, 0:1]
    c_aug = c_scr[...]
    for ci in range(n_chunks):
        grp = slice(ci * H, (ci + 1) * H)
        a = b[grp] + m_prev
        m_t = jnp.maximum(a, m_loc[grp])
        nd = (jnp.exp(a - m_t) * _bdot(qb[grp], c_aug.astype(BF16))
              + jnp.exp(m_loc[grp] - m_t) * nd_loc[grp])
        hh = nd[:, :, :M_V_DIM] / jnp.maximum(jnp.abs(nd[:, :, M_V_DIM:M_V_DIM + 1]), jnp.exp(-m_t))
        hn = _rms(hh, gh_ref[...])
        for h in range(H):
            og = jax.nn.sigmoid(o_ref[rows(ci), h * M_V_DIM:(h + 1) * M_V_DIM])
            out_ref[rows(ci), h * M_V_DIM:(h + 1) * M_V_DIM] = (og * hn[h]).astype(out_ref.dtype)
        m_new = jnp.maximum(b_last[grp] + m_prev, g_last[grp])
        c_aug = (jnp.exp(b_last[grp] + m_prev - m_new) * c_aug
                 + jnp.exp(g_last[grp] - m_new) * dc_loc[grp])
        m_prev = m_new
    c_scr[...] = c_aug
    m_scr[...] = jnp.broadcast_to(m_prev, m_scr.shape)

    @pl.when(t == pl.num_programs(1) - 1)
    def _():
        cfin_ref[0] = c_scr[...]
        mfin_ref[0] = m_scr[...]


def _mlstm(z, gates, gate_bias, g_mhead, c0_aug, m0, batch, seq):
    L = math.gcd(seq, M_CHUNK)
    n_chunks = max(1, min(seq // L, 256 // L))
    tb = L * n_chunks
    nt = seq // tb
    row = lambda b, t: b * nt + t
    kern = functools.partial(_mlstm_kernel, chunk=L, n_chunks=n_chunks)
    return pl.pallas_call(
        kern,
        grid=(batch, nt),
        in_specs=[
            pl.BlockSpec((tb, 512), lambda b, t: (row(b, t), _MQ // 512)),
            pl.BlockSpec((tb, 512), lambda b, t: (row(b, t), _MK // 512)),
            pl.BlockSpec((tb, 1024), lambda b, t: (row(b, t), _MV // 1024)),
            pl.BlockSpec((tb, 1024), lambda b, t: (row(b, t), _MO // 1024)),
            pl.BlockSpec((tb, GATE_PAD), lambda b, t: (row(b, t), 0)),
            pl.BlockSpec((1, GATE_PAD), lambda b, t: (0, 0)),
            pl.BlockSpec((M_HEADS, 1, M_V_DIM), lambda b, t: (0, 0, 0)),
            pl.BlockSpec((1, M_HEADS, M_QK_DIM, 2 * M_V_DIM), lambda b, t: (b, 0, 0, 0)),
            pl.BlockSpec((1, M_HEADS, 1, GATE_PAD), lambda b, t: (b, 0, 0, 0)),
        ],
        out_specs=[
            pl.BlockSpec((tb, M_HEADS * M_V_DIM), lambda b, t: (row(b, t), 0)),
            pl.BlockSpec((1, M_HEADS, M_QK_DIM, 2 * M_V_DIM), lambda b, t: (b, 0, 0, 0)),
            pl.BlockSpec((1, M_HEADS, 1, GATE_PAD), lambda b, t: (b, 0, 0, 0)),
        ],
        out_shape=[
            jax.ShapeDtypeStruct((batch * seq, M_HEADS * M_V_DIM), BF16 if tb % 16 == 0 else F32),
            jax.ShapeDtypeStruct((batch, M_HEADS, M_QK_DIM, 2 * M_V_DIM), F32),
            jax.ShapeDtypeStruct((batch, M_HEADS, 1, GATE_PAD), F32),
        ],
        scratch_shapes=[pltpu.VMEM((M_HEADS, M_QK_DIM, 2 * M_V_DIM), F32),
                        pltpu.VMEM((M_HEADS, 1, GATE_PAD), F32)],
        compiler_params=_cparams("parallel", "arbitrary"),
        name="mlstm",
    )(z, z, z, z, gates, gate_bias, g_mhead.reshape(M_HEADS, 1, M_V_DIM), c0_aug, m0)


def _t5_bucket(rel):
    n = jnp.maximum(rel, 0)
    nf = jnp.maximum(n, 1).astype(F32)
    large = MAX_EXACT + (jnp.log(nf / MAX_EXACT) / math.log(MAX_DISTANCE / MAX_EXACT)
                         * (NUM_BUCKETS - MAX_EXACT)).astype(jnp.int32)
    large = jnp.minimum(large, NUM_BUCKETS - 1)
    return jnp.where(n < MAX_EXACT, n, large)


def _bias_kernel(tab_ref, out_ref):
    h = pl.program_id(0)
    i = lax.broadcasted_iota(jnp.int32, (MOBA_BLOCK, MOBA_BLOCK), 1)
    j = lax.broadcasted_iota(jnp.int32, (MOBA_BLOCK, MOBA_BLOCK), 0)
    for kind, rel in ((0, i - j), (1, MOBA_BLOCK + i - j)):
        bucket = _t5_bucket(rel)
        acc = jnp.zeros((MOBA_BLOCK, MOBA_BLOCK), F32)
        for b in range(NUM_BUCKETS):
            acc = jnp.where(bucket == b, tab_ref[h, b], acc)
        out_ref[0, kind] = acc
    out_ref[0, 2] = jnp.full((MOBA_BLOCK, MOBA_BLOCK), tab_ref[h, NUM_BUCKETS - 1], F32)


def _bias_tiles(rel_table):
    assert MOBA_BLOCK + 1 >= MAX_DISTANCE
    tab = rel_table.T.astype(F32)
    return pl.pallas_call(
        _bias_kernel,
        grid=(A_HEADS,),
        in_specs=[pl.BlockSpec(memory_space=pltpu.SMEM)],
        out_specs=pl.BlockSpec((1, 3, MOBA_BLOCK, MOBA_BLOCK), lambda h: (h, 0, 0, 0)),
        out_shape=jax.ShapeDtypeStruct((A_HEADS, 3, MOBA_BLOCK, MOBA_BLOCK), F32),
        compiler_params=_cparams("parallel"),
        name="t5_bias_tiles",
    )(tab)


def _moba_prompt_kernel(q_ref, k_ref, v_ref, bias_ref, wu_ref, wd_ref, o_ref, wub_ref, wdb_ref,
                        kb_scr, qt_scr, vt_scr, gt_scr, s_scr, *, n_blocks):
    nb = n_blocks
    blk = MOBA_BLOCK
    nbp = gt_scr.shape[0]
    wub_ref[...] = wu_ref[...].astype(BF16)
    wdb_ref[...] = wd_ref[...].astype(BF16)

    kb_scr[...] = k_ref[...].astype(BF16)
    for c in range(nb):
        cols = slice(c * blk, (c + 1) * blk)
        qt_scr[:, cols] = (q_ref[cols, :] * (A_HEAD_DIM ** -0.5)).T.astype(BF16)
        vt_scr[:, cols] = v_ref[cols, :].T.astype(BF16)
    need_gate = nb - 1 > MOBA_TOP_K
    if need_gate:
        means = [jnp.mean(k_ref[n * blk:(n + 1) * blk, :], axis=0, keepdims=True) for n in range(nb)]
        kmean = jnp.concatenate(means + [jnp.zeros((128 - nb, A_HEAD_DIM), F32)], axis=0)
        for c in range(MOBA_TOP_K + 1, nb):
            cols = slice(c * blk, (c + 1) * blk)
            gate = _dot_nt(q_ref[cols, :] * (A_HEAD_DIM ** -0.5), kmean, precision=HIGHEST)
            gt_scr[:, cols] = gate.T[:nbp, :]
    key = lax.broadcasted_iota(jnp.int32, (blk, blk), 0)
    qry = lax.broadcasted_iota(jnp.int32, (blk, blk), 1)
    causal = key <= qry
    blk_id = lax.broadcasted_iota(jnp.int32, (nbp, blk), 0)

    for i in range(nb):
        cols = slice(i * blk, (i + 1) * blk)
        selneg = None
        if i > MOBA_TOP_K:
            g = gt_scr[:, cols]
            rank = jnp.zeros((nbp, blk), F32)
            for j in range(i):
                gj = g[j:j + 1, :]
                rank = rank + ((gj > g) | ((gj == g) & (j < blk_id))).astype(F32)
            selneg = jnp.where(rank < MOBA_TOP_K, 0.0, NEG_INF)
        s_scr[0:(i + 1) * blk, :] = _dot(kb_scr[0:(i + 1) * blk, :], qt_scr[:, cols])
        m = jnp.full((1, blk), NEG_INF, F32)
        for n in range(i + 1):
            rows = slice(n * blk, (n + 1) * blk)
            kind = 0 if n == i else (1 if n == i - 1 else 2)
            st = s_scr[rows, :] + bias_ref[0, kind]
            if n == i:
                st = jnp.where(causal, st, NEG_INF)
            elif selneg is not None:
                st = st + selneg[n:n + 1, :]
            s_scr[rows, :] = st
            m = jnp.maximum(m, jnp.max(st, axis=0, keepdims=True))
        l = jnp.zeros((1, blk), F32)
        acc = jnp.zeros((A_HEAD_DIM, blk), F32)
        for n in range(i + 1):
            rows = slice(n * blk, (n + 1) * blk)
            p = jnp.exp(s_scr[rows, :] - m)
            l = l + jnp.sum(p, axis=0, keepdims=True)
            acc = acc + _dot(vt_scr[:, rows], p.astype(BF16))
        o_ref[cols, :] = (acc / l).T.astype(o_ref.dtype)


def _moba_prompt(aq, ak, av, bias_tiles, w_up, w_down, batch, seq):
    assert seq % MOBA_BLOCK == 0
    nb = seq // MOBA_BLOCK
    assert nb <= 128
    nbp = -(-nb // 8) * 8
    steps = batch * A_HEADS
    slab = D_FF // steps
    assert D_FF % steps == 0 and slab % 128 == 0
    head = pl.BlockSpec((seq, A_HEAD_DIM), lambda b, h: (b, h))
    up_slab = pl.BlockSpec((D_MODEL, slab), lambda b, h: (0, b * A_HEADS + h))
    down_slab = pl.BlockSpec((slab, D_MODEL), lambda b, h: (b * A_HEADS + h, 0))
    kern = functools.partial(_moba_prompt_kernel, n_blocks=nb)
    return pl.pallas_call(
        kern,
        grid=(batch, A_HEADS),
        in_specs=[head, head, head,
                  pl.BlockSpec((1, 3, MOBA_BLOCK, MOBA_BLOCK), lambda b, h: (h, 0, 0, 0)),
                  up_slab, down_slab],
        out_specs=[head, up_slab, down_slab],
        out_shape=[jax.ShapeDtypeStruct((batch * seq, A_HEADS * A_HEAD_DIM), BF16),
                   jax.ShapeDtypeStruct((D_MODEL, D_FF), BF16),
                   jax.ShapeDtypeStruct((D_FF, D_MODEL), BF16)],
        scratch_shapes=[pltpu.VMEM((seq, A_HEAD_DIM), BF16),
                        pltpu.VMEM((A_HEAD_DIM, seq), BF16),
                        pltpu.VMEM((A_HEAD_DIM, seq), BF16),
                        pltpu.VMEM((nbp, seq), F32),
                        pltpu.VMEM((seq, MOBA_BLOCK), F32)],
        compiler_params=_cparams("parallel", "parallel"),
        name="moba_prompt",
    )(aq, ak, av, bias_tiles, w_up, w_down)


PAGES_PER_BLOCK = MOBA_BLOCK // PAGE_SIZE


def _past_blocks(q_ref, k_refs, v_refs, bias_ref, o_ref, m_ref, l_ref, km_ref, is_last_group):
    hq = q_ref.shape[1]
    nq = hq // A_HEADS
    qb = (q_ref[0] * (A_HEAD_DIM ** -0.5)).reshape(A_HEADS, nq, A_HEAD_DIM).astype(BF16)
    n_blocks = len(k_refs) // PAGES_PER_BLOCK

    def head_rows(ref, h):
        return ref[0, 0, pl.ds(h, PAGE_SIZE, stride=A_HEADS), :]

    for g in range(n_blocks):
        pages = range(g * PAGES_PER_BLOCK, (g + 1) * PAGES_PER_BLOCK)
        ksum = sum(jnp.sum(k_refs[p][0, 0].reshape(PAGE_SIZE, A_HEADS, A_HEAD_DIM), axis=0) for p in pages)
        km_ref[0, g] = ksum / MOBA_BLOCK
        kind = jnp.where(is_last_group, 1, 0) if g == n_blocks - 1 else 0
        kh = jnp.stack([jnp.concatenate([head_rows(k_refs[p], h) for p in pages], axis=0)
                        for h in range(A_HEADS)]).astype(BF16)
        vh = jnp.stack([jnp.concatenate([head_rows(v_refs[p], h) for p in pages], axis=0)
                        for h in range(A_HEADS)]).astype(BF16)
        lt = _bdot_nt(qb, kh) + bias_ref[kind]
        m = jnp.max(lt, axis=2, keepdims=True)
        p_ = jnp.exp(lt - m)
        l = jnp.sum(p_, axis=2, keepdims=True)
        o_ref[0, g] = _bdot(p_.astype(BF16), vh).reshape(hq, A_HEAD_DIM)
        m_ref[0, g] = jnp.broadcast_to(m, (A_HEADS, nq, A_HEAD_DIM)).reshape(hq, A_HEAD_DIM)
        l_ref[0, g] = jnp.broadcast_to(l, (A_HEADS, nq, A_HEAD_DIM)).reshape(hq, A_HEAD_DIM)


def _moba_combine_kernel(q_ref, km_ref, kn_ref, vn_ref, ob_ref, op_ref, mp_ref, lp_ref, out_ref):
    nbp = km_ref.shape[2]
    q = q_ref[0] * (A_HEAD_DIM ** -0.5)
    hq = q.shape[0]
    nq = hq // A_HEADS
    gate = jnp.concatenate(
        [_dot_nt(q[h * nq:(h + 1) * nq], km_ref[0, h], precision=HIGHEST) for h in range(A_HEADS)], axis=0)
    lane = lax.broadcasted_iota(jnp.int32, (hq, nbp), 1)
    rank = jnp.zeros((hq, nbp), F32)
    for j in range(nbp):
        gj = gate[:, j:j + 1]
        beats = (gj > gate) | ((gj == gate) & (j < lane))
        rank = rank + beats.astype(F32)
    selneg = jnp.where(rank < MOBA_TOP_K, 0.0, NEG_INF)
    qb = q.astype(BF16)
    lt = _dot_nt(qb, kn_ref[0].astype(BF16))
    r = lax.broadcasted_iota(jnp.int32, lt.shape, 0)
    c = lax.broadcasted_iota(jnp.int32, lt.shape, 1)
    ok = ((c % A_HEADS) == (r // nq)) & ((c // A_HEADS) <= (r % nq))
    lo = jnp.where(ok, lt + ob_ref[...], NEG_INF)
    m = jnp.max(lo, axis=1, keepdims=True)
    for n in range(nbp):
        m = jnp.maximum(m, mp_ref[0, n][:, 0:1] + selneg[:, n:n + 1])
    p = jnp.exp(lo - m)
    l = jnp.sum(p, axis=1, keepdims=True)
    acc = _dot(p.astype(BF16), vn_ref[0].astype(BF16))
    for n in range(nbp):
        w = jnp.exp(mp_ref[0, n][:, 0:1] + selneg[:, n:n + 1] - m)
        l = l + w * lp_ref[0, n][:, 0:1]
        acc = acc + w * op_ref[0, n]
    out_ref[0] = acc / l


def _moba_combine(q_hq, kmean_hn, k_new, v_new, own_bias, o_part, m_part, l_part):
    bs, hq, _ = q_hq.shape
    nbp = o_part.shape[1]
    part = pl.BlockSpec((1, nbp, hq, A_HEAD_DIM), lambda b: (b, 0, 0, 0))
    tok = pl.BlockSpec((1, hq, A_HEAD_DIM), lambda b: (b, 0, 0))
    return pl.pallas_call(
        _moba_combine_kernel,
        grid=(bs,),
        in_specs=[tok,
                  pl.BlockSpec((1, A_HEADS, nbp, A_HEAD_DIM), lambda b: (b, 0, 0, 0)),
                  tok, tok,
                  pl.BlockSpec((hq, hq), lambda b: (0, 0)),
                  part, part, part],
        out_specs=tok,
        out_shape=jax.ShapeDtypeStruct((bs, hq, A_HEAD_DIM), F32),
        compiler_params=_cparams("parallel"),
        name="moba_combine",
    )(q_hq, kmean_hn, k_new, v_new, own_bias, o_part, m_part, l_part)


def _out_proj_kernel(h_ref, om_ref, oa_ref, wm_ref, wa_ref, o_ref, wm_scr, wa_scr):
    @pl.when(pl.program_id(1) == 0)
    def _():
        for r in range(0, wm_scr.shape[0], _CAST_ROWS):
            wm_scr[r:r + _CAST_ROWS, :] = wm_ref[r:r + _CAST_ROWS, :].astype(BF16)
            wa_scr[r:r + _CAST_ROWS, :] = wa_ref[r:r + _CAST_ROWS, :].astype(BF16)

    o_ref[...] = (h_ref[...] + _dot(om_ref[...].astype(BF16), wm_scr[...])
                  + _dot(oa_ref[...].astype(BF16), wa_scr[...]))


def _out_proj(h, out_m, out_a, w_out):
    m = h.shape[0]
    tm = min(512, m)
    tn = 1024
    half = M_HEADS * M_V_DIM
    return pl.pallas_call(
        _out_proj_kernel,
        grid=(D_MODEL // tn, m // tm),
        in_specs=[
            pl.BlockSpec((tm, tn), lambda j, i: (i, j)),
            pl.BlockSpec((tm, half), lambda j, i: (i, 0)),
            pl.BlockSpec((tm, half), lambda j, i: (i, 0)),
            pl.BlockSpec((half, tn), lambda j, i: (0, j)),
            pl.BlockSpec((half, tn), lambda j, i: (1, j)),
        ],
        out_specs=pl.BlockSpec((tm, tn), lambda j, i: (i, j)),
        out_shape=jax.ShapeDtypeStruct((m, D_MODEL), F32),
        scratch_shapes=[pltpu.VMEM((half, tn), BF16), pltpu.VMEM((half, tn), BF16)],
        compiler_params=_cparams("arbitrary", "arbitrary"),
        name="out_proj",
    )(h, out_m, out_a, w_out, w_out)


def _mlp_accumulate(h_ref, g_ref, wu_ref, wd_ref, o_ref, xn_ref):
    @pl.when(pl.program_id(1) == 0)
    def _():
        xn_ref[...] = _rms(h_ref[...], g_ref[...]).astype(BF16)
        o_ref[...] = h_ref[...]

    u = _dot(xn_ref[...], wu_ref[...])
    a = jnp.square(jnp.maximum(u, 0.0)).astype(BF16)
    o_ref[...] += _dot(a, wd_ref[...])


def _mlp_specs(m, imap):
    tm = min(512, m)
    tf = 1024
    in_specs = [
        pl.BlockSpec((tm, D_MODEL), imap(lambda i, f: (i, 0)), pipeline_mode=pl.Buffered(1)),
        pl.BlockSpec((1, D_MODEL), imap(lambda i, f: (0, 0))),
        pl.BlockSpec((D_MODEL, tf), imap(lambda i, f: (0, f))),
        pl.BlockSpec((tf, D_MODEL), imap(lambda i, f: (f, 0))),
    ]
    out_spec = pl.BlockSpec((tm, D_MODEL), imap(lambda i, f: (i, 0)))
    return (m // tm, D_FF // tf), in_specs, out_spec, pltpu.VMEM((tm, D_MODEL), BF16)


def _mlp(h, g, w_up, w_down):
    m = h.shape[0]
    grid, in_specs, out_spec, xn_scr = _mlp_specs(m, lambda f: f)
    return pl.pallas_call(
        _mlp_accumulate,
        grid=grid,
        in_specs=in_specs,
        out_specs=out_spec,
        out_shape=jax.ShapeDtypeStruct((m, D_MODEL), F32),
        scratch_shapes=[xn_scr],
        compiler_params=_cparams("parallel", "arbitrary"),
        name="mlp",
    )(h, g, w_up, w_down)


def _mlp_past_kernel(pt_ref, h_ref, g_ref, wu_ref, wd_ref, q_ref, *refs, n_pages_step, n_steps, n_groups,
                     groups_per_batch):
    del pt_ref
    k_refs = refs[:n_pages_step]
    v_refs = refs[n_pages_step:2 * n_pages_step]
    bias_ref, o_ref, op_ref, mp_ref, lp_ref, km_ref, xn_ref = refs[2 * n_pages_step:]
    _mlp_accumulate(h_ref, g_ref, wu_ref, wd_ref, o_ref, xn_ref)

    step = pl.program_id(0) * pl.num_programs(1) + pl.program_id(1)

    def past_blocks():
        is_last_group = step % groups_per_batch == groups_per_batch - 1
        _past_blocks(q_ref, k_refs, v_refs, bias_ref, op_ref, mp_ref, lp_ref, km_ref, is_last_group)

    if n_groups == n_steps:
        past_blocks()
    else:
        pl.when(step < n_groups)(past_blocks)


def _mlp_past(h, g, w_up, w_down, page_table, q_hq, cache_k, cache_v, past_bias, layer):
    m = h.shape[0]
    bs, n_pages = page_table.shape
    assert n_pages % PAGES_PER_BLOCK == 0
    nbp = n_pages // PAGES_PER_BLOCK
    hq = q_hq.shape[1]
    grid, mlp_in, mlp_out, xn_scr = _mlp_specs(m, lambda f: (lambda i, ff, pt: f(i, ff)))
    steps = grid[0] * grid[1]
    gb = next(d for d in range(1, nbp + 1) if nbp % d == 0 and bs * (nbp // d) <= steps)
    gpb = nbp // gb
    n_groups = bs * gpb
    pps = gb * PAGES_PER_BLOCK
    page = (1, 1, PAGE_SIZE * A_HEADS, A_HEAD_DIM)
    cache_k, cache_v = (c.reshape(c.shape[:2] + page[2:]) for c in (cache_k, cache_v))

    def group(i, f):
        s = jnp.minimum(i * grid[1] + f, n_groups - 1)
        return s // gpb, s % gpb

    def page_spec(p):
        def imap(i, f, pt):
            b, n = group(i, f)
            return (layer, pt[b, pps * n + p], 0, 0)
        return pl.BlockSpec(page, imap)

    def part_map(i, f, pt):
        b, n = group(i, f)
        return (b, n, 0, 0)

    part = pl.BlockSpec((1, gb, hq, A_HEAD_DIM), part_map)
    grid_spec = pltpu.PrefetchScalarGridSpec(
        num_scalar_prefetch=1,
        grid=grid,
        in_specs=mlp_in
        + [pl.BlockSpec((1, hq, A_HEAD_DIM), lambda i, f, pt: (group(i, f)[0], 0, 0))]
        + [page_spec(p) for p in range(pps)] * 2
        + [pl.BlockSpec((2, A_HEADS, hq // A_HEADS, MOBA_BLOCK), lambda i, f, pt: (0, 0, 0, 0))],
        out_specs=[mlp_out, part, part, part, pl.BlockSpec((1, gb, A_HEADS, A_HEAD_DIM), part_map)],
        scratch_shapes=[xn_scr],
    )
    pshape = jax.ShapeDtypeStruct((bs, nbp, hq, A_HEAD_DIM), F32)
    kern = functools.partial(_mlp_past_kernel, n_pages_step=pps, n_steps=steps, n_groups=n_groups,
                             groups_per_batch=gpb)
    return pl.pallas_call(
        kern,
        grid_spec=grid_spec,
        out_shape=[jax.ShapeDtypeStruct((m, D_MODEL), F32), pshape, pshape, pshape,
                   jax.ShapeDtypeStruct((bs, nbp, A_HEADS, A_HEAD_DIM), F32)],
        compiler_params=_cparams("arbitrary", "arbitrary"),
        name="mlp_with_past_blocks",
    )(page_table, h, g, w_up, w_down, q_hq, *([cache_k] * pps), *([cache_v] * pps), past_bias)


def _ple_kernel(h_ref, pe_ref, hs_ref, pes_ref, g_ref, wg_ref, wp_ref, gf_ref, y_ref, ys_ref):
    def gated_embedding(src, pe, dst):
        h = src[...]
        gate = jax.nn.sigmoid(_dot(_rms(h, g_ref[...]).astype(BF16), wg_ref[...]))
        proj = _dot(pe[...].astype(BF16), wp_ref[...])
        dst[...] = _rms(h + gate * proj, gf_ref[...])

    gated_embedding(h_ref, pe_ref, y_ref)

    @pl.when(pl.program_id(0) == 0)
    def _():
        gated_embedding(hs_ref, pes_ref, ys_ref)


def _ple_final(h, pe, hs, pes, g_ple, w_gate, w_proj, g_final):
    m, ms = h.shape[0], hs.shape[0]
    tm = min(512, m)
    vec = pl.BlockSpec((1, D_MODEL), lambda i: (0, 0))
    return pl.pallas_call(
        _ple_kernel,
        grid=(m // tm,),
        in_specs=[
            pl.BlockSpec((tm, D_MODEL), lambda i: (i, 0)),
            pl.BlockSpec((tm, PLE_DIM), lambda i: (i, 0)),
            pl.BlockSpec((ms, D_MODEL), lambda i: (0, 0)),
            pl.BlockSpec((ms, PLE_DIM), lambda i: (0, 0)),
            vec,
            pl.BlockSpec((D_MODEL, D_MODEL), lambda i: (0, 0)),
            pl.BlockSpec((PLE_DIM, D_MODEL), lambda i: (0, 0)),
            vec,
        ],
        out_specs=[pl.BlockSpec((tm, D_MODEL), lambda i: (i, 0)),
                   pl.BlockSpec((ms, D_MODEL), lambda i: (0, 0))],
        out_shape=[jax.ShapeDtypeStruct((m, D_MODEL), F32), jax.ShapeDtypeStruct((ms, D_MODEL), F32)],
        compiler_params=_cparams("arbitrary"),
        name="ple_final_norm",
    )(h, pe, hs, pes, g_ple, w_gate, w_proj, g_final)


def _layer_weights(w_in, b_igate, b_fgate, g_mix, g_mhead, w_out, g_ffn, w_up, w_down,
                   g_ple, w_ple_gate, w_ple_proj):
    gate_bias = jnp.pad(jnp.concatenate([b_igate, b_fgate]), (0, GATE_PAD - 2 * M_HEADS)).reshape(1, GATE_PAD)
    return dict(
        w_in=w_in.T, gate_bias=gate_bias.astype(F32),
        g_mix=g_mix.reshape(1, D_MODEL), g_mhead=g_mhead, w_out=w_out,
        g_ffn=g_ffn.reshape(1, D_MODEL), w_up=w_up, w_down=w_down,
        g_ple=g_ple.reshape(1, D_MODEL), w_ple_gate=w_ple_gate.astype(BF16),
        w_ple_proj=w_ple_proj.astype(BF16))


def _split_state(c_aug, m_fin):
    return c_aug[..., :M_V_DIM], c_aug[..., M_V_DIM], m_fin[..., 0, 0]


def kernel(x_prompt, x_sample, cache_k, cache_v, state_C, state_n, state_m, page_table, p_prompt, p_sample, rel_bias_table, w_in, b_igate, b_fgate, g_mix, g_mhead, w_out, g_ffn, w_up, w_down, g_ple, w_ple_gate, w_ple_proj, g_final):
    depth = w_in.shape[0]
    assert depth == 1, "one decoder layer per call"
    layer = 0
    bp, tp, _ = x_prompt.shape
    bs, ts, _ = x_sample.shape
    past_len = page_table.shape[1] * PAGE_SIZE
    assert past_len % MOBA_BLOCK == 0 and ts <= MOBA_BLOCK
    kv_shape = (A_HEADS, A_HEAD_DIM)
    g_fin = g_final.reshape(1, D_MODEL)

    w = _layer_weights(w_in[layer], b_igate[layer], b_fgate[layer], g_mix[layer], g_mhead[layer], w_out[layer],
                       g_ffn[layer], w_up[layer], w_down[layer], g_ple[layer], w_ple_gate[layer],
                       w_ple_proj[layer])
    bias_tiles = _bias_tiles(rel_bias_table)

    hp = x_prompt.reshape(bp * tp, D_MODEL)
    hs = x_sample.reshape(bs * ts, D_MODEL)
    xn_p, gp, xn_s, gs = _norm_gate(hp, hs, w["g_mix"], w["w_in"])
    zp, aq_p, ak_p, av_p, zs, aq_s, ak_s, av_s = _in_proj(xn_p, xn_s, w["w_in"])
    c0 = jnp.zeros((bp, M_HEADS, M_QK_DIM, 2 * M_V_DIM), F32)
    m0 = jnp.zeros((bp, M_HEADS, 1, GATE_PAD), F32)
    om_p, c_p, m_p = _mlstm(zp, gp, w["gate_bias"], w["g_mhead"], c0, m0, bp, tp)
    oa_p, w["w_up_bf16"], w["w_down_bf16"] = _moba_prompt(aq_p, ak_p, av_p, bias_tiles, w["w_up"], w["w_down"],
                                                          bp, tp)
    k_p = ak_p.reshape((1, bp, tp) + kv_shape)
    v_p = av_p.reshape((1, bp, tp) + kv_shape)
    cp, np_, mp = _split_state(c_p, m_p)

    c0s = jnp.concatenate([state_C[layer], state_n[layer][..., None],
                           jnp.zeros((bs, M_HEADS, M_QK_DIM, M_V_DIM - 1), F32)], axis=-1)
    m0s = jnp.broadcast_to(state_m[layer][..., None, None], (bs, M_HEADS, 1, GATE_PAD))
    om_s, c_s, m_s = _mlstm(zs, gs, w["gate_bias"], w["g_mhead"], c0s, m0s, bs, ts)

    hq = A_HEADS * ts
    q_s = aq_s.reshape(bs, ts, A_HEADS, A_HEAD_DIM)
    q_hq = q_s.transpose(0, 2, 1, 3).reshape(bs, hq, A_HEAD_DIM)
    k_new = ak_s.reshape(bs, hq, A_HEAD_DIM)
    v_new = av_s.reshape(bs, hq, A_HEAD_DIM)
    rows = bias_tiles[:, :, :, :ts].transpose(0, 1, 3, 2)
    expand = lambda a: jnp.repeat(a, A_HEADS, axis=-1).reshape(hq, -1)
    past_bias = jnp.stack([rows[:, 2], rows[:, 1]])
    own_bias = expand(rows[:, 0, :, :ts])
    h1_p = _out_proj(hp, om_p, oa_p, w["w_out"])
    h2_p, o_part, m_part, l_part, kmean = _mlp_past(h1_p, w["g_ffn"], w["w_up_bf16"], w["w_down_bf16"],
                                                    page_table, q_hq, cache_k, cache_v, past_bias, layer)
    oa_hq = _moba_combine(q_hq, kmean.transpose(0, 2, 1, 3), k_new, v_new, own_bias, o_part, m_part, l_part)
    oa_s = oa_hq.reshape(bs, A_HEADS, ts, A_HEAD_DIM).transpose(0, 2, 1, 3).reshape(bs * ts, A_HEADS * A_HEAD_DIM)
    h2_s = _mlp(_out_proj(hs, om_s, oa_s, w["w_out"]), w["g_ffn"], w["w_up_bf16"], w["w_down_bf16"])
    y_p, y_s = _ple_final(h2_p, p_prompt[layer].reshape(bp * tp, PLE_DIM),
                          h2_s, p_sample[layer].reshape(bs * ts, PLE_DIM),
                          w["g_ple"], w["w_ple_gate"], w["w_ple_proj"], g_fin)
    k_s = k_new.reshape((1, bs, ts) + kv_shape)
    v_s = v_new.reshape((1, bs, ts) + kv_shape)
    cs, ns, ms = _split_state(c_s, m_s)

    return (y_p.reshape(bp, tp, D_MODEL), y_s.reshape(bs, ts, D_MODEL),
            k_p, v_p, cp[None], np_[None], mp[None],
            k_s, v_s, cs[None], ns[None], ms[None])
```

```python
import functools
import math

import jax
import jax.numpy as jnp
from jax import lax
from jax.experimental import pallas as pl
from jax.experimental.pallas import tpu as pltpu

F32 = jnp.float32
BF16 = jnp.bfloat16
HIGHEST = lax.Precision.HIGHEST

D_MODEL = 2048
M_HEADS = 8
M_V_DIM = 128
M_QK_DIM = 64
M_CHUNK = 64
A_HEADS = 8
A_HEAD_DIM = 128
MOBA_BLOCK = 256
MOBA_TOP_K = 3
NUM_BUCKETS = 32
MAX_EXACT = NUM_BUCKETS // 2
MAX_DISTANCE = 128
D_FF = 4 * D_MODEL
PLE_DIM = 256
PAGE_SIZE = 128
EPS = 1e-6

_MQ, _MK, _MV, _MO = 0, 512, 1024, 2048
_GATES = 3072
_AQ = 3088
D_MAIN = 6144
GATE_PAD = 128

VMEM_LIMIT = 56 * 1024 * 1024
NEG_INF = float("-inf")


def _cparams(*sem):
    return pltpu.CompilerParams(dimension_semantics=sem, vmem_limit_bytes=VMEM_LIMIT)


def _rms(x, g):
    return x * lax.rsqrt(jnp.mean(x * x, axis=-1, keepdims=True) + EPS) * g


def _dot_nt(a, b, **kw):
    return lax.dot_general(a, b, (((1,), (1,)), ((), ())), preferred_element_type=F32, **kw)


def _dot_tn(a, b, **kw):
    return lax.dot_general(a, b, (((0,), (0,)), ((), ())), preferred_element_type=F32, **kw)


def _dot(a, b, **kw):
    return jnp.dot(a, b, preferred_element_type=F32, **kw)


_IN_TN = 1024
_N_IN_TILES = D_MAIN // _IN_TN
_N_MLSTM_TILES = _GATES // _IN_TN
_GATE_COLS = _AQ - _GATES
_CAST_ROWS = 256


def _norm_gate_kernel(x_ref, xs_ref, g_ref, wg_ref, xn_ref, gate_ref, xns_ref, gates_ref):
    wg = wg_ref[...]
    hi = wg.astype(BF16).astype(F32)
    row = lax.broadcasted_iota(jnp.int32, wg.shape, 0)
    w2 = jnp.where(row < _GATE_COLS, hi,
                   jnp.where(row < 2 * _GATE_COLS, pltpu.roll(wg - hi, _GATE_COLS, axis=0), 0.0))
    w2 = w2.astype(BF16)

    def norm_and_gate(src_ref, dst_ref, gate_dst_ref):
        xn = _rms(src_ref[...], g_ref[...]).astype(BF16)
        dst_ref[...] = xn
        r = _dot_nt(xn, w2)
        gate_dst_ref[...] = r + pltpu.roll(r, GATE_PAD - _GATE_COLS, axis=1)

    norm_and_gate(x_ref, xn_ref, gate_ref)

    @pl.when(pl.program_id(0) == 0)
    def _():
        norm_and_gate(xs_ref, xns_ref, gates_ref)


def _norm_gate(x, xs, g, w_in_t):
    m, ms = x.shape[0], xs.shape[0]
    tm = min(512, m)
    return pl.pallas_call(
        _norm_gate_kernel,
        grid=(m // tm,),
        in_specs=[
            pl.BlockSpec((tm, D_MODEL), lambda i: (i, 0)),
            pl.BlockSpec((ms, D_MODEL), lambda i: (0, 0)),
            pl.BlockSpec((1, D_MODEL), lambda i: (0, 0)),
            pl.BlockSpec((GATE_PAD, D_MODEL), lambda i: (_GATES // GATE_PAD, 0)),
        ],
        out_specs=[pl.BlockSpec((tm, D_MODEL), lambda i: (i, 0)),
                   pl.BlockSpec((tm, GATE_PAD), lambda i: (i, 0)),
                   pl.BlockSpec((ms, D_MODEL), lambda i: (0, 0)),
                   pl.BlockSpec((ms, GATE_PAD), lambda i: (0, 0))],
        out_shape=[jax.ShapeDtypeStruct((m, D_MODEL), BF16), jax.ShapeDtypeStruct((m, GATE_PAD), F32),
                   jax.ShapeDtypeStruct((ms, D_MODEL), BF16), jax.ShapeDtypeStruct((ms, GATE_PAD), F32)],
        compiler_params=_cparams("arbitrary"),
        name="norm_gate",
    )(x, xs, g, w_in_t)


def _in_proj_kernel(xn_ref, xs_ref, wa_ref, wb_ref, zm_ref, aq_ref, ak_ref, av_ref,
                    zms_ref, aqs_ref, aks_ref, avs_ref, w_scr):
    j = pl.program_id(0)
    i = pl.program_id(1)

    @pl.when((i == 0) & (j < _N_MLSTM_TILES))
    def _():
        for r in range(0, _IN_TN, _CAST_ROWS):
            w_scr[r:r + _CAST_ROWS, :] = wa_ref[r:r + _CAST_ROWS, :].astype(BF16)

    @pl.when((i == 0) & (j >= _N_MLSTM_TILES))
    def _():
        body = _IN_TN - _GATE_COLS
        for r in range(0, body, _CAST_ROWS):
            n = min(_CAST_ROWS, body - r)
            w_scr[r:r + n, :] = wa_ref[_GATE_COLS + r:_GATE_COLS + r + n, :].astype(BF16)
        w_scr[body:, :] = wb_ref[:_GATE_COLS, :].astype(BF16)

    def project(dst_ref, dst_s_ref):
        dst_ref[...] = _dot_nt(xn_ref[...], w_scr[...])

        @pl.when(i == 0)
        def _():
            dst_s_ref[...] = _dot_nt(xs_ref[...], w_scr[...])

    @pl.when(j < _N_MLSTM_TILES)
    def _():
        project(zm_ref, zms_ref)

    for t, refs in enumerate(((aq_ref, aqs_ref), (ak_ref, aks_ref), (av_ref, avs_ref))):
        @pl.when(j == _N_MLSTM_TILES + t)
        def _(refs=refs):
            project(*refs)


def _in_proj(xn, xs, w_in_t):
    m, ms = xn.shape[0], xs.shape[0]
    tm = min(512, m)
    tn = _IN_TN
    last = m // tm - 1

    def held(j, i, first_tile, last_tile):
        return jnp.where(j < first_tile, 0, jnp.where(j <= last_tile, i, last))

    def section(t):
        return pl.BlockSpec((tm, tn), lambda j, i: (held(j, i, t, t), 0))

    sec_shape = jax.ShapeDtypeStruct((m, tn), F32)
    sec_s = pl.BlockSpec((ms, tn), lambda j, i: (0, 0))
    sec_s_shape = jax.ShapeDtypeStruct((ms, tn), F32)
    nm = _N_MLSTM_TILES
    return pl.pallas_call(
        _in_proj_kernel,
        grid=(_N_IN_TILES, m // tm),
        in_specs=[
            pl.BlockSpec((tm, D_MODEL), lambda j, i: (i, 0)),
            pl.BlockSpec((ms, D_MODEL), lambda j, i: (0, 0)),
            pl.BlockSpec((tn, D_MODEL), lambda j, i: (j, 0)),
            pl.BlockSpec((GATE_PAD, D_MODEL), lambda j, i: ((j + 1) * (tn // GATE_PAD), 0)),
        ],
        out_specs=[
            pl.BlockSpec((tm, tn), lambda j, i: (held(j, i, 0, nm - 1), jnp.minimum(j, nm - 1))),
            section(nm), section(nm + 1), section(nm + 2),
            pl.BlockSpec((ms, tn), lambda j, i: (0, jnp.minimum(j, nm - 1))),
            sec_s, sec_s, sec_s,
        ],
        out_shape=[jax.ShapeDtypeStruct((m, _GATES), F32), sec_shape, sec_shape, sec_shape,
                   jax.ShapeDtypeStruct((ms, _GATES), F32), sec_s_shape, sec_s_shape, sec_s_shape],
        scratch_shapes=[pltpu.VMEM((tn, D_MODEL), BF16)],
        compiler_params=_cparams("arbitrary", "arbitrary"),
        name="in_proj",
    )(xn, xs, w_in_t, w_in_t)


def _log_sigmoid(x):
    return -(jnp.maximum(-x, 0.0) + jnp.log1p(jnp.exp(-jnp.abs(x))))


def _bdot(a, b):
    return lax.dot_general(a, b, (((2,), (1,)), ((0,), (0,))), preferred_element_type=F32)


def _bdot_nt(a, b):
    return lax.dot_general(a, b, (((2,), (2,)), ((0,), (0,))), preferred_element_type=F32)


def _bdot_tn(a, b):
    return lax.dot_general(a, b, (((1,), (1,)), ((0,), (0,))), preferred_element_type=F32)


def _mlstm_kernel(q_ref, k_ref, v_ref, o_ref, g_ref, gb_ref, gh_ref, c0_ref, m0_ref,
                  out_ref, cfin_ref, mfin_ref, c_scr, m_scr, *, chunk, n_chunks):
    L = chunk
    H = M_HEADS
    tb = L * n_chunks
    t = pl.program_id(1)

    @pl.when(t == 0)
    def _():
        c_scr[...] = c0_ref[0]
        m_scr[...] = m0_ref[0]

    g = g_ref[...] + gb_ref[...]
    lane = lax.broadcasted_iota(jnp.int32, (tb, GATE_PAD), 1)
    gl = jnp.where((lane >= H) & (lane < 2 * H), _log_sigmoid(g), g)
    r = lax.broadcasted_iota(jnp.int32, (tb, tb), 0)
    c = lax.broadcasted_iota(jnp.int32, (tb, tb), 1)
    tri = ((r // L == c // L) & (c <= r)).astype(F32)
    cum = _dot(tri, gl, precision=HIGHEST)

    def rows(ci):
        return slice(ci * L, (ci + 1) * L)

    def stack(f):
        return jnp.stack([f(ci, h) for ci in range(n_chunks) for h in range(H)])

    ones_col = (lax.broadcasted_iota(jnp.int32, (L, M_V_DIM), 1) == 0).astype(BF16)
    li = stack(lambda ci, h: gl[rows(ci), h:h + 1])
    b = stack(lambda ci, h: cum[rows(ci), H + h:H + h + 1])
    qb = stack(lambda ci, h: q_ref[rows(ci), h * M_QK_DIM:(h + 1) * M_QK_DIM]).astype(BF16)
    kc = stack(lambda ci, h: k_ref[rows(ci), h * M_QK_DIM:(h + 1) * M_QK_DIM]) * (M_QK_DIM ** -0.5)
    vaug = stack(lambda ci, h: jnp.concatenate(
        [v_ref[rows(ci), h * M_V_DIM:(h + 1) * M_V_DIM].astype(BF16), ones_col], axis=1))

    rl = lax.broadcasted_iota(jnp.int32, (L, L), 0)
    cl = lax.broadcasted_iota(jnp.int32, (L, L), 1)
    w_row = jnp.sum(jnp.where(rl == cl, li - b, 0.0), axis=1, keepdims=True)
    d = jnp.where(cl <= rl, b + w_row, NEG_INF)
    m_loc = jnp.max(d, axis=2, keepdims=True)
    s = _bdot_nt(qb, kc.astype(BF16)) * jnp.exp(d - m_loc)
    nd_loc = _bdot(s.astype(BF16), vaug)
    g_last = m_loc[:, L - 1:L, :]
    b_last = b[:, L - 1:L, :]
    ws = jnp.exp(b_last - b + li - g_last)
    dc_loc = _bdot_tn((kc * ws).astype(BF16), vaug)

    m_prev = m_scr[:, :, 0:1]
    c_aug = c_scr[...]
    for ci in range(n_chunks):
        grp = slice(ci * H, (ci + 1) * H)
        a = b[grp] + m_prev
        m_t = jnp.maximum(a, m_loc[grp])
        nd = (jnp.exp(a - m_t) * _bdot(qb[grp], c_aug.astype(BF16))
              + jnp.exp(m_loc[grp] - m_t) * nd_loc[grp])
        hh = nd[:, :, :M_V_DIM] / jnp.maximum(jnp.abs(nd[:, :, M_V_DIM:M_V_DIM + 1]), jnp.exp(-m_t))
        hn = _rms(hh, gh_ref[...])
        for h in range(H):
            og = jax.nn.sigmoid(o_ref[rows(ci), h * M_V_DIM:(h + 1) * M_V_DIM])
            out_ref[rows(ci), h * M_V_DIM:(h + 1) * M_V_DIM] = (og * hn[h]).astype(out_ref.dtype)
        m_new = jnp.maximum(b_last[grp] + m_prev, g_last[grp])
        c_aug = (jnp.exp(b_last[grp] + m_prev - m_new) * c_aug
                 + jnp.exp(g_last[grp] - m_new) * dc_loc[grp])
        m_prev = m_new
    c_scr[...] = c_aug
    m_scr[...] = jnp.broadcast_to(m_prev, m_scr.shape)

    @pl.when(t == pl.num_programs(1) - 1)
    def _():
        cfin_ref[0] = c_scr[...]
        mfin_ref[0] = m_scr[...]


def _mlstm(z, gates, gate_bias, g_mhead, c0_aug, m0, batch, seq):
    L = math.gcd(seq, M_CHUNK)
    n_chunks = max(1, min(seq // L, 256 // L))
    tb = L * n_chunks
    nt = seq // tb
    row = lambda b, t: b * nt + t
    kern = functools.partial(_mlstm_kernel, chunk=L, n_chunks=n_chunks)
    return pl.pallas_call(
        kern,
        grid=(batch, nt),
        in_specs=[
            pl.BlockSpec((tb, 512), lambda b, t: (row(b, t), _MQ // 512)),
            pl.BlockSpec((tb, 512), lambda b, t: (row(b, t), _MK // 512)),
            pl.BlockSpec((tb, 1024), lambda b, t: (row(b, t), _MV // 1024)),
            pl.BlockSpec((tb, 1024), lambda b, t: (row(b, t), _MO // 1024)),
            pl.BlockSpec((tb, GATE_PAD), lambda b, t: (row(b, t), 0)),
            pl.BlockSpec((1, GATE_PAD), lambda b, t: (0, 0)),
            pl.BlockSpec((M_HEADS, 1, M_V_DIM), lambda b, t: (0, 0, 0)),
            pl.BlockSpec((1, M_HEADS, M_QK_DIM, 2 * M_V_DIM), lambda b, t: (b, 0, 0, 0)),
            pl.BlockSpec((1, M_HEADS, 1, GATE_PAD), lambda b, t: (b, 0, 0, 0)),
        ],
        out_specs=[
            pl.BlockSpec((tb, M_HEADS * M_V_DIM), lambda b, t: (row(b, t), 0)),
            pl.BlockSpec((1, M_HEADS, M_QK_DIM, 2 * M_V_DIM), lambda b, t: (b, 0, 0, 0)),
            pl.BlockSpec((1, M_HEADS, 1, GATE_PAD), lambda b, t: (b, 0, 0, 0)),
        ],
        out_shape=[
            jax.ShapeDtypeStruct((batch * seq, M_HEADS * M_V_DIM), BF16 if tb % 16 == 0 else F32),
            jax.ShapeDtypeStruct((batch, M_HEADS, M_QK_DIM, 2 * M_V_DIM), F32),
            jax.ShapeDtypeStruct((batch, M_HEADS, 1, GATE_PAD), F32),
        ],
        scratch_shapes=[pltpu.VMEM((M_HEADS, M_QK_DIM, 2 * M_V_DIM), F32),
                        pltpu.VMEM((M_HEADS, 1, GATE_PAD), F32)],
        compiler_params=_cparams("parallel", "arbitrary"),
        name="mlstm",
    )(z, z, z, z, gates, gate_bias, g_mhead.reshape(M_HEADS, 1, M_V_DIM), c0_aug, m0)


def _t5_bucket(rel):
    n = jnp.maximum(rel, 0)
    nf = jnp.maximum(n, 1).astype(F32)
    large = MAX_EXACT + (jnp.log(nf / MAX_EXACT) / math.log(MAX_DISTANCE / MAX_EXACT)
                         * (NUM_BUCKETS - MAX_EXACT)).astype(jnp.int32)
    large = jnp.minimum(large, NUM_BUCKETS - 1)
    return jnp.where(n < MAX_EXACT, n, large)


def _bias_kernel(tab_ref, out_ref):
    h = pl.program_id(0)
    i = lax.broadcasted_iota(jnp.int32, (MOBA_BLOCK, MOBA_BLOCK), 1)
    j = lax.broadcasted_iota(jnp.int32, (MOBA_BLOCK, MOBA_BLOCK), 0)
    for kind, rel in ((0, i - j), (1, MOBA_BLOCK + i - j)):
        bucket = _t5_bucket(rel)
        acc = jnp.zeros((MOBA_BLOCK, MOBA_BLOCK), F32)
        for b in range(NUM_BUCKETS):
            acc = jnp.where(bucket == b, tab_ref[h, b], acc)
        out_ref[0, kind] = acc
    out_ref[0, 2] = jnp.full((MOBA_BLOCK, MOBA_BLOCK), tab_ref[h, NUM_BUCKETS - 1], F32)


def _bias_tiles(rel_table):
    assert MOBA_BLOCK + 1 >= MAX_DISTANCE
    tab = rel_table.T.astype(F32)
    return pl.pallas_call(
        _bias_kernel,
        grid=(A_HEADS,),
        in_specs=[pl.BlockSpec(memory_space=pltpu.SMEM)],
        out_specs=pl.BlockSpec((1, 3, MOBA_BLOCK, MOBA_BLOCK), lambda h: (h, 0, 0, 0)),
        out_shape=jax.ShapeDtypeStruct((A_HEADS, 3, MOBA_BLOCK, MOBA_BLOCK), F32),
        compiler_params=_cparams("parallel"),
        name="t5_bias_tiles",
    )(tab)


def _moba_prompt_kernel(q_ref, k_ref, v_ref, bias_ref, wu_ref, wd_ref, o_ref, wub_ref, wdb_ref,
                        kb_scr, qt_scr, vt_scr, gt_scr, s_scr, *, n_blocks):
    nb = n_blocks
    blk = MOBA_BLOCK
    nbp = gt_scr.shape[0]
    wub_ref[...] = wu_ref[...].astype(BF16)
    wdb_ref[...] = wd_ref[...].astype(BF16)

    kb_scr[...] = k_ref[...].astype(BF16)
    for c in range(nb):
        cols = slice(c * blk, (c + 1) * blk)
        qt_scr[:, cols] = (q_ref[cols, :] * (A_HEAD_DIM ** -0.5)).T.astype(BF16)
        vt_scr[:, cols] = v_ref[cols, :].T.astype(BF16)
    need_gate = nb - 1 > MOBA_TOP_K
    if need_gate:
        means = [jnp.mean(k_ref[n * blk:(n + 1) * blk, :], axis=0, keepdims=True) for n in range(nb)]
        kmean = jnp.concatenate(means + [jnp.zeros((128 - nb, A_HEAD_DIM), F32)], axis=0)
        for c in range(MOBA_TOP_K + 1, nb):
            cols = slice(c * blk, (c + 1) * blk)
            gate = _dot_nt(q_ref[cols, :] * (A_HEAD_DIM ** -0.5), kmean, precision=HIGHEST)
            gt_scr[:, cols] = gate.T[:nbp, :]
    key = lax.broadcasted_iota(jnp.int32, (blk, blk), 0)
    qry = lax.broadcasted_iota(jnp.int32, (blk, blk), 1)
    causal = key <= qry
    blk_id = lax.broadcasted_iota(jnp.int32, (nbp, blk), 0)

    for i in range(nb):
        cols = slice(i * blk, (i + 1) * blk)
        selneg = None
        if i > MOBA_TOP_K:
            g = gt_scr[:, cols]
            rank = jnp.zeros((nbp, blk), F32)
            for j in range(i):
                gj = g[j:j + 1, :]
                rank = rank + ((gj > g) | ((gj == g) & (j < blk_id))).astype(F32)
            selneg = jnp.where(rank < MOBA_TOP_K, 0.0, NEG_INF)
        s_scr[0:(i + 1) * blk, :] = _dot(kb_scr[0:(i + 1) * blk, :], qt_scr[:, cols])
        m = jnp.full((1, blk), NEG_INF, F32)
        for n in range(i + 1):
            rows = slice(n * blk, (n + 1) * blk)
            kind = 0 if n == i else (1 if n == i - 1 else 2)
            st = s_scr[rows, :] + bias_ref[0, kind]
            if n == i:
                st = jnp.where(causal, st, NEG_INF)
            elif selneg is not None:
                st = st + selneg[n:n + 1, :]
            s_scr[rows, :] = st
            m = jnp.maximum(m, jnp.max(st, axis=0, keepdims=True))
        l = jnp.zeros((1, blk), F32)
        acc = jnp.zeros((A_HEAD_DIM, blk), F32)
        for n in range(i + 1):
            rows = slice(n * blk, (n + 1) * blk)
            p = jnp.exp(s_scr[rows, :] - m)
            l = l + jnp.sum(p, axis=0, keepdims=True)
            acc = acc + _dot(vt_scr[:, rows], p.astype(BF16))
        o_ref[cols, :] = (acc / l).T.astype(o_ref.dtype)


def _moba_prompt(aq, ak, av, bias_tiles, w_up, w_down, batch, seq):
    assert seq % MOBA_BLOCK == 0
    nb = seq // MOBA_BLOCK
    assert nb <= 128
    nbp = -(-nb // 8) * 8
    steps = batch * A_HEADS
    slab = D_FF // steps
    assert D_FF % steps == 0 and slab % 128 == 0
    head = pl.BlockSpec((seq, A_HEAD_DIM), lambda b, h: (b, h))
    up_slab = pl.BlockSpec((D_MODEL, slab), lambda b, h: (0, b * A_HEADS + h))
    down_slab = pl.BlockSpec((slab, D_MODEL), lambda b, h: (b * A_HEADS + h, 0))
    kern = functools.partial(_moba_prompt_kernel, n_blocks=nb)
    return pl.pallas_call(
        kern,
        grid=(batch, A_HEADS),
        in_specs=[head, head, head,
                  pl.BlockSpec((1, 3, MOBA_BLOCK, MOBA_BLOCK), lambda b, h: (h, 0, 0, 0)),
                  up_slab, down_slab],
        out_specs=[head, up_slab, down_slab],
        out_shape=[jax.ShapeDtypeStruct((batch * seq, A_HEADS * A_HEAD_DIM), BF16),
                   jax.ShapeDtypeStruct((D_MODEL, D_FF), BF16),
                   jax.ShapeDtypeStruct((D_FF, D_MODEL), BF16)],
        scratch_shapes=[pltpu.VMEM((seq, A_HEAD_DIM), BF16),
                        pltpu.VMEM((A_HEAD_DIM, seq), BF16),
                        pltpu.VMEM((A_HEAD_DIM, seq), BF16),
                        pltpu.VMEM((nbp, seq), F32),
                        pltpu.VMEM((seq, MOBA_BLOCK), F32)],
        compiler_params=_cparams("parallel", "parallel"),
        name="moba_prompt",
    )(aq, ak, av, bias_tiles, w_up, w_down)


PAGES_PER_BLOCK = MOBA_BLOCK // PAGE_SIZE
PAST_BLOCKS_PER_STEP = 8


def _moba_past_kernel(pt_ref, q_ref, *refs, n_pages_step):
    del pt_ref
    k_refs = refs[:n_pages_step]
    v_refs = refs[n_pages_step:2 * n_pages_step]
    bias_ref, o_ref, m_ref, l_ref, km_ref = refs[2 * n_pages_step:]
    hq = q_ref.shape[1]
    nq = hq // A_HEADS
    qb = (q_ref[0] * (A_HEAD_DIM ** -0.5)).reshape(A_HEADS, nq, A_HEAD_DIM).astype(BF16)
    n_blocks = n_pages_step // PAGES_PER_BLOCK
    is_last_step = pl.program_id(1) == pl.num_programs(1) - 1

    def head_rows(ref, h):
        return ref[0, 0, pl.ds(h, PAGE_SIZE, stride=A_HEADS), :]

    for g in range(n_blocks):
        pages = range(g * PAGES_PER_BLOCK, (g + 1) * PAGES_PER_BLOCK)
        ksum = sum(jnp.sum(k_refs[p][0, 0].reshape(PAGE_SIZE, A_HEADS, A_HEAD_DIM), axis=0) for p in pages)
        km_ref[0, g] = ksum / MOBA_BLOCK
        kind = jnp.where(is_last_step, 1, 0) if g == n_blocks - 1 else 0
        kh = jnp.stack([jnp.concatenate([head_rows(k_refs[p], h) for p in pages], axis=0)
                        for h in range(A_HEADS)]).astype(BF16)
        vh = jnp.stack([jnp.concatenate([head_rows(v_refs[p], h) for p in pages], axis=0)
                        for h in range(A_HEADS)]).astype(BF16)
        lt = _bdot_nt(qb, kh) + bias_ref[kind]
        m = jnp.max(lt, axis=2, keepdims=True)
        p_ = jnp.exp(lt - m)
        l = jnp.sum(p_, axis=2, keepdims=True)
        o_ref[0, g] = _bdot(p_.astype(BF16), vh).reshape(hq, A_HEAD_DIM)
        m_ref[0, g] = jnp.broadcast_to(m, (A_HEADS, nq, A_HEAD_DIM)).reshape(hq, A_HEAD_DIM)
        l_ref[0, g] = jnp.broadcast_to(l, (A_HEADS, nq, A_HEAD_DIM)).reshape(hq, A_HEAD_DIM)


def _moba_past(page_table, q_hq, cache_k, cache_v, past_bias, layer):
    bs, n_pages = page_table.shape
    assert n_pages % PAGES_PER_BLOCK == 0
    nbp = n_pages // PAGES_PER_BLOCK
    gb = math.gcd(nbp, PAST_BLOCKS_PER_STEP)
    pps = gb * PAGES_PER_BLOCK
    hq = q_hq.shape[1]
    page = (1, 1, PAGE_SIZE * A_HEADS, A_HEAD_DIM)
    cache_k, cache_v = (c.reshape(c.shape[:2] + page[2:]) for c in (cache_k, cache_v))
    part = pl.BlockSpec((1, gb, hq, A_HEAD_DIM), lambda b, n, pt: (b, n, 0, 0))

    def page_spec(p):
        return pl.BlockSpec(page, lambda b, n, pt: (layer, pt[b, pps * n + p], 0, 0))

    grid_spec = pltpu.PrefetchScalarGridSpec(
        num_scalar_prefetch=1,
        grid=(bs, nbp // gb),
        in_specs=[pl.BlockSpec((1, hq, A_HEAD_DIM), lambda b, n, pt: (b, 0, 0))]
        + [page_spec(p) for p in range(pps)] * 2
        + [pl.BlockSpec((2, A_HEADS, hq // A_HEADS, MOBA_BLOCK), lambda b, n, pt: (0, 0, 0, 0))],
        out_specs=[part, part, part,
                   pl.BlockSpec((1, gb, A_HEADS, A_HEAD_DIM), lambda b, n, pt: (b, n, 0, 0))],
    )
    pshape = jax.ShapeDtypeStruct((bs, nbp, hq, A_HEAD_DIM), F32)
    return pl.pallas_call(
        functools.partial(_moba_past_kernel, n_pages_step=pps),
        grid_spec=grid_spec,
        out_shape=[pshape, pshape, pshape, jax.ShapeDtypeStruct((bs, nbp, A_HEADS, A_HEAD_DIM), F32)],
        compiler_params=_cparams("parallel", "parallel"),
        name="moba_past_blocks",
    )(page_table, q_hq, *([cache_k] * pps), *([cache_v] * pps), past_bias)


def _moba_combine_kernel(q_ref, km_ref, kn_ref, vn_ref, ob_ref, op_ref, mp_ref, lp_ref, out_ref):
    nbp = km_ref.shape[2]
    q = q_ref[0] * (A_HEAD_DIM ** -0.5)
    hq = q.shape[0]
    nq = hq // A_HEADS
    gate = jnp.concatenate(
        [_dot_nt(q[h * nq:(h + 1) * nq], km_ref[0, h], precision=HIGHEST) for h in range(A_HEADS)], axis=0)
    lane = lax.broadcasted_iota(jnp.int32, (hq, nbp), 1)
    rank = jnp.zeros((hq, nbp), F32)
    for j in range(nbp):
        gj = gate[:, j:j + 1]
        beats = (gj > gate) | ((gj == gate) & (j < lane))
        rank = rank + beats.astype(F32)
    selneg = jnp.where(rank < MOBA_TOP_K, 0.0, NEG_INF)
    qb = q.astype(BF16)
    lt = _dot_nt(qb, kn_ref[0].astype(BF16))
    r = lax.broadcasted_iota(jnp.int32, lt.shape, 0)
    c = lax.broadcasted_iota(jnp.int32, lt.shape, 1)
    ok = ((c % A_HEADS) == (r // nq)) & ((c // A_HEADS) <= (r % nq))
    lo = jnp.where(ok, lt + ob_ref[...], NEG_INF)
    m = jnp.max(lo, axis=1, keepdims=True)
    for n in range(nbp):
        m = jnp.maximum(m, mp_ref[0, n][:, 0:1] + selneg[:, n:n + 1])
    p = jnp.exp(lo - m)
    l = jnp.sum(p, axis=1, keepdims=True)
    acc = _dot(p.astype(BF16), vn_ref[0].astype(BF16))
    for n in range(nbp):
        w = jnp.exp(mp_ref[0, n][:, 0:1] + selneg[:, n:n + 1] - m)
        l = l + w * lp_ref[0, n][:, 0:1]
        acc = acc + w * op_ref[0, n]
    out_ref[0] = acc / l


def _moba_combine(q_hq, kmean_hn, k_new, v_new, own_bias, o_part, m_part, l_part):
    bs, hq, _ = q_hq.shape
    nbp = o_part.shape[1]
    part = pl.BlockSpec((1, nbp, hq, A_HEAD_DIM), lambda b: (b, 0, 0, 0))
    tok = pl.BlockSpec((1, hq, A_HEAD_DIM), lambda b: (b, 0, 0))
    return pl.pallas_call(
        _moba_combine_kernel,
        grid=(bs,),
        in_specs=[tok,
                  pl.BlockSpec((1, A_HEADS, nbp, A_HEAD_DIM), lambda b: (b, 0, 0, 0)),
                  tok, tok,
                  pl.BlockSpec((hq, hq), lambda b: (0, 0)),
                  part, part, part],
        out_specs=tok,
        out_shape=jax.ShapeDtypeStruct((bs, hq, A_HEAD_DIM), F32),
        compiler_params=_cparams("parallel"),
        name="moba_combine",
    )(q_hq, kmean_hn, k_new, v_new, own_bias, o_part, m_part, l_part)


def _out_proj_kernel(h_ref, om_ref, oa_ref, hs_ref, oms_ref, oas_ref, wm_ref, wa_ref, o_ref, os_ref,
                     wm_scr, wa_scr):
    def project(h, om, oa, dst):
        dst[...] = (h[...] + _dot(om[...].astype(BF16), wm_scr[...])
                    + _dot(oa[...].astype(BF16), wa_scr[...]))

    @pl.when(pl.program_id(1) == 0)
    def _():
        for r in range(0, wm_scr.shape[0], _CAST_ROWS):
            wm_scr[r:r + _CAST_ROWS, :] = wm_ref[r:r + _CAST_ROWS, :].astype(BF16)
            wa_scr[r:r + _CAST_ROWS, :] = wa_ref[r:r + _CAST_ROWS, :].astype(BF16)
        project(hs_ref, oms_ref, oas_ref, os_ref)

    project(h_ref, om_ref, oa_ref, o_ref)


def _out_proj(h, out_m, out_a, hs, out_ms, out_as, w_out):
    m, ms = h.shape[0], hs.shape[0]
    tm = min(512, m)
    tn = 1024
    half = M_HEADS * M_V_DIM
    return pl.pallas_call(
        _out_proj_kernel,
        grid=(D_MODEL // tn, m // tm),
        in_specs=[
            pl.BlockSpec((tm, tn), lambda j, i: (i, j)),
            pl.BlockSpec((tm, half), lambda j, i: (i, 0)),
            pl.BlockSpec((tm, half), lambda j, i: (i, 0)),
            pl.BlockSpec((ms, tn), lambda j, i: (0, j)),
            pl.BlockSpec((ms, half), lambda j, i: (0, 0)),
            pl.BlockSpec((ms, half), lambda j, i: (0, 0)),
            pl.BlockSpec((half, tn), lambda j, i: (0, j)),
            pl.BlockSpec((half, tn), lambda j, i: (1, j)),
        ],
        out_specs=[pl.BlockSpec((tm, tn), lambda j, i: (i, j)),
                   pl.BlockSpec((ms, tn), lambda j, i: (0, j))],
        out_shape=[jax.ShapeDtypeStruct((m, D_MODEL), F32), jax.ShapeDtypeStruct((ms, D_MODEL), F32)],
        scratch_shapes=[pltpu.VMEM((half, tn), BF16), pltpu.VMEM((half, tn), BF16)],
        compiler_params=_cparams("arbitrary", "arbitrary"),
        name="out_proj",
    )(h, out_m, out_a, hs, out_ms, out_as, w_out, w_out)


def _mlp_kernel(h_ref, hs_ref, g_ref, wu_ref, wd_ref, o_ref, os_ref, xn_ref, xns_ref):
    f = pl.program_id(1)

    def accumulate(src, dst, xn):
        @pl.when(f == 0)
        def _():
            xn[...] = _rms(src[...], g_ref[...]).astype(BF16)
            dst[...] = src[...]

        u = _dot(xn[...], wu_ref[...])
        a = jnp.square(jnp.maximum(u, 0.0)).astype(BF16)
        dst[...] += _dot(a, wd_ref[...])

    accumulate(h_ref, o_ref, xn_ref)

    @pl.when(pl.program_id(0) == 0)
    def _():
        accumulate(hs_ref, os_ref, xns_ref)


def _mlp(h, hs, g, w_up, w_down):
    m, ms = h.shape[0], hs.shape[0]
    tm = min(1024, m)
    tf = 512
    return pl.pallas_call(
        _mlp_kernel,
        grid=(m // tm, D_FF // tf),
        in_specs=[
            pl.BlockSpec((tm, D_MODEL), lambda i, f: (i, 0), pipeline_mode=pl.Buffered(1)),
            pl.BlockSpec((ms, D_MODEL), lambda i, f: (0, 0)),
            pl.BlockSpec((1, D_MODEL), lambda i, f: (0, 0)),
            pl.BlockSpec((D_MODEL, tf), lambda i, f: (0, f)),
            pl.BlockSpec((tf, D_MODEL), lambda i, f: (f, 0)),
        ],
        out_specs=[pl.BlockSpec((tm, D_MODEL), lambda i, f: (i, 0)),
                   pl.BlockSpec((ms, D_MODEL), lambda i, f: (0, 0))],
        out_shape=[jax.ShapeDtypeStruct((m, D_MODEL), F32), jax.ShapeDtypeStruct((ms, D_MODEL), F32)],
        scratch_shapes=[pltpu.VMEM((tm, D_MODEL), BF16), pltpu.VMEM((ms, D_MODEL), BF16)],
        compiler_params=_cparams("arbitrary", "arbitrary"),
        name="mlp",
    )(h, hs, g, w_up, w_down)


def _ple_kernel(h_ref, pe_ref, hs_ref, pes_ref, g_ref, wg_ref, wp_ref, gf_ref, y_ref, ys_ref):
    def gated_embedding(src, pe, dst):
        h = src[...]
        gate = jax.nn.sigmoid(_dot(_rms(h, g_ref[...]).astype(BF16), wg_ref[...]))
        proj = _dot(pe[...].astype(BF16), wp_ref[...])
        dst[...] = _rms(h + gate * proj, gf_ref[...])

    gated_embedding(h_ref, pe_ref, y_ref)

    @pl.when(pl.program_id(0) == 0)
    def _():
        gated_embedding(hs_ref, pes_ref, ys_ref)


def _ple_final(h, pe, hs, pes, g_ple, w_gate, w_proj, g_final):
    m, ms = h.shape[0], hs.shape[0]
    tm = min(512, m)
    vec = pl.BlockSpec((1, D_MODEL), lambda i: (0, 0))
    return pl.pallas_call(
        _ple_kernel,
        grid=(m // tm,),
        in_specs=[
            pl.BlockSpec((tm, D_MODEL), lambda i: (i, 0)),
            pl.BlockSpec((tm, PLE_DIM), lambda i: (i, 0)),
            pl.BlockSpec((ms, D_MODEL), lambda i: (0, 0)),
            pl.BlockSpec((ms, PLE_DIM), lambda i: (0, 0)),
            vec,
            pl.BlockSpec((D_MODEL, D_MODEL), lambda i: (0, 0)),
            pl.BlockSpec((PLE_DIM, D_MODEL), lambda i: (0, 0)),
            vec,
        ],
        out_specs=[pl.BlockSpec((tm, D_MODEL), lambda i: (i, 0)),
                   pl.BlockSpec((ms, D_MODEL), lambda i: (0, 0))],
        out_shape=[jax.ShapeDtypeStruct((m, D_MODEL), F32), jax.ShapeDtypeStruct((ms, D_MODEL), F32)],
        compiler_params=_cparams("arbitrary"),
        name="ple_final_norm",
    )(h, pe, hs, pes, g_ple, w_gate, w_proj, g_final)


def _layer_weights(w_in, b_igate, b_fgate, g_mix, g_mhead, w_out, g_ffn, w_up, w_down,
                   g_ple, w_ple_gate, w_ple_proj):
    gate_bias = jnp.pad(jnp.concatenate([b_igate, b_fgate]), (0, GATE_PAD - 2 * M_HEADS)).reshape(1, GATE_PAD)
    return dict(
        w_in=w_in.T, gate_bias=gate_bias.astype(F32),
        g_mix=g_mix.reshape(1, D_MODEL), g_mhead=g_mhead, w_out=w_out,
        g_ffn=g_ffn.reshape(1, D_MODEL), w_up=w_up, w_down=w_down,
        g_ple=g_ple.reshape(1, D_MODEL), w_ple_gate=w_ple_gate.astype(BF16),
        w_ple_proj=w_ple_proj.astype(BF16))


def _tail(h, out_m, out_a, pe, hs, out_ms, out_as, pes, w, g_final):
    h, hs = _out_proj(h, out_m, out_a, hs, out_ms, out_as, w["w_out"])
    h, hs = _mlp(h, hs, w["g_ffn"], w["w_up_bf16"], w["w_down_bf16"])
    return _ple_final(h, pe, hs, pes, w["g_ple"], w["w_ple_gate"], w["w_ple_proj"], g_final)


def _split_state(c_aug, m_fin):
    return c_aug[..., :M_V_DIM], c_aug[..., M_V_DIM], m_fin[..., 0, 0]


def kernel(x_prompt, x_sample, cache_k, cache_v, state_C, state_n, state_m, page_table, p_prompt, p_sample, rel_bias_table, w_in, b_igate, b_fgate, g_mix, g_mhead, w_out, g_ffn, w_up, w_down, g_ple, w_ple_gate, w_ple_proj, g_final):
    depth = w_in.shape[0]
    assert depth == 1, "one decoder layer per call"
    layer = 0
    bp, tp, _ = x_prompt.shape
    bs, ts, _ = x_sample.shape
    past_len = page_table.shape[1] * PAGE_SIZE
    assert past_len % MOBA_BLOCK == 0 and ts <= MOBA_BLOCK
    kv_shape = (A_HEADS, A_HEAD_DIM)
    g_fin = g_final.reshape(1, D_MODEL)

    w = _layer_weights(w_in[layer], b_igate[layer], b_fgate[layer], g_mix[layer], g_mhead[layer], w_out[layer],
                       g_ffn[layer], w_up[layer], w_down[layer], g_ple[layer], w_ple_gate[layer],
                       w_ple_proj[layer])
    bias_tiles = _bias_tiles(rel_bias_table)

    hp = x_prompt.reshape(bp * tp, D_MODEL)
    hs = x_sample.reshape(bs * ts, D_MODEL)
    xn_p, gp, xn_s, gs = _norm_gate(hp, hs, w["g_mix"], w["w_in"])
    zp, aq_p, ak_p, av_p, zs, aq_s, ak_s, av_s = _in_proj(xn_p, xn_s, w["w_in"])
    c0 = jnp.zeros((bp, M_HEADS, M_QK_DIM, 2 * M_V_DIM), F32)
    m0 = jnp.zeros((bp, M_HEADS, 1, GATE_PAD), F32)
    om_p, c_p, m_p = _mlstm(zp, gp, w["gate_bias"], w["g_mhead"], c0, m0, bp, tp)
    oa_p, w["w_up_bf16"], w["w_down_bf16"] = _moba_prompt(aq_p, ak_p, av_p, bias_tiles, w["w_up"], w["w_down"],
                                                          bp, tp)
    k_p = ak_p.reshape((1, bp, tp) + kv_shape)
    v_p = av_p.reshape((1, bp, tp) + kv_shape)
    cp, np_, mp = _split_state(c_p, m_p)

    c0s = jnp.concatenate([state_C[layer], state_n[layer][..., None],
                           jnp.zeros((bs, M_HEADS, M_QK_DIM, M_V_DIM - 1), F32)], axis=-1)
    m0s = jnp.broadcast_to(state_m[layer][..., None, None], (bs, M_HEADS, 1, GATE_PAD))
    om_s, c_s, m_s = _mlstm(zs, gs, w["gate_bias"], w["g_mhead"], c0s, m0s, bs, ts)

    hq = A_HEADS * ts
    q_s = aq_s.reshape(bs, ts, A_HEADS, A_HEAD_DIM)
    q_hq = q_s.transpose(0, 2, 1, 3).reshape(bs, hq, A_HEAD_DIM)
    k_new = ak_s.reshape(bs, hq, A_HEAD_DIM)
    v_new = av_s.reshape(bs, hq, A_HEAD_DIM)
    rows = bias_tiles[:, :, :, :ts].transpose(0, 1, 3, 2)
    expand = lambda a: jnp.repeat(a, A_HEADS, axis=-1).reshape(hq, -1)
    past_bias = jnp.stack([rows[:, 2], rows[:, 1]])
    own_bias = expand(rows[:, 0, :, :ts])
    o_part, m_part, l_part, kmean = _moba_past(page_table, q_hq, cache_k, cache_v, past_bias, layer)
    oa_hq = _moba_combine(q_hq, kmean.transpose(0, 2, 1, 3), k_new, v_new, own_bias, o_part, m_part, l_part)
    oa_s = oa_hq.reshape(bs, A_HEADS, ts, A_HEAD_DIM).transpose(0, 2, 1, 3).reshape(bs * ts, A_HEADS * A_HEAD_DIM)
    y_p, y_s = _tail(hp, om_p, oa_p, p_prompt[layer].reshape(bp * tp, PLE_DIM),
                     hs, om_s, oa_s, p_sample[layer].reshape(bs * ts, PLE_DIM), w, g_fin)
    k_s = k_new.reshape((1, bs, ts) + kv_shape)
    v_s = v_new.reshape((1, bs, ts) + kv_shape)
    cs, ns, ms = _split_state(c_s, m_s)

    return (y_p.reshape(bp, tp, D_MODEL), y_s.reshape(bs, ts, D_MODEL),
            k_p, v_p, cp[None], np_[None], mp[None],
            k_s, v_s, cs[None], ns[None], ms[None])
```

```python
import functools
import math

import jax
import jax.numpy as jnp
from jax import lax
from jax.experimental import pallas as pl
from jax.experimental.pallas import tpu as pltpu

F32 = jnp.float32
BF16 = jnp.bfloat16
HIGHEST = lax.Precision.HIGHEST

D_MODEL = 2048
M_HEADS = 8
M_V_DIM = 128
M_QK_DIM = 64
M_CHUNK = 64
A_HEADS = 8
A_HEAD_DIM = 128
MOBA_BLOCK = 256
MOBA_TOP_K = 3
NUM_BUCKETS = 32
MAX_EXACT = NUM_BUCKETS // 2
MAX_DISTANCE = 128
D_FF = 4 * D_MODEL
PLE_DIM = 256
PAGE_SIZE = 128
EPS = 1e-6

_MQ, _MK, _MV, _MO = 0, 512, 1024, 2048
_GATES = 3072
_AQ = 3088
D_MAIN = 6144
GATE_PAD = 128

VMEM_LIMIT = 56 * 1024 * 1024
NEG_INF = float("-inf")


def _cparams(*sem):
    return pltpu.CompilerParams(dimension_semantics=sem, vmem_limit_bytes=VMEM_LIMIT)


def _rms(x, g):
    return x * lax.rsqrt(jnp.mean(x * x, axis=-1, keepdims=True) + EPS) * g


def _dot_nt(a, b, **kw):
    return lax.dot_general(a, b, (((1,), (1,)), ((), ())), preferred_element_type=F32, **kw)


def _dot_tn(a, b, **kw):
    return lax.dot_general(a, b, (((0,), (0,)), ((), ())), preferred_element_type=F32, **kw)


def _dot(a, b, **kw):
    return jnp.dot(a, b, preferred_element_type=F32, **kw)


_IN_TN = 1024
_N_IN_TILES = D_MAIN // _IN_TN
_N_MLSTM_TILES = _GATES // _IN_TN
_GATE_COLS = _AQ - _GATES
_CAST_ROWS = 256


def _norm_gate_kernel(x_ref, xs_ref, g_ref, wg_ref, xn_ref, gate_ref, xns_ref, gates_ref):
    wg = wg_ref[...]
    hi = wg.astype(BF16).astype(F32)
    row = lax.broadcasted_iota(jnp.int32, wg.shape, 0)
    w2 = jnp.where(row < _GATE_COLS, hi,
                   jnp.where(row < 2 * _GATE_COLS, pltpu.roll(wg - hi, _GATE_COLS, axis=0), 0.0))
    w2 = w2.astype(BF16)

    def norm_and_gate(src_ref, dst_ref, gate_dst_ref):
        xn = _rms(src_ref[...], g_ref[...]).astype(BF16)
        dst_ref[...] = xn
        r = _dot_nt(xn, w2)
        gate_dst_ref[...] = r + pltpu.roll(r, GATE_PAD - _GATE_COLS, axis=1)

    norm_and_gate(x_ref, xn_ref, gate_ref)

    @pl.when(pl.program_id(0) == 0)
    def _():
        norm_and_gate(xs_ref, xns_ref, gates_ref)


def _norm_gate(x, xs, g, w_in_t):
    m, ms = x.shape[0], xs.shape[0]
    tm = min(512, m)
    return pl.pallas_call(
        _norm_gate_kernel,
        grid=(m // tm,),
        in_specs=[
            pl.BlockSpec((tm, D_MODEL), lambda i: (i, 0)),
            pl.BlockSpec((ms, D_MODEL), lambda i: (0, 0)),
            pl.BlockSpec((1, D_MODEL), lambda i: (0, 0)),
            pl.BlockSpec((GATE_PAD, D_MODEL), lambda i: (_GATES // GATE_PAD, 0)),
        ],
        out_specs=[pl.BlockSpec((tm, D_MODEL), lambda i: (i, 0)),
                   pl.BlockSpec((tm, GATE_PAD), lambda i: (i, 0)),
                   pl.BlockSpec((ms, D_MODEL), lambda i: (0, 0)),
                   pl.BlockSpec((ms, GATE_PAD), lambda i: (0, 0))],
        out_shape=[jax.ShapeDtypeStruct((m, D_MODEL), BF16), jax.ShapeDtypeStruct((m, GATE_PAD), F32),
                   jax.ShapeDtypeStruct((ms, D_MODEL), BF16), jax.ShapeDtypeStruct((ms, GATE_PAD), F32)],
        compiler_params=_cparams("arbitrary"),
        name="norm_gate",
    )(x, xs, g, w_in_t)


def _in_proj_kernel(xn_ref, xs_ref, wa_ref, wb_ref, zm_ref, aq_ref, ak_ref, av_ref, zs_ref, w_scr):
    j = pl.program_id(0)
    i = pl.program_id(1)

    @pl.when((i == 0) & (j < _N_MLSTM_TILES))
    def _():
        for r in range(0, _IN_TN, _CAST_ROWS):
            w_scr[r:r + _CAST_ROWS, :] = wa_ref[r:r + _CAST_ROWS, :].astype(BF16)

    @pl.when((i == 0) & (j >= _N_MLSTM_TILES))
    def _():
        body = _IN_TN - _GATE_COLS
        for r in range(0, body, _CAST_ROWS):
            n = min(_CAST_ROWS, body - r)
            w_scr[r:r + n, :] = wa_ref[_GATE_COLS + r:_GATE_COLS + r + n, :].astype(BF16)
        w_scr[body:, :] = wb_ref[:_GATE_COLS, :].astype(BF16)

    @pl.when(i == 0)
    def _():
        zs_ref[...] = _dot_nt(xs_ref[...], w_scr[...])

    @pl.when(j < _N_MLSTM_TILES)
    def _():
        zm_ref[...] = _dot_nt(xn_ref[...], w_scr[...])

    for t, ref in enumerate((aq_ref, ak_ref, av_ref)):
        @pl.when(j == _N_MLSTM_TILES + t)
        def _(ref=ref):
            ref[...] = _dot_nt(xn_ref[...], w_scr[...])


def _in_proj(xn, xs, w_in_t):
    m, ms = xn.shape[0], xs.shape[0]
    tm = min(512, m)
    tn = _IN_TN
    last = m // tm - 1

    def held(j, i, first_tile, last_tile):
        return jnp.where(j < first_tile, 0, jnp.where(j <= last_tile, i, last))

    def section(t):
        return pl.BlockSpec((tm, tn), lambda j, i: (held(j, i, t, t), 0))

    sec_shape = jax.ShapeDtypeStruct((m, tn), F32)
    nm = _N_MLSTM_TILES
    return pl.pallas_call(
        _in_proj_kernel,
        grid=(_N_IN_TILES, m // tm),
        in_specs=[
            pl.BlockSpec((tm, D_MODEL), lambda j, i: (i, 0)),
            pl.BlockSpec((ms, D_MODEL), lambda j, i: (0, 0)),
            pl.BlockSpec((tn, D_MODEL), lambda j, i: (j, 0)),
            pl.BlockSpec((GATE_PAD, D_MODEL), lambda j, i: ((j + 1) * (tn // GATE_PAD), 0)),
        ],
        out_specs=[
            pl.BlockSpec((tm, tn), lambda j, i: (held(j, i, 0, nm - 1), jnp.minimum(j, nm - 1))),
            section(nm), section(nm + 1), section(nm + 2),
            pl.BlockSpec((ms, tn), lambda j, i: (0, j)),
        ],
        out_shape=[jax.ShapeDtypeStruct((m, _GATES), F32), sec_shape, sec_shape, sec_shape,
                   jax.ShapeDtypeStruct((ms, D_MAIN), F32)],
        scratch_shapes=[pltpu.VMEM((tn, D_MODEL), BF16)],
        compiler_params=_cparams("arbitrary", "arbitrary"),
        name="in_proj",
    )(xn, xs, w_in_t, w_in_t)


def _log_sigmoid(x):
    return -(jnp.maximum(-x, 0.0) + jnp.log1p(jnp.exp(-jnp.abs(x))))


def _bdot(a, b):
    return lax.dot_general(a, b, (((2,), (1,)), ((0,), (0,))), preferred_element_type=F32)


def _bdot_nt(a, b):
    return lax.dot_general(a, b, (((2,), (2,)), ((0,), (0,))), preferred_element_type=F32)


def _bdot_tn(a, b):
    return lax.dot_general(a, b, (((1,), (1,)), ((0,), (0,))), preferred_element_type=F32)


def _mlstm_kernel(q_ref, k_ref, v_ref, o_ref, g_ref, gb_ref, gh_ref, c0_ref, m0_ref,
                  out_ref, cfin_ref, mfin_ref, c_scr, m_scr, *, chunk, n_chunks):
    L = chunk
    H = M_HEADS
    tb = L * n_chunks
    t = pl.program_id(1)

    @pl.when(t == 0)
    def _():
        c_scr[...] = c0_ref[0]
        m_scr[...] = m0_ref[0]

    g = g_ref[...] + gb_ref[...]
    lane = lax.broadcasted_iota(jnp.int32, (tb, GATE_PAD), 1)
    gl = jnp.where((lane >= H) & (lane < 2 * H), _log_sigmoid(g), g)
    r = lax.broadcasted_iota(jnp.int32, (tb, tb), 0)
    c = lax.broadcasted_iota(jnp.int32, (tb, tb), 1)
    tri = ((r // L == c // L) & (c <= r)).astype(F32)
    cum = _dot(tri, gl, precision=HIGHEST)

    def rows(ci):
        return slice(ci * L, (ci + 1) * L)

    def stack(f):
        return jnp.stack([f(ci, h) for ci in range(n_chunks) for h in range(H)])

    ones_col = (lax.broadcasted_iota(jnp.int32, (L, M_V_DIM), 1) == 0).astype(BF16)
    li = stack(lambda ci, h: gl[rows(ci), h:h + 1])
    b = stack(lambda ci, h: cum[rows(ci), H + h:H + h + 1])
    qb = stack(lambda ci, h: q_ref[rows(ci), h * M_QK_DIM:(h + 1) * M_QK_DIM]).astype(BF16)
    kc = stack(lambda ci, h: k_ref[rows(ci), h * M_QK_DIM:(h + 1) * M_QK_DIM]) * (M_QK_DIM ** -0.5)
    vaug = stack(lambda ci, h: jnp.concatenate(
        [v_ref[rows(ci), h * M_V_DIM:(h + 1) * M_V_DIM].astype(BF16), ones_col], axis=1))

    rl = lax.broadcasted_iota(jnp.int32, (L, L), 0)
    cl = lax.broadcasted_iota(jnp.int32, (L, L), 1)
    w_row = jnp.sum(jnp.where(rl == cl, li - b, 0.0), axis=1, keepdims=True)
    d = jnp.where(cl <= rl, b + w_row, NEG_INF)
    m_loc = jnp.max(d, axis=2, keepdims=True)
    s = _bdot_nt(qb, kc.astype(BF16)) * jnp.exp(d - m_loc)
    nd_loc = _bdot(s.astype(BF16), vaug)
    g_last = m_loc[:, L - 1:L, :]
    b_last = b[:, L - 1:L, :]
    ws = jnp.exp(b_last - b + li - g_last)
    dc_loc = _bdot_tn((kc * ws).astype(BF16), vaug)

    m_prev = m_scr[:, :, 0:1]
    c_aug = c_scr[...]
    for ci in range(n_chunks):
        grp = slice(ci * H, (ci + 1) * H)
        a = b[grp] + m_prev
        m_t = jnp.maximum(a, m_loc[grp])
        nd = (jnp.exp(a - m_t) * _bdot(qb[grp], c_aug.astype(BF16))
              + jnp.exp(m_loc[grp] - m_t) * nd_loc[grp])
        hh = nd[:, :, :M_V_DIM] / jnp.maximum(jnp.abs(nd[:, :, M_V_DIM:M_V_DIM + 1]), jnp.exp(-m_t))
        hn = _rms(hh, gh_ref[...])
        for h in range(H):
            og = jax.nn.sigmoid(o_ref[rows(ci), h * M_V_DIM:(h + 1) * M_V_DIM])
            out_ref[rows(ci), h * M_V_DIM:(h + 1) * M_V_DIM] = (og * hn[h]).astype(out_ref.dtype)
        m_new = jnp.maximum(b_last[grp] + m_prev, g_last[grp])
        c_aug = (jnp.exp(b_last[grp] + m_prev - m_new) * c_aug
                 + jnp.exp(g_last[grp] - m_new) * dc_loc[grp])
        m_prev = m_new
    c_scr[...] = c_aug
    m_scr[...] = jnp.broadcast_to(m_prev, m_scr.shape)

    @pl.when(t == pl.num_programs(1) - 1)
    def _():
        cfin_ref[0] = c_scr[...]
        mfin_ref[0] = m_scr[...]


def _mlstm(z, gates, gate_bias, g_mhead, c0_aug, m0, batch, seq):
    L = math.gcd(seq, M_CHUNK)
    n_chunks = max(1, min(seq // L, 256 // L))
    tb = L * n_chunks
    nt = seq // tb
    row = lambda b, t: b * nt + t
    kern = functools.partial(_mlstm_kernel, chunk=L, n_chunks=n_chunks)
    return pl.pallas_call(
        kern,
        grid=(batch, nt),
        in_specs=[
            pl.BlockSpec((tb, 512), lambda b, t: (row(b, t), _MQ // 512)),
            pl.BlockSpec((tb, 512), lambda b, t: (row(b, t), _MK // 512)),
            pl.BlockSpec((tb, 1024), lambda b, t: (row(b, t), _MV // 1024)),
            pl.BlockSpec((tb, 1024), lambda b, t: (row(b, t), _MO // 1024)),
            pl.BlockSpec((tb, GATE_PAD), lambda b, t: (row(b, t), 0)),
            pl.BlockSpec((1, GATE_PAD), lambda b, t: (0, 0)),
            pl.BlockSpec((M_HEADS, 1, M_V_DIM), lambda b, t: (0, 0, 0)),
            pl.BlockSpec((1, M_HEADS, M_QK_DIM, 2 * M_V_DIM), lambda b, t: (b, 0, 0, 0)),
            pl.BlockSpec((1, M_HEADS, 1, GATE_PAD), lambda b, t: (b, 0, 0, 0)),
        ],
        out_specs=[
            pl.BlockSpec((tb, M_HEADS * M_V_DIM), lambda b, t: (row(b, t), 0)),
            pl.BlockSpec((1, M_HEADS, M_QK_DIM, 2 * M_V_DIM), lambda b, t: (b, 0, 0, 0)),
            pl.BlockSpec((1, M_HEADS, 1, GATE_PAD), lambda b, t: (b, 0, 0, 0)),
        ],
        out_shape=[
            jax.ShapeDtypeStruct((batch * seq, M_HEADS * M_V_DIM), BF16 if tb % 16 == 0 else F32),
            jax.ShapeDtypeStruct((batch, M_HEADS, M_QK_DIM, 2 * M_V_DIM), F32),
            jax.ShapeDtypeStruct((batch, M_HEADS, 1, GATE_PAD), F32),
        ],
        scratch_shapes=[pltpu.VMEM((M_HEADS, M_QK_DIM, 2 * M_V_DIM), F32),
                        pltpu.VMEM((M_HEADS, 1, GATE_PAD), F32)],
        compiler_params=_cparams("parallel", "arbitrary"),
        name="mlstm",
    )(z, z, z, z, gates, gate_bias, g_mhead.reshape(M_HEADS, 1, M_V_DIM), c0_aug, m0)


def _t5_bucket(rel):
    n = jnp.maximum(rel, 0)
    nf = jnp.maximum(n, 1).astype(F32)
    large = MAX_EXACT + (jnp.log(nf / MAX_EXACT) / math.log(MAX_DISTANCE / MAX_EXACT)
                         * (NUM_BUCKETS - MAX_EXACT)).astype(jnp.int32)
    large = jnp.minimum(large, NUM_BUCKETS - 1)
    return jnp.where(n < MAX_EXACT, n, large)


def _bias_kernel(tab_ref, out_ref):
    h = pl.program_id(0)
    i = lax.broadcasted_iota(jnp.int32, (MOBA_BLOCK, MOBA_BLOCK), 1)
    j = lax.broadcasted_iota(jnp.int32, (MOBA_BLOCK, MOBA_BLOCK), 0)
    for kind, rel in ((0, i - j), (1, MOBA_BLOCK + i - j)):
        bucket = _t5_bucket(rel)
        vals = [tab_ref[h, b] for b in range(NUM_BUCKETS)]
        bit = 1
        while len(vals) > 1:
            odd = (bucket & bit) != 0
            vals = [jnp.where(odd, vals[2 * m + 1], vals[2 * m]) for m in range(len(vals) // 2)]
            bit *= 2
        out_ref[0, kind] = vals[0]
    out_ref[0, 2] = jnp.full((MOBA_BLOCK, MOBA_BLOCK), tab_ref[h, NUM_BUCKETS - 1], F32)


def _bias_tiles(rel_table):
    assert MOBA_BLOCK + 1 >= MAX_DISTANCE and NUM_BUCKETS & (NUM_BUCKETS - 1) == 0
    tab = rel_table.T.astype(F32)
    return pl.pallas_call(
        _bias_kernel,
        grid=(A_HEADS,),
        in_specs=[pl.BlockSpec(memory_space=pltpu.SMEM)],
        out_specs=pl.BlockSpec((1, 3, MOBA_BLOCK, MOBA_BLOCK), lambda h: (h, 0, 0, 0)),
        out_shape=jax.ShapeDtypeStruct((A_HEADS, 3, MOBA_BLOCK, MOBA_BLOCK), F32),
        compiler_params=_cparams("parallel"),
        name="t5_bias_tiles",
    )(tab)


def _moba_prompt_kernel(q_ref, k_ref, v_ref, bias_ref, wu_ref, wd_ref, o_ref, wub_ref, wdb_ref,
                        kb_scr, qt_scr, vt_scr, gt_scr, s_scr, *, n_blocks):
    nb = n_blocks
    blk = MOBA_BLOCK
    nbp = gt_scr.shape[0]
    wub_ref[...] = wu_ref[...].astype(BF16)
    wdb_ref[...] = wd_ref[...].astype(BF16)

    kb_scr[...] = k_ref[...].astype(BF16)
    for c in range(nb):
        cols = slice(c * blk, (c + 1) * blk)
        qt_scr[:, cols] = (q_ref[cols, :] * (A_HEAD_DIM ** -0.5)).T.astype(BF16)
        vt_scr[:, cols] = v_ref[cols, :].T.astype(BF16)
    need_gate = nb - 1 > MOBA_TOP_K
    if need_gate:
        means = [jnp.mean(k_ref[n * blk:(n + 1) * blk, :], axis=0, keepdims=True) for n in range(nb)]
        kmean = jnp.concatenate(means + [jnp.zeros((128 - nb, A_HEAD_DIM), F32)], axis=0)
        for c in range(MOBA_TOP_K + 1, nb):
            cols = slice(c * blk, (c + 1) * blk)
            gate = _dot_nt(q_ref[cols, :] * (A_HEAD_DIM ** -0.5), kmean, precision=HIGHEST)
            gt_scr[:, cols] = gate.T[:nbp, :]
    key = lax.broadcasted_iota(jnp.int32, (blk, blk), 0)
    qry = lax.broadcasted_iota(jnp.int32, (blk, blk), 1)
    causal = key <= qry
    blk_id = lax.broadcasted_iota(jnp.int32, (nbp, blk), 0)

    for i in range(nb):
        cols = slice(i * blk, (i + 1) * blk)
        selneg = None
        if i > MOBA_TOP_K:
            g = gt_scr[:, cols]
            rank = jnp.zeros((nbp, blk), F32)
            for j in range(i):
                gj = g[j:j + 1, :]
                rank = rank + ((gj > g) | ((gj == g) & (j < blk_id))).astype(F32)
            selneg = jnp.where(rank < MOBA_TOP_K, 0.0, NEG_INF)
        s_scr[0:(i + 1) * blk, :] = _dot(kb_scr[0:(i + 1) * blk, :], qt_scr[:, cols])
        m = jnp.full((1, blk), NEG_INF, F32)
        for n in range(i + 1):
            rows = slice(n * blk, (n + 1) * blk)
            kind = 0 if n == i else (1 if n == i - 1 else 2)
            st = s_scr[rows, :] + bias_ref[0, kind]
            if n == i:
                st = jnp.where(causal, st, NEG_INF)
            elif selneg is not None:
                st = st + selneg[n:n + 1, :]
            s_scr[rows, :] = st
            m = jnp.maximum(m, jnp.max(st, axis=0, keepdims=True))
        l = jnp.zeros((1, blk), F32)
        acc = jnp.zeros((A_HEAD_DIM, blk), F32)
        for n in range(i + 1):
            rows = slice(n * blk, (n + 1) * blk)
            p = jnp.exp(s_scr[rows, :] - m)
            l = l + jnp.sum(p, axis=0, keepdims=True)
            acc = acc + _dot(vt_scr[:, rows], p.astype(BF16))
        o_ref[cols, :] = (acc / l).T.astype(o_ref.dtype)


def _moba_prompt(aq, ak, av, bias_tiles, w_up, w_down, batch, seq):
    assert seq % MOBA_BLOCK == 0
    nb = seq // MOBA_BLOCK
    assert nb <= 128
    nbp = -(-nb // 8) * 8
    steps = batch * A_HEADS
    slab = D_FF // steps
    assert D_FF % steps == 0 and slab % 128 == 0
    head = pl.BlockSpec((seq, A_HEAD_DIM), lambda b, h: (b, h))
    up_slab = pl.BlockSpec((D_MODEL, slab), lambda b, h: (0, b * A_HEADS + h))
    down_slab = pl.BlockSpec((slab, D_MODEL), lambda b, h: (b * A_HEADS + h, 0))
    kern = functools.partial(_moba_prompt_kernel, n_blocks=nb)
    return pl.pallas_call(
        kern,
        grid=(batch, A_HEADS),
        in_specs=[head, head, head,
                  pl.BlockSpec((1, 3, MOBA_BLOCK, MOBA_BLOCK), lambda b, h: (h, 0, 0, 0)),
                  up_slab, down_slab],
        out_specs=[head, up_slab, down_slab],
        out_shape=[jax.ShapeDtypeStruct((batch * seq, A_HEADS * A_HEAD_DIM), BF16),
                   jax.ShapeDtypeStruct((D_MODEL, D_FF), BF16),
                   jax.ShapeDtypeStruct((D_FF, D_MODEL), BF16)],
        scratch_shapes=[pltpu.VMEM((seq, A_HEAD_DIM), BF16),
                        pltpu.VMEM((A_HEAD_DIM, seq), BF16),
                        pltpu.VMEM((A_HEAD_DIM, seq), BF16),
                        pltpu.VMEM((nbp, seq), F32),
                        pltpu.VMEM((seq, MOBA_BLOCK), F32)],
        compiler_params=_cparams("parallel", "parallel"),
        name="moba_prompt",
    )(aq, ak, av, bias_tiles, w_up, w_down)


PAGES_PER_BLOCK = MOBA_BLOCK // PAGE_SIZE
PAST_BLOCKS_PER_STEP = 8


def _moba_past_kernel(pt_ref, q_ref, *refs, n_pages_step):
    del pt_ref
    k_refs = refs[:n_pages_step]
    v_refs = refs[n_pages_step:2 * n_pages_step]
    bias_ref, o_ref, m_ref, l_ref, km_ref = refs[2 * n_pages_step:]
    hq = q_ref.shape[1]
    nq = hq // A_HEADS
    qb = (q_ref[0] * (A_HEAD_DIM ** -0.5)).reshape(A_HEADS, nq, A_HEAD_DIM).astype(BF16)
    n_blocks = n_pages_step // PAGES_PER_BLOCK
    is_last_step = pl.program_id(1) == pl.num_programs(1) - 1

    def head_rows(ref, h):
        return ref[0, 0, pl.ds(h, PAGE_SIZE, stride=A_HEADS), :]

    for g in range(n_blocks):
        pages = range(g * PAGES_PER_BLOCK, (g + 1) * PAGES_PER_BLOCK)
        ksum = sum(jnp.sum(k_refs[p][0, 0].reshape(PAGE_SIZE, A_HEADS, A_HEAD_DIM), axis=0) for p in pages)
        km_ref[0, g] = ksum / MOBA_BLOCK
        kind = jnp.where(is_last_step, 1, 0) if g == n_blocks - 1 else 0
        kh = jnp.stack([jnp.concatenate([head_rows(k_refs[p], h) for p in pages], axis=0)
                        for h in range(A_HEADS)]).astype(BF16)
        vh = jnp.stack([jnp.concatenate([head_rows(v_refs[p], h) for p in pages], axis=0)
                        for h in range(A_HEADS)]).astype(BF16)
        lt = _bdot_nt(qb, kh) + bias_ref[kind]
        m = jnp.max(lt, axis=2, keepdims=True)
        p_ = jnp.exp(lt - m)
        l = jnp.sum(p_, axis=2, keepdims=True)
        o_ref[0, g] = _bdot(p_.astype(BF16), vh).reshape(hq, A_HEAD_DIM)
        m_ref[0, g] = jnp.broadcast_to(m, (A_HEADS, nq, A_HEAD_DIM)).reshape(hq, A_HEAD_DIM)
        l_ref[0, g] = jnp.broadcast_to(l, (A_HEADS, nq, A_HEAD_DIM)).reshape(hq, A_HEAD_DIM)


def _moba_past(page_table, q_hq, cache_k, cache_v, past_bias, layer):
    bs, n_pages = page_table.shape
    assert n_pages % PAGES_PER_BLOCK == 0
    nbp = n_pages // PAGES_PER_BLOCK
    gb = math.gcd(nbp, PAST_BLOCKS_PER_STEP)
    pps = gb * PAGES_PER_BLOCK
    hq = q_hq.shape[1]
    page = (1, 1, PAGE_SIZE * A_HEADS, A_HEAD_DIM)
    cache_k, cache_v = (c.reshape(c.shape[:2] + page[2:]) for c in (cache_k, cache_v))
    part = pl.BlockSpec((1, gb, hq, A_HEAD_DIM), lambda b, n, pt: (b, n, 0, 0))

    def page_spec(p):
        return pl.BlockSpec(page, lambda b, n, pt: (layer, pt[b, pps * n + p], 0, 0))

    grid_spec = pltpu.PrefetchScalarGridSpec(
        num_scalar_prefetch=1,
        grid=(bs, nbp // gb),
        in_specs=[pl.BlockSpec((1, hq, A_HEAD_DIM), lambda b, n, pt: (b, 0, 0))]
        + [page_spec(p) for p in range(pps)] * 2
        + [pl.BlockSpec((2, A_HEADS, hq // A_HEADS, MOBA_BLOCK), lambda b, n, pt: (0, 0, 0, 0))],
        out_specs=[part, part, part,
                   pl.BlockSpec((1, gb, A_HEADS, A_HEAD_DIM), lambda b, n, pt: (b, n, 0, 0))],
    )
    pshape = jax.ShapeDtypeStruct((bs, nbp, hq, A_HEAD_DIM), F32)
    return pl.pallas_call(
        functools.partial(_moba_past_kernel, n_pages_step=pps),
        grid_spec=grid_spec,
        out_shape=[pshape, pshape, pshape, jax.ShapeDtypeStruct((bs, nbp, A_HEADS, A_HEAD_DIM), F32)],
        compiler_params=_cparams("parallel", "parallel"),
        name="moba_past_blocks",
    )(page_table, q_hq, *([cache_k] * pps), *([cache_v] * pps), past_bias)


def _moba_combine_kernel(q_ref, km_ref, kn_ref, vn_ref, ob_ref, op_ref, mp_ref, lp_ref, out_ref):
    nbp = km_ref.shape[2]
    q = q_ref[0] * (A_HEAD_DIM ** -0.5)
    hq = q.shape[0]
    nq = hq // A_HEADS
    gate = jnp.concatenate(
        [_dot_nt(q[h * nq:(h + 1) * nq], km_ref[0, h], precision=HIGHEST) for h in range(A_HEADS)], axis=0)
    lane = lax.broadcasted_iota(jnp.int32, (hq, nbp), 1)
    rank = jnp.zeros((hq, nbp), F32)
    for j in range(nbp):
        gj = gate[:, j:j + 1]
        beats = (gj > gate) | ((gj == gate) & (j < lane))
        rank = rank + beats.astype(F32)
    selneg = jnp.where(rank < MOBA_TOP_K, 0.0, NEG_INF)
    qb = q.astype(BF16)
    lt = _dot_nt(qb, kn_ref[0].astype(BF16))
    r = lax.broadcasted_iota(jnp.int32, lt.shape, 0)
    c = lax.broadcasted_iota(jnp.int32, lt.shape, 1)
    ok = ((c % A_HEADS) == (r // nq)) & ((c // A_HEADS) <= (r % nq))
    lo = jnp.where(ok, lt + ob_ref[...], NEG_INF)
    m = jnp.max(lo, axis=1, keepdims=True)
    for n in range(nbp):
        m = jnp.maximum(m, mp_ref[0, n][:, 0:1] + selneg[:, n:n + 1])
    p = jnp.exp(lo - m)
    l = jnp.sum(p, axis=1, keepdims=True)
    acc = _dot(p.astype(BF16), vn_ref[0].astype(BF16))
    for n in range(nbp):
        w = jnp.exp(mp_ref[0, n][:, 0:1] + selneg[:, n:n + 1] - m)
        l = l + w * lp_ref[0, n][:, 0:1]
        acc = acc + w * op_ref[0, n]
    out_ref[0] = acc / l


def _moba_combine(q_hq, kmean_hn, k_new, v_new, own_bias, o_part, m_part, l_part):
    bs, hq, _ = q_hq.shape
    nbp = o_part.shape[1]
    part = pl.BlockSpec((1, nbp, hq, A_HEAD_DIM), lambda b: (b, 0, 0, 0))
    tok = pl.BlockSpec((1, hq, A_HEAD_DIM), lambda b: (b, 0, 0))
    return pl.pallas_call(
        _moba_combine_kernel,
        grid=(bs,),
        in_specs=[tok,
                  pl.BlockSpec((1, A_HEADS, nbp, A_HEAD_DIM), lambda b: (b, 0, 0, 0)),
                  tok, tok,
                  pl.BlockSpec((hq, hq), lambda b: (0, 0)),
                  part, part, part],
        out_specs=tok,
        out_shape=jax.ShapeDtypeStruct((bs, hq, A_HEAD_DIM), F32),
        compiler_params=_cparams("parallel"),
        name="moba_combine",
    )(q_hq, kmean_hn, k_new, v_new, own_bias, o_part, m_part, l_part)


def _out_proj_kernel(h_ref, om_ref, oa_ref, hs_ref, oms_ref, oas_ref, wm_ref, wa_ref, o_ref, os_ref,
                     wm_scr, wa_scr):
    def project(h, om, oa, dst):
        dst[...] = (h[...] + _dot(om[...].astype(BF16), wm_scr[...])
                    + _dot(oa[...].astype(BF16), wa_scr[...]))

    @pl.when(pl.program_id(1) == 0)
    def _():
        for r in range(0, wm_scr.shape[0], _CAST_ROWS):
            wm_scr[r:r + _CAST_ROWS, :] = wm_ref[r:r + _CAST_ROWS, :].astype(BF16)
            wa_scr[r:r + _CAST_ROWS, :] = wa_ref[r:r + _CAST_ROWS, :].astype(BF16)
        project(hs_ref, oms_ref, oas_ref, os_ref)

    project(h_ref, om_ref, oa_ref, o_ref)


def _out_proj(h, out_m, out_a, hs, out_ms, out_as, w_out):
    m, ms = h.shape[0], hs.shape[0]
    tm = min(512, m)
    tn = 1024
    half = M_HEADS * M_V_DIM
    return pl.pallas_call(
        _out_proj_kernel,
        grid=(D_MODEL // tn, m // tm),
        in_specs=[
            pl.BlockSpec((tm, tn), lambda j, i: (i, j)),
            pl.BlockSpec((tm, half), lambda j, i: (i, 0)),
            pl.BlockSpec((tm, half), lambda j, i: (i, 0)),
            pl.BlockSpec((ms, tn), lambda j, i: (0, j)),
            pl.BlockSpec((ms, half), lambda j, i: (0, 0)),
            pl.BlockSpec((ms, half), lambda j, i: (0, 0)),
            pl.BlockSpec((half, tn), lambda j, i: (0, j)),
            pl.BlockSpec((half, tn), lambda j, i: (1, j)),
        ],
        out_specs=[pl.BlockSpec((tm, tn), lambda j, i: (i, j)),
                   pl.BlockSpec((ms, tn), lambda j, i: (0, j))],
        out_shape=[jax.ShapeDtypeStruct((m, D_MODEL), F32), jax.ShapeDtypeStruct((ms, D_MODEL), F32)],
        scratch_shapes=[pltpu.VMEM((half, tn), BF16), pltpu.VMEM((half, tn), BF16)],
        compiler_params=_cparams("arbitrary", "arbitrary"),
        name="out_proj",
    )(h, out_m, out_a, hs, out_ms, out_as, w_out, w_out)


def _mlp_kernel(h_ref, hs_ref, g_ref, wu_ref, wd_ref, o_ref, os_ref, xn_ref, xns_ref):
    f = pl.program_id(1)

    def accumulate(src, dst, xn):
        @pl.when(f == 0)
        def _():
            xn[...] = _rms(src[...], g_ref[...]).astype(BF16)
            dst[...] = src[...]

        u = _dot(xn[...], wu_ref[...])
        a = jnp.square(jnp.maximum(u, 0.0)).astype(BF16)
        dst[...] += _dot(a, wd_ref[...])

    accumulate(h_ref, o_ref, xn_ref)

    @pl.when(pl.program_id(0) == 0)
    def _():
        accumulate(hs_ref, os_ref, xns_ref)


def _mlp(h, hs, g, w_up, w_down):
    m, ms = h.shape[0], hs.shape[0]
    tm = min(512, m)
    tf = 1024
    return pl.pallas_call(
        _mlp_kernel,
        grid=(m // tm, D_FF // tf),
        in_specs=[
            pl.BlockSpec((tm, D_MODEL), lambda i, f: (i, 0)),
            pl.BlockSpec((ms, D_MODEL), lambda i, f: (0, 0)),
            pl.BlockSpec((1, D_MODEL), lambda i, f: (0, 0)),
            pl.BlockSpec((D_MODEL, tf), lambda i, f: (0, f)),
            pl.BlockSpec((tf, D_MODEL), lambda i, f: (f, 0)),
        ],
        out_specs=[pl.BlockSpec((tm, D_MODEL), lambda i, f: (i, 0)),
                   pl.BlockSpec((ms, D_MODEL), lambda i, f: (0, 0))],
        out_shape=[jax.ShapeDtypeStruct((m, D_MODEL), F32), jax.ShapeDtypeStruct((ms, D_MODEL), F32)],
        scratch_shapes=[pltpu.VMEM((tm, D_MODEL), BF16), pltpu.VMEM((ms, D_MODEL), BF16)],
        compiler_params=_cparams("arbitrary", "arbitrary"),
        name="mlp",
    )(h, hs, g, w_up, w_down)


def _ple_kernel(h_ref, pe_ref, hs_ref, pes_ref, g_ref, wg_ref, wp_ref, gf_ref, y_ref, ys_ref):
    def gated_embedding(src, pe, dst):
        h = src[...]
        gate = jax.nn.sigmoid(_dot(_rms(h, g_ref[...]).astype(BF16), wg_ref[...]))
        proj = _dot(pe[...].astype(BF16), wp_ref[...])
        dst[...] = _rms(h + gate * proj, gf_ref[...])

    gated_embedding(h_ref, pe_ref, y_ref)

    @pl.when(pl.program_id(0) == 0)
    def _():
        gated_embedding(hs_ref, pes_ref, ys_ref)


def _ple_final(h, pe, hs, pes, g_ple, w_gate, w_proj, g_final):
    m, ms = h.shape[0], hs.shape[0]
    tm = min(512, m)
    vec = pl.BlockSpec((1, D_MODEL), lambda i: (0, 0))
    return pl.pallas_call(
        _ple_kernel,
        grid=(m // tm,),
        in_specs=[
            pl.BlockSpec((tm, D_MODEL), lambda i: (i, 0)),
            pl.BlockSpec((tm, PLE_DIM), lambda i: (i, 0)),
            pl.BlockSpec((ms, D_MODEL), lambda i: (0, 0)),
            pl.BlockSpec((ms, PLE_DIM), lambda i: (0, 0)),
            vec,
            pl.BlockSpec((D_MODEL, D_MODEL), lambda i: (0, 0)),
            pl.BlockSpec((PLE_DIM, D_MODEL), lambda i: (0, 0)),
            vec,
        ],
        out_specs=[pl.BlockSpec((tm, D_MODEL), lambda i: (i, 0)),
                   pl.BlockSpec((ms, D_MODEL), lambda i: (0, 0))],
        out_shape=[jax.ShapeDtypeStruct((m, D_MODEL), F32), jax.ShapeDtypeStruct((ms, D_MODEL), F32)],
        compiler_params=_cparams("arbitrary"),
        name="ple_final_norm",
    )(h, pe, hs, pes, g_ple, w_gate, w_proj, g_final)


def _layer_weights(w_in, b_igate, b_fgate, g_mix, g_mhead, w_out, g_ffn, w_up, w_down,
                   g_ple, w_ple_gate, w_ple_proj):
    gate_bias = jnp.pad(jnp.concatenate([b_igate, b_fgate]), (0, GATE_PAD - 2 * M_HEADS)).reshape(1, GATE_PAD)
    return dict(
        w_in=w_in.T, gate_bias=gate_bias.astype(F32),
        g_mix=g_mix.reshape(1, D_MODEL), g_mhead=g_mhead, w_out=w_out,
        g_ffn=g_ffn.reshape(1, D_MODEL), w_up=w_up, w_down=w_down,
        g_ple=g_ple.reshape(1, D_MODEL), w_ple_gate=w_ple_gate.astype(BF16),
        w_ple_proj=w_ple_proj.astype(BF16))


def _tail(h, out_m, out_a, pe, hs, out_ms, out_as, pes, w, g_final):
    h, hs = _out_proj(h, out_m, out_a, hs, out_ms, out_as, w["w_out"])
    h, hs = _mlp(h, hs, w["g_ffn"], w["w_up_bf16"], w["w_down_bf16"])
    return _ple_final(h, pe, hs, pes, w["g_ple"], w["w_ple_gate"], w["w_ple_proj"], g_final)


def _split_state(c_aug, m_fin):
    return c_aug[..., :M_V_DIM], c_aug[..., M_V_DIM], m_fin[..., 0, 0]


def kernel(x_prompt, x_sample, cache_k, cache_v, state_C, state_n, state_m, page_table, p_prompt, p_sample, rel_bias_table, w_in, b_igate, b_fgate, g_mix, g_mhead, w_out, g_ffn, w_up, w_down, g_ple, w_ple_gate, w_ple_proj, g_final):
    depth = w_in.shape[0]
    assert depth == 1, "one decoder layer per call"
    layer = 0
    bp, tp, _ = x_prompt.shape
    bs, ts, _ = x_sample.shape
    past_len = page_table.shape[1] * PAGE_SIZE
    assert past_len % MOBA_BLOCK == 0 and ts <= MOBA_BLOCK
    kv_shape = (A_HEADS, A_HEAD_DIM)
    g_fin = g_final.reshape(1, D_MODEL)

    w = _layer_weights(w_in[layer], b_igate[layer], b_fgate[layer], g_mix[layer], g_mhead[layer], w_out[layer],
                       g_ffn[layer], w_up[layer], w_down[layer], g_ple[layer], w_ple_gate[layer],
                       w_ple_proj[layer])
    bias_tiles = _bias_tiles(rel_bias_table)

    hp = x_prompt.reshape(bp * tp, D_MODEL)
    hs = x_sample.reshape(bs * ts, D_MODEL)
    xn_p, gp, xn_s, gs = _norm_gate(hp, hs, w["g_mix"], w["w_in"])
    zp, aq_p, ak_p, av_p, zs = _in_proj(xn_p, xn_s, w["w_in"])
    aq_s, ak_s, av_s = (zs[:, _GATES + t * _IN_TN:_GATES + (t + 1) * _IN_TN] for t in range(3))
    c0 = jnp.zeros((bp, M_HEADS, M_QK_DIM, 2 * M_V_DIM), F32)
    m0 = jnp.zeros((bp, M_HEADS, 1, GATE_PAD), F32)
    om_p, c_p, m_p = _mlstm(zp, gp, w["gate_bias"], w["g_mhead"], c0, m0, bp, tp)
    oa_p, w["w_up_bf16"], w["w_down_bf16"] = _moba_prompt(aq_p, ak_p, av_p, bias_tiles, w["w_up"], w["w_down"],
                                                          bp, tp)
    k_p = ak_p.reshape((1, bp, tp) + kv_shape)
    v_p = av_p.reshape((1, bp, tp) + kv_shape)
    cp, np_, mp = _split_state(c_p, m_p)

    c0s = jnp.concatenate([state_C[layer], state_n[layer][..., None],
                           jnp.zeros((bs, M_HEADS, M_QK_DIM, M_V_DIM - 1), F32)], axis=-1)
    m0s = jnp.broadcast_to(state_m[layer][..., None, None], (bs, M_HEADS, 1, GATE_PAD))
    om_s, c_s, m_s = _mlstm(zs, gs, w["gate_bias"], w["g_mhead"], c0s, m0s, bs, ts)

    hq = A_HEADS * ts
    q_s = aq_s.reshape(bs, ts, A_HEADS, A_HEAD_DIM)
    q_hq = q_s.transpose(0, 2, 1, 3).reshape(bs, hq, A_HEAD_DIM)
    k_new = ak_s.reshape(bs, hq, A_HEAD_DIM)
    v_new = av_s.reshape(bs, hq, A_HEAD_DIM)
    rows = bias_tiles[:, :, :, :ts].transpose(0, 1, 3, 2)
    expand = lambda a: jnp.repeat(a, A_HEADS, axis=-1).reshape(hq, -1)
    past_bias = jnp.stack([rows[:, 2], rows[:, 1]])
    own_bias = expand(rows[:, 0, :, :ts])
    o_part, m_part, l_part, kmean = _moba_past(page_table, q_hq, cache_k, cache_v, past_bias, layer)
    oa_hq = _moba_combine(q_hq, kmean.transpose(0, 2, 1, 3), k_new, v_new, own_bias, o_part, m_part, l_part)
    oa_s = oa_hq.reshape(bs, A_HEADS, ts, A_HEAD_DIM).transpose(0, 2, 1, 3).reshape(bs * ts, A_HEADS * A_HEAD_DIM)
    y_p, y_s = _tail(hp, om_p, oa_p, p_prompt[layer].reshape(bp * tp, PLE_DIM),
                     hs, om_s, oa_s, p_sample[layer].reshape(bs * ts, PLE_DIM), w, g_fin)
    k_s = k_new.reshape((1, bs, ts) + kv_shape)
    v_s = v_new.reshape((1, bs, ts) + kv_shape)
    cs, ns, ms = _split_state(c_s, m_s)

    return (y_p.reshape(bp, tp, D_MODEL), y_s.reshape(bs, ts, D_MODEL),
            k_p, v_p, cp[None], np_[None], mp[None],
            k_s, v_s, cs[None], ns[None], ms[None])
```

```python
import functools
import math

import jax
import jax.numpy as jnp
from jax import lax
from jax.experimental import pallas as pl
from jax.experimental.pallas import tpu as pltpu

F32 = jnp.float32
BF16 = jnp.bfloat16
HIGHEST = lax.Precision.HIGHEST

D_MODEL = 2048
M_HEADS = 8
M_V_DIM = 128
M_QK_DIM = 64
M_CHUNK = 64
A_HEADS = 8
A_HEAD_DIM = 128
MOBA_BLOCK = 256
MOBA_TOP_K = 3
NUM_BUCKETS = 32
MAX_EXACT = NUM_BUCKETS // 2
MAX_DISTANCE = 128
D_FF = 4 * D_MODEL
PLE_DIM = 256
PAGE_SIZE = 128
EPS = 1e-6

_MQ, _MK, _MV, _MO = 0, 512, 1024, 2048
_GATES = 3072
_AQ = 3088
D_MAIN = 6144
GATE_PAD = 128

VMEM_LIMIT = 56 * 1024 * 1024
NEG_INF = float("-inf")


def _cparams(*sem):
    return pltpu.CompilerParams(dimension_semantics=sem, vmem_limit_bytes=VMEM_LIMIT)


def _rms(x, g):
    return x * lax.rsqrt(jnp.mean(x * x, axis=-1, keepdims=True) + EPS) * g


def _dot_nt(a, b, **kw):
    return lax.dot_general(a, b, (((1,), (1,)), ((), ())), preferred_element_type=F32, **kw)


def _dot_tn(a, b, **kw):
    return lax.dot_general(a, b, (((0,), (0,)), ((), ())), preferred_element_type=F32, **kw)


def _dot(a, b, **kw):
    return jnp.dot(a, b, preferred_element_type=F32, **kw)


_IN_TN = 1024
_N_IN_TILES = D_MAIN // _IN_TN
_N_MLSTM_TILES = _GATES // _IN_TN
_GATE_COLS = _AQ - _GATES
_CAST_ROWS = 256


def _norm_gate_kernel(x_ref, xs_ref, g_ref, wg_ref, xn_ref, gate_ref, xns_ref, gates_ref):
    wg = wg_ref[...]
    hi = wg.astype(BF16).astype(F32)
    row = lax.broadcasted_iota(jnp.int32, wg.shape, 0)
    w2 = jnp.where(row < _GATE_COLS, hi,
                   jnp.where(row < 2 * _GATE_COLS, pltpu.roll(wg - hi, _GATE_COLS, axis=0), 0.0))
    w2 = w2.astype(BF16)

    def norm_and_gate(src_ref, dst_ref, gate_dst_ref):
        xn = _rms(src_ref[...], g_ref[...]).astype(BF16)
        dst_ref[...] = xn
        r = _dot_nt(xn, w2)
        gate_dst_ref[...] = r + pltpu.roll(r, GATE_PAD - _GATE_COLS, axis=1)

    norm_and_gate(x_ref, xn_ref, gate_ref)

    @pl.when(pl.program_id(0) == 0)
    def _():
        norm_and_gate(xs_ref, xns_ref, gates_ref)


def _norm_gate(x, xs, g, w_in_t):
    m, ms = x.shape[0], xs.shape[0]
    tm = min(512, m)
    return pl.pallas_call(
        _norm_gate_kernel,
        grid=(m // tm,),
        in_specs=[
            pl.BlockSpec((tm, D_MODEL), lambda i: (i, 0)),
            pl.BlockSpec((ms, D_MODEL), lambda i: (0, 0)),
            pl.BlockSpec((1, D_MODEL), lambda i: (0, 0)),
            pl.BlockSpec((GATE_PAD, D_MODEL), lambda i: (_GATES // GATE_PAD, 0)),
        ],
        out_specs=[pl.BlockSpec((tm, D_MODEL), lambda i: (i, 0)),
                   pl.BlockSpec((tm, GATE_PAD), lambda i: (i, 0)),
                   pl.BlockSpec((ms, D_MODEL), lambda i: (0, 0)),
                   pl.BlockSpec((ms, GATE_PAD), lambda i: (0, 0))],
        out_shape=[jax.ShapeDtypeStruct((m, D_MODEL), BF16), jax.ShapeDtypeStruct((m, GATE_PAD), F32),
                   jax.ShapeDtypeStruct((ms, D_MODEL), BF16), jax.ShapeDtypeStruct((ms, GATE_PAD), F32)],
        compiler_params=_cparams("arbitrary"),
        name="norm_gate",
    )(x, xs, g, w_in_t)


def _in_proj_kernel(xn_ref, xs_ref, wa_ref, wb_ref, zm_ref, aq_ref, ak_ref, av_ref, zs_ref, w_scr):
    j = pl.program_id(0)
    i = pl.program_id(1)

    @pl.when((i == 0) & (j < _N_MLSTM_TILES))
    def _():
        for r in range(0, _IN_TN, _CAST_ROWS):
            w_scr[r:r + _CAST_ROWS, :] = wa_ref[r:r + _CAST_ROWS, :].astype(BF16)

    @pl.when((i == 0) & (j >= _N_MLSTM_TILES))
    def _():
        body = _IN_TN - _GATE_COLS
        for r in range(0, body, _CAST_ROWS):
            n = min(_CAST_ROWS, body - r)
            w_scr[r:r + n, :] = wa_ref[_GATE_COLS + r:_GATE_COLS + r + n, :].astype(BF16)
        w_scr[body:, :] = wb_ref[:_GATE_COLS, :].astype(BF16)

    @pl.when(i == 0)
    def _():
        zs_ref[...] = _dot_nt(xs_ref[...], w_scr[...])

    @pl.when(j < _N_MLSTM_TILES)
    def _():
        zm_ref[...] = _dot_nt(xn_ref[...], w_scr[...])

    for t, ref in enumerate((aq_ref, ak_ref, av_ref)):
        @pl.when(j == _N_MLSTM_TILES + t)
        def _(ref=ref):
            ref[...] = _dot_nt(xn_ref[...], w_scr[...])


def _in_proj(xn, xs, w_in_t):
    m, ms = xn.shape[0], xs.shape[0]
    tm = min(512, m)
    tn = _IN_TN
    last = m // tm - 1

    def held(j, i, first_tile, last_tile):
        return jnp.where(j < first_tile, 0, jnp.where(j <= last_tile, i, last))

    def section(t):
        return pl.BlockSpec((tm, tn), lambda j, i: (held(j, i, t, t), 0))

    sec_shape = jax.ShapeDtypeStruct((m, tn), F32)
    nm = _N_MLSTM_TILES
    return pl.pallas_call(
        _in_proj_kernel,
        grid=(_N_IN_TILES, m // tm),
        in_specs=[
            pl.BlockSpec((tm, D_MODEL), lambda j, i: (i, 0)),
            pl.BlockSpec((ms, D_MODEL), lambda j, i: (0, 0)),
            pl.BlockSpec((tn, D_MODEL), lambda j, i: (j, 0)),
            pl.BlockSpec((GATE_PAD, D_MODEL), lambda j, i: ((j + 1) * (tn // GATE_PAD), 0)),
        ],
        out_specs=[
            pl.BlockSpec((tm, tn), lambda j, i: (held(j, i, 0, nm - 1), jnp.minimum(j, nm - 1))),
            section(nm), section(nm + 1), section(nm + 2),
            pl.BlockSpec((ms, tn), lambda j, i: (0, j)),
        ],
        out_shape=[jax.ShapeDtypeStruct((m, _GATES), F32), sec_shape, sec_shape, sec_shape,
                   jax.ShapeDtypeStruct((ms, D_MAIN), F32)],
        scratch_shapes=[pltpu.VMEM((tn, D_MODEL), BF16)],
        compiler_params=_cparams("arbitrary", "arbitrary"),
        name="in_proj",
    )(xn, xs, w_in_t, w_in_t)


def _log_sigmoid(x):
    return -(jnp.maximum(-x, 0.0) + jnp.log1p(jnp.exp(-jnp.abs(x))))


def _bdot(a, b):
    return lax.dot_general(a, b, (((2,), (1,)), ((0,), (0,))), preferred_element_type=F32)


def _bdot_nt(a, b):
    return lax.dot_general(a, b, (((2,), (2,)), ((0,), (0,))), preferred_element_type=F32)


def _bdot_tn(a, b):
    return lax.dot_general(a, b, (((1,), (1,)), ((0,), (0,))), preferred_element_type=F32)


def _mlstm_kernel(q_ref, k_ref, v_ref, o_ref, g_ref, gb_ref, gh_ref, c0_ref, m0_ref,
                  out_ref, cfin_ref, mfin_ref, c_scr, m_scr, *, chunk, n_chunks):
    L = chunk
    H = M_HEADS
    tb = L * n_chunks
    t = pl.program_id(1)

    @pl.when(t == 0)
    def _():
        c_scr[...] = c0_ref[0]
        m_scr[...] = m0_ref[0]

    g = g_ref[...] + gb_ref[...]
    lane = lax.broadcasted_iota(jnp.int32, (tb, GATE_PAD), 1)
    gl = jnp.where((lane >= H) & (lane < 2 * H), _log_sigmoid(g), g)
    r = lax.broadcasted_iota(jnp.int32, (tb, tb), 0)
    c = lax.broadcasted_iota(jnp.int32, (tb, tb), 1)
    tri = ((r // L == c // L) & (c <= r)).astype(F32)
    cum = _dot(tri, gl, precision=HIGHEST)

    def rows(ci):
        return slice(ci * L, (ci + 1) * L)

    def stack(f):
        return jnp.stack([f(ci, h) for ci in range(n_chunks) for h in range(H)])

    ones_col = (lax.broadcasted_iota(jnp.int32, (L, M_V_DIM), 1) == 0).astype(BF16)
    li = stack(lambda ci, h: gl[rows(ci), h:h + 1])
    b = stack(lambda ci, h: cum[rows(ci), H + h:H + h + 1])
    qb = stack(lambda ci, h: q_ref[rows(ci), h * M_QK_DIM:(h + 1) * M_QK_DIM]).astype(BF16)
    kc = stack(lambda ci, h: k_ref[rows(ci), h * M_QK_DIM:(h + 1) * M_QK_DIM]) * (M_QK_DIM ** -0.5)
    vaug = stack(lambda ci, h: jnp.concatenate(
        [v_ref[rows(ci), h * M_V_DIM:(h + 1) * M_V_DIM].astype(BF16), ones_col], axis=1))

    rl = lax.broadcasted_iota(jnp.int32, (L, L), 0)
    cl = lax.broadcasted_iota(jnp.int32, (L, L), 1)
    w_row = jnp.sum(jnp.where(rl == cl, li - b, 0.0), axis=1, keepdims=True)
    d = jnp.where(cl <= rl, b + w_row, NEG_INF)
    m_loc = jnp.max(d, axis=2, keepdims=True)
    s = _bdot_nt(qb, kc.astype(BF16)) * jnp.exp(d - m_loc)
    nd_loc = _bdot(s.astype(BF16), vaug)
    g_last = m_loc[:, L - 1:L, :]
    b_last = b[:, L - 1:L, :]
    ws = jnp.exp(b_last - b + li - g_last)
    dc_loc = _bdot_tn((kc * ws).astype(BF16), vaug)

    m_prev = m_scr[:, :, 0:1]
    c_aug = c_scr[...]
    for ci in range(n_chunks):
        grp = slice(ci * H, (ci + 1) * H)
        a = b[grp] + m_prev
        m_t = jnp.maximum(a, m_loc[grp])
        nd = (jnp.exp(a - m_t) * _bdot(qb[grp], c_aug.astype(BF16))
              + jnp.exp(m_loc[grp] - m_t) * nd_loc[grp])
        hh = nd[:, :, :M_V_DIM] / jnp.maximum(jnp.abs(nd[:, :, M_V_DIM:M_V_DIM + 1]), jnp.exp(-m_t))
        hn = _rms(hh, gh_ref[...])
        for h in range(H):
            og = jax.nn.sigmoid(o_ref[rows(ci), h * M_V_DIM:(h + 1) * M_V_DIM])
            out_ref[rows(ci), h * M_V_DIM:(h + 1) * M_V_DIM] = (og * hn[h]).astype(out_ref.dtype)
        m_new = jnp.maximum(b_last[grp] + m_prev, g_last[grp])
        c_aug = (jnp.exp(b_last[grp] + m_prev - m_new) * c_aug
                 + jnp.exp(g_last[grp] - m_new) * dc_loc[grp])
        m_prev = m_new
    c_scr[...] = c_aug
    m_scr[...] = jnp.broadcast_to(m_prev, m_scr.shape)

    @pl.when(t == pl.num_programs(1) - 1)
    def _():
        cfin_ref[0] = c_scr[...]
        mfin_ref[0] = m_scr[...]


def _mlstm(z, gates, gate_bias, g_mhead, c0_aug, m0, batch, seq):
    L = math.gcd(seq, M_CHUNK)
    n_chunks = max(1, min(seq // L, 256 // L))
    tb = L * n_chunks
    nt = seq // tb
    row = lambda b, t: b * nt + t
    kern = functools.partial(_mlstm_kernel, chunk=L, n_chunks=n_chunks)
    return pl.pallas_call(
        kern,
        grid=(batch, nt),
        in_specs=[
            pl.BlockSpec((tb, 512), lambda b, t: (row(b, t), _MQ // 512)),
            pl.BlockSpec((tb, 512), lambda b, t: (row(b, t), _MK // 512)),
            pl.BlockSpec((tb, 1024), lambda b, t: (row(b, t), _MV // 1024)),
            pl.BlockSpec((tb, 1024), lambda b, t: (row(b, t), _MO // 1024)),
            pl.BlockSpec((tb, GATE_PAD), lambda b, t: (row(b, t), 0)),
            pl.BlockSpec((1, GATE_PAD), lambda b, t: (0, 0)),
            pl.BlockSpec((M_HEADS, 1, M_V_DIM), lambda b, t: (0, 0, 0)),
            pl.BlockSpec((1, M_HEADS, M_QK_DIM, 2 * M_V_DIM), lambda b, t: (b, 0, 0, 0)),
            pl.BlockSpec((1, M_HEADS, 1, GATE_PAD), lambda b, t: (b, 0, 0, 0)),
        ],
        out_specs=[
            pl.BlockSpec((tb, M_HEADS * M_V_DIM), lambda b, t: (row(b, t), 0)),
            pl.BlockSpec((1, M_HEADS, M_QK_DIM, 2 * M_V_DIM), lambda b, t: (b, 0, 0, 0)),
            pl.BlockSpec((1, M_HEADS, 1, GATE_PAD), lambda b, t: (b, 0, 0, 0)),
        ],
        out_shape=[
            jax.ShapeDtypeStruct((batch * seq, M_HEADS * M_V_DIM), BF16 if tb % 16 == 0 else F32),
            jax.ShapeDtypeStruct((batch, M_HEADS, M_QK_DIM, 2 * M_V_DIM), F32),
            jax.ShapeDtypeStruct((batch, M_HEADS, 1, GATE_PAD), F32),
        ],
        scratch_shapes=[pltpu.VMEM((M_HEADS, M_QK_DIM, 2 * M_V_DIM), F32),
                        pltpu.VMEM((M_HEADS, 1, GATE_PAD), F32)],
        compiler_params=_cparams("parallel", "arbitrary"),
        name="mlstm",
    )(z, z, z, z, gates, gate_bias, g_mhead.reshape(M_HEADS, 1, M_V_DIM), c0_aug, m0)


def _t5_bucket(rel):
    n = jnp.maximum(rel, 0)
    nf = jnp.maximum(n, 1).astype(F32)
    large = MAX_EXACT + (jnp.log(nf / MAX_EXACT) / math.log(MAX_DISTANCE / MAX_EXACT)
                         * (NUM_BUCKETS - MAX_EXACT)).astype(jnp.int32)
    large = jnp.minimum(large, NUM_BUCKETS - 1)
    return jnp.where(n < MAX_EXACT, n, large)


def _bias_kernel(tab_ref, out_ref):
    h = pl.program_id(0)
    i = lax.broadcasted_iota(jnp.int32, (MOBA_BLOCK, MOBA_BLOCK), 1)
    j = lax.broadcasted_iota(jnp.int32, (MOBA_BLOCK, MOBA_BLOCK), 0)
    for kind, rel in ((0, i - j), (1, MOBA_BLOCK + i - j)):
        bucket = _t5_bucket(rel)
        vals = [tab_ref[h, b] for b in range(NUM_BUCKETS)]
        bit = 1
        while len(vals) > 1:
            odd = (bucket & bit) != 0
            vals = [jnp.where(odd, vals[2 * m + 1], vals[2 * m]) for m in range(len(vals) // 2)]
            bit *= 2
        out_ref[0, kind] = vals[0]
    out_ref[0, 2] = jnp.full((MOBA_BLOCK, MOBA_BLOCK), tab_ref[h, NUM_BUCKETS - 1], F32)


def _bias_tiles(rel_table):
    assert MOBA_BLOCK + 1 >= MAX_DISTANCE and NUM_BUCKETS & (NUM_BUCKETS - 1) == 0
    tab = rel_table.T.astype(F32)
    return pl.pallas_call(
        _bias_kernel,
        grid=(A_HEADS,),
        in_specs=[pl.BlockSpec(memory_space=pltpu.SMEM)],
        out_specs=pl.BlockSpec((1, 3, MOBA_BLOCK, MOBA_BLOCK), lambda h: (h, 0, 0, 0)),
        out_shape=jax.ShapeDtypeStruct((A_HEADS, 3, MOBA_BLOCK, MOBA_BLOCK), F32),
        compiler_params=_cparams("parallel"),
        name="t5_bias_tiles",
    )(tab)


_VT_ROWS = A_HEAD_DIM + 16


def _moba_prompt_kernel(q_ref, k_ref, v_ref, bias_ref, wu_ref, wd_ref, o_ref, wub_ref, wdb_ref,
                        kb_scr, qt_scr, vt_scr, gt_scr, s_scr, p_scr, *, n_blocks):
    nb = n_blocks
    blk = MOBA_BLOCK
    nbp = gt_scr.shape[0]
    wub_ref[...] = wu_ref[...].astype(BF16)
    wdb_ref[...] = wd_ref[...].astype(BF16)

    kb_scr[...] = k_ref[...].astype(BF16)
    for c in range(nb):
        cols = slice(c * blk, (c + 1) * blk)
        qt_scr[:, cols] = (q_ref[cols, :] * (A_HEAD_DIM ** -0.5)).T.astype(BF16)
        vt_scr[0:A_HEAD_DIM, cols] = v_ref[cols, :].T.astype(BF16)
    ones_row = (lax.broadcasted_iota(jnp.int32, (_VT_ROWS - A_HEAD_DIM, nb * blk), 0) == 0).astype(BF16)
    vt_scr[A_HEAD_DIM:, :] = ones_row
    need_gate = nb - 1 > MOBA_TOP_K
    if need_gate:
        means = [jnp.mean(k_ref[n * blk:(n + 1) * blk, :], axis=0, keepdims=True) for n in range(nb)]
        kmean = jnp.concatenate(means + [jnp.zeros((128 - nb, A_HEAD_DIM), F32)], axis=0)
        for c in range(MOBA_TOP_K + 1, nb):
            cols = slice(c * blk, (c + 1) * blk)
            gate = _dot_nt(q_ref[cols, :] * (A_HEAD_DIM ** -0.5), kmean, precision=HIGHEST)
            gt_scr[:, cols] = gate.T[:nbp, :]
    key = lax.broadcasted_iota(jnp.int32, (blk, blk), 0)
    qry = lax.broadcasted_iota(jnp.int32, (blk, blk), 1)
    causal = key <= qry
    blk_id = lax.broadcasted_iota(jnp.int32, (nbp, blk), 0)

    def scores(i):
        cols = slice(i * blk, (i + 1) * blk)
        s_scr[i % 2, 0:(i + 1) * blk, :] = _dot(kb_scr[0:(i + 1) * blk, :], qt_scr[:, cols])

    def softmax(i):
        cols = slice(i * blk, (i + 1) * blk)
        s_buf, p_buf = s_scr.at[i % 2], p_scr.at[i % 2]
        selneg = None
        if i > MOBA_TOP_K:
            g = gt_scr[:, cols]
            rank = jnp.zeros((nbp, blk), F32)
            for j in range(i):
                gj = g[j:j + 1, :]
                rank = rank + ((gj > g) | ((gj == g) & (j < blk_id))).astype(F32)
            selneg = jnp.where(rank < MOBA_TOP_K, 0.0, NEG_INF)
        m = jnp.full((1, blk), NEG_INF, F32)
        for n in range(i + 1):
            rows = slice(n * blk, (n + 1) * blk)
            kind = 0 if n == i else (1 if n == i - 1 else 2)
            st = s_buf[rows, :] + bias_ref[0, kind]
            if n == i:
                st = jnp.where(causal, st, NEG_INF)
            elif selneg is not None:
                st = st + selneg[n:n + 1, :]
            s_buf[rows, :] = st
            m = jnp.maximum(m, jnp.max(st, axis=0, keepdims=True))
        for n in range(i + 1):
            rows = slice(n * blk, (n + 1) * blk)
            p_buf[rows, :] = jnp.exp(s_buf[rows, :] - m).astype(BF16)

    def values(i):
        cols = slice(i * blk, (i + 1) * blk)
        nd = _dot(vt_scr[:, 0:(i + 1) * blk], p_scr[i % 2, 0:(i + 1) * blk, :])
        out = nd[0:A_HEAD_DIM, :] / nd[A_HEAD_DIM:A_HEAD_DIM + 1, :]
        o_ref[cols, :] = out.T.astype(o_ref.dtype)

    scores(0)
    for i in range(nb):
        if i + 1 < nb:
            scores(i + 1)
        softmax(i)
        values(i)


def _moba_prompt(aq, ak, av, bias_tiles, w_up, w_down, batch, seq):
    assert seq % MOBA_BLOCK == 0
    nb = seq // MOBA_BLOCK
    assert nb <= 128
    nbp = -(-nb // 8) * 8
    steps = batch * A_HEADS
    slab = D_FF // steps
    assert D_FF % steps == 0 and slab % 128 == 0
    head = pl.BlockSpec((seq, A_HEAD_DIM), lambda b, h: (b, h))
    up_slab = pl.BlockSpec((D_MODEL, slab), lambda b, h: (0, b * A_HEADS + h))
    down_slab = pl.BlockSpec((slab, D_MODEL), lambda b, h: (b * A_HEADS + h, 0))
    kern = functools.partial(_moba_prompt_kernel, n_blocks=nb)
    return pl.pallas_call(
        kern,
        grid=(batch, A_HEADS),
        in_specs=[head, head, head,
                  pl.BlockSpec((1, 3, MOBA_BLOCK, MOBA_BLOCK), lambda b, h: (h, 0, 0, 0)),
                  up_slab, down_slab],
        out_specs=[head, up_slab, down_slab],
        out_shape=[jax.ShapeDtypeStruct((batch * seq, A_HEADS * A_HEAD_DIM), BF16),
                   jax.ShapeDtypeStruct((D_MODEL, D_FF), BF16),
                   jax.ShapeDtypeStruct((D_FF, D_MODEL), BF16)],
        scratch_shapes=[pltpu.VMEM((seq, A_HEAD_DIM), BF16),
                        pltpu.VMEM((A_HEAD_DIM, seq), BF16),
                        pltpu.VMEM((_VT_ROWS, seq), BF16),
                        pltpu.VMEM((nbp, seq), F32),
                        pltpu.VMEM((2, seq, MOBA_BLOCK), F32),
                        pltpu.VMEM((2, seq, MOBA_BLOCK), BF16)],
        compiler_params=_cparams("parallel", "parallel"),
        name="moba_prompt",
    )(aq, ak, av, bias_tiles, w_up, w_down)


PAGES_PER_BLOCK = MOBA_BLOCK // PAGE_SIZE
PAST_BLOCKS_PER_STEP = 8


def _moba_past_kernel(pt_ref, q_ref, *refs, n_pages_step):
    del pt_ref
    k_refs = refs[:n_pages_step]
    v_refs = refs[n_pages_step:2 * n_pages_step]
    bias_ref, o_ref, m_ref, l_ref, km_ref = refs[2 * n_pages_step:]
    hq = q_ref.shape[1]
    nq = hq // A_HEADS
    qb = (q_ref[0] * (A_HEAD_DIM ** -0.5)).reshape(A_HEADS, nq, A_HEAD_DIM).astype(BF16)
    n_blocks = n_pages_step // PAGES_PER_BLOCK
    is_last_step = pl.program_id(1) == pl.num_programs(1) - 1

    def head_rows(ref, h):
        return ref[0, 0, pl.ds(h, PAGE_SIZE, stride=A_HEADS), :]

    for g in range(n_blocks):
        pages = range(g * PAGES_PER_BLOCK, (g + 1) * PAGES_PER_BLOCK)
        ksum = sum(jnp.sum(k_refs[p][0, 0].reshape(PAGE_SIZE, A_HEADS, A_HEAD_DIM), axis=0) for p in pages)
        km_ref[0, g] = ksum / MOBA_BLOCK
        kind = jnp.where(is_last_step, 1, 0) if g == n_blocks - 1 else 0
        kh = jnp.stack([jnp.concatenate([head_rows(k_refs[p], h) for p in pages], axis=0)
                        for h in range(A_HEADS)]).astype(BF16)
        vh = jnp.stack([jnp.concatenate([head_rows(v_refs[p], h) for p in pages], axis=0)
                        for h in range(A_HEADS)]).astype(BF16)
        lt = _bdot_nt(qb, kh) + bias_ref[kind]
        m = jnp.max(lt, axis=2, keepdims=True)
        p_ = jnp.exp(lt - m)
        l = jnp.sum(p_, axis=2, keepdims=True)
        o_ref[0, g] = _bdot(p_.astype(BF16), vh).reshape(hq, A_HEAD_DIM)
        m_ref[0, g] = jnp.broadcast_to(m, (A_HEADS, nq, A_HEAD_DIM)).reshape(hq, A_HEAD_DIM)
        l_ref[0, g] = jnp.broadcast_to(l, (A_HEADS, nq, A_HEAD_DIM)).reshape(hq, A_HEAD_DIM)


def _moba_past(page_table, q_hq, cache_k, cache_v, past_bias, layer):
    bs, n_pages = page_table.shape
    assert n_pages % PAGES_PER_BLOCK == 0
    nbp = n_pages // PAGES_PER_BLOCK
    gb = math.gcd(nbp, PAST_BLOCKS_PER_STEP)
    pps = gb * PAGES_PER_BLOCK
    hq = q_hq.shape[1]
    page = (1, 1, PAGE_SIZE * A_HEADS, A_HEAD_DIM)
    cache_k, cache_v = (c.reshape(c.shape[:2] + page[2:]) for c in (cache_k, cache_v))
    part = pl.BlockSpec((1, gb, hq, A_HEAD_DIM), lambda b, n, pt: (b, n, 0, 0))

    def page_spec(p):
        return pl.BlockSpec(page, lambda b, n, pt: (layer, pt[b, pps * n + p], 0, 0))

    grid_spec = pltpu.PrefetchScalarGridSpec(
        num_scalar_prefetch=1,
        grid=(bs, nbp // gb),
        in_specs=[pl.BlockSpec((1, hq, A_HEAD_DIM), lambda b, n, pt: (b, 0, 0))]
        + [page_spec(p) for p in range(pps)] * 2
        + [pl.BlockSpec((2, A_HEADS, hq // A_HEADS, MOBA_BLOCK), lambda b, n, pt: (0, 0, 0, 0))],
        out_specs=[part, part, part,
                   pl.BlockSpec((1, gb, A_HEADS, A_HEAD_DIM), lambda b, n, pt: (b, n, 0, 0))],
    )
    pshape = jax.ShapeDtypeStruct((bs, nbp, hq, A_HEAD_DIM), F32)
    return pl.pallas_call(
        functools.partial(_moba_past_kernel, n_pages_step=pps),
        grid_spec=grid_spec,
        out_shape=[pshape, pshape, pshape, jax.ShapeDtypeStruct((bs, nbp, A_HEADS, A_HEAD_DIM), F32)],
        compiler_params=_cparams("parallel", "parallel"),
        name="moba_past_blocks",
    )(page_table, q_hq, *([cache_k] * pps), *([cache_v] * pps), past_bias)


def _moba_combine_kernel(q_ref, km_ref, kn_ref, vn_ref, ob_ref, op_ref, mp_ref, lp_ref, out_ref):
    nbp = km_ref.shape[2]
    q = q_ref[0] * (A_HEAD_DIM ** -0.5)
    hq = q.shape[0]
    nq = hq // A_HEADS
    gate = jnp.concatenate(
        [_dot_nt(q[h * nq:(h + 1) * nq], km_ref[0, h], precision=HIGHEST) for h in range(A_HEADS)], axis=0)
    lane = lax.broadcasted_iota(jnp.int32, (hq, nbp), 1)
    rank = jnp.zeros((hq, nbp), F32)
    for j in range(nbp):
        gj = gate[:, j:j + 1]
        beats = (gj > gate) | ((gj == gate) & (j < lane))
        rank = rank + beats.astype(F32)
    selneg = jnp.where(rank < MOBA_TOP_K, 0.0, NEG_INF)
    qb = q.astype(BF16)
    lt = _dot_nt(qb, kn_ref[0].astype(BF16))
    r = lax.broadcasted_iota(jnp.int32, lt.shape, 0)
    c = lax.broadcasted_iota(jnp.int32, lt.shape, 1)
    ok = ((c % A_HEADS) == (r // nq)) & ((c // A_HEADS) <= (r % nq))
    lo = jnp.where(ok, lt + ob_ref[...], NEG_INF)
    m = jnp.max(lo, axis=1, keepdims=True)
    for n in range(nbp):
        m = jnp.maximum(m, mp_ref[0, n][:, 0:1] + selneg[:, n:n + 1])
    p = jnp.exp(lo - m)
    l = jnp.sum(p, axis=1, keepdims=True)
    acc = _dot(p.astype(BF16), vn_ref[0].astype(BF16))
    for n in range(nbp):
        w = jnp.exp(mp_ref[0, n][:, 0:1] + selneg[:, n:n + 1] - m)
        l = l + w * lp_ref[0, n][:, 0:1]
        acc = acc + w * op_ref[0, n]
    out_ref[0] = acc / l


def _moba_combine(q_hq, kmean_hn, k_new, v_new, own_bias, o_part, m_part, l_part):
    bs, hq, _ = q_hq.shape
    nbp = o_part.shape[1]
    part = pl.BlockSpec((1, nbp, hq, A_HEAD_DIM), lambda b: (b, 0, 0, 0))
    tok = pl.BlockSpec((1, hq, A_HEAD_DIM), lambda b: (b, 0, 0))
    return pl.pallas_call(
        _moba_combine_kernel,
        grid=(bs,),
        in_specs=[tok,
                  pl.BlockSpec((1, A_HEADS, nbp, A_HEAD_DIM), lambda b: (b, 0, 0, 0)),
                  tok, tok,
                  pl.BlockSpec((hq, hq), lambda b: (0, 0)),
                  part, part, part],
        out_specs=tok,
        out_shape=jax.ShapeDtypeStruct((bs, hq, A_HEAD_DIM), F32),
        compiler_params=_cparams("parallel"),
        name="moba_combine",
    )(q_hq, kmean_hn, k_new, v_new, own_bias, o_part, m_part, l_part)


def _out_proj_kernel(h_ref, om_ref, oa_ref, hs_ref, oms_ref, oas_ref, wm_ref, wa_ref, o_ref, os_ref,
                     wm_scr, wa_scr):
    def project(h, om, oa, dst):
        dst[...] = (h[...] + _dot(om[...].astype(BF16), wm_scr[...])
                    + _dot(oa[...].astype(BF16), wa_scr[...]))

    @pl.when(pl.program_id(1) == 0)
    def _():
        for r in range(0, wm_scr.shape[0], _CAST_ROWS):
            wm_scr[r:r + _CAST_ROWS, :] = wm_ref[r:r + _CAST_ROWS, :].astype(BF16)
            wa_scr[r:r + _CAST_ROWS, :] = wa_ref[r:r + _CAST_ROWS, :].astype(BF16)
        project(hs_ref, oms_ref, oas_ref, os_ref)

    project(h_ref, om_ref, oa_ref, o_ref)


def _out_proj(h, out_m, out_a, hs, out_ms, out_as, w_out):
    m, ms = h.shape[0], hs.shape[0]
    tm = min(512, m)
    tn = 1024
    half = M_HEADS * M_V_DIM
    return pl.pallas_call(
        _out_proj_kernel,
        grid=(D_MODEL // tn, m // tm),
        in_specs=[
            pl.BlockSpec((tm, tn), lambda j, i: (i, j)),
            pl.BlockSpec((tm, half), lambda j, i: (i, 0)),
            pl.BlockSpec((tm, half), lambda j, i: (i, 0)),
            pl.BlockSpec((ms, tn), lambda j, i: (0, j)),
            pl.BlockSpec((ms, half), lambda j, i: (0, 0)),
            pl.BlockSpec((ms, half), lambda j, i: (0, 0)),
            pl.BlockSpec((half, tn), lambda j, i: (0, j)),
            pl.BlockSpec((half, tn), lambda j, i: (1, j)),
        ],
        out_specs=[pl.BlockSpec((tm, tn), lambda j, i: (i, j)),
                   pl.BlockSpec((ms, tn), lambda j, i: (0, j))],
        out_shape=[jax.ShapeDtypeStruct((m, D_MODEL), F32), jax.ShapeDtypeStruct((ms, D_MODEL), F32)],
        scratch_shapes=[pltpu.VMEM((half, tn), BF16), pltpu.VMEM((half, tn), BF16)],
        compiler_params=_cparams("arbitrary", "arbitrary"),
        name="out_proj",
    )(h, out_m, out_a, hs, out_ms, out_as, w_out, w_out)


def _mlp_kernel(h_ref, hs_ref, g_ref, wu_ref, wd_ref, o_ref, os_ref, xn_ref, xns_ref):
    f = pl.program_id(1)

    def accumulate(src, dst, xn):
        @pl.when(f == 0)
        def _():
            xn[...] = _rms(src[...], g_ref[...]).astype(BF16)
            dst[...] = src[...]

        u = _dot(xn[...], wu_ref[...])
        a = jnp.square(jnp.maximum(u, 0.0)).astype(BF16)
        dst[...] += _dot(a, wd_ref[...])

    accumulate(h_ref, o_ref, xn_ref)

    @pl.when(pl.program_id(0) == 0)
    def _():
        accumulate(hs_ref, os_ref, xns_ref)


def _mlp(h, hs, g, w_up, w_down):
    m, ms = h.shape[0], hs.shape[0]
    tm = min(512, m)
    tf = 1024
    return pl.pallas_call(
        _mlp_kernel,
        grid=(m // tm, D_FF // tf),
        in_specs=[
            pl.BlockSpec((tm, D_MODEL), lambda i, f: (i, 0)),
            pl.BlockSpec((ms, D_MODEL), lambda i, f: (0, 0)),
            pl.BlockSpec((1, D_MODEL), lambda i, f: (0, 0)),
            pl.BlockSpec((D_MODEL, tf), lambda i, f: (0, f)),
            pl.BlockSpec((tf, D_MODEL), lambda i, f: (f, 0)),
        ],
        out_specs=[pl.BlockSpec((tm, D_MODEL), lambda i, f: (i, 0)),
                   pl.BlockSpec((ms, D_MODEL), lambda i, f: (0, 0))],
        out_shape=[jax.ShapeDtypeStruct((m, D_MODEL), F32), jax.ShapeDtypeStruct((ms, D_MODEL), F32)],
        scratch_shapes=[pltpu.VMEM((tm, D_MODEL), BF16), pltpu.VMEM((ms, D_MODEL), BF16)],
        compiler_params=_cparams("arbitrary", "arbitrary"),
        name="mlp",
    )(h, hs, g, w_up, w_down)


def _ple_kernel(h_ref, pe_ref, hs_ref, pes_ref, g_ref, wg_ref, wp_ref, gf_ref, y_ref, ys_ref):
    def gated_embedding(src, pe, dst):
        h = src[...]
        gate = jax.nn.sigmoid(_dot(_rms(h, g_ref[...]).astype(BF16), wg_ref[...]))
        proj = _dot(pe[...].astype(BF16), wp_ref[...])
        dst[...] = _rms(h + gate * proj, gf_ref[...])

    gated_embedding(h_ref, pe_ref, y_ref)

    @pl.when(pl.program_id(0) == 0)
    def _():
        gated_embedding(hs_ref, pes_ref, ys_ref)


def _ple_final(h, pe, hs, pes, g_ple, w_gate, w_proj, g_final):
    m, ms = h.shape[0], hs.shape[0]
    tm = min(512, m)
    vec = pl.BlockSpec((1, D_MODEL), lambda i: (0, 0))
    return pl.pallas_call(
        _ple_kernel,
        grid=(m // tm,),
        in_specs=[
            pl.BlockSpec((tm, D_MODEL), lambda i: (i, 0)),
            pl.BlockSpec((tm, PLE_DIM), lambda i: (i, 0)),
            pl.BlockSpec((ms, D_MODEL), lambda i: (0, 0)),
            pl.BlockSpec((ms, PLE_DIM), lambda i: (0, 0)),
            vec,
            pl.BlockSpec((D_MODEL, D_MODEL), lambda i: (0, 0)),
            pl.BlockSpec((PLE_DIM, D_MODEL), lambda i: (0, 0)),
            vec,
        ],
        out_specs=[pl.BlockSpec((tm, D_MODEL), lambda i: (i, 0)),
                   pl.BlockSpec((ms, D_MODEL), lambda i: (0, 0))],
        out_shape=[jax.ShapeDtypeStruct((m, D_MODEL), F32), jax.ShapeDtypeStruct((ms, D_MODEL), F32)],
        compiler_params=_cparams("arbitrary"),
        name="ple_final_norm",
    )(h, pe, hs, pes, g_ple, w_gate, w_proj, g_final)


def _layer_weights(w_in, b_igate, b_fgate, g_mix, g_mhead, w_out, g_ffn, w_up, w_down,
                   g_ple, w_ple_gate, w_ple_proj):
    gate_bias = jnp.pad(jnp.concatenate([b_igate, b_fgate]), (0, GATE_PAD - 2 * M_HEADS)).reshape(1, GATE_PAD)
    return dict(
        w_in=w_in.T, gate_bias=gate_bias.astype(F32),
        g_mix=g_mix.reshape(1, D_MODEL), g_mhead=g_mhead, w_out=w_out,
        g_ffn=g_ffn.reshape(1, D_MODEL), w_up=w_up, w_down=w_down,
        g_ple=g_ple.reshape(1, D_MODEL), w_ple_gate=w_ple_gate.astype(BF16),
        w_ple_proj=w_ple_proj.astype(BF16))


def _tail(h, out_m, out_a, pe, hs, out_ms, out_as, pes, w, g_final):
    h, hs = _out_proj(h, out_m, out_a, hs, out_ms, out_as, w["w_out"])
    h, hs = _mlp(h, hs, w["g_ffn"], w["w_up_bf16"], w["w_down_bf16"])
    return _ple_final(h, pe, hs, pes, w["g_ple"], w["w_ple_gate"], w["w_ple_proj"], g_final)


def _split_state(c_aug, m_fin):
    return c_aug[..., :M_V_DIM], c_aug[..., M_V_DIM], m_fin[..., 0, 0]


def kernel(x_prompt, x_sample, cache_k, cache_v, state_C, state_n, state_m, page_table, p_prompt, p_sample, rel_bias_table, w_in, b_igate, b_fgate, g_mix, g_mhead, w_out, g_ffn, w_up, w_down, g_ple, w_ple_gate, w_ple_proj, g_final):
    depth = w_in.shape[0]
    assert depth == 1, "one decoder layer per call"
    layer = 0
    bp, tp, _ = x_prompt.shape
    bs, ts, _ = x_sample.shape
    past_len = page_table.shape[1] * PAGE_SIZE
    assert past_len % MOBA_BLOCK == 0 and ts <= MOBA_BLOCK
    kv_shape = (A_HEADS, A_HEAD_DIM)
    g_fin = g_final.reshape(1, D_MODEL)

    w = _layer_weights(w_in[layer], b_igate[layer], b_fgate[layer], g_mix[layer], g_mhead[layer], w_out[layer],
                       g_ffn[layer], w_up[layer], w_down[layer], g_ple[layer], w_ple_gate[layer],
                       w_ple_proj[layer])
    bias_tiles = _bias_tiles(rel_bias_table)

    hp = x_prompt.reshape(bp * tp, D_MODEL)
    hs = x_sample.reshape(bs * ts, D_MODEL)
    xn_p, gp, xn_s, gs = _norm_gate(hp, hs, w["g_mix"], w["w_in"])
    zp, aq_p, ak_p, av_p, zs = _in_proj(xn_p, xn_s, w["w_in"])
    aq_s, ak_s, av_s = (zs[:, _GATES + t * _IN_TN:_GATES + (t + 1) * _IN_TN] for t in range(3))
    c0 = jnp.zeros((bp, M_HEADS, M_QK_DIM, 2 * M_V_DIM), F32)
    m0 = jnp.zeros((bp, M_HEADS, 1, GATE_PAD), F32)
    om_p, c_p, m_p = _mlstm(zp, gp, w["gate_bias"], w["g_mhead"], c0, m0, bp, tp)
    oa_p, w["w_up_bf16"], w["w_down_bf16"] = _moba_prompt(aq_p, ak_p, av_p, bias_tiles, w["w_up"], w["w_down"],
                                                          bp, tp)
    k_p = ak_p.reshape((1, bp, tp) + kv_shape)
    v_p = av_p.reshape((1, bp, tp) + kv_shape)
    cp, np_, mp = _split_state(c_p, m_p)

    c0s = jnp.concatenate([state_C[layer], state_n[layer][..., None],
                           jnp.zeros((bs, M_HEADS, M_QK_DIM, M_V_DIM - 1), F32)], axis=-1)
    m0s = jnp.broadcast_to(state_m[layer][..., None, None], (bs, M_HEADS, 1, GATE_PAD))
    om_s, c_s, m_s = _mlstm(zs, gs, w["gate_bias"], w["g_mhead"], c0s, m0s, bs, ts)

    hq = A_HEADS * ts
    q_s = aq_s.reshape(bs, ts, A_HEADS, A_HEAD_DIM)
    q_hq = q_s.transpose(0, 2, 1, 3).reshape(bs, hq, A_HEAD_DIM)
    k_new = ak_s.reshape(bs, hq, A_HEAD_DIM)
    v_new = av_s.reshape(bs, hq, A_HEAD_DIM)
    rows = bias_tiles[:, :, :, :ts].transpose(0, 1, 3, 2)
    expand = lambda a: jnp.repeat(a, A_HEADS, axis=-1).reshape(hq, -1)
    past_bias = jnp.stack([rows[:, 2], rows[:, 1]])
    own_bias = expand(rows[:, 0, :, :ts])
    o_part, m_part, l_part, kmean = _moba_past(page_table, q_hq, cache_k, cache_v, past_bias, layer)
    oa_hq = _moba_combine(q_hq, kmean.transpose(0, 2, 1, 3), k_new, v_new, own_bias, o_part, m_part, l_part)
    oa_s = oa_hq.reshape(bs, A_HEADS, ts, A_HEAD_DIM).transpose(0, 2, 1, 3).reshape(bs * ts, A_HEADS * A_HEAD_DIM)
    y_p, y_s = _tail(hp, om_p, oa_p, p_prompt[layer].reshape(bp * tp, PLE_DIM),
                     hs, om_s, oa_s, p_sample[layer].reshape(bs * ts, PLE_DIM), w, g_fin)
    k_s = k_new.reshape((1, bs, ts) + kv_shape)
    v_s = v_new.reshape((1, bs, ts) + kv_shape)
    cs, ns, ms = _split_state(c_s, m_s)

    return (y_p.reshape(bp, tp, D_MODEL), y_s.reshape(bs, ts, D_MODEL),
            k_p, v_p, cp[None], np_[None], mp[None],
            k_s, v_s, cs[None], ns[None], ms[None])
```

```python
import functools
import math

import jax
import jax.numpy as jnp
from jax import lax
from jax.experimental import pallas as pl
from jax.experimental.pallas import tpu as pltpu

F32 = jnp.float32
BF16 = jnp.bfloat16
HIGHEST = lax.Precision.HIGHEST

D_MODEL = 2048
M_HEADS = 8
M_V_DIM = 128
M_QK_DIM = 64
M_CHUNK = 64
A_HEADS = 8
A_HEAD_DIM = 128
MOBA_BLOCK = 256
MOBA_TOP_K = 3
NUM_BUCKETS = 32
MAX_EXACT = NUM_BUCKETS // 2
MAX_DISTANCE = 128
D_FF = 4 * D_MODEL
PLE_DIM = 256
PAGE_SIZE = 128
EPS = 1e-6

_MQ, _MK, _MV, _MO = 0, 512, 1024, 2048
_GATES = 3072
_AQ = 3088
D_MAIN = 6144
GATE_PAD = 128

_MLP_TF = 1024
VMEM_LIMIT = 56 * 1024 * 1024
NEG_INF = float("-inf")


def _cparams(*sem):
    return pltpu.CompilerParams(dimension_semantics=sem, vmem_limit_bytes=VMEM_LIMIT)


def _rms(x, g):
    return x * lax.rsqrt(jnp.mean(x * x, axis=-1, keepdims=True) + EPS) * g


def _dot_nt(a, b, **kw):
    return lax.dot_general(a, b, (((1,), (1,)), ((), ())), preferred_element_type=F32, **kw)


def _dot_tn(a, b, **kw):
    return lax.dot_general(a, b, (((0,), (0,)), ((), ())), preferred_element_type=F32, **kw)


def _dot(a, b, **kw):
    return jnp.dot(a, b, preferred_element_type=F32, **kw)


_IN_TN = 1024
_N_IN_TILES = D_MAIN // _IN_TN
_N_MLSTM_TILES = _GATES // _IN_TN
_GATE_COLS = _AQ - _GATES
_CAST_ROWS = 256


def _norm_gate_kernel(x_ref, xs_ref, g_ref, wg_ref, xn_ref, gate_ref, xns_ref, gates_ref):
    wg = wg_ref[...]
    hi = wg.astype(BF16).astype(F32)
    row = lax.broadcasted_iota(jnp.int32, wg.shape, 0)
    w2 = jnp.where(row < _GATE_COLS, hi,
                   jnp.where(row < 2 * _GATE_COLS, pltpu.roll(wg - hi, _GATE_COLS, axis=0), 0.0))
    w2 = w2.astype(BF16)

    def norm_and_gate(src_ref, dst_ref, gate_dst_ref):
        xn = _rms(src_ref[...], g_ref[...]).astype(BF16)
        dst_ref[...] = xn
        r = _dot_nt(xn, w2)
        gate_dst_ref[...] = r + pltpu.roll(r, GATE_PAD - _GATE_COLS, axis=1)

    norm_and_gate(x_ref, xn_ref, gate_ref)

    @pl.when(pl.program_id(0) == 0)
    def _():
        norm_and_gate(xs_ref, xns_ref, gates_ref)


def _norm_gate(x, xs, g, w_in_t):
    m, ms = x.shape[0], xs.shape[0]
    tm = min(512, m)
    return pl.pallas_call(
        _norm_gate_kernel,
        grid=(m // tm,),
        in_specs=[
            pl.BlockSpec((tm, D_MODEL), lambda i: (i, 0)),
            pl.BlockSpec((ms, D_MODEL), lambda i: (0, 0)),
            pl.BlockSpec((1, D_MODEL), lambda i: (0, 0)),
            pl.BlockSpec((GATE_PAD, D_MODEL), lambda i: (_GATES // GATE_PAD, 0)),
        ],
        out_specs=[pl.BlockSpec((tm, D_MODEL), lambda i: (i, 0)),
                   pl.BlockSpec((tm, GATE_PAD), lambda i: (i, 0)),
                   pl.BlockSpec((ms, D_MODEL), lambda i: (0, 0)),
                   pl.BlockSpec((ms, GATE_PAD), lambda i: (0, 0))],
        out_shape=[jax.ShapeDtypeStruct((m, D_MODEL), BF16), jax.ShapeDtypeStruct((m, GATE_PAD), F32),
                   jax.ShapeDtypeStruct((ms, D_MODEL), BF16), jax.ShapeDtypeStruct((ms, GATE_PAD), F32)],
        compiler_params=_cparams("arbitrary"),
        name="norm_gate",
    )(x, xs, g, w_in_t)


def _in_proj_kernel(xn_ref, xs_ref, wa_ref, wb_ref, zm_ref, aq_ref, ak_ref, av_ref, zs_ref, w_scr):
    j = pl.program_id(0)
    i = pl.program_id(1)

    @pl.when((i == 0) & (j < _N_MLSTM_TILES))
    def _():
        for r in range(0, _IN_TN, _CAST_ROWS):
            w_scr[r:r + _CAST_ROWS, :] = wa_ref[r:r + _CAST_ROWS, :].astype(BF16)

    @pl.when((i == 0) & (j >= _N_MLSTM_TILES))
    def _():
        body = _IN_TN - _GATE_COLS
        for r in range(0, body, _CAST_ROWS):
            n = min(_CAST_ROWS, body - r)
            w_scr[r:r + n, :] = wa_ref[_GATE_COLS + r:_GATE_COLS + r + n, :].astype(BF16)
        w_scr[body:, :] = wb_ref[:_GATE_COLS, :].astype(BF16)

    @pl.when(i == 0)
    def _():
        zs_ref[...] = _dot_nt(xs_ref[...], w_scr[...])

    @pl.when(j < _N_MLSTM_TILES)
    def _():
        zm_ref[...] = _dot_nt(xn_ref[...], w_scr[...])

    for t, ref in enumerate((aq_ref, ak_ref, av_ref)):
        @pl.when(j == _N_MLSTM_TILES + t)
        def _(ref=ref):
            ref[...] = _dot_nt(xn_ref[...], w_scr[...])


def _in_proj(xn, xs, w_in_t):
    m, ms = xn.shape[0], xs.shape[0]
    tm = min(512, m)
    tn = _IN_TN
    last = m // tm - 1

    def held(j, i, first_tile, last_tile):
        return jnp.where(j < first_tile, 0, jnp.where(j <= last_tile, i, last))

    def section(t):
        return pl.BlockSpec((tm, tn), lambda j, i: (held(j, i, t, t), 0))

    sec_shape = jax.ShapeDtypeStruct((m, tn), F32)
    nm = _N_MLSTM_TILES
    return pl.pallas_call(
        _in_proj_kernel,
        grid=(_N_IN_TILES, m // tm),
        in_specs=[
            pl.BlockSpec((tm, D_MODEL), lambda j, i: (i, 0)),
            pl.BlockSpec((ms, D_MODEL), lambda j, i: (0, 0)),
            pl.BlockSpec((tn, D_MODEL), lambda j, i: (j, 0)),
            pl.BlockSpec((GATE_PAD, D_MODEL), lambda j, i: ((j + 1) * (tn // GATE_PAD), 0)),
        ],
        out_specs=[
            pl.BlockSpec((tm, tn), lambda j, i: (held(j, i, 0, nm - 1), jnp.minimum(j, nm - 1))),
            section(nm), section(nm + 1), section(nm + 2),
            pl.BlockSpec((ms, tn), lambda j, i: (0, j)),
        ],
        out_shape=[jax.ShapeDtypeStruct((m, _GATES), F32), sec_shape, sec_shape, sec_shape,
                   jax.ShapeDtypeStruct((ms, D_MAIN), F32)],
        scratch_shapes=[pltpu.VMEM((tn, D_MODEL), BF16)],
        compiler_params=_cparams("arbitrary", "arbitrary"),
        name="in_proj",
    )(xn, xs, w_in_t, w_in_t)


def _log_sigmoid(x):
    return -(jnp.maximum(-x, 0.0) + jnp.log1p(jnp.exp(-jnp.abs(x))))


def _bdot(a, b):
    return lax.dot_general(a, b, (((2,), (1,)), ((0,), (0,))), preferred_element_type=F32)


def _bdot_nt(a, b):
    return lax.dot_general(a, b, (((2,), (2,)), ((0,), (0,))), preferred_element_type=F32)


def _bdot_tn(a, b):
    return lax.dot_general(a, b, (((1,), (1,)), ((0,), (0,))), preferred_element_type=F32)


def _mlstm_kernel(q_ref, k_ref, v_ref, o_ref, g_ref, gb_ref, gh_ref, c0_ref, m0_ref,
                  out_ref, cfin_ref, mfin_ref, c_scr, m_scr, *, chunk, n_chunks):
    L = chunk
    H = M_HEADS
    tb = L * n_chunks
    t = pl.program_id(1)

    @pl.when(t == 0)
    def _():
        c_scr[...] = c0_ref[0]
        m_scr[...] = m0_ref[0]

    g = g_ref[...] + gb_ref[...]
    lane = lax.broadcasted_iota(jnp.int32, (tb, GATE_PAD), 1)
    gl = jnp.where((lane >= H) & (lane < 2 * H), _log_sigmoid(g), g)
    r = lax.broadcasted_iota(jnp.int32, (tb, tb), 0)
    c = lax.broadcasted_iota(jnp.int32, (tb, tb), 1)
    tri = ((r // L == c // L) & (c <= r)).astype(F32)
    cum = _dot(tri, gl, precision=HIGHEST)

    def rows(ci):
        return slice(ci * L, (ci + 1) * L)

    def stack(f):
        return jnp.stack([f(ci, h) for ci in range(n_chunks) for h in range(H)])

    ones_col = (lax.broadcasted_iota(jnp.int32, (L, M_V_DIM), 1) == 0).astype(BF16)
    li = stack(lambda ci, h: gl[rows(ci), h:h + 1])
    b = stack(lambda ci, h: cum[rows(ci), H + h:H + h + 1])
    qb = stack(lambda ci, h: q_ref[rows(ci), h * M_QK_DIM:(h + 1) * M_QK_DIM]).astype(BF16)
    kc = stack(lambda ci, h: k_ref[rows(ci), h * M_QK_DIM:(h + 1) * M_QK_DIM]) * (M_QK_DIM ** -0.5)
    vaug = stack(lambda ci, h: jnp.concatenate(
        [v_ref[rows(ci), h * M_V_DIM:(h + 1) * M_V_DIM].astype(BF16), ones_col], axis=1))

    rl = lax.broadcasted_iota(jnp.int32, (L, L), 0)
    cl = lax.broadcasted_iota(jnp.int32, (L, L), 1)
    w_row = jnp.sum(jnp.where(rl == cl, li - b, 0.0), axis=1, keepdims=True)
    d = jnp.where(cl <= rl, b + w_row, NEG_INF)
    m_loc = jnp.max(d, axis=2, keepdims=True)
    s = _bdot_nt(qb, kc.astype(BF16)) * jnp.exp(d - m_loc)
    nd_loc = _bdot(s.astype(BF16), vaug)
    g_last = m_loc[:, L - 1:L, :]
    b_last = b[:, L - 1:L, :]
    ws = jnp.exp(b_last - b + li - g_last)
    dc_loc = _bdot_tn((kc * ws).astype(BF16), vaug)

    m_prev = m_scr[:, :, 0:1]
    c_aug = c_scr[...]
    for ci in range(n_chunks):
        grp = slice(ci * H, (ci + 1) * H)
        a = b[grp] + m_prev
        m_t = jnp.maximum(a, m_loc[grp])
        nd = (jnp.exp(a - m_t) * _bdot(qb[grp], c_aug.astype(BF16))
              + jnp.exp(m_loc[grp] - m_t) * nd_loc[grp])
        hh = nd[:, :, :M_V_DIM] / jnp.maximum(jnp.abs(nd[:, :, M_V_DIM:M_V_DIM + 1]), jnp.exp(-m_t))
        hn = _rms(hh, gh_ref[...])
        for h in range(H):
            og = jax.nn.sigmoid(o_ref[rows(ci), h * M_V_DIM:(h + 1) * M_V_DIM])
            out_ref[rows(ci), h * M_V_DIM:(h + 1) * M_V_DIM] = (og * hn[h]).astype(out_ref.dtype)
        m_new = jnp.maximum(b_last[grp] + m_prev, g_last[grp])
        c_aug = (jnp.exp(b_last[grp] + m_prev - m_new) * c_aug
                 + jnp.exp(g_last[grp] - m_new) * dc_loc[grp])
        m_prev = m_new
    c_scr[...] = c_aug
    m_scr[...] = jnp.broadcast_to(m_prev, m_scr.shape)

    @pl.when(t == pl.num_programs(1) - 1)
    def _():
        cfin_ref[0] = c_scr[...]
        mfin_ref[0] = m_scr[...]


def _mlstm(z, gates, gate_bias, g_mhead, c0_aug, m0, batch, seq):
    L = math.gcd(seq, M_CHUNK)
    n_chunks = max(1, min(seq // L, 256 // L))
    tb = L * n_chunks
    nt = seq // tb
    row = lambda b, t: b * nt + t
    kern = functools.partial(_mlstm_kernel, chunk=L, n_chunks=n_chunks)
    return pl.pallas_call(
        kern,
        grid=(batch, nt),
        in_specs=[
            pl.BlockSpec((tb, 512), lambda b, t: (row(b, t), _MQ // 512)),
            pl.BlockSpec((tb, 512), lambda b, t: (row(b, t), _MK // 512)),
            pl.BlockSpec((tb, 1024), lambda b, t: (row(b, t), _MV // 1024)),
            pl.BlockSpec((tb, 1024), lambda b, t: (row(b, t), _MO // 1024)),
            pl.BlockSpec((tb, GATE_PAD), lambda b, t: (row(b, t), 0)),
            pl.BlockSpec((1, GATE_PAD), lambda b, t: (0, 0)),
            pl.BlockSpec((M_HEADS, 1, M_V_DIM), lambda b, t: (0, 0, 0)),
            pl.BlockSpec((1, M_HEADS, M_QK_DIM, 2 * M_V_DIM), lambda b, t: (b, 0, 0, 0)),
            pl.BlockSpec((1, M_HEADS, 1, GATE_PAD), lambda b, t: (b, 0, 0, 0)),
        ],
        out_specs=[
            pl.BlockSpec((tb, M_HEADS * M_V_DIM), lambda b, t: (row(b, t), 0)),
            pl.BlockSpec((1, M_HEADS, M_QK_DIM, 2 * M_V_DIM), lambda b, t: (b, 0, 0, 0)),
            pl.BlockSpec((1, M_HEADS, 1, GATE_PAD), lambda b, t: (b, 0, 0, 0)),
        ],
        out_shape=[
            jax.ShapeDtypeStruct((batch * seq, M_HEADS * M_V_DIM), BF16 if tb % 16 == 0 else F32),
            jax.ShapeDtypeStruct((batch, M_HEADS, M_QK_DIM, 2 * M_V_DIM), F32),
            jax.ShapeDtypeStruct((batch, M_HEADS, 1, GATE_PAD), F32),
        ],
        scratch_shapes=[pltpu.VMEM((M_HEADS, M_QK_DIM, 2 * M_V_DIM), F32),
                        pltpu.VMEM((M_HEADS, 1, GATE_PAD), F32)],
        compiler_params=_cparams("parallel", "arbitrary"),
        name="mlstm",
    )(z, z, z, z, gates, gate_bias, g_mhead.reshape(M_HEADS, 1, M_V_DIM), c0_aug, m0)


def _t5_bucket(rel):
    n = jnp.maximum(rel, 0)
    nf = jnp.maximum(n, 1).astype(F32)
    large = MAX_EXACT + (jnp.log(nf / MAX_EXACT) / math.log(MAX_DISTANCE / MAX_EXACT)
                         * (NUM_BUCKETS - MAX_EXACT)).astype(jnp.int32)
    large = jnp.minimum(large, NUM_BUCKETS - 1)
    return jnp.where(n < MAX_EXACT, n, large)


def _bias_kernel(tab_ref, out_ref):
    h = pl.program_id(0)
    i = lax.broadcasted_iota(jnp.int32, (MOBA_BLOCK, MOBA_BLOCK), 1)
    j = lax.broadcasted_iota(jnp.int32, (MOBA_BLOCK, MOBA_BLOCK), 0)
    for kind, rel in ((0, i - j), (1, MOBA_BLOCK + i - j)):
        bucket = _t5_bucket(rel)
        vals = [tab_ref[h, b] for b in range(NUM_BUCKETS)]
        bit = 1
        while len(vals) > 1:
            odd = (bucket & bit) != 0
            vals = [jnp.where(odd, vals[2 * m + 1], vals[2 * m]) for m in range(len(vals) // 2)]
            bit *= 2
        out_ref[0, kind] = vals[0]
    out_ref[0, 2] = jnp.full((MOBA_BLOCK, MOBA_BLOCK), tab_ref[h, NUM_BUCKETS - 1], F32)


def _bias_tiles(rel_table):
    assert MOBA_BLOCK + 1 >= MAX_DISTANCE and NUM_BUCKETS & (NUM_BUCKETS - 1) == 0
    tab = rel_table.T.astype(F32)
    return pl.pallas_call(
        _bias_kernel,
        grid=(A_HEADS,),
        in_specs=[pl.BlockSpec(memory_space=pltpu.SMEM)],
        out_specs=pl.BlockSpec((1, 3, MOBA_BLOCK, MOBA_BLOCK), lambda h: (h, 0, 0, 0)),
        out_shape=jax.ShapeDtypeStruct((A_HEADS, 3, MOBA_BLOCK, MOBA_BLOCK), F32),
        compiler_params=_cparams("parallel"),
        name="t5_bias_tiles",
    )(tab)


_VT_ROWS = A_HEAD_DIM + 16


def _moba_prompt_kernel(q_ref, k_ref, v_ref, bias_ref, wu_ref, wd_ref, o_ref, wub_ref, wdb_ref,
                        kb_scr, qt_scr, vt_scr, gt_scr, s_scr, p_scr, *, n_blocks):
    nb = n_blocks
    blk = MOBA_BLOCK
    nbp = gt_scr.shape[0]
    wub_ref[0] = wu_ref[...].astype(BF16)
    wdb_ref[...] = wd_ref[...].astype(BF16)

    kb_scr[...] = k_ref[...].astype(BF16)
    for c in range(nb):
        cols = slice(c * blk, (c + 1) * blk)
        qt_scr[:, cols] = (q_ref[cols, :] * (A_HEAD_DIM ** -0.5)).T.astype(BF16)
        vt_scr[0:A_HEAD_DIM, cols] = v_ref[cols, :].T.astype(BF16)
    ones_row = (lax.broadcasted_iota(jnp.int32, (_VT_ROWS - A_HEAD_DIM, nb * blk), 0) == 0).astype(BF16)
    vt_scr[A_HEAD_DIM:, :] = ones_row
    need_gate = nb - 1 > MOBA_TOP_K
    if need_gate:
        means = [jnp.mean(k_ref[n * blk:(n + 1) * blk, :], axis=0, keepdims=True) for n in range(nb)]
        kmean = jnp.concatenate(means + [jnp.zeros((128 - nb, A_HEAD_DIM), F32)], axis=0)
        for c in range(MOBA_TOP_K + 1, nb):
            cols = slice(c * blk, (c + 1) * blk)
            gate = _dot_nt(q_ref[cols, :] * (A_HEAD_DIM ** -0.5), kmean, precision=HIGHEST)
            gt_scr[:, cols] = gate.T[:nbp, :]
    key = lax.broadcasted_iota(jnp.int32, (blk, blk), 0)
    qry = lax.broadcasted_iota(jnp.int32, (blk, blk), 1)
    causal = key <= qry
    blk_id = lax.broadcasted_iota(jnp.int32, (nbp, blk), 0)

    def scores(i):
        cols = slice(i * blk, (i + 1) * blk)
        s_scr[i % 2, 0:(i + 1) * blk, :] = _dot(kb_scr[0:(i + 1) * blk, :], qt_scr[:, cols])

    def softmax(i):
        cols = slice(i * blk, (i + 1) * blk)
        s_buf, p_buf = s_scr.at[i % 2], p_scr.at[i % 2]
        selneg = None
        if i > MOBA_TOP_K:
            g = gt_scr[:, cols]
            rank = jnp.zeros((nbp, blk), F32)
            for j in range(i):
                gj = g[j:j + 1, :]
                rank = rank + ((gj > g) | ((gj == g) & (j < blk_id))).astype(F32)
            selneg = jnp.where(rank < MOBA_TOP_K, 0.0, NEG_INF)
        m = jnp.full((1, blk), NEG_INF, F32)
        for n in range(i + 1):
            rows = slice(n * blk, (n + 1) * blk)
            kind = 0 if n == i else (1 if n == i - 1 else 2)
            st = s_buf[rows, :] + bias_ref[0, kind]
            if n == i:
                st = jnp.where(causal, st, NEG_INF)
            elif selneg is not None:
                st = st + selneg[n:n + 1, :]
            s_buf[rows, :] = st
            m = jnp.maximum(m, jnp.max(st, axis=0, keepdims=True))
        for n in range(i + 1):
            rows = slice(n * blk, (n + 1) * blk)
            p_buf[rows, :] = jnp.exp(s_buf[rows, :] - m).astype(BF16)

    def values(i):
        cols = slice(i * blk, (i + 1) * blk)
        nd = _dot(vt_scr[:, 0:(i + 1) * blk], p_scr[i % 2, 0:(i + 1) * blk, :])
        out = nd[0:A_HEAD_DIM, :] / nd[A_HEAD_DIM:A_HEAD_DIM + 1, :]
        o_ref[cols, :] = out.T.astype(o_ref.dtype)

    scores(0)
    for i in range(nb):
        if i + 1 < nb:
            scores(i + 1)
        softmax(i)
        values(i)


def _moba_prompt(aq, ak, av, bias_tiles, w_up, w_down, batch, seq):
    assert seq % MOBA_BLOCK == 0
    nb = seq // MOBA_BLOCK
    assert nb <= 128
    nbp = -(-nb // 8) * 8
    steps = batch * A_HEADS
    slab = D_FF // steps
    assert D_FF % steps == 0 and slab % 128 == 0 and _MLP_TF % slab == 0
    slabs_per_tile = _MLP_TF // slab
    head = pl.BlockSpec((seq, A_HEAD_DIM), lambda b, h: (b, h))
    up_slab = pl.BlockSpec((D_MODEL, slab), lambda b, h: (0, b * A_HEADS + h))
    up_tile_slab = pl.BlockSpec(
        (1, D_MODEL, slab),
        lambda b, h: ((b * A_HEADS + h) // slabs_per_tile, 0, (b * A_HEADS + h) % slabs_per_tile))
    down_slab = pl.BlockSpec((slab, D_MODEL), lambda b, h: (b * A_HEADS + h, 0))
    kern = functools.partial(_moba_prompt_kernel, n_blocks=nb)
    return pl.pallas_call(
        kern,
        grid=(batch, A_HEADS),
        in_specs=[head, head, head,
                  pl.BlockSpec((1, 3, MOBA_BLOCK, MOBA_BLOCK), lambda b, h: (h, 0, 0, 0)),
                  up_slab, down_slab],
        out_specs=[head, up_tile_slab, down_slab],
        out_shape=[jax.ShapeDtypeStruct((batch * seq, A_HEADS * A_HEAD_DIM), BF16),
                   jax.ShapeDtypeStruct((D_FF // _MLP_TF, D_MODEL, _MLP_TF), BF16),
                   jax.ShapeDtypeStruct((D_FF, D_MODEL), BF16)],
        scratch_shapes=[pltpu.VMEM((seq, A_HEAD_DIM), BF16),
                        pltpu.VMEM((A_HEAD_DIM, seq), BF16),
                        pltpu.VMEM((_VT_ROWS, seq), BF16),
                        pltpu.VMEM((nbp, seq), F32),
                        pltpu.VMEM((2, seq, MOBA_BLOCK), F32),
                        pltpu.VMEM((2, seq, MOBA_BLOCK), BF16)],
        compiler_params=_cparams("parallel", "parallel"),
        name="moba_prompt",
    )(aq, ak, av, bias_tiles, w_up, w_down)


PAGES_PER_BLOCK = MOBA_BLOCK // PAGE_SIZE
PAST_BLOCKS_PER_STEP = 8


def _moba_past_kernel(pt_ref, q_ref, *refs, n_pages_step):
    del pt_ref
    k_refs = refs[:n_pages_step]
    v_refs = refs[n_pages_step:2 * n_pages_step]
    bias_ref, o_ref, m_ref, l_ref, km_ref = refs[2 * n_pages_step:]
    hq = q_ref.shape[1]
    nq = hq // A_HEADS
    qb = (q_ref[0] * (A_HEAD_DIM ** -0.5)).reshape(A_HEADS, nq, A_HEAD_DIM).astype(BF16)
    n_blocks = n_pages_step // PAGES_PER_BLOCK
    is_last_step = pl.program_id(1) == pl.num_programs(1) - 1

    def head_rows(ref, h):
        return ref[0, 0, pl.ds(h, PAGE_SIZE, stride=A_HEADS), :]

    for g in range(n_blocks):
        pages = range(g * PAGES_PER_BLOCK, (g + 1) * PAGES_PER_BLOCK)
        ksum = sum(jnp.sum(k_refs[p][0, 0].reshape(PAGE_SIZE, A_HEADS, A_HEAD_DIM), axis=0) for p in pages)
        km_ref[0, g] = ksum / MOBA_BLOCK
        kind = jnp.where(is_last_step, 1, 0) if g == n_blocks - 1 else 0
        kh = jnp.stack([jnp.concatenate([head_rows(k_refs[p], h) for p in pages], axis=0)
                        for h in range(A_HEADS)]).astype(BF16)
        vh = jnp.stack([jnp.concatenate([head_rows(v_refs[p], h) for p in pages], axis=0)
                        for h in range(A_HEADS)]).astype(BF16)
        lt = _bdot_nt(qb, kh) + bias_ref[kind]
        m = jnp.max(lt, axis=2, keepdims=True)
        p_ = jnp.exp(lt - m)
        l = jnp.sum(p_, axis=2, keepdims=True)
        o_ref[0, g] = _bdot(p_.astype(BF16), vh).reshape(hq, A_HEAD_DIM)
        m_ref[0, g] = jnp.broadcast_to(m, (A_HEADS, nq, A_HEAD_DIM)).reshape(hq, A_HEAD_DIM)
        l_ref[0, g] = jnp.broadcast_to(l, (A_HEADS, nq, A_HEAD_DIM)).reshape(hq, A_HEAD_DIM)


def _moba_past(page_table, q_hq, cache_k, cache_v, past_bias, layer):
    bs, n_pages = page_table.shape
    assert n_pages % PAGES_PER_BLOCK == 0
    nbp = n_pages // PAGES_PER_BLOCK
    gb = math.gcd(nbp, PAST_BLOCKS_PER_STEP)
    pps = gb * PAGES_PER_BLOCK
    hq = q_hq.shape[1]
    page = (1, 1, PAGE_SIZE * A_HEADS, A_HEAD_DIM)
    cache_k, cache_v = (c.reshape(c.shape[:2] + page[2:]) for c in (cache_k, cache_v))
    part = pl.BlockSpec((1, gb, hq, A_HEAD_DIM), lambda b, n, pt: (b, n, 0, 0))

    def page_spec(p):
        return pl.BlockSpec(page, lambda b, n, pt: (layer, pt[b, pps * n + p], 0, 0))

    grid_spec = pltpu.PrefetchScalarGridSpec(
        num_scalar_prefetch=1,
        grid=(bs, nbp // gb),
        in_specs=[pl.BlockSpec((1, hq, A_HEAD_DIM), lambda b, n, pt: (b, 0, 0))]
        + [page_spec(p) for p in range(pps)] * 2
        + [pl.BlockSpec((2, A_HEADS, hq // A_HEADS, MOBA_BLOCK), lambda b, n, pt: (0, 0, 0, 0))],
        out_specs=[part, part, part,
                   pl.BlockSpec((1, gb, A_HEADS, A_HEAD_DIM), lambda b, n, pt: (b, n, 0, 0))],
    )
    pshape = jax.ShapeDtypeStruct((bs, nbp, hq, A_HEAD_DIM), F32)
    return pl.pallas_call(
        functools.partial(_moba_past_kernel, n_pages_step=pps),
        grid_spec=grid_spec,
        out_shape=[pshape, pshape, pshape, jax.ShapeDtypeStruct((bs, nbp, A_HEADS, A_HEAD_DIM), F32)],
        compiler_params=_cparams("parallel", "parallel"),
        name="moba_past_blocks",
    )(page_table, q_hq, *([cache_k] * pps), *([cache_v] * pps), past_bias)


def _moba_combine_kernel(q_ref, km_ref, kn_ref, vn_ref, ob_ref, op_ref, mp_ref, lp_ref, out_ref):
    nbp = km_ref.shape[2]
    q = q_ref[0] * (A_HEAD_DIM ** -0.5)
    hq = q.shape[0]
    nq = hq // A_HEADS
    gate = jnp.concatenate(
        [_dot_nt(q[h * nq:(h + 1) * nq], km_ref[0, h], precision=HIGHEST) for h in range(A_HEADS)], axis=0)
    lane = lax.broadcasted_iota(jnp.int32, (hq, nbp), 1)
    rank = jnp.zeros((hq, nbp), F32)
    for j in range(nbp):
        gj = gate[:, j:j + 1]
        beats = (gj > gate) | ((gj == gate) & (j < lane))
        rank = rank + beats.astype(F32)
    selneg = jnp.where(rank < MOBA_TOP_K, 0.0, NEG_INF)
    qb = q.astype(BF16)
    lt = _dot_nt(qb, kn_ref[0].astype(BF16))
    r = lax.broadcasted_iota(jnp.int32, lt.shape, 0)
    c = lax.broadcasted_iota(jnp.int32, lt.shape, 1)
    ok = ((c % A_HEADS) == (r // nq)) & ((c // A_HEADS) <= (r % nq))
    lo = jnp.where(ok, lt + ob_ref[...], NEG_INF)
    m = jnp.max(lo, axis=1, keepdims=True)
    for n in range(nbp):
        m = jnp.maximum(m, mp_ref[0, n][:, 0:1] + selneg[:, n:n + 1])
    p = jnp.exp(lo - m)
    l = jnp.sum(p, axis=1, keepdims=True)
    acc = _dot(p.astype(BF16), vn_ref[0].astype(BF16))
    for n in range(nbp):
        w = jnp.exp(mp_ref[0, n][:, 0:1] + selneg[:, n:n + 1] - m)
        l = l + w * lp_ref[0, n][:, 0:1]
        acc = acc + w * op_ref[0, n]
    out_ref[0] = acc / l


def _moba_combine(q_hq, kmean_hn, k_new, v_new, own_bias, o_part, m_part, l_part):
    bs, hq, _ = q_hq.shape
    nbp = o_part.shape[1]
    part = pl.BlockSpec((1, nbp, hq, A_HEAD_DIM), lambda b: (b, 0, 0, 0))
    tok = pl.BlockSpec((1, hq, A_HEAD_DIM), lambda b: (b, 0, 0))
    return pl.pallas_call(
        _moba_combine_kernel,
        grid=(bs,),
        in_specs=[tok,
                  pl.BlockSpec((1, A_HEADS, nbp, A_HEAD_DIM), lambda b: (b, 0, 0, 0)),
                  tok, tok,
                  pl.BlockSpec((hq, hq), lambda b: (0, 0)),
                  part, part, part],
        out_specs=tok,
        out_shape=jax.ShapeDtypeStruct((bs, hq, A_HEAD_DIM), F32),
        compiler_params=_cparams("parallel"),
        name="moba_combine",
    )(q_hq, kmean_hn, k_new, v_new, own_bias, o_part, m_part, l_part)


def _out_proj_kernel(h_ref, om_ref, oa_ref, hs_ref, oms_ref, oas_ref, wm_ref, wa_ref, o_ref, os_ref,
                     wm_scr, wa_scr):
    def project(h, om, oa, dst):
        dst[...] = (h[...] + _dot(om[...].astype(BF16), wm_scr[...])
                    + _dot(oa[...].astype(BF16), wa_scr[...]))

    @pl.when(pl.program_id(1) == 0)
    def _():
        for r in range(0, wm_scr.shape[0], _CAST_ROWS):
            wm_scr[r:r + _CAST_ROWS, :] = wm_ref[r:r + _CAST_ROWS, :].astype(BF16)
            wa_scr[r:r + _CAST_ROWS, :] = wa_ref[r:r + _CAST_ROWS, :].astype(BF16)
        project(hs_ref, oms_ref, oas_ref, os_ref)

    project(h_ref, om_ref, oa_ref, o_ref)


def _out_proj(h, out_m, out_a, hs, out_ms, out_as, w_out):
    m, ms = h.shape[0], hs.shape[0]
    tm = min(512, m)
    tn = 1024
    half = M_HEADS * M_V_DIM
    return pl.pallas_call(
        _out_proj_kernel,
        grid=(D_MODEL // tn, m // tm),
        in_specs=[
            pl.BlockSpec((tm, tn), lambda j, i: (i, j)),
            pl.BlockSpec((tm, half), lambda j, i: (i, 0)),
            pl.BlockSpec((tm, half), lambda j, i: (i, 0)),
            pl.BlockSpec((ms, tn), lambda j, i: (0, j)),
            pl.BlockSpec((ms, half), lambda j, i: (0, 0)),
            pl.BlockSpec((ms, half), lambda j, i: (0, 0)),
            pl.BlockSpec((half, tn), lambda j, i: (0, j)),
            pl.BlockSpec((half, tn), lambda j, i: (1, j)),
        ],
        out_specs=[pl.BlockSpec((tm, tn), lambda j, i: (i, j)),
                   pl.BlockSpec((ms, tn), lambda j, i: (0, j))],
        out_shape=[jax.ShapeDtypeStruct((m, D_MODEL), F32), jax.ShapeDtypeStruct((ms, D_MODEL), F32)],
        scratch_shapes=[pltpu.VMEM((half, tn), BF16), pltpu.VMEM((half, tn), BF16)],
        compiler_params=_cparams("arbitrary", "arbitrary"),
        name="out_proj",
    )(h, out_m, out_a, hs, out_ms, out_as, w_out, w_out)


def _mlp_kernel(h_ref, hs_ref, g_ref, wu_ref, wd_ref, o_ref, os_ref, xn_ref, xns_ref):
    f = pl.program_id(1)

    def accumulate(src, dst, xn):
        @pl.when(f == 0)
        def _():
            xn[...] = _rms(src[...], g_ref[...]).astype(BF16)
            dst[...] = src[...]

        u = _dot(xn[...], wu_ref[0])
        a = jnp.square(jnp.maximum(u, 0.0)).astype(BF16)
        dst[...] += _dot(a, wd_ref[...])

    accumulate(h_ref, o_ref, xn_ref)

    @pl.when(pl.program_id(0) == 0)
    def _():
        accumulate(hs_ref, os_ref, xns_ref)


def _mlp(h, hs, g, w_up, w_down):
    m, ms = h.shape[0], hs.shape[0]
    tm = min(512, m)
    tf = _MLP_TF
    return pl.pallas_call(
        _mlp_kernel,
        grid=(m // tm, D_FF // tf),
        in_specs=[
            pl.BlockSpec((tm, D_MODEL), lambda i, f: (i, 0)),
            pl.BlockSpec((ms, D_MODEL), lambda i, f: (0, 0)),
            pl.BlockSpec((1, D_MODEL), lambda i, f: (0, 0)),
            pl.BlockSpec((1, D_MODEL, tf), lambda i, f: (f, 0, 0)),
            pl.BlockSpec((tf, D_MODEL), lambda i, f: (f, 0)),
        ],
        out_specs=[pl.BlockSpec((tm, D_MODEL), lambda i, f: (i, 0)),
                   pl.BlockSpec((ms, D_MODEL), lambda i, f: (0, 0))],
        out_shape=[jax.ShapeDtypeStruct((m, D_MODEL), F32), jax.ShapeDtypeStruct((ms, D_MODEL), F32)],
        scratch_shapes=[pltpu.VMEM((tm, D_MODEL), BF16), pltpu.VMEM((ms, D_MODEL), BF16)],
        compiler_params=_cparams("arbitrary", "arbitrary"),
        name="mlp",
    )(h, hs, g, w_up, w_down)


def _ple_kernel(h_ref, pe_ref, hs_ref, pes_ref, g_ref, wg_ref, wp_ref, gf_ref, y_ref, ys_ref):
    def gated_embedding(src, pe, dst):
        h = src[...]
        gate = jax.nn.sigmoid(_dot(_rms(h, g_ref[...]).astype(BF16), wg_ref[...]))
        proj = _dot(pe[...].astype(BF16), wp_ref[...])
        dst[...] = _rms(h + gate * proj, gf_ref[...])

    gated_embedding(h_ref, pe_ref, y_ref)

    @pl.when(pl.program_id(0) == 0)
    def _():
        gated_embedding(hs_ref, pes_ref, ys_ref)


def _ple_final(h, pe, hs, pes, g_ple, w_gate, w_proj, g_final):
    m, ms = h.shape[0], hs.shape[0]
    tm = min(512, m)
    vec = pl.BlockSpec((1, D_MODEL), lambda i: (0, 0))
    return pl.pallas_call(
        _ple_kernel,
        grid=(m // tm,),
        in_specs=[
            pl.BlockSpec((tm, D_MODEL), lambda i: (i, 0)),
            pl.BlockSpec((tm, PLE_DIM), lambda i: (i, 0)),
            pl.BlockSpec((ms, D_MODEL), lambda i: (0, 0)),
            pl.BlockSpec((ms, PLE_DIM), lambda i: (0, 0)),
            vec,
            pl.BlockSpec((D_MODEL, D_MODEL), lambda i: (0, 0)),
            pl.BlockSpec((PLE_DIM, D_MODEL), lambda i: (0, 0)),
            vec,
        ],
        out_specs=[pl.BlockSpec((tm, D_MODEL), lambda i: (i, 0)),
                   pl.BlockSpec((ms, D_MODEL), lambda i: (0, 0))],
        out_shape=[jax.ShapeDtypeStruct((m, D_MODEL), F32), jax.ShapeDtypeStruct((ms, D_MODEL), F32)],
        compiler_params=_cparams("arbitrary"),
        name="ple_final_norm",
    )(h, pe, hs, pes, g_ple, w_gate, w_proj, g_final)


def _layer_weights(w_in, b_igate, b_fgate, g_mix, g_mhead, w_out, g_ffn, w_up, w_down,
                   g_ple, w_ple_gate, w_ple_proj):
    gate_bias = jnp.pad(jnp.concatenate([b_igate, b_fgate]), (0, GATE_PAD - 2 * M_HEADS)).reshape(1, GATE_PAD)
    return dict(
        w_in=w_in.T, gate_bias=gate_bias.astype(F32),
        g_mix=g_mix.reshape(1, D_MODEL), g_mhead=g_mhead, w_out=w_out,
        g_ffn=g_ffn.reshape(1, D_MODEL), w_up=w_up, w_down=w_down,
        g_ple=g_ple.reshape(1, D_MODEL), w_ple_gate=w_ple_gate.astype(BF16),
        w_ple_proj=w_ple_proj.astype(BF16))


def _tail(h, out_m, out_a, pe, hs, out_ms, out_as, pes, w, g_final):
    h, hs = _out_proj(h, out_m, out_a, hs, out_ms, out_as, w["w_out"])
    h, hs = _mlp(h, hs, w["g_ffn"], w["w_up_bf16"], w["w_down_bf16"])
    return _ple_final(h, pe, hs, pes, w["g_ple"], w["w_ple_gate"], w["w_ple_proj"], g_final)


def _split_state(c_aug, m_fin):
    return c_aug[..., :M_V_DIM], c_aug[..., M_V_DIM], m_fin[..., 0, 0]


def kernel(x_prompt, x_sample, cache_k, cache_v, state_C, state_n, state_m, page_table, p_prompt, p_sample, rel_bias_table, w_in, b_igate, b_fgate, g_mix, g_mhead, w_out, g_ffn, w_up, w_down, g_ple, w_ple_gate, w_ple_proj, g_final):
    depth = w_in.shape[0]
    assert depth == 1, "one decoder layer per call"
    layer = 0
    bp, tp, _ = x_prompt.shape
    bs, ts, _ = x_sample.shape
    past_len = page_table.shape[1] * PAGE_SIZE
    assert past_len % MOBA_BLOCK == 0 and ts <= MOBA_BLOCK
    kv_shape = (A_HEADS, A_HEAD_DIM)
    g_fin = g_final.reshape(1, D_MODEL)

    w = _layer_weights(w_in[layer], b_igate[layer], b_fgate[layer], g_mix[layer], g_mhead[layer], w_out[layer],
                       g_ffn[layer], w_up[layer], w_down[layer], g_ple[layer], w_ple_gate[layer],
                       w_ple_proj[layer])
    bias_tiles = _bias_tiles(rel_bias_table)

    hp = x_prompt.reshape(bp * tp, D_MODEL)
    hs = x_sample.reshape(bs * ts, D_MODEL)
    xn_p, gp, xn_s, gs = _norm_gate(hp, hs, w["g_mix"], w["w_in"])
    zp, aq_p, ak_p, av_p, zs = _in_proj(xn_p, xn_s, w["w_in"])
    aq_s, ak_s, av_s = (zs[:, _GATES + t * _IN_TN:_GATES + (t + 1) * _IN_TN] for t in range(3))
    c0 = jnp.zeros((bp, M_HEADS, M_QK_DIM, 2 * M_V_DIM), F32)
    m0 = jnp.zeros((bp, M_HEADS, 1, GATE_PAD), F32)
    om_p, c_p, m_p = _mlstm(zp, gp, w["gate_bias"], w["g_mhead"], c0, m0, bp, tp)
    oa_p, w["w_up_bf16"], w["w_down_bf16"] = _moba_prompt(aq_p, ak_p, av_p, bias_tiles, w["w_up"], w["w_down"],
                                                          bp, tp)
    k_p = ak_p.reshape((1, bp, tp) + kv_shape)
    v_p = av_p.reshape((1, bp, tp) + kv_shape)
    cp, np_, mp = _split_state(c_p, m_p)

    c0s = jnp.concatenate([state_C[layer], state_n[layer][..., None],
                           jnp.zeros((bs, M_HEADS, M_QK_DIM, M_V_DIM - 1), F32)], axis=-1)
    m0s = jnp.broadcast_to(state_m[layer][..., None, None], (bs, M_HEADS, 1, GATE_PAD))
    om_s, c_s, m_s = _mlstm(zs, gs, w["gate_bias"], w["g_mhead"], c0s, m0s, bs, ts)

    hq = A_HEADS * ts
    q_s = aq_s.reshape(bs, ts, A_HEADS, A_HEAD_DIM)
    q_hq = q_s.transpose(0, 2, 1, 3).reshape(bs, hq, A_HEAD_DIM)
    k_new = ak_s.reshape(bs, hq, A_HEAD_DIM)
    v_new = av_s.reshape(bs, hq, A_HEAD_DIM)
    rows = bias_tiles[:, :, :, :ts].transpose(0, 1, 3, 2)
    expand = lambda a: jnp.repeat(a, A_HEADS, axis=-1).reshape(hq, -1)
    past_bias = jnp.stack([rows[:, 2], rows[:, 1]])
    own_bias = expand(rows[:, 0, :, :ts])
    o_part, m_part, l_part, kmean = _moba_past(page_table, q_hq, cache_k, cache_v, past_bias, layer)
    oa_hq = _moba_combine(q_hq, kmean.transpose(0, 2, 1, 3), k_new, v_new, own_bias, o_part, m_part, l_part)
    oa_s = oa_hq.reshape(bs, A_HEADS, ts, A_HEAD_DIM).transpose(0, 2, 1, 3).reshape(bs * ts, A_HEADS * A_HEAD_DIM)
    y_p, y_s = _tail(hp, om_p, oa_p, p_prompt[layer].reshape(bp * tp, PLE_DIM),
                     hs, om_s, oa_s, p_sample[layer].reshape(bs * ts, PLE_DIM), w, g_fin)
    k_s = k_new.reshape((1, bs, ts) + kv_shape)
    v_s = v_new.reshape((1, bs, ts) + kv_shape)
    cs, ns, ms = _split_state(c_s, m_s)

    return (y_p.reshape(bp, tp, D_MODEL), y_s.reshape(bs, ts, D_MODEL),
            k_p, v_p, cp[None], np_[None], mp[None],
            k_s, v_s, cs[None], ns[None], ms[None])
```

```python
import functools
import math

import jax
import jax.numpy as jnp
from jax import lax
from jax.experimental import pallas as pl
from jax.experimental.pallas import tpu as pltpu

F32 = jnp.float32
BF16 = jnp.bfloat16
HIGHEST = lax.Precision.HIGHEST

D_MODEL = 2048
M_HEADS = 8
M_V_DIM = 128
M_QK_DIM = 64
M_CHUNK = 64
A_HEADS = 8
A_HEAD_DIM = 128
MOBA_BLOCK = 256
MOBA_TOP_K = 3
NUM_BUCKETS = 32
MAX_EXACT = NUM_BUCKETS // 2
MAX_DISTANCE = 128
D_FF = 4 * D_MODEL
PLE_DIM = 256
PAGE_SIZE = 128
EPS = 1e-6

_MQ, _MK, _MV, _MO = 0, 512, 1024, 2048
_GATES = 3072
_AQ = 3088
D_MAIN = 6144
GATE_PAD = 128

VMEM_LIMIT = 56 * 1024 * 1024
NEG_INF = float("-inf")


def _cparams(*sem):
    return pltpu.CompilerParams(dimension_semantics=sem, vmem_limit_bytes=VMEM_LIMIT)


def _rms(x, g):
    return x * lax.rsqrt(jnp.mean(x * x, axis=-1, keepdims=True) + EPS) * g


def _dot_nt(a, b, **kw):
    return lax.dot_general(a, b, (((1,), (1,)), ((), ())), preferred_element_type=F32, **kw)


def _dot_tn(a, b, **kw):
    return lax.dot_general(a, b, (((0,), (0,)), ((), ())), preferred_element_type=F32, **kw)


def _dot(a, b, **kw):
    return jnp.dot(a, b, preferred_element_type=F32, **kw)


_IN_TN = 1024
_N_IN_TILES = D_MAIN // _IN_TN
_N_MLSTM_TILES = _GATES // _IN_TN
_GATE_COLS = _AQ - _GATES
_CAST_ROWS = 256


def _norm_gate_kernel(x_ref, xs_ref, g_ref, wg_ref, xn_ref, gate_ref, xns_ref, gates_ref):
    wg = wg_ref[...]
    hi = wg.astype(BF16).astype(F32)
    row = lax.broadcasted_iota(jnp.int32, wg.shape, 0)
    w2 = jnp.where(row < _GATE_COLS, hi,
                   jnp.where(row < 2 * _GATE_COLS, pltpu.roll(wg - hi, _GATE_COLS, axis=0), 0.0))
    w2 = w2.astype(BF16)

    def norm_and_gate(src_ref, dst_ref, gate_dst_ref):
        xn = _rms(src_ref[...], g_ref[...]).astype(BF16)
        dst_ref[...] = xn
        r = _dot_nt(xn, w2)
        gate_dst_ref[...] = r + pltpu.roll(r, GATE_PAD - _GATE_COLS, axis=1)

    norm_and_gate(x_ref, xn_ref, gate_ref)

    @pl.when(pl.program_id(0) == 0)
    def _():
        norm_and_gate(xs_ref, xns_ref, gates_ref)


def _norm_gate(x, xs, g, w_in_t):
    m, ms = x.shape[0], xs.shape[0]
    tm = 1024 if m % 1024 == 0 else min(512, m)
    assert m % tm == 0
    return pl.pallas_call(
        _norm_gate_kernel,
        grid=(m // tm,),
        in_specs=[
            pl.BlockSpec((tm, D_MODEL), lambda i: (i, 0)),
            pl.BlockSpec((ms, D_MODEL), lambda i: (0, 0)),
            pl.BlockSpec((1, D_MODEL), lambda i: (0, 0)),
            pl.BlockSpec((GATE_PAD, D_MODEL), lambda i: (_GATES // GATE_PAD, 0)),
        ],
        out_specs=[pl.BlockSpec((tm, D_MODEL), lambda i: (i, 0)),
                   pl.BlockSpec((tm, GATE_PAD), lambda i: (i, 0)),
                   pl.BlockSpec((ms, D_MODEL), lambda i: (0, 0)),
                   pl.BlockSpec((ms, GATE_PAD), lambda i: (0, 0))],
        out_shape=[jax.ShapeDtypeStruct((m, D_MODEL), BF16), jax.ShapeDtypeStruct((m, GATE_PAD), F32),
                   jax.ShapeDtypeStruct((ms, D_MODEL), BF16), jax.ShapeDtypeStruct((ms, GATE_PAD), F32)],
        compiler_params=_cparams("arbitrary"),
        name="norm_gate",
    )(x, xs, g, w_in_t)


def _in_proj_kernel(xn_ref, xs_ref, wa_ref, wb_ref, zm_ref, aq_ref, ak_ref, av_ref, zs_ref, w_scr):
    j = pl.program_id(0)
    i = pl.program_id(1)

    @pl.when((i == 0) & (j < _N_MLSTM_TILES))
    def _():
        for r in range(0, _IN_TN, _CAST_ROWS):
            w_scr[r:r + _CAST_ROWS, :] = wa_ref[r:r + _CAST_ROWS, :].astype(BF16)

    @pl.when((i == 0) & (j >= _N_MLSTM_TILES))
    def _():
        body = _IN_TN - _GATE_COLS
        for r in range(0, body, _CAST_ROWS):
            n = min(_CAST_ROWS, body - r)
            w_scr[r:r + n, :] = wa_ref[_GATE_COLS + r:_GATE_COLS + r + n, :].astype(BF16)
        w_scr[body:, :] = wb_ref[:_GATE_COLS, :].astype(BF16)

    @pl.when(i == 0)
    def _():
        zs_ref[...] = _dot_nt(xs_ref[...], w_scr[...])

    @pl.when(j < _N_MLSTM_TILES)
    def _():
        zm_ref[...] = _dot_nt(xn_ref[...], w_scr[...])

    for t, ref in enumerate((aq_ref, ak_ref, av_ref)):
        @pl.when(j == _N_MLSTM_TILES + t)
        def _(ref=ref):
            ref[...] = _dot_nt(xn_ref[...], w_scr[...])


def _in_proj(xn, xs, w_in_t):
    m, ms = xn.shape[0], xs.shape[0]
    tm = min(512, m)
    assert m % tm == 0
    tn = _IN_TN
    last = m // tm - 1

    def held(j, i, first_tile, last_tile):
        return jnp.where(j < first_tile, 0, jnp.where(j <= last_tile, i, last))

    def section(t):
        return pl.BlockSpec((tm, tn), lambda j, i: (held(j, i, t, t), 0))

    sec_shape = jax.ShapeDtypeStruct((m, tn), F32)
    nm = _N_MLSTM_TILES
    return pl.pallas_call(
        _in_proj_kernel,
        grid=(_N_IN_TILES, m // tm),
        in_specs=[
            pl.BlockSpec((tm, D_MODEL), lambda j, i: (i, 0)),
            pl.BlockSpec((ms, D_MODEL), lambda j, i: (0, 0)),
            pl.BlockSpec((tn, D_MODEL), lambda j, i: (j, 0)),
            pl.BlockSpec((GATE_PAD, D_MODEL), lambda j, i: ((j + 1) * (tn // GATE_PAD), 0)),
        ],
        out_specs=[
            pl.BlockSpec((tm, tn), lambda j, i: (held(j, i, 0, nm - 1), jnp.minimum(j, nm - 1))),
            section(nm), section(nm + 1), section(nm + 2),
            pl.BlockSpec((ms, tn), lambda j, i: (0, j)),
        ],
        out_shape=[jax.ShapeDtypeStruct((m, _GATES), F32), sec_shape, sec_shape, sec_shape,
                   jax.ShapeDtypeStruct((ms, D_MAIN), F32)],
        scratch_shapes=[pltpu.VMEM((tn, D_MODEL), BF16)],
        compiler_params=_cparams("arbitrary", "arbitrary"),
        name="in_proj",
    )(xn, xs, w_in_t, w_in_t)


def _log_sigmoid(x):
    return -(jnp.maximum(-x, 0.0) + jnp.log1p(jnp.exp(-jnp.abs(x))))


def _bdot(a, b):
    return lax.dot_general(a, b, (((2,), (1,)), ((0,), (0,))), preferred_element_type=F32)


def _bdot_nt(a, b):
    return lax.dot_general(a, b, (((2,), (2,)), ((0,), (0,))), preferred_element_type=F32)


def _bdot_tn(a, b):
    return lax.dot_general(a, b, (((1,), (1,)), ((0,), (0,))), preferred_element_type=F32)


def _mlstm_kernel(q_ref, k_ref, v_ref, o_ref, g_ref, gb_ref, gh_ref, c0_ref, m0_ref,
                  out_ref, cfin_ref, mfin_ref, c_scr, m_scr, *, chunk, n_chunks):
    L = chunk
    H = M_HEADS
    tb = L * n_chunks
    t = pl.program_id(1)

    @pl.when(t == 0)
    def _():
        c_scr[...] = c0_ref[0]
        m_scr[...] = m0_ref[0]

    g = g_ref[...] + gb_ref[...]
    lane = lax.broadcasted_iota(jnp.int32, (tb, GATE_PAD), 1)
    gl = jnp.where((lane >= H) & (lane < 2 * H), _log_sigmoid(g), g)
    r = lax.broadcasted_iota(jnp.int32, (tb, tb), 0)
    c = lax.broadcasted_iota(jnp.int32, (tb, tb), 1)
    tri = ((r // L == c // L) & (c <= r)).astype(F32)
    cum = _dot(tri, gl, precision=HIGHEST)

    def rows(ci):
        return slice(ci * L, (ci + 1) * L)

    def stack(f):
        return jnp.stack([f(ci, h) for ci in range(n_chunks) for h in range(H)])

    ones_col = (lax.broadcasted_iota(jnp.int32, (L, M_V_DIM), 1) == 0).astype(BF16)
    li = stack(lambda ci, h: gl[rows(ci), h:h + 1])
    b = stack(lambda ci, h: cum[rows(ci), H + h:H + h + 1])
    qb = stack(lambda ci, h: q_ref[rows(ci), h * M_QK_DIM:(h + 1) * M_QK_DIM]).astype(BF16)
    kc = stack(lambda ci, h: k_ref[rows(ci), h * M_QK_DIM:(h + 1) * M_QK_DIM]) * (M_QK_DIM ** -0.5)
    vaug = stack(lambda ci, h: jnp.concatenate(
        [v_ref[rows(ci), h * M_V_DIM:(h + 1) * M_V_DIM].astype(BF16), ones_col], axis=1))

    rl = lax.broadcasted_iota(jnp.int32, (L, L), 0)
    cl = lax.broadcasted_iota(jnp.int32, (L, L), 1)
    w_row = jnp.sum(jnp.where(rl == cl, li - b, 0.0), axis=1, keepdims=True)
    d = jnp.where(cl <= rl, b + w_row, NEG_INF)
    m_loc = jnp.max(d, axis=2, keepdims=True)
    s = _bdot_nt(qb, kc.astype(BF16)) * jnp.exp(d - m_loc)
    nd_loc = _bdot(s.astype(BF16), vaug)
    g_last = m_loc[:, L - 1:L, :]
    b_last = b[:, L - 1:L, :]
    ws = jnp.exp(b_last - b + li - g_last)
    dc_loc = _bdot_tn((kc * ws).astype(BF16), vaug)

    m_prev = m_scr[:, :, 0:1]
    c_aug = c_scr[...]
    for ci in range(n_chunks):
        grp = slice(ci * H, (ci + 1) * H)
        a = b[grp] + m_prev
        m_t = jnp.maximum(a, m_loc[grp])
        nd = (jnp.exp(a - m_t) * _bdot(qb[grp], c_aug.astype(BF16))
              + jnp.exp(m_loc[grp] - m_t) * nd_loc[grp])
        hh = nd[:, :, :M_V_DIM] / jnp.maximum(jnp.abs(nd[:, :, M_V_DIM:M_V_DIM + 1]), jnp.exp(-m_t))
        hn = _rms(hh, gh_ref[...])
        for h in range(H):
            og = jax.nn.sigmoid(o_ref[rows(ci), h * M_V_DIM:(h + 1) * M_V_DIM])
            out_ref[rows(ci), h * M_V_DIM:(h + 1) * M_V_DIM] = (og * hn[h]).astype(out_ref.dtype)
        m_new = jnp.maximum(b_last[grp] + m_prev, g_last[grp])
        c_aug = (jnp.exp(b_last[grp] + m_prev - m_new) * c_aug
                 + jnp.exp(g_last[grp] - m_new) * dc_loc[grp])
        m_prev = m_new
    c_scr[...] = c_aug
    m_scr[...] = jnp.broadcast_to(m_prev, m_scr.shape)

    @pl.when(t == pl.num_programs(1) - 1)
    def _():
        cfin_ref[0] = c_scr[...]
        mfin_ref[0] = m_scr[...]


def _mlstm(z, gates, gate_bias, g_mhead, c0_aug, m0, batch, seq):
    L = math.gcd(seq, M_CHUNK)
    n_chunks = max(1, min(seq // L, 256 // L))
    tb = L * n_chunks
    nt = seq // tb
    row = lambda b, t: b * nt + t
    kern = functools.partial(_mlstm_kernel, chunk=L, n_chunks=n_chunks)
    return pl.pallas_call(
        kern,
        grid=(batch, nt),
        in_specs=[
            pl.BlockSpec((tb, 512), lambda b, t: (row(b, t), _MQ // 512)),
            pl.BlockSpec((tb, 512), lambda b, t: (row(b, t), _MK // 512)),
            pl.BlockSpec((tb, 1024), lambda b, t: (row(b, t), _MV // 1024)),
            pl.BlockSpec((tb, 1024), lambda b, t: (row(b, t), _MO // 1024)),
            pl.BlockSpec((tb, GATE_PAD), lambda b, t: (row(b, t), 0)),
            pl.BlockSpec((1, GATE_PAD), lambda b, t: (0, 0)),
            pl.BlockSpec((M_HEADS, 1, M_V_DIM), lambda b, t: (0, 0, 0)),
            pl.BlockSpec((1, M_HEADS, M_QK_DIM, 2 * M_V_DIM), lambda b, t: (b, 0, 0, 0)),
            pl.BlockSpec((1, M_HEADS, 1, GATE_PAD), lambda b, t: (b, 0, 0, 0)),
        ],
        out_specs=[
            pl.BlockSpec((tb, M_HEADS * M_V_DIM), lambda b, t: (row(b, t), 0)),
            pl.BlockSpec((1, M_HEADS, M_QK_DIM, 2 * M_V_DIM), lambda b, t: (b, 0, 0, 0)),
            pl.BlockSpec((1, M_HEADS, 1, GATE_PAD), lambda b, t: (b, 0, 0, 0)),
        ],
        out_shape=[
            jax.ShapeDtypeStruct((batch * seq, M_HEADS * M_V_DIM), BF16 if tb % 16 == 0 else F32),
            jax.ShapeDtypeStruct((batch, M_HEADS, M_QK_DIM, 2 * M_V_DIM), F32),
            jax.ShapeDtypeStruct((batch, M_HEADS, 1, GATE_PAD), F32),
        ],
        scratch_shapes=[pltpu.VMEM((M_HEADS, M_QK_DIM, 2 * M_V_DIM), F32),
                        pltpu.VMEM((M_HEADS, 1, GATE_PAD), F32)],
        compiler_params=_cparams("parallel", "arbitrary"),
        name="mlstm",
    )(z, z, z, z, gates, gate_bias, g_mhead.reshape(M_HEADS, 1, M_V_DIM), c0_aug, m0)


def _t5_bucket(rel):
    n = jnp.maximum(rel, 0)
    nf = jnp.maximum(n, 1).astype(F32)
    large = MAX_EXACT + (jnp.log(nf / MAX_EXACT) / math.log(MAX_DISTANCE / MAX_EXACT)
                         * (NUM_BUCKETS - MAX_EXACT)).astype(jnp.int32)
    large = jnp.minimum(large, NUM_BUCKETS - 1)
    return jnp.where(n < MAX_EXACT, n, large)


def _bias_kernel(tab_ref, out_ref):
    h = pl.program_id(0)
    i = lax.broadcasted_iota(jnp.int32, (MOBA_BLOCK, MOBA_BLOCK), 1)
    j = lax.broadcasted_iota(jnp.int32, (MOBA_BLOCK, MOBA_BLOCK), 0)
    for kind, rel in ((0, i - j), (1, MOBA_BLOCK + i - j)):
        bucket = _t5_bucket(rel)
        vals = [tab_ref[h, b] for b in range(NUM_BUCKETS)]
        bit = 1
        while len(vals) > 1:
            odd = (bucket & bit) != 0
            vals = [jnp.where(odd, vals[2 * m + 1], vals[2 * m]) for m in range(len(vals) // 2)]
            bit *= 2
        out_ref[0, kind] = vals[0]
    out_ref[0, 2] = jnp.full((MOBA_BLOCK, MOBA_BLOCK), tab_ref[h, NUM_BUCKETS - 1], F32)


def _bias_tiles(rel_table):
    assert MOBA_BLOCK + 1 >= MAX_DISTANCE and NUM_BUCKETS & (NUM_BUCKETS - 1) == 0
    tab = rel_table.T.astype(F32)
    return pl.pallas_call(
        _bias_kernel,
        grid=(A_HEADS,),
        in_specs=[pl.BlockSpec(memory_space=pltpu.SMEM)],
        out_specs=pl.BlockSpec((1, 3, MOBA_BLOCK, MOBA_BLOCK), lambda h: (h, 0, 0, 0)),
        out_shape=jax.ShapeDtypeStruct((A_HEADS, 3, MOBA_BLOCK, MOBA_BLOCK), F32),
        compiler_params=_cparams("parallel"),
        name="t5_bias_tiles",
    )(tab)


_VT_ROWS = A_HEAD_DIM + 16


def _moba_prompt_kernel(q_ref, k_ref, v_ref, bias_ref, wu_ref, wd_ref, wg_ref, o_ref, wub_ref, wdb_ref, wgb_ref,
                        kb_scr, qt_scr, vt_scr, gt_scr, s_scr, p_scr, *, n_blocks):
    nb = n_blocks
    blk = MOBA_BLOCK
    nbp = gt_scr.shape[0]
    wub_ref[...] = wu_ref[...].astype(BF16)
    wdb_ref[...] = wd_ref[...].astype(BF16)
    wgb_ref[...] = wg_ref[...].astype(BF16)

    kb_scr[...] = k_ref[...].astype(BF16)
    for c in range(nb):
        cols = slice(c * blk, (c + 1) * blk)
        qt_scr[:, cols] = (q_ref[cols, :] * (A_HEAD_DIM ** -0.5)).T.astype(BF16)
        vt_scr[0:A_HEAD_DIM, cols] = v_ref[cols, :].T.astype(BF16)
    ones_row = (lax.broadcasted_iota(jnp.int32, (_VT_ROWS - A_HEAD_DIM, nb * blk), 0) == 0).astype(BF16)
    vt_scr[A_HEAD_DIM:, :] = ones_row
    need_gate = nb - 1 > MOBA_TOP_K
    if need_gate:
        means = [jnp.mean(k_ref[n * blk:(n + 1) * blk, :], axis=0, keepdims=True) for n in range(nb)]
        kmean = jnp.concatenate(means + [jnp.zeros((128 - nb, A_HEAD_DIM), F32)], axis=0)
        for c in range(MOBA_TOP_K + 1, nb):
            cols = slice(c * blk, (c + 1) * blk)
            gate = _dot_nt(q_ref[cols, :] * (A_HEAD_DIM ** -0.5), kmean, precision=HIGHEST)
            gt_scr[:, cols] = gate.T[:nbp, :]
    key = lax.broadcasted_iota(jnp.int32, (blk, blk), 0)
    qry = lax.broadcasted_iota(jnp.int32, (blk, blk), 1)
    causal = key <= qry
    blk_id = lax.broadcasted_iota(jnp.int32, (nbp, blk), 0)

    def scores(i):
        cols = slice(i * blk, (i + 1) * blk)
        s_scr[i % 2, 0:(i + 1) * blk, :] = _dot(kb_scr[0:(i + 1) * blk, :], qt_scr[:, cols])

    def softmax(i):
        cols = slice(i * blk, (i + 1) * blk)
        s_buf, p_buf = s_scr.at[i % 2], p_scr.at[i % 2]
        selneg = None
        if i > MOBA_TOP_K:
            g = gt_scr[:, cols]
            rank = jnp.zeros((nbp, blk), F32)
            for j in range(i):
                gj = g[j:j + 1, :]
                rank = rank + ((gj > g) | ((gj == g) & (j < blk_id))).astype(F32)
            selneg = jnp.where(rank < MOBA_TOP_K, 0.0, NEG_INF)
        m = jnp.full((1, blk), NEG_INF, F32)
        for n in range(i + 1):
            rows = slice(n * blk, (n + 1) * blk)
            kind = 0 if n == i else (1 if n == i - 1 else 2)
            st = s_buf[rows, :] + bias_ref[0, kind]
            if n == i:
                st = jnp.where(causal, st, NEG_INF)
            elif selneg is not None:
                st = st + selneg[n:n + 1, :]
            s_buf[rows, :] = st
            m = jnp.maximum(m, jnp.max(st, axis=0, keepdims=True))
        for n in range(i + 1):
            rows = slice(n * blk, (n + 1) * blk)
            p_buf[rows, :] = jnp.exp(s_buf[rows, :] - m).astype(BF16)

    def values(i):
        cols = slice(i * blk, (i + 1) * blk)
        nd = _dot(vt_scr[:, 0:(i + 1) * blk], p_scr[i % 2, 0:(i + 1) * blk, :])
        out = nd[0:A_HEAD_DIM, :] / nd[A_HEAD_DIM:A_HEAD_DIM + 1, :]
        o_ref[cols, :] = out.T.astype(o_ref.dtype)

    scores(0)
    for i in range(nb):
        if i + 1 < nb:
            scores(i + 1)
        if i > 0:
            values(i - 1)
        softmax(i)
    values(nb - 1)


def _moba_prompt(aq, ak, av, bias_tiles, w_up, w_down, w_ple_gate, batch, seq):
    assert seq % MOBA_BLOCK == 0
    nb = seq // MOBA_BLOCK
    assert nb <= 128
    nbp = -(-nb // 8) * 8
    steps = batch * A_HEADS
    slab = D_FF // steps
    gate_rows = D_MODEL // steps
    assert D_FF % steps == 0 and slab % 128 == 0 and D_MODEL % steps == 0 and gate_rows % 16 == 0
    head = pl.BlockSpec((seq, A_HEAD_DIM), lambda b, h: (b, h))
    up_slab = pl.BlockSpec((D_MODEL, slab), lambda b, h: (0, b * A_HEADS + h))
    down_slab = pl.BlockSpec((slab, D_MODEL), lambda b, h: (b * A_HEADS + h, 0))
    gate_slab = pl.BlockSpec((gate_rows, D_MODEL), lambda b, h: (b * A_HEADS + h, 0))
    kern = functools.partial(_moba_prompt_kernel, n_blocks=nb)
    return pl.pallas_call(
        kern,
        grid=(batch, A_HEADS),
        in_specs=[head, head, head,
                  pl.BlockSpec((1, 3, MOBA_BLOCK, MOBA_BLOCK), lambda b, h: (h, 0, 0, 0)),
                  up_slab, down_slab, gate_slab],
        out_specs=[head, up_slab, down_slab, gate_slab],
        out_shape=[jax.ShapeDtypeStruct((batch * seq, A_HEADS * A_HEAD_DIM), BF16),
                   jax.ShapeDtypeStruct((D_MODEL, D_FF), BF16),
                   jax.ShapeDtypeStruct((D_FF, D_MODEL), BF16),
                   jax.ShapeDtypeStruct((D_MODEL, D_MODEL), BF16)],
        scratch_shapes=[pltpu.VMEM((seq, A_HEAD_DIM), BF16),
                        pltpu.VMEM((A_HEAD_DIM, seq), BF16),
                        pltpu.VMEM((_VT_ROWS, seq), BF16),
                        pltpu.VMEM((nbp, seq), F32),
                        pltpu.VMEM((2, seq, MOBA_BLOCK), F32),
                        pltpu.VMEM((2, seq, MOBA_BLOCK), BF16)],
        compiler_params=_cparams("parallel", "parallel"),
        name="moba_prompt",
    )(aq, ak, av, bias_tiles, w_up, w_down, w_ple_gate)


PAGES_PER_BLOCK = MOBA_BLOCK // PAGE_SIZE
PAST_BLOCKS_PER_STEP = 8


def _moba_past_kernel(pt_ref, q_ref, *refs, n_pages_step):
    del pt_ref
    k_refs = refs[:n_pages_step]
    v_refs = refs[n_pages_step:2 * n_pages_step]
    bias_ref, o_ref, m_ref, l_ref, km_ref = refs[2 * n_pages_step:]
    hq = q_ref.shape[1]
    nq = hq // A_HEADS
    qb = (q_ref[0] * (A_HEAD_DIM ** -0.5)).reshape(A_HEADS, nq, A_HEAD_DIM).astype(BF16)
    n_blocks = n_pages_step // PAGES_PER_BLOCK
    is_last_step = pl.program_id(1) == pl.num_programs(1) - 1

    def head_rows(ref, h):
        return ref[0, 0, pl.ds(h, PAGE_SIZE, stride=A_HEADS), :]

    for g in range(n_blocks):
        pages = range(g * PAGES_PER_BLOCK, (g + 1) * PAGES_PER_BLOCK)
        ksum = sum(jnp.sum(k_refs[p][0, 0].reshape(PAGE_SIZE, A_HEADS, A_HEAD_DIM), axis=0) for p in pages)
        km_ref[0, g] = ksum / MOBA_BLOCK
        kind = jnp.where(is_last_step, 1, 0) if g == n_blocks - 1 else 0
        kh = jnp.stack([jnp.concatenate([head_rows(k_refs[p], h) for p in pages], axis=0)
                        for h in range(A_HEADS)]).astype(BF16)
        vh = jnp.stack([jnp.concatenate([head_rows(v_refs[p], h) for p in pages], axis=0)
                        for h in range(A_HEADS)]).astype(BF16)
        lt = _bdot_nt(qb, kh) + bias_ref[kind]
        m = jnp.max(lt, axis=2, keepdims=True)
        p_ = jnp.exp(lt - m)
        l = jnp.sum(p_, axis=2, keepdims=True)
        o_ref[0, g] = _bdot(p_.astype(BF16), vh).reshape(hq, A_HEAD_DIM)
        m_ref[0, g] = jnp.broadcast_to(m, (A_HEADS, nq, A_HEAD_DIM)).reshape(hq, A_HEAD_DIM)
        l_ref[0, g] = jnp.broadcast_to(l, (A_HEADS, nq, A_HEAD_DIM)).reshape(hq, A_HEAD_DIM)


def _moba_past(page_table, q_hq, cache_k, cache_v, past_bias, layer):
    bs, n_pages = page_table.shape
    assert n_pages % PAGES_PER_BLOCK == 0
    nbp = n_pages // PAGES_PER_BLOCK
    gb = math.gcd(nbp, PAST_BLOCKS_PER_STEP)
    pps = gb * PAGES_PER_BLOCK
    hq = q_hq.shape[1]
    page = (1, 1, PAGE_SIZE * A_HEADS, A_HEAD_DIM)
    cache_k, cache_v = (c.reshape(c.shape[:2] + page[2:]) for c in (cache_k, cache_v))
    part = pl.BlockSpec((1, gb, hq, A_HEAD_DIM), lambda b, n, pt: (b, n, 0, 0))

    def page_spec(p):
        return pl.BlockSpec(page, lambda b, n, pt: (layer, pt[b, pps * n + p], 0, 0))

    grid_spec = pltpu.PrefetchScalarGridSpec(
        num_scalar_prefetch=1,
        grid=(bs, nbp // gb),
        in_specs=[pl.BlockSpec((1, hq, A_HEAD_DIM), lambda b, n, pt: (b, 0, 0))]
        + [page_spec(p) for p in range(pps)] * 2
        + [pl.BlockSpec((2, A_HEADS, hq // A_HEADS, MOBA_BLOCK), lambda b, n, pt: (0, 0, 0, 0))],
        out_specs=[part, part, part,
                   pl.BlockSpec((1, gb, A_HEADS, A_HEAD_DIM), lambda b, n, pt: (b, n, 0, 0))],
    )
    pshape = jax.ShapeDtypeStruct((bs, nbp, hq, A_HEAD_DIM), F32)
    return pl.pallas_call(
        functools.partial(_moba_past_kernel, n_pages_step=pps),
        grid_spec=grid_spec,
        out_shape=[pshape, pshape, pshape, jax.ShapeDtypeStruct((bs, nbp, A_HEADS, A_HEAD_DIM), F32)],
        compiler_params=_cparams("parallel", "parallel"),
        name="moba_past_blocks",
    )(page_table, q_hq, *([cache_k] * pps), *([cache_v] * pps), past_bias)


def _moba_combine_kernel(q_ref, km_ref, kn_ref, vn_ref, ob_ref, op_ref, mp_ref, lp_ref, out_ref):
    nbp = km_ref.shape[2]
    q = q_ref[0] * (A_HEAD_DIM ** -0.5)
    hq = q.shape[0]
    nq = hq // A_HEADS
    gate = jnp.concatenate(
        [_dot_nt(q[h * nq:(h + 1) * nq], km_ref[0, h], precision=HIGHEST) for h in range(A_HEADS)], axis=0)
    lane = lax.broadcasted_iota(jnp.int32, (hq, nbp), 1)
    rank = jnp.zeros((hq, nbp), F32)
    for j in range(nbp):
        gj = gate[:, j:j + 1]
        beats = (gj > gate) | ((gj == gate) & (j < lane))
        rank = rank + beats.astype(F32)
    selneg = jnp.where(rank < MOBA_TOP_K, 0.0, NEG_INF)
    qb = q.astype(BF16)
    lt = _dot_nt(qb, kn_ref[0].astype(BF16))
    r = lax.broadcasted_iota(jnp.int32, lt.shape, 0)
    c = lax.broadcasted_iota(jnp.int32, lt.shape, 1)
    ok = ((c % A_HEADS) == (r // nq)) & ((c // A_HEADS) <= (r % nq))
    lo = jnp.where(ok, lt + ob_ref[...], NEG_INF)
    m = jnp.max(lo, axis=1, keepdims=True)
    for n in range(nbp):
        m = jnp.maximum(m, mp_ref[0, n][:, 0:1] + selneg[:, n:n + 1])
    p = jnp.exp(lo - m)
    l = jnp.sum(p, axis=1, keepdims=True)
    acc = _dot(p.astype(BF16), vn_ref[0].astype(BF16))
    for n in range(nbp):
        w = jnp.exp(mp_ref[0, n][:, 0:1] + selneg[:, n:n + 1] - m)
        l = l + w * lp_ref[0, n][:, 0:1]
        acc = acc + w * op_ref[0, n]
    out_ref[0] = acc / l


def _moba_combine(q_hq, kmean_hn, k_new, v_new, own_bias, o_part, m_part, l_part):
    bs, hq, _ = q_hq.shape
    nbp = o_part.shape[1]
    part = pl.BlockSpec((1, nbp, hq, A_HEAD_DIM), lambda b: (b, 0, 0, 0))
    tok = pl.BlockSpec((1, hq, A_HEAD_DIM), lambda b: (b, 0, 0))
    return pl.pallas_call(
        _moba_combine_kernel,
        grid=(bs,),
        in_specs=[tok,
                  pl.BlockSpec((1, A_HEADS, nbp, A_HEAD_DIM), lambda b: (b, 0, 0, 0)),
                  tok, tok,
                  pl.BlockSpec((hq, hq), lambda b: (0, 0)),
                  part, part, part],
        out_specs=tok,
        out_shape=jax.ShapeDtypeStruct((bs, hq, A_HEAD_DIM), F32),
        compiler_params=_cparams("parallel"),
        name="moba_combine",
    )(q_hq, kmean_hn, k_new, v_new, own_bias, o_part, m_part, l_part)


def _out_proj_kernel(h_ref, om_ref, oa_ref, hs_ref, oms_ref, oas_ref, wm_ref, wa_ref, o_ref, os_ref,
                     wm_scr, wa_scr):
    def project(h, om, oa, dst):
        dst[...] = (h[...] + _dot(om[...].astype(BF16), wm_scr[...])
                    + _dot(oa[...].astype(BF16), wa_scr[...]))

    @pl.when(pl.program_id(0) == 0)
    def _():
        for r in range(0, wm_scr.shape[0], _CAST_ROWS):
            wm_scr[r:r + _CAST_ROWS, :] = wm_ref[r:r + _CAST_ROWS, :].astype(BF16)
            wa_scr[r:r + _CAST_ROWS, :] = wa_ref[r:r + _CAST_ROWS, :].astype(BF16)
        project(hs_ref, oms_ref, oas_ref, os_ref)

    project(h_ref, om_ref, oa_ref, o_ref)


def _out_proj(h, out_m, out_a, hs, out_ms, out_as, w_out):
    m, ms = h.shape[0], hs.shape[0]
    tm = min(512, m)
    assert m % tm == 0
    half = M_HEADS * M_V_DIM
    rows = pl.BlockSpec((tm, D_MODEL), lambda i: (i, 0))
    mix = pl.BlockSpec((tm, half), lambda i: (i, 0))
    rows_s = pl.BlockSpec((ms, D_MODEL), lambda i: (0, 0))
    mix_s = pl.BlockSpec((ms, half), lambda i: (0, 0))
    return pl.pallas_call(
        _out_proj_kernel,
        grid=(m // tm,),
        in_specs=[
            rows, mix, mix, rows_s, mix_s, mix_s,
            pl.BlockSpec((half, D_MODEL), lambda i: (0, 0), pipeline_mode=pl.Buffered(1)),
            pl.BlockSpec((half, D_MODEL), lambda i: (1, 0), pipeline_mode=pl.Buffered(1)),
        ],
        out_specs=[rows, rows_s],
        out_shape=[jax.ShapeDtypeStruct((m, D_MODEL), F32), jax.ShapeDtypeStruct((ms, D_MODEL), F32)],
        scratch_shapes=[pltpu.VMEM((half, D_MODEL), BF16), pltpu.VMEM((half, D_MODEL), BF16)],
        compiler_params=_cparams("arbitrary"),
        name="out_proj",
    )(h, out_m, out_a, hs, out_ms, out_as, w_out, w_out)


def _mlp_kernel(h_ref, hs_ref, g_ref, wu_ref, wd_ref, o_ref, os_ref, xn_ref, xns_ref):
    f = pl.program_id(1)

    def accumulate(src, dst, xn):
        @pl.when(f == 0)
        def _():
            xn[...] = _rms(src[...], g_ref[...]).astype(BF16)
            dst[...] = src[...]

        u = _dot(xn[...], wu_ref[...])
        a = jnp.square(jnp.maximum(u, 0.0)).astype(BF16)
        dst[...] += _dot(a, wd_ref[...])

    accumulate(h_ref, o_ref, xn_ref)

    @pl.when(pl.program_id(0) == 0)
    def _():
        accumulate(hs_ref, os_ref, xns_ref)


def _mlp(h, hs, g, w_up, w_down):
    m, ms = h.shape[0], hs.shape[0]
    tm = min(512, m)
    assert m % tm == 0
    tf = 1024
    return pl.pallas_call(
        _mlp_kernel,
        grid=(m // tm, D_FF // tf),
        in_specs=[
            pl.BlockSpec((tm, D_MODEL), lambda i, f: (i, 0)),
            pl.BlockSpec((ms, D_MODEL), lambda i, f: (0, 0)),
            pl.BlockSpec((1, D_MODEL), lambda i, f: (0, 0)),
            pl.BlockSpec((D_MODEL, tf), lambda i, f: (0, f)),
            pl.BlockSpec((tf, D_MODEL), lambda i, f: (f, 0)),
        ],
        out_specs=[pl.BlockSpec((tm, D_MODEL), lambda i, f: (i, 0)),
                   pl.BlockSpec((ms, D_MODEL), lambda i, f: (0, 0))],
        out_shape=[jax.ShapeDtypeStruct((m, D_MODEL), F32), jax.ShapeDtypeStruct((ms, D_MODEL), F32)],
        scratch_shapes=[pltpu.VMEM((tm, D_MODEL), BF16), pltpu.VMEM((ms, D_MODEL), BF16)],
        compiler_params=_cparams("arbitrary", "arbitrary"),
        name="mlp",
    )(h, hs, g, w_up, w_down)


def _ple_kernel(h_ref, pe_ref, hs_ref, pes_ref, g_ref, wg_ref, wp_ref, gf_ref, y_ref, ys_ref):
    def gated_embedding(src, pe, dst):
        h = src[...]
        gate = jax.nn.sigmoid(_dot(_rms(h, g_ref[...]).astype(BF16), wg_ref[...]))
        proj = _dot(pe[...].astype(BF16), wp_ref[...])
        dst[...] = _rms(h + gate * proj, gf_ref[...])

    gated_embedding(h_ref, pe_ref, y_ref)

    @pl.when(pl.program_id(0) == 0)
    def _():
        gated_embedding(hs_ref, pes_ref, ys_ref)


def _ple_final(h, pe, hs, pes, g_ple, w_gate, w_proj, g_final):
    m, ms = h.shape[0], hs.shape[0]
    tm = min(512, m)
    assert m % tm == 0
    vec = pl.BlockSpec((1, D_MODEL), lambda i: (0, 0))
    return pl.pallas_call(
        _ple_kernel,
        grid=(m // tm,),
        in_specs=[
            pl.BlockSpec((tm, D_MODEL), lambda i: (i, 0)),
            pl.BlockSpec((tm, PLE_DIM), lambda i: (i, 0)),
            pl.BlockSpec((ms, D_MODEL), lambda i: (0, 0)),
            pl.BlockSpec((ms, PLE_DIM), lambda i: (0, 0)),
            vec,
            pl.BlockSpec((D_MODEL, D_MODEL), lambda i: (0, 0)),
            pl.BlockSpec((PLE_DIM, D_MODEL), lambda i: (0, 0)),
            vec,
        ],
        out_specs=[pl.BlockSpec((tm, D_MODEL), lambda i: (i, 0)),
                   pl.BlockSpec((ms, D_MODEL), lambda i: (0, 0))],
        out_shape=[jax.ShapeDtypeStruct((m, D_MODEL), F32), jax.ShapeDtypeStruct((ms, D_MODEL), F32)],
        compiler_params=_cparams("arbitrary"),
        name="ple_final_norm",
    )(h, pe, hs, pes, g_ple, w_gate, w_proj, g_final)


def _layer_weights(w_in, b_igate, b_fgate, g_mix, g_mhead, w_out, g_ffn, w_up, w_down,
                   g_ple, w_ple_gate, w_ple_proj):
    gate_bias = jnp.pad(jnp.concatenate([b_igate, b_fgate]), (0, GATE_PAD - 2 * M_HEADS)).reshape(1, GATE_PAD)
    return dict(
        w_in=w_in.T, gate_bias=gate_bias.astype(F32),
        g_mix=g_mix.reshape(1, D_MODEL), g_mhead=g_mhead, w_out=w_out,
        g_ffn=g_ffn.reshape(1, D_MODEL), w_up=w_up, w_down=w_down,
        g_ple=g_ple.reshape(1, D_MODEL), w_ple_gate=w_ple_gate,
        w_ple_proj=w_ple_proj.astype(BF16))


def _tail(h, out_m, out_a, pe, hs, out_ms, out_as, pes, w, g_final):
    h, hs = _out_proj(h, out_m, out_a, hs, out_ms, out_as, w["w_out"])
    h, hs = _mlp(h, hs, w["g_ffn"], w["w_up_bf16"], w["w_down_bf16"])
    return _ple_final(h, pe, hs, pes, w["g_ple"], w["w_ple_gate_bf16"], w["w_ple_proj"], g_final)


def _split_state(c_aug, m_fin):
    return c_aug[..., :M_V_DIM], c_aug[..., M_V_DIM], m_fin[..., 0, 0]


def kernel(x_prompt, x_sample, cache_k, cache_v, state_C, state_n, state_m, page_table, p_prompt, p_sample, rel_bias_table, w_in, b_igate, b_fgate, g_mix, g_mhead, w_out, g_ffn, w_up, w_down, g_ple, w_ple_gate, w_ple_proj, g_final):
    depth = w_in.shape[0]
    assert depth == 1, "one decoder layer per call"
    layer = 0
    bp, tp, _ = x_prompt.shape
    bs, ts, _ = x_sample.shape
    past_len = page_table.shape[1] * PAGE_SIZE
    assert past_len % MOBA_BLOCK == 0 and ts <= MOBA_BLOCK
    kv_shape = (A_HEADS, A_HEAD_DIM)
    g_fin = g_final.reshape(1, D_MODEL)

    w = _layer_weights(w_in[layer], b_igate[layer], b_fgate[layer], g_mix[layer], g_mhead[layer], w_out[layer],
                       g_ffn[layer], w_up[layer], w_down[layer], g_ple[layer], w_ple_gate[layer],
                       w_ple_proj[layer])
    bias_tiles = _bias_tiles(rel_bias_table)

    hp = x_prompt.reshape(bp * tp, D_MODEL)
    hs = x_sample.reshape(bs * ts, D_MODEL)
    xn_p, gp, xn_s, gs = _norm_gate(hp, hs, w["g_mix"], w["w_in"])
    zp, aq_p, ak_p, av_p, zs = _in_proj(xn_p, xn_s, w["w_in"])
    aq_s, ak_s, av_s = (zs[:, _GATES + t * _IN_TN:_GATES + (t + 1) * _IN_TN] for t in range(3))
    c0 = jnp.zeros((bp, M_HEADS, M_QK_DIM, 2 * M_V_DIM), F32)
    m0 = jnp.zeros((bp, M_HEADS, 1, GATE_PAD), F32)
    om_p, c_p, m_p = _mlstm(zp, gp, w["gate_bias"], w["g_mhead"], c0, m0, bp, tp)
    oa_p, w["w_up_bf16"], w["w_down_bf16"], w["w_ple_gate_bf16"] = _moba_prompt(
        aq_p, ak_p, av_p, bias_tiles, w["w_up"], w["w_down"], w["w_ple_gate"], bp, tp)
    k_p = ak_p.reshape((1, bp, tp) + kv_shape)
    v_p = av_p.reshape((1, bp, tp) + kv_shape)
    cp, np_, mp = _split_state(c_p, m_p)

    c0s = jnp.concatenate([state_C[layer], state_n[layer][..., None],
                           jnp.zeros((bs, M_HEADS, M_QK_DIM, M_V_DIM - 1), F32)], axis=-1)
    m0s = jnp.broadcast_to(state_m[layer][..., None, None], (bs, M_HEADS, 1, GATE_PAD))
    om_s, c_s, m_s = _mlstm(zs, gs, w["gate_bias"], w["g_mhead"], c0s, m0s, bs, ts)

    hq = A_HEADS * ts
    q_s = aq_s.reshape(bs, ts, A_HEADS, A_HEAD_DIM)
    q_hq = q_s.transpose(0, 2, 1, 3).reshape(bs, hq, A_HEAD_DIM)
    k_new = ak_s.reshape(bs, hq, A_HEAD_DIM)
    v_new = av_s.reshape(bs, hq, A_HEAD_DIM)
    rows = bias_tiles[:, :, :, :ts].transpose(0, 1, 3, 2)
    expand = lambda a: jnp.repeat(a, A_HEADS, axis=-1).reshape(hq, -1)
    past_bias = jnp.stack([rows[:, 2], rows[:, 1]])
    own_bias = expand(rows[:, 0, :, :ts])
    o_part, m_part, l_part, kmean = _moba_past(page_table, q_hq, cache_k, cache_v, past_bias, layer)
    oa_hq = _moba_combine(q_hq, kmean.transpose(0, 2, 1, 3), k_new, v_new, own_bias, o_part, m_part, l_part)
    oa_s = oa_hq.reshape(bs, A_HEADS, ts, A_HEAD_DIM).transpose(0, 2, 1, 3).reshape(bs * ts, A_HEADS * A_HEAD_DIM)
    y_p, y_s = _tail(hp, om_p, oa_p, p_prompt[layer].reshape(bp * tp, PLE_DIM),
                     hs, om_s, oa_s, p_sample[layer].reshape(bs * ts, PLE_DIM), w, g_fin)
    k_s = k_new.reshape((1, bs, ts) + kv_shape)
    v_s = v_new.reshape((1, bs, ts) + kv_shape)
    cs, ns, ms = _split_state(c_s, m_s)

    return (y_p.reshape(bp, tp, D_MODEL), y_s.reshape(bs, ts, D_MODEL),
            k_p, v_p, cp[None], np_[None], mp[None],
            k_s, v_s, cs[None], ns[None], ms[None])
```

```python
import functools
import math

import jax
import jax.numpy as jnp
from jax import lax
from jax.experimental import pallas as pl
from jax.experimental.pallas import tpu as pltpu

F32 = jnp.float32
BF16 = jnp.bfloat16
HIGHEST = lax.Precision.HIGHEST

D_MODEL = 2048
M_HEADS = 8
M_V_DIM = 128
M_QK_DIM = 64
M_CHUNK = 64
A_HEADS = 8
A_HEAD_DIM = 128
MOBA_BLOCK = 256
MOBA_TOP_K = 3
NUM_BUCKETS = 32
MAX_EXACT = NUM_BUCKETS // 2
MAX_DISTANCE = 128
D_FF = 4 * D_MODEL
PLE_DIM = 256
PAGE_SIZE = 128
EPS = 1e-6

_MQ, _MK, _MV, _MO = 0, 512, 1024, 2048
_GATES = 3072
_AQ = 3088
D_MAIN = 6144
GATE_PAD = 128

VMEM_LIMIT = 56 * 1024 * 1024
NEG_INF = float("-inf")


def _cparams(*sem):
    return pltpu.CompilerParams(dimension_semantics=sem, vmem_limit_bytes=VMEM_LIMIT)


def _rms(x, g):
    return x * lax.rsqrt(jnp.mean(x * x, axis=-1, keepdims=True) + EPS) * g


def _dot_nt(a, b, **kw):
    return lax.dot_general(a, b, (((1,), (1,)), ((), ())), preferred_element_type=F32, **kw)


def _dot_tn(a, b, **kw):
    return lax.dot_general(a, b, (((0,), (0,)), ((), ())), preferred_element_type=F32, **kw)


def _dot(a, b, **kw):
    return jnp.dot(a, b, preferred_element_type=F32, **kw)


_IN_TN = 1024
_N_IN_TILES = D_MAIN // _IN_TN
_N_MLSTM_TILES = _GATES // _IN_TN
_GATE_COLS = _AQ - _GATES
_CAST_ROWS = 256


def _norm_gate_kernel(x_ref, xs_ref, g_ref, wg_ref, xn_ref, gate_ref, xns_ref, gates_ref):
    wg = wg_ref[...]
    hi = wg.astype(BF16).astype(F32)
    row = lax.broadcasted_iota(jnp.int32, wg.shape, 0)
    w2 = jnp.where(row < _GATE_COLS, hi,
                   jnp.where(row < 2 * _GATE_COLS, pltpu.roll(wg - hi, _GATE_COLS, axis=0), 0.0))
    w2 = w2.astype(BF16)

    def norm_and_gate(src_ref, dst_ref, gate_dst_ref):
        xn = _rms(src_ref[...], g_ref[...]).astype(BF16)
        dst_ref[...] = xn
        r = _dot_nt(xn, w2)
        gate_dst_ref[...] = r + pltpu.roll(r, GATE_PAD - _GATE_COLS, axis=1)

    norm_and_gate(x_ref, xn_ref, gate_ref)

    @pl.when(pl.program_id(0) == 0)
    def _():
        norm_and_gate(xs_ref, xns_ref, gates_ref)


def _norm_gate(x, xs, g, w_in_t):
    m, ms = x.shape[0], xs.shape[0]
    tm = 1024 if m % 1024 == 0 else min(512, m)
    assert m % tm == 0
    return pl.pallas_call(
        _norm_gate_kernel,
        grid=(m // tm,),
        in_specs=[
            pl.BlockSpec((tm, D_MODEL), lambda i: (i, 0)),
            pl.BlockSpec((ms, D_MODEL), lambda i: (0, 0)),
            pl.BlockSpec((1, D_MODEL), lambda i: (0, 0)),
            pl.BlockSpec((GATE_PAD, D_MODEL), lambda i: (_GATES // GATE_PAD, 0)),
        ],
        out_specs=[pl.BlockSpec((tm, D_MODEL), lambda i: (i, 0)),
                   pl.BlockSpec((tm, GATE_PAD), lambda i: (i, 0)),
                   pl.BlockSpec((ms, D_MODEL), lambda i: (0, 0)),
                   pl.BlockSpec((ms, GATE_PAD), lambda i: (0, 0))],
        out_shape=[jax.ShapeDtypeStruct((m, D_MODEL), BF16), jax.ShapeDtypeStruct((m, GATE_PAD), F32),
                   jax.ShapeDtypeStruct((ms, D_MODEL), BF16), jax.ShapeDtypeStruct((ms, GATE_PAD), F32)],
        compiler_params=_cparams("arbitrary"),
        name="norm_gate",
    )(x, xs, g, w_in_t)


def _in_proj_kernel(xn_ref, xs_ref, wa_ref, wb_ref, zm_ref, aq_ref, ak_ref, av_ref, zs_ref, w_scr):
    j = pl.program_id(0)
    i = pl.program_id(1)

    @pl.when((i == 0) & (j < _N_MLSTM_TILES))
    def _():
        for r in range(0, _IN_TN, _CAST_ROWS):
            w_scr[r:r + _CAST_ROWS, :] = wa_ref[r:r + _CAST_ROWS, :].astype(BF16)

    @pl.when((i == 0) & (j >= _N_MLSTM_TILES))
    def _():
        body = _IN_TN - _GATE_COLS
        for r in range(0, body, _CAST_ROWS):
            n = min(_CAST_ROWS, body - r)
            w_scr[r:r + n, :] = wa_ref[_GATE_COLS + r:_GATE_COLS + r + n, :].astype(BF16)
        w_scr[body:, :] = wb_ref[:_GATE_COLS, :].astype(BF16)

    @pl.when(i == 0)
    def _():
        zs_ref[...] = _dot_nt(xs_ref[...], w_scr[...])

    @pl.when(j < _N_MLSTM_TILES)
    def _():
        zm_ref[...] = _dot_nt(xn_ref[...], w_scr[...])

    for t, ref in enumerate((aq_ref, ak_ref, av_ref)):
        @pl.when(j == _N_MLSTM_TILES + t)
        def _(ref=ref):
            ref[...] = _dot_nt(xn_ref[...], w_scr[...])


def _in_proj(xn, xs, w_in_t):
    m, ms = xn.shape[0], xs.shape[0]
    tm = min(512, m)
    assert m % tm == 0
    tn = _IN_TN
    last = m // tm - 1

    def held(j, i, first_tile, last_tile):
        return jnp.where(j < first_tile, 0, jnp.where(j <= last_tile, i, last))

    def section(t):
        return pl.BlockSpec((tm, tn), lambda j, i: (held(j, i, t, t), 0))

    sec_shape = jax.ShapeDtypeStruct((m, tn), F32)
    nm = _N_MLSTM_TILES
    return pl.pallas_call(
        _in_proj_kernel,
        grid=(_N_IN_TILES, m // tm),
        in_specs=[
            pl.BlockSpec((tm, D_MODEL), lambda j, i: (i, 0)),
            pl.BlockSpec((ms, D_MODEL), lambda j, i: (0, 0)),
            pl.BlockSpec((tn, D_MODEL), lambda j, i: (j, 0)),
            pl.BlockSpec((GATE_PAD, D_MODEL), lambda j, i: ((j + 1) * (tn // GATE_PAD), 0)),
        ],
        out_specs=[
            pl.BlockSpec((tm, tn), lambda j, i: (held(j, i, 0, nm - 1), jnp.minimum(j, nm - 1))),
            section(nm), section(nm + 1), section(nm + 2),
            pl.BlockSpec((ms, tn), lambda j, i: (0, j)),
        ],
        out_shape=[jax.ShapeDtypeStruct((m, _GATES), F32), sec_shape, sec_shape, sec_shape,
                   jax.ShapeDtypeStruct((ms, D_MAIN), F32)],
        scratch_shapes=[pltpu.VMEM((tn, D_MODEL), BF16)],
        compiler_params=_cparams("arbitrary", "arbitrary"),
        name="in_proj",
    )(xn, xs, w_in_t, w_in_t)


def _log_sigmoid(x):
    return -(jnp.maximum(-x, 0.0) + jnp.log1p(jnp.exp(-jnp.abs(x))))


def _bdot(a, b):
    return lax.dot_general(a, b, (((2,), (1,)), ((0,), (0,))), preferred_element_type=F32)


def _bdot_nt(a, b):
    return lax.dot_general(a, b, (((2,), (2,)), ((0,), (0,))), preferred_element_type=F32)


def _bdot_tn(a, b):
    return lax.dot_general(a, b, (((1,), (1,)), ((0,), (0,))), preferred_element_type=F32)


def _mlstm_kernel(q_ref, k_ref, v_ref, o_ref, g_ref, gb_ref, gh_ref, c0_ref, m0_ref,
                  out_ref, cfin_ref, mfin_ref, c_scr, m_scr, *, chunk, n_chunks):
    L = chunk
    H = M_HEADS
    tb = L * n_chunks
    t = pl.program_id(1)

    @pl.when(t == 0)
    def _():
        c_scr[...] = c0_ref[0]
        m_scr[...] = m0_ref[0]

    g = g_ref[...] + gb_ref[...]
    lane = lax.broadcasted_iota(jnp.int32, (tb, GATE_PAD), 1)
    gl = jnp.where((lane >= H) & (lane < 2 * H), _log_sigmoid(g), g)
    r = lax.broadcasted_iota(jnp.int32, (tb, tb), 0)
    c = lax.broadcasted_iota(jnp.int32, (tb, tb), 1)
    tri = ((r // L == c // L) & (c <= r)).astype(F32)
    cum = _dot(tri, gl, precision=HIGHEST)

    def rows(ci):
        return slice(ci * L, (ci + 1) * L)

    def stack(f):
        return jnp.stack([f(ci, h) for ci in range(n_chunks) for h in range(H)])

    ones_col = (lax.broadcasted_iota(jnp.int32, (L, M_V_DIM), 1) == 0).astype(BF16)
    li = stack(lambda ci, h: gl[rows(ci), h:h + 1])
    b = stack(lambda ci, h: cum[rows(ci), H + h:H + h + 1])
    qb = stack(lambda ci, h: q_ref[rows(ci), h * M_QK_DIM:(h + 1) * M_QK_DIM]).astype(BF16)
    kc = stack(lambda ci, h: k_ref[rows(ci), h * M_QK_DIM:(h + 1) * M_QK_DIM]) * (M_QK_DIM ** -0.5)
    vaug = stack(lambda ci, h: jnp.concatenate(
        [v_ref[rows(ci), h * M_V_DIM:(h + 1) * M_V_DIM].astype(BF16), ones_col], axis=1))

    rl = lax.broadcasted_iota(jnp.int32, (L, L), 0)
    cl = lax.broadcasted_iota(jnp.int32, (L, L), 1)
    w_row = jnp.sum(jnp.where(rl == cl, li - b, 0.0), axis=1, keepdims=True)
    d = jnp.where(cl <= rl, b + w_row, NEG_INF)
    m_loc = jnp.max(d, axis=2, keepdims=True)
    s = _bdot_nt(qb, kc.astype(BF16)) * jnp.exp(d - m_loc)
    nd_loc = _bdot(s.astype(BF16), vaug)
    g_last = m_loc[:, L - 1:L, :]
    b_last = b[:, L - 1:L, :]
    ws = jnp.exp(b_last - b + li - g_last)
    dc_loc = _bdot_tn((kc * ws).astype(BF16), vaug)

    m_prev = m_scr[:, :, 0:1]
    c_aug = c_scr[...]
    for ci in range(n_chunks):
        grp = slice(ci * H, (ci + 1) * H)
        a = b[grp] + m_prev
        m_t = jnp.maximum(a, m_loc[grp])
        nd = (jnp.exp(a - m_t) * _bdot(qb[grp], c_aug.astype(BF16))
              + jnp.exp(m_loc[grp] - m_t) * nd_loc[grp])
        hh = nd[:, :, :M_V_DIM] / jnp.maximum(jnp.abs(nd[:, :, M_V_DIM:M_V_DIM + 1]), jnp.exp(-m_t))
        hn = _rms(hh, gh_ref[...])
        for h in range(H):
            og = jax.nn.sigmoid(o_ref[rows(ci), h * M_V_DIM:(h + 1) * M_V_DIM])
            out_ref[rows(ci), h * M_V_DIM:(h + 1) * M_V_DIM] = (og * hn[h]).astype(out_ref.dtype)
        m_new = jnp.maximum(b_last[grp] + m_prev, g_last[grp])
        c_aug = (jnp.exp(b_last[grp] + m_prev - m_new) * c_aug
                 + jnp.exp(g_last[grp] - m_new) * dc_loc[grp])
        m_prev = m_new
    c_scr[...] = c_aug
    m_scr[...] = jnp.broadcast_to(m_prev, m_scr.shape)

    @pl.when(t == pl.num_programs(1) - 1)
    def _():
        cfin_ref[0] = c_scr[...]
        mfin_ref[0] = m_scr[...]


def _mlstm(z, gates, gate_bias, g_mhead, c0_aug, m0, batch, seq):
    L = math.gcd(seq, M_CHUNK)
    n_chunks = max(1, min(seq // L, 256 // L))
    tb = L * n_chunks
    nt = seq // tb
    row = lambda b, t: b * nt + t
    kern = functools.partial(_mlstm_kernel, chunk=L, n_chunks=n_chunks)
    return pl.pallas_call(
        kern,
        grid=(batch, nt),
        in_specs=[
            pl.BlockSpec((tb, 512), lambda b, t: (row(b, t), _MQ // 512)),
            pl.BlockSpec((tb, 512), lambda b, t: (row(b, t), _MK // 512)),
            pl.BlockSpec((tb, 1024), lambda b, t: (row(b, t), _MV // 1024)),
            pl.BlockSpec((tb, 1024), lambda b, t: (row(b, t), _MO // 1024)),
            pl.BlockSpec((tb, GATE_PAD), lambda b, t: (row(b, t), 0)),
            pl.BlockSpec((1, GATE_PAD), lambda b, t: (0, 0)),
            pl.BlockSpec((M_HEADS, 1, M_V_DIM), lambda b, t: (0, 0, 0)),
            pl.BlockSpec((1, M_HEADS, M_QK_DIM, 2 * M_V_DIM), lambda b, t: (b, 0, 0, 0)),
            pl.BlockSpec((1, M_HEADS, 1, GATE_PAD), lambda b, t: (b, 0, 0, 0)),
        ],
        out_specs=[
            pl.BlockSpec((tb, M_HEADS * M_V_DIM), lambda b, t: (row(b, t), 0)),
            pl.BlockSpec((1, M_HEADS, M_QK_DIM, 2 * M_V_DIM), lambda b, t: (b, 0, 0, 0)),
            pl.BlockSpec((1, M_HEADS, 1, GATE_PAD), lambda b, t: (b, 0, 0, 0)),
        ],
        out_shape=[
            jax.ShapeDtypeStruct((batch * seq, M_HEADS * M_V_DIM), BF16 if tb % 16 == 0 else F32),
            jax.ShapeDtypeStruct((batch, M_HEADS, M_QK_DIM, 2 * M_V_DIM), F32),
            jax.ShapeDtypeStruct((batch, M_HEADS, 1, GATE_PAD), F32),
        ],
        scratch_shapes=[pltpu.VMEM((M_HEADS, M_QK_DIM, 2 * M_V_DIM), F32),
                        pltpu.VMEM((M_HEADS, 1, GATE_PAD), F32)],
        compiler_params=_cparams("parallel", "arbitrary"),
        name="mlstm",
    )(z, z, z, z, gates, gate_bias, g_mhead.reshape(M_HEADS, 1, M_V_DIM), c0_aug, m0)


def _t5_bucket(rel):
    n = jnp.maximum(rel, 0)
    nf = jnp.maximum(n, 1).astype(F32)
    large = MAX_EXACT + (jnp.log(nf / MAX_EXACT) / math.log(MAX_DISTANCE / MAX_EXACT)
                         * (NUM_BUCKETS - MAX_EXACT)).astype(jnp.int32)
    large = jnp.minimum(large, NUM_BUCKETS - 1)
    return jnp.where(n < MAX_EXACT, n, large)


def _bias_kernel(tab_ref, out_ref):
    h = pl.program_id(0)
    i = lax.broadcasted_iota(jnp.int32, (MOBA_BLOCK, MOBA_BLOCK), 1)
    j = lax.broadcasted_iota(jnp.int32, (MOBA_BLOCK, MOBA_BLOCK), 0)
    for kind, rel in ((0, i - j), (1, MOBA_BLOCK + i - j)):
        bucket = _t5_bucket(rel)
        vals = [tab_ref[h, b] for b in range(NUM_BUCKETS)]
        bit = 1
        while len(vals) > 1:
            odd = (bucket & bit) != 0
            vals = [jnp.where(odd, vals[2 * m + 1], vals[2 * m]) for m in range(len(vals) // 2)]
            bit *= 2
        out_ref[0, kind] = vals[0]
    out_ref[0, 2] = jnp.full((MOBA_BLOCK, MOBA_BLOCK), tab_ref[h, NUM_BUCKETS - 1], F32)


def _bias_tiles(rel_table):
    assert MOBA_BLOCK + 1 >= MAX_DISTANCE and NUM_BUCKETS & (NUM_BUCKETS - 1) == 0
    tab = rel_table.T.astype(F32)
    return pl.pallas_call(
        _bias_kernel,
        grid=(A_HEADS,),
        in_specs=[pl.BlockSpec(memory_space=pltpu.SMEM)],
        out_specs=pl.BlockSpec((1, 3, MOBA_BLOCK, MOBA_BLOCK), lambda h: (h, 0, 0, 0)),
        out_shape=jax.ShapeDtypeStruct((A_HEADS, 3, MOBA_BLOCK, MOBA_BLOCK), F32),
        compiler_params=_cparams("parallel"),
        name="t5_bias_tiles",
    )(tab)


_VT_ROWS = A_HEAD_DIM + 16


def _moba_prompt_kernel(q_ref, k_ref, v_ref, bias_ref, wu_ref, wd_ref, wg_ref, wo_ref,
                        o_ref, wub_ref, wdb_ref, wgb_ref, wob_ref,
                        kb_scr, qt_scr, vt_scr, gt_scr, s_scr, p_scr, *, n_blocks):
    nb = n_blocks
    blk = MOBA_BLOCK
    nbp = gt_scr.shape[0]
    wub_ref[...] = wu_ref[...].astype(BF16)
    wdb_ref[...] = wd_ref[...].astype(BF16)
    wgb_ref[...] = wg_ref[...].astype(BF16)
    wob_ref[...] = wo_ref[...].astype(BF16)

    kb_scr[...] = k_ref[...].astype(BF16)
    for c in range(nb):
        cols = slice(c * blk, (c + 1) * blk)
        qt_scr[:, cols] = (q_ref[cols, :] * (A_HEAD_DIM ** -0.5)).T.astype(BF16)
        vt_scr[0:A_HEAD_DIM, cols] = v_ref[cols, :].T.astype(BF16)
    ones_row = (lax.broadcasted_iota(jnp.int32, (_VT_ROWS - A_HEAD_DIM, nb * blk), 0) == 0).astype(BF16)
    vt_scr[A_HEAD_DIM:, :] = ones_row
    need_gate = nb - 1 > MOBA_TOP_K
    if need_gate:
        means = [jnp.mean(k_ref[n * blk:(n + 1) * blk, :], axis=0, keepdims=True) for n in range(nb)]
        kmean = jnp.concatenate(means + [jnp.zeros((128 - nb, A_HEAD_DIM), F32)], axis=0)
        for c in range(MOBA_TOP_K + 1, nb):
            cols = slice(c * blk, (c + 1) * blk)
            gate = _dot_nt(q_ref[cols, :] * (A_HEAD_DIM ** -0.5), kmean, precision=HIGHEST)
            gt_scr[:, cols] = gate.T[:nbp, :]
    key = lax.broadcasted_iota(jnp.int32, (blk, blk), 0)
    qry = lax.broadcasted_iota(jnp.int32, (blk, blk), 1)
    causal = key <= qry
    blk_id = lax.broadcasted_iota(jnp.int32, (nbp, blk), 0)

    def scores(i):
        cols = slice(i * blk, (i + 1) * blk)
        s_scr[i % 2, 0:(i + 1) * blk, :] = _dot(kb_scr[0:(i + 1) * blk, :], qt_scr[:, cols])

    def softmax(i):
        cols = slice(i * blk, (i + 1) * blk)
        s_buf, p_buf = s_scr.at[i % 2], p_scr.at[i % 2]
        selneg = None
        if i > MOBA_TOP_K:
            g = gt_scr[:, cols]
            rank = jnp.zeros((nbp, blk), F32)
            for j in range(i):
                gj = g[j:j + 1, :]
                rank = rank + ((gj > g) | ((gj == g) & (j < blk_id))).astype(F32)
            selneg = jnp.where(rank < MOBA_TOP_K, 0.0, NEG_INF)
        m = jnp.full((1, blk), NEG_INF, F32)
        for n in range(i + 1):
            rows = slice(n * blk, (n + 1) * blk)
            kind = 0 if n == i else (1 if n == i - 1 else 2)
            st = s_buf[rows, :] + bias_ref[0, kind]
            if n == i:
                st = jnp.where(causal, st, NEG_INF)
            elif selneg is not None:
                st = st + selneg[n:n + 1, :]
            s_buf[rows, :] = st
            m = jnp.maximum(m, jnp.max(st, axis=0, keepdims=True))
        for n in range(i + 1):
            rows = slice(n * blk, (n + 1) * blk)
            p_buf[rows, :] = jnp.exp(s_buf[rows, :] - m).astype(BF16)

    def values(i):
        cols = slice(i * blk, (i + 1) * blk)
        nd = _dot(vt_scr[:, 0:(i + 1) * blk], p_scr[i % 2, 0:(i + 1) * blk, :])
        out = nd[0:A_HEAD_DIM, :] / nd[A_HEAD_DIM:A_HEAD_DIM + 1, :]
        o_ref[cols, :] = out.T.astype(o_ref.dtype)

    scores(0)
    for i in range(nb):
        if i + 1 < nb:
            scores(i + 1)
        if i > 0:
            values(i - 1)
        softmax(i)
    values(nb - 1)


def _moba_prompt(aq, ak, av, bias_tiles, w_up, w_down, w_ple_gate, w_out, batch, seq):
    assert seq % MOBA_BLOCK == 0
    nb = seq // MOBA_BLOCK
    assert nb <= 128
    nbp = -(-nb // 8) * 8
    steps = batch * A_HEADS
    slab = D_FF // steps
    gate_rows = D_MODEL // steps
    assert D_FF % steps == 0 and slab % 128 == 0 and D_MODEL % steps == 0 and gate_rows % 16 == 0
    head = pl.BlockSpec((seq, A_HEAD_DIM), lambda b, h: (b, h))
    up_slab = pl.BlockSpec((D_MODEL, slab), lambda b, h: (0, b * A_HEADS + h))
    down_slab = pl.BlockSpec((slab, D_MODEL), lambda b, h: (b * A_HEADS + h, 0))
    gate_slab = pl.BlockSpec((gate_rows, D_MODEL), lambda b, h: (b * A_HEADS + h, 0))
    kern = functools.partial(_moba_prompt_kernel, n_blocks=nb)
    return pl.pallas_call(
        kern,
        grid=(batch, A_HEADS),
        in_specs=[head, head, head,
                  pl.BlockSpec((1, 3, MOBA_BLOCK, MOBA_BLOCK), lambda b, h: (h, 0, 0, 0)),
                  up_slab, down_slab, gate_slab, gate_slab],
        out_specs=[head, up_slab, down_slab, gate_slab, gate_slab],
        out_shape=[jax.ShapeDtypeStruct((batch * seq, A_HEADS * A_HEAD_DIM), BF16),
                   jax.ShapeDtypeStruct((D_MODEL, D_FF), BF16),
                   jax.ShapeDtypeStruct((D_FF, D_MODEL), BF16),
                   jax.ShapeDtypeStruct((D_MODEL, D_MODEL), BF16),
                   jax.ShapeDtypeStruct((D_MODEL, D_MODEL), BF16)],
        scratch_shapes=[pltpu.VMEM((seq, A_HEAD_DIM), BF16),
                        pltpu.VMEM((A_HEAD_DIM, seq), BF16),
                        pltpu.VMEM((_VT_ROWS, seq), BF16),
                        pltpu.VMEM((nbp, seq), F32),
                        pltpu.VMEM((2, seq, MOBA_BLOCK), F32),
                        pltpu.VMEM((2, seq, MOBA_BLOCK), BF16)],
        compiler_params=_cparams("parallel", "parallel"),
        name="moba_prompt",
    )(aq, ak, av, bias_tiles, w_up, w_down, w_ple_gate, w_out)


PAGES_PER_BLOCK = MOBA_BLOCK // PAGE_SIZE
PAST_BLOCKS_PER_STEP = 8


def _moba_past_kernel(pt_ref, q_ref, *refs, n_pages_step):
    del pt_ref
    k_refs = refs[:n_pages_step]
    v_refs = refs[n_pages_step:2 * n_pages_step]
    bias_ref, o_ref, m_ref, l_ref, km_ref = refs[2 * n_pages_step:]
    hq = q_ref.shape[1]
    nq = hq // A_HEADS
    qb = (q_ref[0] * (A_HEAD_DIM ** -0.5)).reshape(A_HEADS, nq, A_HEAD_DIM).astype(BF16)
    n_blocks = n_pages_step // PAGES_PER_BLOCK
    is_last_step = pl.program_id(1) == pl.num_programs(1) - 1

    def head_rows(ref, h):
        return ref[0, 0, pl.ds(h, PAGE_SIZE, stride=A_HEADS), :]

    for g in range(n_blocks):
        pages = range(g * PAGES_PER_BLOCK, (g + 1) * PAGES_PER_BLOCK)
        ksum = sum(jnp.sum(k_refs[p][0, 0].reshape(PAGE_SIZE, A_HEADS, A_HEAD_DIM), axis=0) for p in pages)
        km_ref[0, g] = ksum / MOBA_BLOCK
        kind = jnp.where(is_last_step, 1, 0) if g == n_blocks - 1 else 0
        kh = jnp.stack([jnp.concatenate([head_rows(k_refs[p], h) for p in pages], axis=0)
                        for h in range(A_HEADS)]).astype(BF16)
        vh = jnp.stack([jnp.concatenate([head_rows(v_refs[p], h) for p in pages], axis=0)
                        for h in range(A_HEADS)]).astype(BF16)
        lt = _bdot_nt(qb, kh) + bias_ref[kind]
        m = jnp.max(lt, axis=2, keepdims=True)
        p_ = jnp.exp(lt - m)
        l = jnp.sum(p_, axis=2, keepdims=True)
        o_ref[0, g] = _bdot(p_.astype(BF16), vh).reshape(hq, A_HEAD_DIM)
        m_ref[0, g] = jnp.broadcast_to(m, (A_HEADS, nq, A_HEAD_DIM)).reshape(hq, A_HEAD_DIM)
        l_ref[0, g] = jnp.broadcast_to(l, (A_HEADS, nq, A_HEAD_DIM)).reshape(hq, A_HEAD_DIM)


def _moba_past(page_table, q_hq, cache_k, cache_v, past_bias, layer):
    bs, n_pages = page_table.shape
    assert n_pages % PAGES_PER_BLOCK == 0
    nbp = n_pages // PAGES_PER_BLOCK
    gb = math.gcd(nbp, PAST_BLOCKS_PER_STEP)
    pps = gb * PAGES_PER_BLOCK
    hq = q_hq.shape[1]
    page = (1, 1, PAGE_SIZE * A_HEADS, A_HEAD_DIM)
    cache_k, cache_v = (c.reshape(c.shape[:2] + page[2:]) for c in (cache_k, cache_v))
    part = pl.BlockSpec((1, gb, hq, A_HEAD_DIM), lambda b, n, pt: (b, n, 0, 0))

    def page_spec(p):
        return pl.BlockSpec(page, lambda b, n, pt: (layer, pt[b, pps * n + p], 0, 0))

    grid_spec = pltpu.PrefetchScalarGridSpec(
        num_scalar_prefetch=1,
        grid=(bs, nbp // gb),
        in_specs=[pl.BlockSpec((1, hq, A_HEAD_DIM), lambda b, n, pt: (b, 0, 0))]
        + [page_spec(p) for p in range(pps)] * 2
        + [pl.BlockSpec((2, A_HEADS, hq // A_HEADS, MOBA_BLOCK), lambda b, n, pt: (0, 0, 0, 0))],
        out_specs=[part, part, part,
                   pl.BlockSpec((1, gb, A_HEADS, A_HEAD_DIM), lambda b, n, pt: (b, n, 0, 0))],
    )
    pshape = jax.ShapeDtypeStruct((bs, nbp, hq, A_HEAD_DIM), F32)
    return pl.pallas_call(
        functools.partial(_moba_past_kernel, n_pages_step=pps),
        grid_spec=grid_spec,
        out_shape=[pshape, pshape, pshape, jax.ShapeDtypeStruct((bs, nbp, A_HEADS, A_HEAD_DIM), F32)],
        compiler_params=_cparams("parallel", "parallel"),
        name="moba_past_blocks",
    )(page_table, q_hq, *([cache_k] * pps), *([cache_v] * pps), past_bias)


def _moba_combine_kernel(q_ref, km_ref, kn_ref, vn_ref, ob_ref, op_ref, mp_ref, lp_ref, out_ref):
    nbp = km_ref.shape[2]
    q = q_ref[0] * (A_HEAD_DIM ** -0.5)
    hq = q.shape[0]
    nq = hq // A_HEADS
    gate = jnp.concatenate(
        [_dot_nt(q[h * nq:(h + 1) * nq], km_ref[0, h], precision=HIGHEST) for h in range(A_HEADS)], axis=0)
    lane = lax.broadcasted_iota(jnp.int32, (hq, nbp), 1)
    rank = jnp.zeros((hq, nbp), F32)
    for j in range(nbp):
        gj = gate[:, j:j + 1]
        beats = (gj > gate) | ((gj == gate) & (j < lane))
        rank = rank + beats.astype(F32)
    selneg = jnp.where(rank < MOBA_TOP_K, 0.0, NEG_INF)
    qb = q.astype(BF16)
    lt = _dot_nt(qb, kn_ref[0].astype(BF16))
    r = lax.broadcasted_iota(jnp.int32, lt.shape, 0)
    c = lax.broadcasted_iota(jnp.int32, lt.shape, 1)
    ok = ((c % A_HEADS) == (r // nq)) & ((c // A_HEADS) <= (r % nq))
    lo = jnp.where(ok, lt + ob_ref[...], NEG_INF)
    m = jnp.max(lo, axis=1, keepdims=True)
    for n in range(nbp):
        m = jnp.maximum(m, mp_ref[0, n][:, 0:1] + selneg[:, n:n + 1])
    p = jnp.exp(lo - m)
    l = jnp.sum(p, axis=1, keepdims=True)
    acc = _dot(p.astype(BF16), vn_ref[0].astype(BF16))
    for n in range(nbp):
        w = jnp.exp(mp_ref[0, n][:, 0:1] + selneg[:, n:n + 1] - m)
        l = l + w * lp_ref[0, n][:, 0:1]
        acc = acc + w * op_ref[0, n]
    out_ref[0] = acc / l


def _moba_combine(q_hq, kmean_hn, k_new, v_new, own_bias, o_part, m_part, l_part):
    bs, hq, _ = q_hq.shape
    nbp = o_part.shape[1]
    part = pl.BlockSpec((1, nbp, hq, A_HEAD_DIM), lambda b: (b, 0, 0, 0))
    tok = pl.BlockSpec((1, hq, A_HEAD_DIM), lambda b: (b, 0, 0))
    return pl.pallas_call(
        _moba_combine_kernel,
        grid=(bs,),
        in_specs=[tok,
                  pl.BlockSpec((1, A_HEADS, nbp, A_HEAD_DIM), lambda b: (b, 0, 0, 0)),
                  tok, tok,
                  pl.BlockSpec((hq, hq), lambda b: (0, 0)),
                  part, part, part],
        out_specs=tok,
        out_shape=jax.ShapeDtypeStruct((bs, hq, A_HEAD_DIM), F32),
        compiler_params=_cparams("parallel"),
        name="moba_combine",
    )(q_hq, kmean_hn, k_new, v_new, own_bias, o_part, m_part, l_part)


def _out_proj_kernel(h_ref, om_ref, oa_ref, hs_ref, oms_ref, oas_ref, g_ref, w_ref, o_ref, xn_ref, os_ref, xns_ref):
    half = om_ref.shape[1]

    def project(h, om, oa, dst, xn_dst):
        r = (h[...] + _dot(om[...].astype(BF16), w_ref[0:half, :])
             + _dot(oa[...].astype(BF16), w_ref[half:, :]))
        dst[...] = r
        xn_dst[...] = _rms(r, g_ref[...]).astype(BF16)

    project(h_ref, om_ref, oa_ref, o_ref, xn_ref)

    @pl.when(pl.program_id(0) == 0)
    def _():
        project(hs_ref, oms_ref, oas_ref, os_ref, xns_ref)


def _out_proj(h, out_m, out_a, hs, out_ms, out_as, g_ffn, w_out_bf16):
    m, ms = h.shape[0], hs.shape[0]
    tm = min(512, m)
    assert m % tm == 0
    half = M_HEADS * M_V_DIM
    rows = pl.BlockSpec((tm, D_MODEL), lambda i: (i, 0))
    mix = pl.BlockSpec((tm, half), lambda i: (i, 0))
    rows_s = pl.BlockSpec((ms, D_MODEL), lambda i: (0, 0))
    mix_s = pl.BlockSpec((ms, half), lambda i: (0, 0))
    return pl.pallas_call(
        _out_proj_kernel,
        grid=(m // tm,),
        in_specs=[
            rows, mix, mix, rows_s, mix_s, mix_s,
            pl.BlockSpec((1, D_MODEL), lambda i: (0, 0)),
            pl.BlockSpec((D_MODEL, D_MODEL), lambda i: (0, 0), pipeline_mode=pl.Buffered(1)),
        ],
        out_specs=[rows, rows, rows_s, rows_s],
        out_shape=[jax.ShapeDtypeStruct((m, D_MODEL), F32), jax.ShapeDtypeStruct((m, D_MODEL), BF16),
                   jax.ShapeDtypeStruct((ms, D_MODEL), F32), jax.ShapeDtypeStruct((ms, D_MODEL), BF16)],
        compiler_params=_cparams("arbitrary"),
        name="out_proj",
    )(h, out_m, out_a, hs, out_ms, out_as, g_ffn, w_out_bf16)


def _mlp_kernel(xn_ref, xns_ref, wu_ref, wd_ref, o_ref, os_ref):
    f = pl.program_id(1)

    def accumulate(xn, dst):
        @pl.when(f == 0)
        def _():
            dst[...] = jnp.zeros(dst.shape, F32)

        u = _dot(xn[...], wu_ref[...])
        a = jnp.square(jnp.maximum(u, 0.0)).astype(BF16)
        dst[...] += _dot(a, wd_ref[...])

    accumulate(xn_ref, o_ref)

    @pl.when(pl.program_id(0) == 0)
    def _():
        accumulate(xns_ref, os_ref)


def _mlp(xn, xns, w_up, w_down):
    m, ms = xn.shape[0], xns.shape[0]
    tm = 1024 if m % 1024 == 0 else min(512, m)
    assert m % tm == 0
    tf = 1024
    return pl.pallas_call(
        _mlp_kernel,
        grid=(m // tm, D_FF // tf),
        in_specs=[
            pl.BlockSpec((tm, D_MODEL), lambda i, f: (i, 0)),
            pl.BlockSpec((ms, D_MODEL), lambda i, f: (0, 0)),
            pl.BlockSpec((D_MODEL, tf), lambda i, f: (0, f)),
            pl.BlockSpec((tf, D_MODEL), lambda i, f: (f, 0)),
        ],
        out_specs=[pl.BlockSpec((tm, D_MODEL), lambda i, f: (i, 0)),
                   pl.BlockSpec((ms, D_MODEL), lambda i, f: (0, 0))],
        out_shape=[jax.ShapeDtypeStruct((m, D_MODEL), F32), jax.ShapeDtypeStruct((ms, D_MODEL), F32)],
        compiler_params=_cparams("arbitrary", "arbitrary"),
        name="mlp",
    )(xn, xns, w_up, w_down)


def _ple_kernel(h_ref, d_ref, pe_ref, hs_ref, ds_ref, pes_ref, g_ref, wg_ref, wp_ref, gf_ref, y_ref, ys_ref):
    def gated_embedding(src, delta, pe, dst):
        h = src[...] + delta[...]
        gate = jax.nn.sigmoid(_dot(_rms(h, g_ref[...]).astype(BF16), wg_ref[...]))
        proj = _dot(pe[...].astype(BF16), wp_ref[...])
        dst[...] = _rms(h + gate * proj, gf_ref[...])

    gated_embedding(h_ref, d_ref, pe_ref, y_ref)

    @pl.when(pl.program_id(0) == 0)
    def _():
        gated_embedding(hs_ref, ds_ref, pes_ref, ys_ref)


def _ple_final(h, delta, pe, hs, deltas, pes, g_ple, w_gate, w_proj, g_final):
    m, ms = h.shape[0], hs.shape[0]
    tm = min(512, m)
    assert m % tm == 0
    vec = pl.BlockSpec((1, D_MODEL), lambda i: (0, 0))
    rows = pl.BlockSpec((tm, D_MODEL), lambda i: (i, 0))
    rows_s = pl.BlockSpec((ms, D_MODEL), lambda i: (0, 0))
    return pl.pallas_call(
        _ple_kernel,
        grid=(m // tm,),
        in_specs=[
            rows, rows,
            pl.BlockSpec((tm, PLE_DIM), lambda i: (i, 0)),
            rows_s, rows_s,
            pl.BlockSpec((ms, PLE_DIM), lambda i: (0, 0)),
            vec,
            pl.BlockSpec((D_MODEL, D_MODEL), lambda i: (0, 0), pipeline_mode=pl.Buffered(1)),
            pl.BlockSpec((PLE_DIM, D_MODEL), lambda i: (0, 0), pipeline_mode=pl.Buffered(1)),
            vec,
        ],
        out_specs=[rows, rows_s],
        out_shape=[jax.ShapeDtypeStruct((m, D_MODEL), F32), jax.ShapeDtypeStruct((ms, D_MODEL), F32)],
        compiler_params=_cparams("arbitrary"),
        name="ple_final_norm",
    )(h, delta, pe, hs, deltas, pes, g_ple, w_gate, w_proj, g_final)


def _layer_weights(w_in, b_igate, b_fgate, g_mix, g_mhead, w_out, g_ffn, w_up, w_down,
                   g_ple, w_ple_gate, w_ple_proj):
    gate_bias = jnp.pad(jnp.concatenate([b_igate, b_fgate]), (0, GATE_PAD - 2 * M_HEADS)).reshape(1, GATE_PAD)
    return dict(
        w_in=w_in.T, gate_bias=gate_bias.astype(F32),
        g_mix=g_mix.reshape(1, D_MODEL), g_mhead=g_mhead, w_out=w_out,
        g_ffn=g_ffn.reshape(1, D_MODEL), w_up=w_up, w_down=w_down,
        g_ple=g_ple.reshape(1, D_MODEL), w_ple_gate=w_ple_gate,
        w_ple_proj=w_ple_proj.astype(BF16))


def _tail(h, out_m, out_a, pe, hs, out_ms, out_as, pes, w, g_final):
    h, xn, hs, xns = _out_proj(h, out_m, out_a, hs, out_ms, out_as, w["g_ffn"], w["w_out_bf16"])
    delta, deltas = _mlp(xn, xns, w["w_up_bf16"], w["w_down_bf16"])
    return _ple_final(h, delta, pe, hs, deltas, pes, w["g_ple"], w["w_ple_gate_bf16"], w["w_ple_proj"], g_final)


def _split_state(c_aug, m_fin):
    return c_aug[..., :M_V_DIM], c_aug[..., M_V_DIM], m_fin[..., 0, 0]


def kernel(x_prompt, x_sample, cache_k, cache_v, state_C, state_n, state_m, page_table, p_prompt, p_sample, rel_bias_table, w_in, b_igate, b_fgate, g_mix, g_mhead, w_out, g_ffn, w_up, w_down, g_ple, w_ple_gate, w_ple_proj, g_final):
    depth = w_in.shape[0]
    assert depth == 1, "one decoder layer per call"
    layer = 0
    bp, tp, _ = x_prompt.shape
    bs, ts, _ = x_sample.shape
    past_len = page_table.shape[1] * PAGE_SIZE
    assert past_len % MOBA_BLOCK == 0 and ts <= MOBA_BLOCK
    kv_shape = (A_HEADS, A_HEAD_DIM)
    g_fin = g_final.reshape(1, D_MODEL)

    w = _layer_weights(w_in[layer], b_igate[layer], b_fgate[layer], g_mix[layer], g_mhead[layer], w_out[layer],
                       g_ffn[layer], w_up[layer], w_down[layer], g_ple[layer], w_ple_gate[layer],
                       w_ple_proj[layer])
    bias_tiles = _bias_tiles(rel_bias_table)

    hp = x_prompt.reshape(bp * tp, D_MODEL)
    hs = x_sample.reshape(bs * ts, D_MODEL)
    xn_p, gp, xn_s, gs = _norm_gate(hp, hs, w["g_mix"], w["w_in"])
    zp, aq_p, ak_p, av_p, zs = _in_proj(xn_p, xn_s, w["w_in"])
    aq_s, ak_s, av_s = (zs[:, _GATES + t * _IN_TN:_GATES + (t + 1) * _IN_TN] for t in range(3))
    c0 = jnp.zeros((bp, M_HEADS, M_QK_DIM, 2 * M_V_DIM), F32)
    m0 = jnp.zeros((bp, M_HEADS, 1, GATE_PAD), F32)
    om_p, c_p, m_p = _mlstm(zp, gp, w["gate_bias"], w["g_mhead"], c0, m0, bp, tp)
    oa_p, w["w_up_bf16"], w["w_down_bf16"], w["w_ple_gate_bf16"], w["w_out_bf16"] = _moba_prompt(
        aq_p, ak_p, av_p, bias_tiles, w["w_up"], w["w_down"], w["w_ple_gate"], w["w_out"], bp, tp)
    k_p = ak_p.reshape((1, bp, tp) + kv_shape)
    v_p = av_p.reshape((1, bp, tp) + kv_shape)
    cp, np_, mp = _split_state(c_p, m_p)

    c0s = jnp.concatenate([state_C[layer], state_n[layer][..., None],
                           jnp.zeros((bs, M_HEADS, M_QK_DIM, M_V_DIM - 1), F32)], axis=-1)
    m0s = jnp.broadcast_to(state_m[layer][..., None, None], (bs, M_HEADS, 1, GATE_PAD))
    om_s, c_s, m_s = _mlstm(zs, gs, w["gate_bias"], w["g_mhead"], c0s, m0s, bs, ts)

    hq = A_HEADS * ts
    q_s = aq_s.reshape(bs, ts, A_HEADS, A_HEAD_DIM)
    q_hq = q_s.transpose(0, 2, 1, 3).reshape(bs, hq, A_HEAD_DIM)
    k_new = ak_s.reshape(bs, hq, A_HEAD_DIM)
    v_new = av_s.reshape(bs, hq, A_HEAD_DIM)
    rows = bias_tiles[:, :, :, :ts].transpose(0, 1, 3, 2)
    expand = lambda a: jnp.repeat(a, A_HEADS, axis=-1).reshape(hq, -1)
    past_bias = jnp.stack([rows[:, 2], rows[:, 1]])
    own_bias = expand(rows[:, 0, :, :ts])
    o_part, m_part, l_part, kmean = _moba_past(page_table, q_hq, cache_k, cache_v, past_bias, layer)
    oa_hq = _moba_combine(q_hq, kmean.transpose(0, 2, 1, 3), k_new, v_new, own_bias, o_part, m_part, l_part)
    oa_s = oa_hq.reshape(bs, A_HEADS, ts, A_HEAD_DIM).transpose(0, 2, 1, 3).reshape(bs * ts, A_HEADS * A_HEAD_DIM)
    y_p, y_s = _tail(hp, om_p, oa_p, p_prompt[layer].reshape(bp * tp, PLE_DIM),
                     hs, om_s, oa_s, p_sample[layer].reshape(bs * ts, PLE_DIM), w, g_fin)
    k_s = k_new.reshape((1, bs, ts) + kv_shape)
    v_s = v_new.reshape((1, bs, ts) + kv_shape)
    cs, ns, ms = _split_state(c_s, m_s)

    return (y_p.reshape(bp, tp, D_MODEL), y_s.reshape(bs, ts, D_MODEL),
            k_p, v_p, cp[None], np_[None], mp[None],
            k_s, v_s, cs[None], ns[None], ms[None])
```

```python
import functools
import math

import jax
import jax.numpy as jnp
from jax import lax
from jax.experimental import pallas as pl
from jax.experimental.pallas import tpu as pltpu

F32 = jnp.float32
BF16 = jnp.bfloat16
HIGHEST = lax.Precision.HIGHEST

D_MODEL = 2048
M_HEADS = 8
M_V_DIM = 128
M_QK_DIM = 64
M_CHUNK = 64
A_HEADS = 8
A_HEAD_DIM = 128
MOBA_BLOCK = 256
MOBA_TOP_K = 3
NUM_BUCKETS = 32
MAX_EXACT = NUM_BUCKETS // 2
MAX_DISTANCE = 128
D_FF = 4 * D_MODEL
PLE_DIM = 256
PAGE_SIZE = 128
EPS = 1e-6

_MQ, _MK, _MV, _MO = 0, 512, 1024, 2048
_GATES = 3072
_AQ = 3088
D_MAIN = 6144
GATE_PAD = 128

VMEM_LIMIT = 56 * 1024 * 1024
NEG_INF = float("-inf")


def _cparams(*sem):
    return pltpu.CompilerParams(dimension_semantics=sem, vmem_limit_bytes=VMEM_LIMIT)


def _rms(x, g):
    return x * lax.rsqrt(jnp.mean(x * x, axis=-1, keepdims=True) + EPS) * g


def _dot_nt(a, b, **kw):
    return lax.dot_general(a, b, (((1,), (1,)), ((), ())), preferred_element_type=F32, **kw)


def _dot_tn(a, b, **kw):
    return lax.dot_general(a, b, (((0,), (0,)), ((), ())), preferred_element_type=F32, **kw)


def _dot(a, b, **kw):
    return jnp.dot(a, b, preferred_element_type=F32, **kw)


_IN_TN = 1024
_N_IN_TILES = D_MAIN // _IN_TN
_N_MLSTM_TILES = _GATES // _IN_TN
_GATE_COLS = _AQ - _GATES
_CAST_ROWS = 256


def _norm_gate_kernel(x_ref, xs_ref, g_ref, wg_ref, xn_ref, gate_ref, xns_ref, gates_ref):
    wg = wg_ref[...]
    hi = wg.astype(BF16).astype(F32)
    row = lax.broadcasted_iota(jnp.int32, wg.shape, 0)
    w2 = jnp.where(row < _GATE_COLS, hi,
                   jnp.where(row < 2 * _GATE_COLS, pltpu.roll(wg - hi, _GATE_COLS, axis=0), 0.0))
    w2 = w2.astype(BF16)

    def norm_and_gate(src_ref, dst_ref, gate_dst_ref):
        xn = _rms(src_ref[...], g_ref[...]).astype(BF16)
        dst_ref[...] = xn
        r = _dot_nt(xn, w2)
        gate_dst_ref[...] = r + pltpu.roll(r, GATE_PAD - _GATE_COLS, axis=1)

    norm_and_gate(x_ref, xn_ref, gate_ref)

    @pl.when(pl.program_id(0) == 0)
    def _():
        norm_and_gate(xs_ref, xns_ref, gates_ref)


def _norm_gate(x, xs, g, w_in_t):
    m, ms = x.shape[0], xs.shape[0]
    tm = 1024 if m % 1024 == 0 else min(512, m)
    assert m % tm == 0
    return pl.pallas_call(
        _norm_gate_kernel,
        grid=(m // tm,),
        in_specs=[
            pl.BlockSpec((tm, D_MODEL), lambda i: (i, 0)),
            pl.BlockSpec((ms, D_MODEL), lambda i: (0, 0)),
            pl.BlockSpec((1, D_MODEL), lambda i: (0, 0)),
            pl.BlockSpec((GATE_PAD, D_MODEL), lambda i: (_GATES // GATE_PAD, 0)),
        ],
        out_specs=[pl.BlockSpec((tm, D_MODEL), lambda i: (i, 0)),
                   pl.BlockSpec((tm, GATE_PAD), lambda i: (i, 0)),
                   pl.BlockSpec((ms, D_MODEL), lambda i: (0, 0)),
                   pl.BlockSpec((ms, GATE_PAD), lambda i: (0, 0))],
        out_shape=[jax.ShapeDtypeStruct((m, D_MODEL), BF16), jax.ShapeDtypeStruct((m, GATE_PAD), F32),
                   jax.ShapeDtypeStruct((ms, D_MODEL), BF16), jax.ShapeDtypeStruct((ms, GATE_PAD), F32)],
        compiler_params=_cparams("arbitrary"),
        name="norm_gate",
    )(x, xs, g, w_in_t)


def _cast_weight_tile(wa_ref, wb_ref, w_scr, shifted):
    skip = _GATE_COLS if shifted else 0
    body = _IN_TN - skip
    for r in range(0, body, _CAST_ROWS):
        n = min(_CAST_ROWS, body - r)
        w_scr[r:r + n, :] = wa_ref[skip + r:skip + r + n, :].astype(BF16)
    if shifted:
        w_scr[body:, :] = wb_ref[:_GATE_COLS, :].astype(BF16)


def _in_proj_mlstm_kernel(xn_ref, xs_ref, wa_ref, z_ref, zs_ref, w_scr):
    @pl.when(pl.program_id(1) == 0)
    def _():
        _cast_weight_tile(wa_ref, None, w_scr, shifted=False)
        zs_ref[...] = _dot_nt(xs_ref[...], w_scr[...])

    z_ref[...] = _dot_nt(xn_ref[...], w_scr[...])


def _in_proj_moba_kernel(xn_ref, xs_ref, wa_ref, wb_ref, aq_ref, ak_ref, av_ref, zs_ref, w_scr):
    j = pl.program_id(0)

    @pl.when(pl.program_id(1) == 0)
    def _():
        _cast_weight_tile(wa_ref, wb_ref, w_scr, shifted=True)
        zs_ref[...] = _dot_nt(xs_ref[...], w_scr[...])

    for t, ref in enumerate((aq_ref, ak_ref, av_ref)):
        @pl.when(j == t)
        def _(ref=ref):
            ref[...] = _dot_nt(xn_ref[...], w_scr[...])


def _in_proj(xn, xs, w_in_t):
    m, ms = xn.shape[0], xs.shape[0]
    tn = _IN_TN
    nm = _N_MLSTM_TILES
    scratch = [pltpu.VMEM((tn, D_MODEL), BF16)]
    sample_rows = pl.BlockSpec((ms, D_MODEL), lambda j, i: (0, 0))
    sample_out = pl.BlockSpec((ms, tn), lambda j, i: (0, j))

    tm = 1024 if m % 1024 == 0 else min(512, m)
    assert m % tm == 0
    zm, zs_m = pl.pallas_call(
        _in_proj_mlstm_kernel,
        grid=(nm, m // tm),
        in_specs=[
            pl.BlockSpec((tm, D_MODEL), lambda j, i: (i, 0)),
            sample_rows,
            pl.BlockSpec((tn, D_MODEL), lambda j, i: (j, 0)),
        ],
        out_specs=[pl.BlockSpec((tm, tn), lambda j, i: (i, j)), sample_out],
        out_shape=[jax.ShapeDtypeStruct((m, _GATES), F32), jax.ShapeDtypeStruct((ms, _GATES), F32)],
        scratch_shapes=scratch,
        compiler_params=_cparams("arbitrary", "arbitrary"),
        name="in_proj_mlstm",
    )(xn, xs, w_in_t)

    tm = min(512, m)
    assert m % tm == 0
    last = m // tm - 1

    def section(t):
        return pl.BlockSpec((tm, tn), lambda j, i: (jnp.where(j < t, 0, jnp.where(j == t, i, last)), 0))

    sec_shape = jax.ShapeDtypeStruct((m, tn), F32)
    aq, ak, av, zs_a = pl.pallas_call(
        _in_proj_moba_kernel,
        grid=(_N_IN_TILES - nm, m // tm),
        in_specs=[
            pl.BlockSpec((tm, D_MODEL), lambda j, i: (i, 0)),
            sample_rows,
            pl.BlockSpec((tn, D_MODEL), lambda j, i: (nm + j, 0)),
            pl.BlockSpec((GATE_PAD, D_MODEL), lambda j, i: ((nm + j + 1) * (tn // GATE_PAD), 0)),
        ],
        out_specs=[section(0), section(1), section(2), sample_out],
        out_shape=[sec_shape, sec_shape, sec_shape, jax.ShapeDtypeStruct((ms, D_MAIN - _GATES), F32)],
        scratch_shapes=scratch,
        compiler_params=_cparams("arbitrary", "arbitrary"),
        name="in_proj_moba",
    )(xn, xs, w_in_t, w_in_t)
    return zm, aq, ak, av, zs_m, zs_a


def _log_sigmoid(x):
    return -(jnp.maximum(-x, 0.0) + jnp.log1p(jnp.exp(-jnp.abs(x))))


def _bdot(a, b):
    return lax.dot_general(a, b, (((2,), (1,)), ((0,), (0,))), preferred_element_type=F32)


def _bdot_nt(a, b):
    return lax.dot_general(a, b, (((2,), (2,)), ((0,), (0,))), preferred_element_type=F32)


def _bdot_tn(a, b):
    return lax.dot_general(a, b, (((1,), (1,)), ((0,), (0,))), preferred_element_type=F32)


def _mlstm_kernel(q_ref, k_ref, v_ref, o_ref, g_ref, gb_ref, gh_ref, c0_ref, m0_ref,
                  out_ref, cfin_ref, mfin_ref, c_scr, m_scr, *, chunk, n_chunks):
    L = chunk
    H = M_HEADS
    tb = L * n_chunks
    t = pl.program_id(1)

    @pl.when(t == 0)
    def _():
        c_scr[...] = c0_ref[0]
        m_scr[...] = m0_ref[0]

    g = g_ref[...] + gb_ref[...]
    lane = lax.broadcasted_iota(jnp.int32, (tb, GATE_PAD), 1)
    gl = jnp.where((lane >= H) & (lane < 2 * H), _log_sigmoid(g), g)
    r = lax.broadcasted_iota(jnp.int32, (tb, tb), 0)
    c = lax.broadcasted_iota(jnp.int32, (tb, tb), 1)
    tri = ((r // L == c // L) & (c <= r)).astype(F32)
    cum = _dot(tri, gl, precision=HIGHEST)

    def rows(ci):
        return slice(ci * L, (ci + 1) * L)

    def stack(f):
        return jnp.stack([f(ci, h) for ci in range(n_chunks) for h in range(H)])

    ones_col = (lax.broadcasted_iota(jnp.int32, (L, M_V_DIM), 1) == 0).astype(BF16)
    li = stack(lambda ci, h: gl[rows(ci), h:h + 1])
    b = stack(lambda ci, h: cum[rows(ci), H + h:H + h + 1])
    qb = stack(lambda ci, h: q_ref[rows(ci), h * M_QK_DIM:(h + 1) * M_QK_DIM]).astype(BF16)
    kc = stack(lambda ci, h: k_ref[rows(ci), h * M_QK_DIM:(h + 1) * M_QK_DIM]) * (M_QK_DIM ** -0.5)
    vaug = stack(lambda ci, h: jnp.concatenate(
        [v_ref[rows(ci), h * M_V_DIM:(h + 1) * M_V_DIM].astype(BF16), ones_col], axis=1))

    rl = lax.broadcasted_iota(jnp.int32, (L, L), 0)
    cl = lax.broadcasted_iota(jnp.int32, (L, L), 1)
    w_row = jnp.sum(jnp.where(rl == cl, li - b, 0.0), axis=1, keepdims=True)
    d = jnp.where(cl <= rl, b + w_row, NEG_INF)
    m_loc = jnp.max(d, axis=2, keepdims=True)
    s = _bdot_nt(qb, kc.astype(BF16)) * jnp.exp(d - m_loc)
    nd_loc = _bdot(s.astype(BF16), vaug)
    g_last = m_loc[:, L - 1:L, :]
    b_last = b[:, L - 1:L, :]
    ws = jnp.exp(b_last - b + li - g_last)
    dc_loc = _bdot_tn((kc * ws).astype(BF16), vaug)

    m_prev = m_scr[:, :, 0:1]
    c_aug = c_scr[...]
    for ci in range(n_chunks):
        grp = slice(ci * H, (ci + 1) * H)
        a = b[grp] + m_prev
        m_t = jnp.maximum(a, m_loc[grp])
        nd = (jnp.exp(a - m_t) * _bdot(qb[grp], c_aug.astype(BF16))
              + jnp.exp(m_loc[grp] - m_t) * nd_loc[grp])
        hh = nd[:, :, :M_V_DIM] / jnp.maximum(jnp.abs(nd[:, :, M_V_DIM:M_V_DIM + 1]), jnp.exp(-m_t))
        hn = _rms(hh, gh_ref[...])
        for h in range(H):
            og = jax.nn.sigmoid(o_ref[rows(ci), h * M_V_DIM:(h + 1) * M_V_DIM])
            out_ref[rows(ci), h * M_V_DIM:(h + 1) * M_V_DIM] = (og * hn[h]).astype(out_ref.dtype)
        m_new = jnp.maximum(b_last[grp] + m_prev, g_last[grp])
        c_aug = (jnp.exp(b_last[grp] + m_prev - m_new) * c_aug
                 + jnp.exp(g_last[grp] - m_new) * dc_loc[grp])
        m_prev = m_new
    c_scr[...] = c_aug
    m_scr[...] = jnp.broadcast_to(m_prev, m_scr.shape)

    @pl.when(t == pl.num_programs(1) - 1)
    def _():
        cfin_ref[0] = c_scr[...]
        mfin_ref[0] = m_scr[...]


def _mlstm(z, gates, gate_bias, g_mhead, c0_aug, m0, batch, seq):
    L = math.gcd(seq, M_CHUNK)
    n_chunks = max(1, min(seq // L, 256 // L))
    tb = L * n_chunks
    nt = seq // tb
    row = lambda b, t: b * nt + t
    kern = functools.partial(_mlstm_kernel, chunk=L, n_chunks=n_chunks)
    return pl.pallas_call(
        kern,
        grid=(batch, nt),
        in_specs=[
            pl.BlockSpec((tb, 512), lambda b, t: (row(b, t), _MQ // 512)),
            pl.BlockSpec((tb, 512), lambda b, t: (row(b, t), _MK // 512)),
            pl.BlockSpec((tb, 1024), lambda b, t: (row(b, t), _MV // 1024)),
            pl.BlockSpec((tb, 1024), lambda b, t: (row(b, t), _MO // 1024)),
            pl.BlockSpec((tb, GATE_PAD), lambda b, t: (row(b, t), 0)),
            pl.BlockSpec((1, GATE_PAD), lambda b, t: (0, 0)),
            pl.BlockSpec((M_HEADS, 1, M_V_DIM), lambda b, t: (0, 0, 0)),
            pl.BlockSpec((1, M_HEADS, M_QK_DIM, 2 * M_V_DIM), lambda b, t: (b, 0, 0, 0)),
            pl.BlockSpec((1, M_HEADS, 1, GATE_PAD), lambda b, t: (b, 0, 0, 0)),
        ],
        out_specs=[
            pl.BlockSpec((tb, M_HEADS * M_V_DIM), lambda b, t: (row(b, t), 0)),
            pl.BlockSpec((1, M_HEADS, M_QK_DIM, 2 * M_V_DIM), lambda b, t: (b, 0, 0, 0)),
            pl.BlockSpec((1, M_HEADS, 1, GATE_PAD), lambda b, t: (b, 0, 0, 0)),
        ],
        out_shape=[
            jax.ShapeDtypeStruct((batch * seq, M_HEADS * M_V_DIM), BF16 if tb % 16 == 0 else F32),
            jax.ShapeDtypeStruct((batch, M_HEADS, M_QK_DIM, 2 * M_V_DIM), F32),
            jax.ShapeDtypeStruct((batch, M_HEADS, 1, GATE_PAD), F32),
        ],
        scratch_shapes=[pltpu.VMEM((M_HEADS, M_QK_DIM, 2 * M_V_DIM), F32),
                        pltpu.VMEM((M_HEADS, 1, GATE_PAD), F32)],
        compiler_params=_cparams("parallel", "arbitrary"),
        name="mlstm",
    )(z, z, z, z, gates, gate_bias, g_mhead.reshape(M_HEADS, 1, M_V_DIM), c0_aug, m0)


def _t5_bucket(rel):
    n = jnp.maximum(rel, 0)
    nf = jnp.maximum(n, 1).astype(F32)
    large = MAX_EXACT + (jnp.log(nf / MAX_EXACT) / math.log(MAX_DISTANCE / MAX_EXACT)
                         * (NUM_BUCKETS - MAX_EXACT)).astype(jnp.int32)
    large = jnp.minimum(large, NUM_BUCKETS - 1)
    return jnp.where(n < MAX_EXACT, n, large)


def _bias_kernel(tab_ref, out_ref):
    h = pl.program_id(0)
    i = lax.broadcasted_iota(jnp.int32, (MOBA_BLOCK, MOBA_BLOCK), 1)
    j = lax.broadcasted_iota(jnp.int32, (MOBA_BLOCK, MOBA_BLOCK), 0)
    for kind, rel in ((0, i - j), (1, MOBA_BLOCK + i - j)):
        bucket = _t5_bucket(rel)
        vals = [tab_ref[h, b] for b in range(NUM_BUCKETS)]
        bit = 1
        while len(vals) > 1:
            odd = (bucket & bit) != 0
            vals = [jnp.where(odd, vals[2 * m + 1], vals[2 * m]) for m in range(len(vals) // 2)]
            bit *= 2
        out_ref[0, kind] = vals[0]
    out_ref[0, 2] = jnp.full((MOBA_BLOCK, MOBA_BLOCK), tab_ref[h, NUM_BUCKETS - 1], F32)


def _bias_tiles(rel_table):
    assert MOBA_BLOCK + 1 >= MAX_DISTANCE and NUM_BUCKETS & (NUM_BUCKETS - 1) == 0
    tab = rel_table.T.astype(F32)
    return pl.pallas_call(
        _bias_kernel,
        grid=(A_HEADS,),
        in_specs=[pl.BlockSpec(memory_space=pltpu.SMEM)],
        out_specs=pl.BlockSpec((1, 3, MOBA_BLOCK, MOBA_BLOCK), lambda h: (h, 0, 0, 0)),
        out_shape=jax.ShapeDtypeStruct((A_HEADS, 3, MOBA_BLOCK, MOBA_BLOCK), F32),
        compiler_params=_cparams("parallel"),
        name="t5_bias_tiles",
    )(tab)


_VT_ROWS = A_HEAD_DIM + 16


def _moba_prompt_kernel(q_ref, k_ref, v_ref, bias_ref, wu_ref, wd_ref, wg_ref, wo_ref,
                        o_ref, wub_ref, wdb_ref, wgb_ref, wob_ref,
                        kb_scr, qt_scr, vt_scr, gt_scr, s_scr, p_scr, *, n_blocks):
    nb = n_blocks
    blk = MOBA_BLOCK
    nbp = gt_scr.shape[0]
    wub_ref[...] = wu_ref[...].astype(BF16)
    wdb_ref[...] = wd_ref[...].astype(BF16)
    wgb_ref[...] = wg_ref[...].astype(BF16)
    wob_ref[...] = wo_ref[...].astype(BF16)

    kb_scr[...] = k_ref[...].astype(BF16)
    for c in range(nb):
        cols = slice(c * blk, (c + 1) * blk)
        qt_scr[:, cols] = (q_ref[cols, :] * (A_HEAD_DIM ** -0.5)).T.astype(BF16)
        vt_scr[0:A_HEAD_DIM, cols] = v_ref[cols, :].T.astype(BF16)
    ones_row = (lax.broadcasted_iota(jnp.int32, (_VT_ROWS - A_HEAD_DIM, nb * blk), 0) == 0).astype(BF16)
    vt_scr[A_HEAD_DIM:, :] = ones_row
    need_gate = nb - 1 > MOBA_TOP_K
    if need_gate:
        means = [jnp.mean(k_ref[n * blk:(n + 1) * blk, :], axis=0, keepdims=True) for n in range(nb)]
        kmean = jnp.concatenate(means + [jnp.zeros((128 - nb, A_HEAD_DIM), F32)], axis=0)
        for c in range(MOBA_TOP_K + 1, nb):
            cols = slice(c * blk, (c + 1) * blk)
            gate = _dot_nt(q_ref[cols, :] * (A_HEAD_DIM ** -0.5), kmean, precision=HIGHEST)
            gt_scr[:, cols] = gate.T[:nbp, :]
    key = lax.broadcasted_iota(jnp.int32, (blk, blk), 0)
    qry = lax.broadcasted_iota(jnp.int32, (blk, blk), 1)
    causal = key <= qry
    blk_id = lax.broadcasted_iota(jnp.int32, (nbp, blk), 0)

    def scores(i):
        cols = slice(i * blk, (i + 1) * blk)
        s_scr[i % 2, 0:(i + 1) * blk, :] = _dot(kb_scr[0:(i + 1) * blk, :], qt_scr[:, cols])

    def softmax(i):
        cols = slice(i * blk, (i + 1) * blk)
        s_buf, p_buf = s_scr.at[i % 2], p_scr.at[i % 2]
        selneg = None
        if i > MOBA_TOP_K:
            g = gt_scr[:, cols]
            rank = jnp.zeros((nbp, blk), F32)
            for j in range(i):
                gj = g[j:j + 1, :]
                rank = rank + ((gj > g) | ((gj == g) & (j < blk_id))).astype(F32)
            selneg = jnp.where(rank < MOBA_TOP_K, 0.0, NEG_INF)
        m = jnp.full((1, blk), NEG_INF, F32)
        for n in range(i + 1):
            rows = slice(n * blk, (n + 1) * blk)
            kind = 0 if n == i else (1 if n == i - 1 else 2)
            st = s_buf[rows, :] + bias_ref[0, kind]
            if n == i:
                st = jnp.where(causal, st, NEG_INF)
            elif selneg is not None:
                st = st + selneg[n:n + 1, :]
            s_buf[rows, :] = st
            m = jnp.maximum(m, jnp.max(st, axis=0, keepdims=True))
        for n in range(i + 1):
            rows = slice(n * blk, (n + 1) * blk)
            p_buf[rows, :] = jnp.exp(s_buf[rows, :] - m).astype(BF16)

    def values(i):
        cols = slice(i * blk, (i + 1) * blk)
        nd = _dot(vt_scr[:, 0:(i + 1) * blk], p_scr[i % 2, 0:(i + 1) * blk, :])
        out = nd[0:A_HEAD_DIM, :] / nd[A_HEAD_DIM:A_HEAD_DIM + 1, :]
        o_ref[cols, :] = out.T.astype(o_ref.dtype)

    scores(0)
    for i in range(nb):
        if i + 1 < nb:
            scores(i + 1)
        if i > 0:
            values(i - 1)
        softmax(i)
    values(nb - 1)


def _moba_prompt(aq, ak, av, bias_tiles, w_up, w_down, w_ple_gate, w_out, batch, seq):
    assert seq % MOBA_BLOCK == 0
    nb = seq // MOBA_BLOCK
    assert nb <= 128
    nbp = -(-nb // 8) * 8
    steps = batch * A_HEADS
    slab = D_FF // steps
    gate_rows = D_MODEL // steps
    assert D_FF % steps == 0 and slab % 128 == 0 and D_MODEL % steps == 0 and gate_rows % 16 == 0
    head = pl.BlockSpec((seq, A_HEAD_DIM), lambda b, h: (b, h))
    up_slab = pl.BlockSpec((D_MODEL, slab), lambda b, h: (0, b * A_HEADS + h))
    down_slab = pl.BlockSpec((slab, D_MODEL), lambda b, h: (b * A_HEADS + h, 0))
    gate_slab = pl.BlockSpec((gate_rows, D_MODEL), lambda b, h: (b * A_HEADS + h, 0))
    kern = functools.partial(_moba_prompt_kernel, n_blocks=nb)
    return pl.pallas_call(
        kern,
        grid=(batch, A_HEADS),
        in_specs=[head, head, head,
                  pl.BlockSpec((1, 3, MOBA_BLOCK, MOBA_BLOCK), lambda b, h: (h, 0, 0, 0)),
                  up_slab, down_slab, gate_slab, gate_slab],
        out_specs=[head, up_slab, down_slab, gate_slab, gate_slab],
        out_shape=[jax.ShapeDtypeStruct((batch * seq, A_HEADS * A_HEAD_DIM), BF16),
                   jax.ShapeDtypeStruct((D_MODEL, D_FF), BF16),
                   jax.ShapeDtypeStruct((D_FF, D_MODEL), BF16),
                   jax.ShapeDtypeStruct((D_MODEL, D_MODEL), BF16),
                   jax.ShapeDtypeStruct((D_MODEL, D_MODEL), BF16)],
        scratch_shapes=[pltpu.VMEM((seq, A_HEAD_DIM), BF16),
                        pltpu.VMEM((A_HEAD_DIM, seq), BF16),
                        pltpu.VMEM((_VT_ROWS, seq), BF16),
                        pltpu.VMEM((nbp, seq), F32),
                        pltpu.VMEM((2, seq, MOBA_BLOCK), F32),
                        pltpu.VMEM((2, seq, MOBA_BLOCK), BF16)],
        compiler_params=_cparams("parallel", "parallel"),
        name="moba_prompt",
    )(aq, ak, av, bias_tiles, w_up, w_down, w_ple_gate, w_out)


PAGES_PER_BLOCK = MOBA_BLOCK // PAGE_SIZE
PAST_BLOCKS_PER_STEP = 8


def _moba_past_kernel(pt_ref, q_ref, *refs, n_pages_step):
    del pt_ref
    k_refs = refs[:n_pages_step]
    v_refs = refs[n_pages_step:2 * n_pages_step]
    bias_ref, o_ref, m_ref, l_ref, km_ref = refs[2 * n_pages_step:]
    hq = q_ref.shape[1]
    nq = hq // A_HEADS
    qb = (q_ref[0] * (A_HEAD_DIM ** -0.5)).reshape(A_HEADS, nq, A_HEAD_DIM).astype(BF16)
    n_blocks = n_pages_step // PAGES_PER_BLOCK
    is_last_step = pl.program_id(1) == pl.num_programs(1) - 1

    def head_rows(ref, h):
        return ref[0, 0, pl.ds(h, PAGE_SIZE, stride=A_HEADS), :]

    for g in range(n_blocks):
        pages = range(g * PAGES_PER_BLOCK, (g + 1) * PAGES_PER_BLOCK)
        ksum = sum(jnp.sum(k_refs[p][0, 0].reshape(PAGE_SIZE, A_HEADS, A_HEAD_DIM), axis=0) for p in pages)
        km_ref[0, g] = ksum / MOBA_BLOCK
        kind = jnp.where(is_last_step, 1, 0) if g == n_blocks - 1 else 0
        kh = jnp.stack([jnp.concatenate([head_rows(k_refs[p], h) for p in pages], axis=0)
                        for h in range(A_HEADS)]).astype(BF16)
        vh = jnp.stack([jnp.concatenate([head_rows(v_refs[p], h) for p in pages], axis=0)
                        for h in range(A_HEADS)]).astype(BF16)
        lt = _bdot_nt(qb, kh) + bias_ref[kind]
        m = jnp.max(lt, axis=2, keepdims=True)
        p_ = jnp.exp(lt - m)
        l = jnp.sum(p_, axis=2, keepdims=True)
        o_ref[0, g] = _bdot(p_.astype(BF16), vh).reshape(hq, A_HEAD_DIM)
        m_ref[0, g] = jnp.broadcast_to(m, (A_HEADS, nq, A_HEAD_DIM)).reshape(hq, A_HEAD_DIM)
        l_ref[0, g] = jnp.broadcast_to(l, (A_HEADS, nq, A_HEAD_DIM)).reshape(hq, A_HEAD_DIM)


def _moba_past(page_table, q_hq, cache_k, cache_v, past_bias, layer):
    bs, n_pages = page_table.shape
    assert n_pages % PAGES_PER_BLOCK == 0
    nbp = n_pages // PAGES_PER_BLOCK
    gb = math.gcd(nbp, PAST_BLOCKS_PER_STEP)
    pps = gb * PAGES_PER_BLOCK
    hq = q_hq.shape[1]
    page = (1, 1, PAGE_SIZE * A_HEADS, A_HEAD_DIM)
    cache_k, cache_v = (c.reshape(c.shape[:2] + page[2:]) for c in (cache_k, cache_v))
    part = pl.BlockSpec((1, gb, hq, A_HEAD_DIM), lambda b, n, pt: (b, n, 0, 0))

    def page_spec(p):
        return pl.BlockSpec(page, lambda b, n, pt: (layer, pt[b, pps * n + p], 0, 0))

    grid_spec = pltpu.PrefetchScalarGridSpec(
        num_scalar_prefetch=1,
        grid=(bs, nbp // gb),
        in_specs=[pl.BlockSpec((1, hq, A_HEAD_DIM), lambda b, n, pt: (b, 0, 0))]
        + [page_spec(p) for p in range(pps)] * 2
        + [pl.BlockSpec((2, A_HEADS, hq // A_HEADS, MOBA_BLOCK), lambda b, n, pt: (0, 0, 0, 0))],
        out_specs=[part, part, part,
                   pl.BlockSpec((1, gb, A_HEADS, A_HEAD_DIM), lambda b, n, pt: (b, n, 0, 0))],
    )
    pshape = jax.ShapeDtypeStruct((bs, nbp, hq, A_HEAD_DIM), F32)
    return pl.pallas_call(
        functools.partial(_moba_past_kernel, n_pages_step=pps),
        grid_spec=grid_spec,
        out_shape=[pshape, pshape, pshape, jax.ShapeDtypeStruct((bs, nbp, A_HEADS, A_HEAD_DIM), F32)],
        compiler_params=_cparams("parallel", "parallel"),
        name="moba_past_blocks",
    )(page_table, q_hq, *([cache_k] * pps), *([cache_v] * pps), past_bias)


def _moba_combine_kernel(q_ref, km_ref, kn_ref, vn_ref, ob_ref, op_ref, mp_ref, lp_ref, out_ref):
    nbp = km_ref.shape[2]
    q = q_ref[0] * (A_HEAD_DIM ** -0.5)
    hq = q.shape[0]
    nq = hq // A_HEADS
    gate = jnp.concatenate(
        [_dot_nt(q[h * nq:(h + 1) * nq], km_ref[0, h], precision=HIGHEST) for h in range(A_HEADS)], axis=0)
    lane = lax.broadcasted_iota(jnp.int32, (hq, nbp), 1)
    rank = jnp.zeros((hq, nbp), F32)
    for j in range(nbp):
        gj = gate[:, j:j + 1]
        beats = (gj > gate) | ((gj == gate) & (j < lane))
        rank = rank + beats.astype(F32)
    selneg = jnp.where(rank < MOBA_TOP_K, 0.0, NEG_INF)
    qb = q.astype(BF16)
    lt = _dot_nt(qb, kn_ref[0].astype(BF16))
    r = lax.broadcasted_iota(jnp.int32, lt.shape, 0)
    c = lax.broadcasted_iota(jnp.int32, lt.shape, 1)
    ok = ((c % A_HEADS) == (r // nq)) & ((c // A_HEADS) <= (r % nq))
    lo = jnp.where(ok, lt + ob_ref[...], NEG_INF)
    m = jnp.max(lo, axis=1, keepdims=True)
    for n in range(nbp):
        m = jnp.maximum(m, mp_ref[0, n][:, 0:1] + selneg[:, n:n + 1])
    p = jnp.exp(lo - m)
    l = jnp.sum(p, axis=1, keepdims=True)
    acc = _dot(p.astype(BF16), vn_ref[0].astype(BF16))
    for n in range(nbp):
        w = jnp.exp(mp_ref[0, n][:, 0:1] + selneg[:, n:n + 1] - m)
        l = l + w * lp_ref[0, n][:, 0:1]
        acc = acc + w * op_ref[0, n]
    out_ref[0] = acc / l


def _moba_combine(q_hq, kmean_hn, k_new, v_new, own_bias, o_part, m_part, l_part):
    bs, hq, _ = q_hq.shape
    nbp = o_part.shape[1]
    part = pl.BlockSpec((1, nbp, hq, A_HEAD_DIM), lambda b: (b, 0, 0, 0))
    tok = pl.BlockSpec((1, hq, A_HEAD_DIM), lambda b: (b, 0, 0))
    return pl.pallas_call(
        _moba_combine_kernel,
        grid=(bs,),
        in_specs=[tok,
                  pl.BlockSpec((1, A_HEADS, nbp, A_HEAD_DIM), lambda b: (b, 0, 0, 0)),
                  tok, tok,
                  pl.BlockSpec((hq, hq), lambda b: (0, 0)),
                  part, part, part],
        out_specs=tok,
        out_shape=jax.ShapeDtypeStruct((bs, hq, A_HEAD_DIM), F32),
        compiler_params=_cparams("parallel"),
        name="moba_combine",
    )(q_hq, kmean_hn, k_new, v_new, own_bias, o_part, m_part, l_part)


def _out_proj_kernel(h_ref, om_ref, oa_ref, hs_ref, oms_ref, oas_ref, g_ref, w_ref, o_ref, xn_ref, os_ref, xns_ref):
    half = om_ref.shape[1]

    def project(h, om, oa, dst, xn_dst):
        r = (h[...] + _dot(om[...].astype(BF16), w_ref[0:half, :])
             + _dot(oa[...].astype(BF16), w_ref[half:, :]))
        dst[...] = r
        xn_dst[...] = _rms(r, g_ref[...]).astype(BF16)

    project(h_ref, om_ref, oa_ref, o_ref, xn_ref)

    @pl.when(pl.program_id(0) == 0)
    def _():
        project(hs_ref, oms_ref, oas_ref, os_ref, xns_ref)


def _out_proj(h, out_m, out_a, hs, out_ms, out_as, g_ffn, w_out_bf16):
    m, ms = h.shape[0], hs.shape[0]
    tm = min(512, m)
    assert m % tm == 0
    half = M_HEADS * M_V_DIM
    rows = pl.BlockSpec((tm, D_MODEL), lambda i: (i, 0))
    mix = pl.BlockSpec((tm, half), lambda i: (i, 0))
    rows_s = pl.BlockSpec((ms, D_MODEL), lambda i: (0, 0))
    mix_s = pl.BlockSpec((ms, half), lambda i: (0, 0))
    return pl.pallas_call(
        _out_proj_kernel,
        grid=(m // tm,),
        in_specs=[
            rows, mix, mix, rows_s, mix_s, mix_s,
            pl.BlockSpec((1, D_MODEL), lambda i: (0, 0)),
            pl.BlockSpec((D_MODEL, D_MODEL), lambda i: (0, 0), pipeline_mode=pl.Buffered(1)),
        ],
        out_specs=[rows, rows, rows_s, rows_s],
        out_shape=[jax.ShapeDtypeStruct((m, D_MODEL), F32), jax.ShapeDtypeStruct((m, D_MODEL), BF16),
                   jax.ShapeDtypeStruct((ms, D_MODEL), F32), jax.ShapeDtypeStruct((ms, D_MODEL), BF16)],
        compiler_params=_cparams("arbitrary"),
        name="out_proj",
    )(h, out_m, out_a, hs, out_ms, out_as, g_ffn, w_out_bf16)


def _mlp_kernel(xn_ref, xns_ref, wu_ref, wd_ref, o_ref, os_ref):
    f = pl.program_id(1)

    def accumulate(xn, dst):
        @pl.when(f == 0)
        def _():
            dst[...] = jnp.zeros(dst.shape, F32)

        u = _dot(xn[...], wu_ref[...])
        a = jnp.square(jnp.maximum(u, 0.0)).astype(BF16)
        dst[...] += _dot(a, wd_ref[...])

    accumulate(xn_ref, o_ref)

    @pl.when(pl.program_id(0) == 0)
    def _():
        accumulate(xns_ref, os_ref)


def _mlp(xn, xns, w_up, w_down):
    m, ms = xn.shape[0], xns.shape[0]
    tm = 1024 if m % 1024 == 0 else min(512, m)
    assert m % tm == 0
    tf = 1024
    return pl.pallas_call(
        _mlp_kernel,
        grid=(m // tm, D_FF // tf),
        in_specs=[
            pl.BlockSpec((tm, D_MODEL), lambda i, f: (i, 0)),
            pl.BlockSpec((ms, D_MODEL), lambda i, f: (0, 0)),
            pl.BlockSpec((D_MODEL, tf), lambda i, f: (0, f)),
            pl.BlockSpec((tf, D_MODEL), lambda i, f: (f, 0)),
        ],
        out_specs=[pl.BlockSpec((tm, D_MODEL), lambda i, f: (i, 0)),
                   pl.BlockSpec((ms, D_MODEL), lambda i, f: (0, 0))],
        out_shape=[jax.ShapeDtypeStruct((m, D_MODEL), F32), jax.ShapeDtypeStruct((ms, D_MODEL), F32)],
        compiler_params=_cparams("arbitrary", "arbitrary"),
        name="mlp",
    )(xn, xns, w_up, w_down)


def _ple_kernel(h_ref, d_ref, pe_ref, hs_ref, ds_ref, pes_ref, g_ref, wg_ref, wp_ref, gf_ref, y_ref, ys_ref):
    def gated_embedding(src, delta, pe, dst):
        h = src[...] + delta[...]
        gate = jax.nn.sigmoid(_dot(_rms(h, g_ref[...]).astype(BF16), wg_ref[...]))
        proj = _dot(pe[...].astype(BF16), wp_ref[...])
        dst[...] = _rms(h + gate * proj, gf_ref[...])

    gated_embedding(h_ref, d_ref, pe_ref, y_ref)

    @pl.when(pl.program_id(0) == 0)
    def _():
        gated_embedding(hs_ref, ds_ref, pes_ref, ys_ref)


def _ple_final(h, delta, pe, hs, deltas, pes, g_ple, w_gate, w_proj, g_final):
    m, ms = h.shape[0], hs.shape[0]
    tm = min(512, m)
    assert m % tm == 0
    vec = pl.BlockSpec((1, D_MODEL), lambda i: (0, 0))
    rows = pl.BlockSpec((tm, D_MODEL), lambda i: (i, 0))
    rows_s = pl.BlockSpec((ms, D_MODEL), lambda i: (0, 0))
    return pl.pallas_call(
        _ple_kernel,
        grid=(m // tm,),
        in_specs=[
            rows, rows,
            pl.BlockSpec((tm, PLE_DIM), lambda i: (i, 0)),
            rows_s, rows_s,
            pl.BlockSpec((ms, PLE_DIM), lambda i: (0, 0)),
            vec,
            pl.BlockSpec((D_MODEL, D_MODEL), lambda i: (0, 0), pipeline_mode=pl.Buffered(1)),
            pl.BlockSpec((PLE_DIM, D_MODEL), lambda i: (0, 0), pipeline_mode=pl.Buffered(1)),
            vec,
        ],
        out_specs=[rows, rows_s],
        out_shape=[jax.ShapeDtypeStruct((m, D_MODEL), F32), jax.ShapeDtypeStruct((ms, D_MODEL), F32)],
        compiler_params=_cparams("arbitrary"),
        name="ple_final_norm",
    )(h, delta, pe, hs, deltas, pes, g_ple, w_gate, w_proj, g_final)


def _layer_weights(w_in, b_igate, b_fgate, g_mix, g_mhead, w_out, g_ffn, w_up, w_down,
                   g_ple, w_ple_gate, w_ple_proj):
    gate_bias = jnp.pad(jnp.concatenate([b_igate, b_fgate]), (0, GATE_PAD - 2 * M_HEADS)).reshape(1, GATE_PAD)
    return dict(
        w_in=w_in.T, gate_bias=gate_bias.astype(F32),
        g_mix=g_mix.reshape(1, D_MODEL), g_mhead=g_mhead, w_out=w_out,
        g_ffn=g_ffn.reshape(1, D_MODEL), w_up=w_up, w_down=w_down,
        g_ple=g_ple.reshape(1, D_MODEL), w_ple_gate=w_ple_gate,
        w_ple_proj=w_ple_proj.astype(BF16))


def _tail(h, out_m, out_a, pe, hs, out_ms, out_as, pes, w, g_final):
    h, xn, hs, xns = _out_proj(h, out_m, out_a, hs, out_ms, out_as, w["g_ffn"], w["w_out_bf16"])
    delta, deltas = _mlp(xn, xns, w["w_up_bf16"], w["w_down_bf16"])
    return _ple_final(h, delta, pe, hs, deltas, pes, w["g_ple"], w["w_ple_gate_bf16"], w["w_ple_proj"], g_final)


def _split_state(c_aug, m_fin):
    return c_aug[..., :M_V_DIM], c_aug[..., M_V_DIM], m_fin[..., 0, 0]


def kernel(x_prompt, x_sample, cache_k, cache_v, state_C, state_n, state_m, page_table, p_prompt, p_sample, rel_bias_table, w_in, b_igate, b_fgate, g_mix, g_mhead, w_out, g_ffn, w_up, w_down, g_ple, w_ple_gate, w_ple_proj, g_final):
    depth = w_in.shape[0]
    assert depth == 1, "one decoder layer per call"
    layer = 0
    bp, tp, _ = x_prompt.shape
    bs, ts, _ = x_sample.shape
    past_len = page_table.shape[1] * PAGE_SIZE
    assert past_len % MOBA_BLOCK == 0 and ts <= MOBA_BLOCK
    kv_shape = (A_HEADS, A_HEAD_DIM)
    g_fin = g_final.reshape(1, D_MODEL)

    w = _layer_weights(w_in[layer], b_igate[layer], b_fgate[layer], g_mix[layer], g_mhead[layer], w_out[layer],
                       g_ffn[layer], w_up[layer], w_down[layer], g_ple[layer], w_ple_gate[layer],
                       w_ple_proj[layer])
    bias_tiles = _bias_tiles(rel_bias_table)

    hp = x_prompt.reshape(bp * tp, D_MODEL)
    hs = x_sample.reshape(bs * ts, D_MODEL)
    xn_p, gp, xn_s, gs = _norm_gate(hp, hs, w["g_mix"], w["w_in"])
    zp, aq_p, ak_p, av_p, zs, zs_a = _in_proj(xn_p, xn_s, w["w_in"])
    aq_s, ak_s, av_s = (zs_a[:, t * _IN_TN:(t + 1) * _IN_TN] for t in range(3))
    c0 = jnp.zeros((bp, M_HEADS, M_QK_DIM, 2 * M_V_DIM), F32)
    m0 = jnp.zeros((bp, M_HEADS, 1, GATE_PAD), F32)
    om_p, c_p, m_p = _mlstm(zp, gp, w["gate_bias"], w["g_mhead"], c0, m0, bp, tp)
    oa_p, w["w_up_bf16"], w["w_down_bf16"], w["w_ple_gate_bf16"], w["w_out_bf16"] = _moba_prompt(
        aq_p, ak_p, av_p, bias_tiles, w["w_up"], w["w_down"], w["w_ple_gate"], w["w_out"], bp, tp)
    k_p = ak_p.reshape((1, bp, tp) + kv_shape)
    v_p = av_p.reshape((1, bp, tp) + kv_shape)
    cp, np_, mp = _split_state(c_p, m_p)

    c0s = jnp.concatenate([state_C[layer], state_n[layer][..., None],
                           jnp.zeros((bs, M_HEADS, M_QK_DIM, M_V_DIM - 1), F32)], axis=-1)
    m0s = jnp.broadcast_to(state_m[layer][..., None, None], (bs, M_HEADS, 1, GATE_PAD))
    om_s, c_s, m_s = _mlstm(zs, gs, w["gate_bias"], w["g_mhead"], c0s, m0s, bs, ts)

    hq = A_HEADS * ts
    q_s = aq_s.reshape(bs, ts, A_HEADS, A_HEAD_DIM)
    q_hq = q_s.transpose(0, 2, 1, 3).reshape(bs, hq, A_HEAD_DIM)
    k_new = ak_s.reshape(bs, hq, A_HEAD_DIM)
    v_new = av_s.reshape(bs, hq, A_HEAD_DIM)
    rows = bias_tiles[:, :, :, :ts].transpose(0, 1, 3, 2)
    expand = lambda a: jnp.repeat(a, A_HEADS, axis=-1).reshape(hq, -1)
    past_bias = jnp.stack([rows[:, 2], rows[:, 1]])
    own_bias = expand(rows[:, 0, :, :ts])
    o_part, m_part, l_part, kmean = _moba_past(page_table, q_hq, cache_k, cache_v, past_bias, layer)
    oa_hq = _moba_combine(q_hq, kmean.transpose(0, 2, 1, 3), k_new, v_new, own_bias, o_part, m_part, l_part)
    oa_s = oa_hq.reshape(bs, A_HEADS, ts, A_HEAD_DIM).transpose(0, 2, 1, 3).reshape(bs * ts, A_HEADS * A_HEAD_DIM)
    y_p, y_s = _tail(hp, om_p, oa_p, p_prompt[layer].reshape(bp * tp, PLE_DIM),
                     hs, om_s, oa_s, p_sample[layer].reshape(bs * ts, PLE_DIM), w, g_fin)
    k_s = k_new.reshape((1, bs, ts) + kv_shape)
    v_s = v_new.reshape((1, bs, ts) + kv_shape)
    cs, ns, ms = _split_state(c_s, m_s)

    return (y_p.reshape(bp, tp, D_MODEL), y_s.reshape(bs, ts, D_MODEL),
            k_p, v_p, cp[None], np_[None], mp[None],
            k_s, v_s, cs[None], ns[None], ms[None])
```

```python
import functools
import math

import jax
import jax.numpy as jnp
from jax import lax
from jax.experimental import pallas as pl
from jax.experimental.pallas import tpu as pltpu

F32 = jnp.float32
BF16 = jnp.bfloat16
HIGHEST = lax.Precision.HIGHEST

D_MODEL = 2048
M_HEADS = 8
M_V_DIM = 128
M_QK_DIM = 64
M_CHUNK = 64
A_HEADS = 8
A_HEAD_DIM = 128
MOBA_BLOCK = 256
MOBA_TOP_K = 3
NUM_BUCKETS = 32
MAX_EXACT = NUM_BUCKETS // 2
MAX_DISTANCE = 128
D_FF = 4 * D_MODEL
PLE_DIM = 256
PAGE_SIZE = 128
EPS = 1e-6

_MQ, _MK, _MV, _MO = 0, 512, 1024, 2048
_GATES = 3072
_AQ = 3088
D_MAIN = 6144
GATE_PAD = 128

VMEM_LIMIT = 56 * 1024 * 1024
NEG_INF = float("-inf")


def _cparams(*sem):
    return pltpu.CompilerParams(dimension_semantics=sem, vmem_limit_bytes=VMEM_LIMIT)


def _rms(x, g):
    return x * lax.rsqrt(jnp.mean(x * x, axis=-1, keepdims=True) + EPS) * g


def _dot_nt(a, b, **kw):
    return lax.dot_general(a, b, (((1,), (1,)), ((), ())), preferred_element_type=F32, **kw)


def _dot_tn(a, b, **kw):
    return lax.dot_general(a, b, (((0,), (0,)), ((), ())), preferred_element_type=F32, **kw)


def _dot(a, b, **kw):
    return jnp.dot(a, b, preferred_element_type=F32, **kw)


_IN_TN = 1024
_N_IN_TILES = D_MAIN // _IN_TN
_N_MLSTM_TILES = _GATES // _IN_TN
_GATE_COLS = _AQ - _GATES
_CAST_ROWS = 256


def _norm_gate_kernel(x_ref, xs_ref, g_ref, wg_ref, xn_ref, gate_ref, xns_ref, gates_ref):
    wg = wg_ref[...]
    hi = wg.astype(BF16).astype(F32)
    row = lax.broadcasted_iota(jnp.int32, wg.shape, 0)
    w2 = jnp.where(row < _GATE_COLS, hi,
                   jnp.where(row < 2 * _GATE_COLS, pltpu.roll(wg - hi, _GATE_COLS, axis=0), 0.0))
    w2 = w2.astype(BF16)

    def norm_and_gate(src_ref, dst_ref, gate_dst_ref):
        xn = _rms(src_ref[...], g_ref[...]).astype(BF16)
        dst_ref[...] = xn
        r = _dot_nt(xn, w2)
        gate_dst_ref[...] = r + pltpu.roll(r, GATE_PAD - _GATE_COLS, axis=1)

    norm_and_gate(x_ref, xn_ref, gate_ref)

    @pl.when(pl.program_id(0) == 0)
    def _():
        norm_and_gate(xs_ref, xns_ref, gates_ref)


def _norm_gate(x, xs, g, w_in_t):
    m, ms = x.shape[0], xs.shape[0]
    tm = 1024 if m % 1024 == 0 else min(512, m)
    assert m % tm == 0
    return pl.pallas_call(
        _norm_gate_kernel,
        grid=(m // tm,),
        in_specs=[
            pl.BlockSpec((tm, D_MODEL), lambda i: (i, 0)),
            pl.BlockSpec((ms, D_MODEL), lambda i: (0, 0)),
            pl.BlockSpec((1, D_MODEL), lambda i: (0, 0)),
            pl.BlockSpec((GATE_PAD, D_MODEL), lambda i: (_GATES // GATE_PAD, 0)),
        ],
        out_specs=[pl.BlockSpec((tm, D_MODEL), lambda i: (i, 0)),
                   pl.BlockSpec((tm, GATE_PAD), lambda i: (i, 0)),
                   pl.BlockSpec((ms, D_MODEL), lambda i: (0, 0)),
                   pl.BlockSpec((ms, GATE_PAD), lambda i: (0, 0))],
        out_shape=[jax.ShapeDtypeStruct((m, D_MODEL), BF16), jax.ShapeDtypeStruct((m, GATE_PAD), F32),
                   jax.ShapeDtypeStruct((ms, D_MODEL), BF16), jax.ShapeDtypeStruct((ms, GATE_PAD), F32)],
        compiler_params=_cparams("arbitrary"),
        name="norm_gate",
    )(x, xs, g, w_in_t)


def _cast_weight_tile(wa_ref, wb_ref, w_scr, shifted):
    skip = _GATE_COLS if shifted else 0
    body = _IN_TN - skip
    for r in range(0, body, _CAST_ROWS):
        n = min(_CAST_ROWS, body - r)
        w_scr[r:r + n, :] = wa_ref[skip + r:skip + r + n, :].astype(BF16)
    if shifted:
        w_scr[body:, :] = wb_ref[:_GATE_COLS, :].astype(BF16)


def _in_proj_mlstm_kernel(xn_ref, xs_ref, wa_ref, z_ref, zs_ref, w_scr):
    @pl.when(pl.program_id(1) == 0)
    def _():
        _cast_weight_tile(wa_ref, None, w_scr, shifted=False)
        zs_ref[0] = _dot_nt(xs_ref[...], w_scr[...])

    z_ref[0] = _dot_nt(xn_ref[...], w_scr[...])


def _in_proj_moba_kernel(xn_ref, xs_ref, wa_ref, wb_ref, aq_ref, ak_ref, av_ref, zs_ref, w_scr):
    j = pl.program_id(0)

    @pl.when(pl.program_id(1) == 0)
    def _():
        _cast_weight_tile(wa_ref, wb_ref, w_scr, shifted=True)
        zs_ref[...] = _dot_nt(xs_ref[...], w_scr[...])

    for t, ref in enumerate((aq_ref, ak_ref, av_ref)):
        @pl.when(j == t)
        def _(ref=ref):
            ref[...] = _dot_nt(xn_ref[...], w_scr[...])


def _in_proj(xn, xs, w_in_t):
    m, ms = xn.shape[0], xs.shape[0]
    tn = _IN_TN
    nm = _N_MLSTM_TILES
    scratch = [pltpu.VMEM((tn, D_MODEL), BF16)]
    sample_rows = pl.BlockSpec((ms, D_MODEL), lambda j, i: (0, 0))
    sample_out = pl.BlockSpec((ms, tn), lambda j, i: (0, j))

    tm = 1024 if m % 1024 == 0 else min(512, m)
    assert m % tm == 0
    zm, zs_m = pl.pallas_call(
        _in_proj_mlstm_kernel,
        grid=(nm, m // tm),
        in_specs=[
            pl.BlockSpec((tm, D_MODEL), lambda j, i: (i, 0)),
            sample_rows,
            pl.BlockSpec((tn, D_MODEL), lambda j, i: (j, 0)),
        ],
        out_specs=[pl.BlockSpec((1, tm, tn), lambda j, i: (j, i, 0)),
                   pl.BlockSpec((1, ms, tn), lambda j, i: (j, 0, 0))],
        out_shape=[jax.ShapeDtypeStruct((nm, m, tn), F32), jax.ShapeDtypeStruct((nm, ms, tn), F32)],
        scratch_shapes=scratch,
        compiler_params=_cparams("arbitrary", "arbitrary"),
        name="in_proj_mlstm",
    )(xn, xs, w_in_t)

    tm = min(512, m)
    assert m % tm == 0
    last = m // tm - 1

    def section(t):
        return pl.BlockSpec((tm, tn), lambda j, i: (jnp.where(j < t, 0, jnp.where(j == t, i, last)), 0))

    sec_shape = jax.ShapeDtypeStruct((m, tn), F32)
    aq, ak, av, zs_a = pl.pallas_call(
        _in_proj_moba_kernel,
        grid=(_N_IN_TILES - nm, m // tm),
        in_specs=[
            pl.BlockSpec((tm, D_MODEL), lambda j, i: (i, 0)),
            sample_rows,
            pl.BlockSpec((tn, D_MODEL), lambda j, i: (nm + j, 0)),
            pl.BlockSpec((GATE_PAD, D_MODEL), lambda j, i: ((nm + j + 1) * (tn // GATE_PAD), 0)),
        ],
        out_specs=[section(0), section(1), section(2), sample_out],
        out_shape=[sec_shape, sec_shape, sec_shape, jax.ShapeDtypeStruct((ms, D_MAIN - _GATES), F32)],
        scratch_shapes=scratch,
        compiler_params=_cparams("arbitrary", "arbitrary"),
        name="in_proj_moba",
    )(xn, xs, w_in_t, w_in_t)
    return zm, aq, ak, av, zs_m, zs_a


def _log_sigmoid(x):
    return -(jnp.maximum(-x, 0.0) + jnp.log1p(jnp.exp(-jnp.abs(x))))


def _bdot(a, b):
    return lax.dot_general(a, b, (((2,), (1,)), ((0,), (0,))), preferred_element_type=F32)


def _bdot_nt(a, b):
    return lax.dot_general(a, b, (((2,), (2,)), ((0,), (0,))), preferred_element_type=F32)


def _bdot_tn(a, b):
    return lax.dot_general(a, b, (((1,), (1,)), ((0,), (0,))), preferred_element_type=F32)


def _mlstm_kernel(q_ref, k_ref, v_ref, o_ref, g_ref, gb_ref, gh_ref, c0_ref, m0_ref,
                  out_ref, cfin_ref, mfin_ref, c_scr, m_scr, *, chunk, n_chunks):
    L = chunk
    H = M_HEADS
    tb = L * n_chunks
    t = pl.program_id(1)

    @pl.when(t == 0)
    def _():
        c_scr[...] = c0_ref[0]
        m_scr[...] = m0_ref[0]

    g = g_ref[...] + gb_ref[...]
    lane = lax.broadcasted_iota(jnp.int32, (tb, GATE_PAD), 1)
    gl = jnp.where((lane >= H) & (lane < 2 * H), _log_sigmoid(g), g)
    r = lax.broadcasted_iota(jnp.int32, (tb, tb), 0)
    c = lax.broadcasted_iota(jnp.int32, (tb, tb), 1)
    tri = ((r // L == c // L) & (c <= r)).astype(F32)
    cum = _dot(tri, gl, precision=HIGHEST)

    def rows(ci):
        return slice(ci * L, (ci + 1) * L)

    def stack(f):
        return jnp.stack([f(ci, h) for ci in range(n_chunks) for h in range(H)])

    ones_col = (lax.broadcasted_iota(jnp.int32, (L, M_V_DIM), 1) == 0).astype(BF16)
    li = stack(lambda ci, h: gl[rows(ci), h:h + 1])
    b = stack(lambda ci, h: cum[rows(ci), H + h:H + h + 1])
    qb = stack(lambda ci, h: q_ref[0, rows(ci), h * M_QK_DIM:(h + 1) * M_QK_DIM]).astype(BF16)
    kc = stack(lambda ci, h: k_ref[0, rows(ci), h * M_QK_DIM:(h + 1) * M_QK_DIM]) * (M_QK_DIM ** -0.5)
    vaug = stack(lambda ci, h: jnp.concatenate(
        [v_ref[0, rows(ci), h * M_V_DIM:(h + 1) * M_V_DIM].astype(BF16), ones_col], axis=1))

    rl = lax.broadcasted_iota(jnp.int32, (L, L), 0)
    cl = lax.broadcasted_iota(jnp.int32, (L, L), 1)
    w_row = jnp.sum(jnp.where(rl == cl, li - b, 0.0), axis=1, keepdims=True)
    d = jnp.where(cl <= rl, b + w_row, NEG_INF)
    m_loc = jnp.max(d, axis=2, keepdims=True)
    s = _bdot_nt(qb, kc.astype(BF16)) * jnp.exp(d - m_loc)
    nd_loc = _bdot(s.astype(BF16), vaug)
    g_last = m_loc[:, L - 1:L, :]
    b_last = b[:, L - 1:L, :]
    ws = jnp.exp(b_last - b + li - g_last)
    dc_loc = _bdot_tn((kc * ws).astype(BF16), vaug)

    m_prev = m_scr[:, :, 0:1]
    c_aug = c_scr[...]
    for ci in range(n_chunks):
        grp = slice(ci * H, (ci + 1) * H)
        a = b[grp] + m_prev
        m_t = jnp.maximum(a, m_loc[grp])
        nd = (jnp.exp(a - m_t) * _bdot(qb[grp], c_aug.astype(BF16))
              + jnp.exp(m_loc[grp] - m_t) * nd_loc[grp])
        hh = nd[:, :, :M_V_DIM] / jnp.maximum(jnp.abs(nd[:, :, M_V_DIM:M_V_DIM + 1]), jnp.exp(-m_t))
        hn = _rms(hh, gh_ref[...])
        for h in range(H):
            og = jax.nn.sigmoid(o_ref[0, rows(ci), h * M_V_DIM:(h + 1) * M_V_DIM])
            out_ref[rows(ci), h * M_V_DIM:(h + 1) * M_V_DIM] = (og * hn[h]).astype(out_ref.dtype)
        m_new = jnp.maximum(b_last[grp] + m_prev, g_last[grp])
        c_aug = (jnp.exp(b_last[grp] + m_prev - m_new) * c_aug
                 + jnp.exp(g_last[grp] - m_new) * dc_loc[grp])
        m_prev = m_new
    c_scr[...] = c_aug
    m_scr[...] = jnp.broadcast_to(m_prev, m_scr.shape)

    @pl.when(t == pl.num_programs(1) - 1)
    def _():
        cfin_ref[0] = c_scr[...]
        mfin_ref[0] = m_scr[...]


def _mlstm(z, gates, gate_bias, g_mhead, c0_aug, m0, batch, seq):
    L = math.gcd(seq, M_CHUNK)
    n_chunks = max(1, min(seq // L, 256 // L))
    tb = L * n_chunks
    nt = seq // tb
    row = lambda b, t: b * nt + t
    kern = functools.partial(_mlstm_kernel, chunk=L, n_chunks=n_chunks)
    return pl.pallas_call(
        kern,
        grid=(batch, nt),
        in_specs=[
            pl.BlockSpec((1, tb, 512), lambda b, t: (_MQ // _IN_TN, row(b, t), 0)),
            pl.BlockSpec((1, tb, 512), lambda b, t: (_MK // _IN_TN, row(b, t), 1)),
            pl.BlockSpec((1, tb, 1024), lambda b, t: (_MV // _IN_TN, row(b, t), 0)),
            pl.BlockSpec((1, tb, 1024), lambda b, t: (_MO // _IN_TN, row(b, t), 0)),
            pl.BlockSpec((tb, GATE_PAD), lambda b, t: (row(b, t), 0)),
            pl.BlockSpec((1, GATE_PAD), lambda b, t: (0, 0)),
            pl.BlockSpec((M_HEADS, 1, M_V_DIM), lambda b, t: (0, 0, 0)),
            pl.BlockSpec((1, M_HEADS, M_QK_DIM, 2 * M_V_DIM), lambda b, t: (b, 0, 0, 0)),
            pl.BlockSpec((1, M_HEADS, 1, GATE_PAD), lambda b, t: (b, 0, 0, 0)),
        ],
        out_specs=[
            pl.BlockSpec((tb, M_HEADS * M_V_DIM), lambda b, t: (row(b, t), 0)),
            pl.BlockSpec((1, M_HEADS, M_QK_DIM, 2 * M_V_DIM), lambda b, t: (b, 0, 0, 0)),
            pl.BlockSpec((1, M_HEADS, 1, GATE_PAD), lambda b, t: (b, 0, 0, 0)),
        ],
        out_shape=[
            jax.ShapeDtypeStruct((batch * seq, M_HEADS * M_V_DIM), BF16 if tb % 16 == 0 else F32),
            jax.ShapeDtypeStruct((batch, M_HEADS, M_QK_DIM, 2 * M_V_DIM), F32),
            jax.ShapeDtypeStruct((batch, M_HEADS, 1, GATE_PAD), F32),
        ],
        scratch_shapes=[pltpu.VMEM((M_HEADS, M_QK_DIM, 2 * M_V_DIM), F32),
                        pltpu.VMEM((M_HEADS, 1, GATE_PAD), F32)],
        compiler_params=_cparams("parallel", "arbitrary"),
        name="mlstm",
    )(z, z, z, z, gates, gate_bias, g_mhead.reshape(M_HEADS, 1, M_V_DIM), c0_aug, m0)


def _t5_bucket(rel):
    n = jnp.maximum(rel, 0)
    nf = jnp.maximum(n, 1).astype(F32)
    large = MAX_EXACT + (jnp.log(nf / MAX_EXACT) / math.log(MAX_DISTANCE / MAX_EXACT)
                         * (NUM_BUCKETS - MAX_EXACT)).astype(jnp.int32)
    large = jnp.minimum(large, NUM_BUCKETS - 1)
    return jnp.where(n < MAX_EXACT, n, large)


def _bias_kernel(tab_ref, out_ref):
    h = pl.program_id(0)
    i = lax.broadcasted_iota(jnp.int32, (MOBA_BLOCK, MOBA_BLOCK), 1)
    j = lax.broadcasted_iota(jnp.int32, (MOBA_BLOCK, MOBA_BLOCK), 0)
    for kind, rel in ((0, i - j), (1, MOBA_BLOCK + i - j)):
        bucket = _t5_bucket(rel)
        vals = [tab_ref[h, b] for b in range(NUM_BUCKETS)]
        bit = 1
        while len(vals) > 1:
            odd = (bucket & bit) != 0
            vals = [jnp.where(odd, vals[2 * m + 1], vals[2 * m]) for m in range(len(vals) // 2)]
            bit *= 2
        out_ref[0, kind] = vals[0]
    out_ref[0, 2] = jnp.full((MOBA_BLOCK, MOBA_BLOCK), tab_ref[h, NUM_BUCKETS - 1], F32)


def _bias_tiles(rel_table):
    assert MOBA_BLOCK + 1 >= MAX_DISTANCE and NUM_BUCKETS & (NUM_BUCKETS - 1) == 0
    tab = rel_table.T.astype(F32)
    return pl.pallas_call(
        _bias_kernel,
        grid=(A_HEADS,),
        in_specs=[pl.BlockSpec(memory_space=pltpu.SMEM)],
        out_specs=pl.BlockSpec((1, 3, MOBA_BLOCK, MOBA_BLOCK), lambda h: (h, 0, 0, 0)),
        out_shape=jax.ShapeDtypeStruct((A_HEADS, 3, MOBA_BLOCK, MOBA_BLOCK), F32),
        compiler_params=_cparams("parallel"),
        name="t5_bias_tiles",
    )(tab)


_VT_ROWS = A_HEAD_DIM + 16


def _moba_prompt_kernel(q_ref, k_ref, v_ref, bias_ref, wu_ref, wd_ref, wg_ref, wo_ref,
                        o_ref, wub_ref, wdb_ref, wgb_ref, wob_ref,
                        kb_scr, qt_scr, vt_scr, gt_scr, s_scr, p_scr, *, n_blocks):
    nb = n_blocks
    blk = MOBA_BLOCK
    nbp = gt_scr.shape[0]
    wub_ref[...] = wu_ref[...].astype(BF16)
    wdb_ref[...] = wd_ref[...].astype(BF16)
    wgb_ref[...] = wg_ref[...].astype(BF16)
    wob_ref[...] = wo_ref[...].astype(BF16)

    kb_scr[...] = k_ref[...].astype(BF16)
    for c in range(nb):
        cols = slice(c * blk, (c + 1) * blk)
        qt_scr[:, cols] = (q_ref[cols, :] * (A_HEAD_DIM ** -0.5)).T.astype(BF16)
        vt_scr[0:A_HEAD_DIM, cols] = v_ref[cols, :].T.astype(BF16)
    ones_row = (lax.broadcasted_iota(jnp.int32, (_VT_ROWS - A_HEAD_DIM, nb * blk), 0) == 0).astype(BF16)
    vt_scr[A_HEAD_DIM:, :] = ones_row
    need_gate = nb - 1 > MOBA_TOP_K
    if need_gate:
        means = [jnp.mean(k_ref[n * blk:(n + 1) * blk, :], axis=0, keepdims=True) for n in range(nb)]
        kmean = jnp.concatenate(means + [jnp.zeros((128 - nb, A_HEAD_DIM), F32)], axis=0)
        for c in range(MOBA_TOP_K + 1, nb):
            cols = slice(c * blk, (c + 1) * blk)
            gate = _dot_nt(q_ref[cols, :] * (A_HEAD_DIM ** -0.5), kmean, precision=HIGHEST)
            gt_scr[:, cols] = gate.T[:nbp, :]
    key = lax.broadcasted_iota(jnp.int32, (blk, blk), 0)
    qry = lax.broadcasted_iota(jnp.int32, (blk, blk), 1)
    causal = key <= qry
    blk_id = lax.broadcasted_iota(jnp.int32, (nbp, blk), 0)

    def scores(i):
        cols = slice(i * blk, (i + 1) * blk)
        s_scr[i % 2, 0:(i + 1) * blk, :] = _dot(kb_scr[0:(i + 1) * blk, :], qt_scr[:, cols])

    def softmax(i):
        cols = slice(i * blk, (i + 1) * blk)
        s_buf, p_buf = s_scr.at[i % 2], p_scr.at[i % 2]
        selneg = None
        if i > MOBA_TOP_K:
            g = gt_scr[:, cols]
            rank = jnp.zeros((nbp, blk), F32)
            for j in range(i):
                gj = g[j:j + 1, :]
                rank = rank + ((gj > g) | ((gj == g) & (j < blk_id))).astype(F32)
            selneg = jnp.where(rank < MOBA_TOP_K, 0.0, NEG_INF)
        m = jnp.full((1, blk), NEG_INF, F32)
        for n in range(i + 1):
            rows = slice(n * blk, (n + 1) * blk)
            kind = 0 if n == i else (1 if n == i - 1 else 2)
            st = s_buf[rows, :] + bias_ref[0, kind]
            if n == i:
                st = jnp.where(causal, st, NEG_INF)
            elif selneg is not None:
                st = st + selneg[n:n + 1, :]
            s_buf[rows, :] = st
            m = jnp.maximum(m, jnp.max(st, axis=0, keepdims=True))
        for n in range(i + 1):
            rows = slice(n * blk, (n + 1) * blk)
            p_buf[rows, :] = jnp.exp(s_buf[rows, :] - m).astype(BF16)

    def values(i):
        cols = slice(i * blk, (i + 1) * blk)
        nd = _dot(vt_scr[:, 0:(i + 1) * blk], p_scr[i % 2, 0:(i + 1) * blk, :])
        out = nd[0:A_HEAD_DIM, :] / nd[A_HEAD_DIM:A_HEAD_DIM + 1, :]
        o_ref[cols, :] = out.T.astype(o_ref.dtype)

    scores(0)
    for i in range(nb):
        if i + 1 < nb:
            scores(i + 1)
        if i > 0:
            values(i - 1)
        softmax(i)
    values(nb - 1)


def _moba_prompt(aq, ak, av, bias_tiles, w_up, w_down, w_ple_gate, w_out, batch, seq):
    assert seq % MOBA_BLOCK == 0
    nb = seq // MOBA_BLOCK
    assert nb <= 128
    nbp = -(-nb // 8) * 8
    steps = batch * A_HEADS
    slab = D_FF // steps
    gate_rows = D_MODEL // steps
    assert D_FF % steps == 0 and slab % 128 == 0 and D_MODEL % steps == 0 and gate_rows % 16 == 0
    head = pl.BlockSpec((seq, A_HEAD_DIM), lambda b, h: (b, h))
    up_slab = pl.BlockSpec((D_MODEL, slab), lambda b, h: (0, b * A_HEADS + h))
    down_slab = pl.BlockSpec((slab, D_MODEL), lambda b, h: (b * A_HEADS + h, 0))
    gate_slab = pl.BlockSpec((gate_rows, D_MODEL), lambda b, h: (b * A_HEADS + h, 0))
    kern = functools.partial(_moba_prompt_kernel, n_blocks=nb)
    return pl.pallas_call(
        kern,
        grid=(batch, A_HEADS),
        in_specs=[head, head, head,
                  pl.BlockSpec((1, 3, MOBA_BLOCK, MOBA_BLOCK), lambda b, h: (h, 0, 0, 0)),
                  up_slab, down_slab, gate_slab, gate_slab],
        out_specs=[head, up_slab, down_slab, gate_slab, gate_slab],
        out_shape=[jax.ShapeDtypeStruct((batch * seq, A_HEADS * A_HEAD_DIM), BF16),
                   jax.ShapeDtypeStruct((D_MODEL, D_FF), BF16),
                   jax.ShapeDtypeStruct((D_FF, D_MODEL), BF16),
                   jax.ShapeDtypeStruct((D_MODEL, D_MODEL), BF16),
                   jax.ShapeDtypeStruct((D_MODEL, D_MODEL), BF16)],
        scratch_shapes=[pltpu.VMEM((seq, A_HEAD_DIM), BF16),
                        pltpu.VMEM((A_HEAD_DIM, seq), BF16),
                        pltpu.VMEM((_VT_ROWS, seq), BF16),
                        pltpu.VMEM((nbp, seq), F32),
                        pltpu.VMEM((2, seq, MOBA_BLOCK), F32),
                        pltpu.VMEM((2, seq, MOBA_BLOCK), BF16)],
        compiler_params=_cparams("parallel", "parallel"),
        name="moba_prompt",
    )(aq, ak, av, bias_tiles, w_up, w_down, w_ple_gate, w_out)


PAGES_PER_BLOCK = MOBA_BLOCK // PAGE_SIZE
PAST_BLOCKS_PER_STEP = 8


def _moba_past_kernel(pt_ref, q_ref, *refs, n_pages_step):
    del pt_ref
    k_refs = refs[:n_pages_step]
    v_refs = refs[n_pages_step:2 * n_pages_step]
    bias_ref, o_ref, m_ref, l_ref, km_ref = refs[2 * n_pages_step:]
    hq = q_ref.shape[1]
    nq = hq // A_HEADS
    qb = (q_ref[0] * (A_HEAD_DIM ** -0.5)).reshape(A_HEADS, nq, A_HEAD_DIM).astype(BF16)
    n_blocks = n_pages_step // PAGES_PER_BLOCK
    is_last_step = pl.program_id(1) == pl.num_programs(1) - 1

    def head_rows(ref, h):
        return ref[0, 0, pl.ds(h, PAGE_SIZE, stride=A_HEADS), :]

    for g in range(n_blocks):
        pages = range(g * PAGES_PER_BLOCK, (g + 1) * PAGES_PER_BLOCK)
        ksum = sum(jnp.sum(k_refs[p][0, 0].reshape(PAGE_SIZE, A_HEADS, A_HEAD_DIM), axis=0) for p in pages)
        km_ref[0, g] = ksum / MOBA_BLOCK
        kind = jnp.where(is_last_step, 1, 0) if g == n_blocks - 1 else 0
        kh = jnp.stack([jnp.concatenate([head_rows(k_refs[p], h) for p in pages], axis=0)
                        for h in range(A_HEADS)]).astype(BF16)
        vh = jnp.stack([jnp.concatenate([head_rows(v_refs[p], h) for p in pages], axis=0)
                        for h in range(A_HEADS)]).astype(BF16)
        lt = _bdot_nt(qb, kh) + bias_ref[kind]
        m = jnp.max(lt, axis=2, keepdims=True)
        p_ = jnp.exp(lt - m)
        l = jnp.sum(p_, axis=2, keepdims=True)
        o_ref[0, g] = _bdot(p_.astype(BF16), vh).reshape(hq, A_HEAD_DIM)
        m_ref[0, g] = jnp.broadcast_to(m, (A_HEADS, nq, A_HEAD_DIM)).reshape(hq, A_HEAD_DIM)
        l_ref[0, g] = jnp.broadcast_to(l, (A_HEADS, nq, A_HEAD_DIM)).reshape(hq, A_HEAD_DIM)


def _moba_past(page_table, q_hq, cache_k, cache_v, past_bias, layer):
    bs, n_pages = page_table.shape
    assert n_pages % PAGES_PER_BLOCK == 0
    nbp = n_pages // PAGES_PER_BLOCK
    gb = math.gcd(nbp, PAST_BLOCKS_PER_STEP)
    pps = gb * PAGES_PER_BLOCK
    hq = q_hq.shape[1]
    page = (1, 1, PAGE_SIZE * A_HEADS, A_HEAD_DIM)
    cache_k, cache_v = (c.reshape(c.shape[:2] + page[2:]) for c in (cache_k, cache_v))
    part = pl.BlockSpec((1, gb, hq, A_HEAD_DIM), lambda b, n, pt: (b, n, 0, 0))

    def page_spec(p):
        return pl.BlockSpec(page, lambda b, n, pt: (layer, pt[b, pps * n + p], 0, 0))

    grid_spec = pltpu.PrefetchScalarGridSpec(
        num_scalar_prefetch=1,
        grid=(bs, nbp // gb),
        in_specs=[pl.BlockSpec((1, hq, A_HEAD_DIM), lambda b, n, pt: (b, 0, 0))]
        + [page_spec(p) for p in range(pps)] * 2
        + [pl.BlockSpec((2, A_HEADS, hq // A_HEADS, MOBA_BLOCK), lambda b, n, pt: (0, 0, 0, 0))],
        out_specs=[part, part, part,
                   pl.BlockSpec((1, gb, A_HEADS, A_HEAD_DIM), lambda b, n, pt: (b, n, 0, 0))],
    )
    pshape = jax.ShapeDtypeStruct((bs, nbp, hq, A_HEAD_DIM), F32)
    return pl.pallas_call(
        functools.partial(_moba_past_kernel, n_pages_step=pps),
        grid_spec=grid_spec,
        out_shape=[pshape, pshape, pshape, jax.ShapeDtypeStruct((bs, nbp, A_HEADS, A_HEAD_DIM), F32)],
        compiler_params=_cparams("parallel", "parallel"),
        name="moba_past_blocks",
    )(page_table, q_hq, *([cache_k] * pps), *([cache_v] * pps), past_bias)


def _moba_combine_kernel(q_ref, km_ref, kn_ref, vn_ref, ob_ref, op_ref, mp_ref, lp_ref, out_ref):
    nbp = km_ref.shape[2]
    q = q_ref[0] * (A_HEAD_DIM ** -0.5)
    hq = q.shape[0]
    nq = hq // A_HEADS
    gate = jnp.concatenate(
        [_dot_nt(q[h * nq:(h + 1) * nq], km_ref[0, h], precision=HIGHEST) for h in range(A_HEADS)], axis=0)
    lane = lax.broadcasted_iota(jnp.int32, (hq, nbp), 1)
    rank = jnp.zeros((hq, nbp), F32)
    for j in range(nbp):
        gj = gate[:, j:j + 1]
        beats = (gj > gate) | ((gj == gate) & (j < lane))
        rank = rank + beats.astype(F32)
    selneg = jnp.where(rank < MOBA_TOP_K, 0.0, NEG_INF)
    qb = q.astype(BF16)
    lt = _dot_nt(qb, kn_ref[0].astype(BF16))
    r = lax.broadcasted_iota(jnp.int32, lt.shape, 0)
    c = lax.broadcasted_iota(jnp.int32, lt.shape, 1)
    ok = ((c % A_HEADS) == (r // nq)) & ((c // A_HEADS) <= (r % nq))
    lo = jnp.where(ok, lt + ob_ref[...], NEG_INF)
    m = jnp.max(lo, axis=1, keepdims=True)
    for n in range(nbp):
        m = jnp.maximum(m, mp_ref[0, n][:, 0:1] + selneg[:, n:n + 1])
    p = jnp.exp(lo - m)
    l = jnp.sum(p, axis=1, keepdims=True)
    acc = _dot(p.astype(BF16), vn_ref[0].astype(BF16))
    for n in range(nbp):
        w = jnp.exp(mp_ref[0, n][:, 0:1] + selneg[:, n:n + 1] - m)
        l = l + w * lp_ref[0, n][:, 0:1]
        acc = acc + w * op_ref[0, n]
    out_ref[0] = acc / l


def _moba_combine(q_hq, kmean_hn, k_new, v_new, own_bias, o_part, m_part, l_part):
    bs, hq, _ = q_hq.shape
    nbp = o_part.shape[1]
    part = pl.BlockSpec((1, nbp, hq, A_HEAD_DIM), lambda b: (b, 0, 0, 0))
    tok = pl.BlockSpec((1, hq, A_HEAD_DIM), lambda b: (b, 0, 0))
    return pl.pallas_call(
        _moba_combine_kernel,
        grid=(bs,),
        in_specs=[tok,
                  pl.BlockSpec((1, A_HEADS, nbp, A_HEAD_DIM), lambda b: (b, 0, 0, 0)),
                  tok, tok,
                  pl.BlockSpec((hq, hq), lambda b: (0, 0)),
                  part, part, part],
        out_specs=tok,
        out_shape=jax.ShapeDtypeStruct((bs, hq, A_HEAD_DIM), F32),
        compiler_params=_cparams("parallel"),
        name="moba_combine",
    )(q_hq, kmean_hn, k_new, v_new, own_bias, o_part, m_part, l_part)


def _out_proj_kernel(h_ref, om_ref, oa_ref, hs_ref, oms_ref, oas_ref, g_ref, w_ref, o_ref, xn_ref, os_ref, xns_ref):
    half = om_ref.shape[1]

    def project(h, om, oa, dst, xn_dst):
        r = (h[...] + _dot(om[...].astype(BF16), w_ref[0:half, :])
             + _dot(oa[...].astype(BF16), w_ref[half:, :]))
        dst[...] = r
        xn_dst[...] = _rms(r, g_ref[...]).astype(BF16)

    project(h_ref, om_ref, oa_ref, o_ref, xn_ref)

    @pl.when(pl.program_id(0) == 0)
    def _():
        project(hs_ref, oms_ref, oas_ref, os_ref, xns_ref)


def _out_proj(h, out_m, out_a, hs, out_ms, out_as, g_ffn, w_out_bf16):
    m, ms = h.shape[0], hs.shape[0]
    tm = min(512, m)
    assert m % tm == 0
    half = M_HEADS * M_V_DIM
    rows = pl.BlockSpec((tm, D_MODEL), lambda i: (i, 0))
    mix = pl.BlockSpec((tm, half), lambda i: (i, 0))
    rows_s = pl.BlockSpec((ms, D_MODEL), lambda i: (0, 0))
    mix_s = pl.BlockSpec((ms, half), lambda i: (0, 0))
    return pl.pallas_call(
        _out_proj_kernel,
        grid=(m // tm,),
        in_specs=[
            rows, mix, mix, rows_s, mix_s, mix_s,
            pl.BlockSpec((1, D_MODEL), lambda i: (0, 0)),
            pl.BlockSpec((D_MODEL, D_MODEL), lambda i: (0, 0), pipeline_mode=pl.Buffered(1)),
        ],
        out_specs=[rows, rows, rows_s, rows_s],
        out_shape=[jax.ShapeDtypeStruct((m, D_MODEL), F32), jax.ShapeDtypeStruct((m, D_MODEL), BF16),
                   jax.ShapeDtypeStruct((ms, D_MODEL), F32), jax.ShapeDtypeStruct((ms, D_MODEL), BF16)],
        compiler_params=_cparams("arbitrary"),
        name="out_proj",
    )(h, out_m, out_a, hs, out_ms, out_as, g_ffn, w_out_bf16)


def _mlp_kernel(xn_ref, xns_ref, wu_ref, wd_ref, o_ref, os_ref):
    f = pl.program_id(1)

    def accumulate(xn, dst):
        @pl.when(f == 0)
        def _():
            dst[...] = jnp.zeros(dst.shape, F32)

        u = _dot(xn[...], wu_ref[...])
        a = jnp.square(jnp.maximum(u, 0.0)).astype(BF16)
        dst[...] += _dot(a, wd_ref[...])

    accumulate(xn_ref, o_ref)

    @pl.when(pl.program_id(0) == 0)
    def _():
        accumulate(xns_ref, os_ref)


def _mlp(xn, xns, w_up, w_down):
    m, ms = xn.shape[0], xns.shape[0]
    tm = 1024 if m % 1024 == 0 else min(512, m)
    assert m % tm == 0
    tf = 1024
    return pl.pallas_call(
        _mlp_kernel,
        grid=(m // tm, D_FF // tf),
        in_specs=[
            pl.BlockSpec((tm, D_MODEL), lambda i, f: (i, 0)),
            pl.BlockSpec((ms, D_MODEL), lambda i, f: (0, 0)),
            pl.BlockSpec((D_MODEL, tf), lambda i, f: (0, f)),
            pl.BlockSpec((tf, D_MODEL), lambda i, f: (f, 0)),
        ],
        out_specs=[pl.BlockSpec((tm, D_MODEL), lambda i, f: (i, 0)),
                   pl.BlockSpec((ms, D_MODEL), lambda i, f: (0, 0))],
        out_shape=[jax.ShapeDtypeStruct((m, D_MODEL), F32), jax.ShapeDtypeStruct((ms, D_MODEL), F32)],
        compiler_params=_cparams("arbitrary", "arbitrary"),
        name="mlp",
    )(xn, xns, w_up, w_down)


def _ple_kernel(h_ref, d_ref, pe_ref, hs_ref, ds_ref, pes_ref, g_ref, wg_ref, wp_ref, gf_ref, y_ref, ys_ref):
    def gated_embedding(src, delta, pe, dst):
        h = src[...] + delta[...]
        gate = jax.nn.sigmoid(_dot(_rms(h, g_ref[...]).astype(BF16), wg_ref[...]))
        proj = _dot(pe[...].astype(BF16), wp_ref[...])
        dst[...] = _rms(h + gate * proj, gf_ref[...])

    gated_embedding(h_ref, d_ref, pe_ref, y_ref)

    @pl.when(pl.program_id(0) == 0)
    def _():
        gated_embedding(hs_ref, ds_ref, pes_ref, ys_ref)


def _ple_final(h, delta, pe, hs, deltas, pes, g_ple, w_gate, w_proj, g_final):
    m, ms = h.shape[0], hs.shape[0]
    tm = min(512, m)
    assert m % tm == 0
    vec = pl.BlockSpec((1, D_MODEL), lambda i: (0, 0))
    rows = pl.BlockSpec((tm, D_MODEL), lambda i: (i, 0))
    rows_s = pl.BlockSpec((ms, D_MODEL), lambda i: (0, 0))
    return pl.pallas_call(
        _ple_kernel,
        grid=(m // tm,),
        in_specs=[
            rows, rows,
            pl.BlockSpec((tm, PLE_DIM), lambda i: (i, 0)),
            rows_s, rows_s,
            pl.BlockSpec((ms, PLE_DIM), lambda i: (0, 0)),
            vec,
            pl.BlockSpec((D_MODEL, D_MODEL), lambda i: (0, 0), pipeline_mode=pl.Buffered(1)),
            pl.BlockSpec((PLE_DIM, D_MODEL), lambda i: (0, 0), pipeline_mode=pl.Buffered(1)),
            vec,
        ],
        out_specs=[rows, rows_s],
        out_shape=[jax.ShapeDtypeStruct((m, D_MODEL), F32), jax.ShapeDtypeStruct((ms, D_MODEL), F32)],
        compiler_params=_cparams("arbitrary"),
        name="ple_final_norm",
    )(h, delta, pe, hs, deltas, pes, g_ple, w_gate, w_proj, g_final)


def _layer_weights(w_in, b_igate, b_fgate, g_mix, g_mhead, w_out, g_ffn, w_up, w_down,
                   g_ple, w_ple_gate, w_ple_proj):
    gate_bias = jnp.pad(jnp.concatenate([b_igate, b_fgate]), (0, GATE_PAD - 2 * M_HEADS)).reshape(1, GATE_PAD)
    return dict(
        w_in=w_in.T, gate_bias=gate_bias.astype(F32),
        g_mix=g_mix.reshape(1, D_MODEL), g_mhead=g_mhead, w_out=w_out,
        g_ffn=g_ffn.reshape(1, D_MODEL), w_up=w_up, w_down=w_down,
        g_ple=g_ple.reshape(1, D_MODEL), w_ple_gate=w_ple_gate,
        w_ple_proj=w_ple_proj.astype(BF16))


def _tail(h, out_m, out_a, pe, hs, out_ms, out_as, pes, w, g_final):
    h, xn, hs, xns = _out_proj(h, out_m, out_a, hs, out_ms, out_as, w["g_ffn"], w["w_out_bf16"])
    delta, deltas = _mlp(xn, xns, w["w_up_bf16"], w["w_down_bf16"])
    return _ple_final(h, delta, pe, hs, deltas, pes, w["g_ple"], w["w_ple_gate_bf16"], w["w_ple_proj"], g_final)


def _split_state(c_aug, m_fin):
    return c_aug[..., :M_V_DIM], c_aug[..., M_V_DIM], m_fin[..., 0, 0]


def kernel(x_prompt, x_sample, cache_k, cache_v, state_C, state_n, state_m, page_table, p_prompt, p_sample, rel_bias_table, w_in, b_igate, b_fgate, g_mix, g_mhead, w_out, g_ffn, w_up, w_down, g_ple, w_ple_gate, w_ple_proj, g_final):
    depth = w_in.shape[0]
    assert depth == 1, "one decoder layer per call"
    layer = 0
    bp, tp, _ = x_prompt.shape
    bs, ts, _ = x_sample.shape
    past_len = page_table.shape[1] * PAGE_SIZE
    assert past_len % MOBA_BLOCK == 0 and ts <= MOBA_BLOCK
    kv_shape = (A_HEADS, A_HEAD_DIM)
    g_fin = g_final.reshape(1, D_MODEL)

    w = _layer_weights(w_in[layer], b_igate[layer], b_fgate[layer], g_mix[layer], g_mhead[layer], w_out[layer],
                       g_ffn[layer], w_up[layer], w_down[layer], g_ple[layer], w_ple_gate[layer],
                       w_ple_proj[layer])
    bias_tiles = _bias_tiles(rel_bias_table)

    hp = x_prompt.reshape(bp * tp, D_MODEL)
    hs = x_sample.reshape(bs * ts, D_MODEL)
    xn_p, gp, xn_s, gs = _norm_gate(hp, hs, w["g_mix"], w["w_in"])
    zp, aq_p, ak_p, av_p, zs, zs_a = _in_proj(xn_p, xn_s, w["w_in"])
    aq_s, ak_s, av_s = (zs_a[:, t * _IN_TN:(t + 1) * _IN_TN] for t in range(3))
    c0 = jnp.zeros((bp, M_HEADS, M_QK_DIM, 2 * M_V_DIM), F32)
    m0 = jnp.zeros((bp, M_HEADS, 1, GATE_PAD), F32)
    om_p, c_p, m_p = _mlstm(zp, gp, w["gate_bias"], w["g_mhead"], c0, m0, bp, tp)
    oa_p, w["w_up_bf16"], w["w_down_bf16"], w["w_ple_gate_bf16"], w["w_out_bf16"] = _moba_prompt(
        aq_p, ak_p, av_p, bias_tiles, w["w_up"], w["w_down"], w["w_ple_gate"], w["w_out"], bp, tp)
    k_p = ak_p.reshape((1, bp, tp) + kv_shape)
    v_p = av_p.reshape((1, bp, tp) + kv_shape)
    cp, np_, mp = _split_state(c_p, m_p)

    c0s = jnp.concatenate([state_C[layer], state_n[layer][..., None],
                           jnp.zeros((bs, M_HEADS, M_QK_DIM, M_V_DIM - 1), F32)], axis=-1)
    m0s = jnp.broadcast_to(state_m[layer][..., None, None], (bs, M_HEADS, 1, GATE_PAD))
    om_s, c_s, m_s = _mlstm(zs, gs, w["gate_bias"], w["g_mhead"], c0s, m0s, bs, ts)

    hq = A_HEADS * ts
    q_s = aq_s.reshape(bs, ts, A_HEADS, A_HEAD_DIM)
    q_hq = q_s.transpose(0, 2, 1, 3).reshape(bs, hq, A_HEAD_DIM)
    k_new = ak_s.reshape(bs, hq, A_HEAD_DIM)
    v_new = av_s.reshape(bs, hq, A_HEAD_DIM)
    rows = bias_tiles[:, :, :, :ts].transpose(0, 1, 3, 2)
    expand = lambda a: jnp.repeat(a, A_HEADS, axis=-1).reshape(hq, -1)
    past_bias = jnp.stack([rows[:, 2], rows[:, 1]])
    own_bias = expand(rows[:, 0, :, :ts])
    o_part, m_part, l_part, kmean = _moba_past(page_table, q_hq, cache_k, cache_v, past_bias, layer)
    oa_hq = _moba_combine(q_hq, kmean.transpose(0, 2, 1, 3), k_new, v_new, own_bias, o_part, m_part, l_part)
    oa_s = oa_hq.reshape(bs, A_HEADS, ts, A_HEAD_DIM).transpose(0, 2, 1, 3).reshape(bs * ts, A_HEADS * A_HEAD_DIM)
    y_p, y_s = _tail(hp, om_p, oa_p, p_prompt[layer].reshape(bp * tp, PLE_DIM),
                     hs, om_s, oa_s, p_sample[layer].reshape(bs * ts, PLE_DIM), w, g_fin)
    k_s = k_new.reshape((1, bs, ts) + kv_shape)
    v_s = v_new.reshape((1, bs, ts) + kv_shape)
    cs, ns, ms = _split_state(c_s, m_s)

    return (y_p.reshape(bp, tp, D_MODEL), y_s.reshape(bs, ts, D_MODEL),
            k_p, v_p, cp[None], np_[None], mp[None],
            k_s, v_s, cs[None], ns[None], ms[None])
```

```python
import functools
import math

import jax
import jax.numpy as jnp
from jax import lax
from jax.experimental import pallas as pl
from jax.experimental.pallas import tpu as pltpu

F32 = jnp.float32
BF16 = jnp.bfloat16
HIGHEST = lax.Precision.HIGHEST

D_MODEL = 2048
M_HEADS = 8
M_V_DIM = 128
M_QK_DIM = 64
M_CHUNK = 64
A_HEADS = 8
A_HEAD_DIM = 128
MOBA_BLOCK = 256
MOBA_TOP_K = 3
NUM_BUCKETS = 32
MAX_EXACT = NUM_BUCKETS // 2
MAX_DISTANCE = 128
D_FF = 4 * D_MODEL
PLE_DIM = 256
PAGE_SIZE = 128
EPS = 1e-6

_MQ, _MK, _MV, _MO = 0, 512, 1024, 2048
_GATES = 3072
_AQ = 3088
D_MAIN = 6144
GATE_PAD = 128

VMEM_LIMIT = 56 * 1024 * 1024
NEG_INF = float("-inf")


def _cparams(*sem):
    return pltpu.CompilerParams(dimension_semantics=sem, vmem_limit_bytes=VMEM_LIMIT)


def _rms(x, g):
    return x * lax.rsqrt(jnp.mean(x * x, axis=-1, keepdims=True) + EPS) * g


def _dot_nt(a, b, **kw):
    return lax.dot_general(a, b, (((1,), (1,)), ((), ())), preferred_element_type=F32, **kw)


def _dot_tn(a, b, **kw):
    return lax.dot_general(a, b, (((0,), (0,)), ((), ())), preferred_element_type=F32, **kw)


def _dot(a, b, **kw):
    return jnp.dot(a, b, preferred_element_type=F32, **kw)


_IN_TN = 1024
_N_IN_TILES = D_MAIN // _IN_TN
_N_MLSTM_TILES = _GATES // _IN_TN
_GATE_COLS = _AQ - _GATES
_CAST_ROWS = 256


def _norm_gate_kernel(x_ref, xs_ref, g_ref, wg_ref, xn_ref, gate_ref, xns_ref, gates_ref):
    wg = wg_ref[...]
    hi = wg.astype(BF16).astype(F32)
    row = lax.broadcasted_iota(jnp.int32, wg.shape, 0)
    w2 = jnp.where(row < _GATE_COLS, hi,
                   jnp.where(row < 2 * _GATE_COLS, pltpu.roll(wg - hi, _GATE_COLS, axis=0), 0.0))
    w2 = w2.astype(BF16)

    def norm_and_gate(src_ref, dst_ref, gate_dst_ref):
        xn = _rms(src_ref[...], g_ref[...]).astype(BF16)
        dst_ref[...] = xn
        r = _dot_nt(xn, w2)
        gate_dst_ref[...] = r + pltpu.roll(r, GATE_PAD - _GATE_COLS, axis=1)

    norm_and_gate(x_ref, xn_ref, gate_ref)

    @pl.when(pl.program_id(0) == 0)
    def _():
        norm_and_gate(xs_ref, xns_ref, gates_ref)


def _norm_gate(x, xs, g, w_in_t):
    m, ms = x.shape[0], xs.shape[0]
    tm = 1024 if m % 1024 == 0 else min(512, m)
    assert m % tm == 0
    return pl.pallas_call(
        _norm_gate_kernel,
        grid=(m // tm,),
        in_specs=[
            pl.BlockSpec((tm, D_MODEL), lambda i: (i, 0)),
            pl.BlockSpec((ms, D_MODEL), lambda i: (0, 0)),
            pl.BlockSpec((1, D_MODEL), lambda i: (0, 0)),
            pl.BlockSpec((GATE_PAD, D_MODEL), lambda i: (_GATES // GATE_PAD, 0)),
        ],
        out_specs=[pl.BlockSpec((tm, D_MODEL), lambda i: (i, 0)),
                   pl.BlockSpec((tm, GATE_PAD), lambda i: (i, 0)),
                   pl.BlockSpec((ms, D_MODEL), lambda i: (0, 0)),
                   pl.BlockSpec((ms, GATE_PAD), lambda i: (0, 0))],
        out_shape=[jax.ShapeDtypeStruct((m, D_MODEL), BF16), jax.ShapeDtypeStruct((m, GATE_PAD), F32),
                   jax.ShapeDtypeStruct((ms, D_MODEL), BF16), jax.ShapeDtypeStruct((ms, GATE_PAD), F32)],
        compiler_params=_cparams("arbitrary"),
        name="norm_gate",
    )(x, xs, g, w_in_t)


def _cast_weight_tile(wa_ref, wb_ref, w_scr, shifted):
    skip = _GATE_COLS if shifted else 0
    body = _IN_TN - skip
    for r in range(0, body, _CAST_ROWS):
        n = min(_CAST_ROWS, body - r)
        w_scr[r:r + n, :] = wa_ref[skip + r:skip + r + n, :].astype(BF16)
    if shifted:
        w_scr[body:, :] = wb_ref[:_GATE_COLS, :].astype(BF16)


def _in_proj_mlstm_kernel(xn_ref, xs_ref, wa_ref, z_ref, zs_ref, w_scr):
    @pl.when(pl.program_id(1) == 0)
    def _():
        _cast_weight_tile(wa_ref, None, w_scr, shifted=False)
        zs_ref[...] = _dot_nt(xs_ref[...], w_scr[...])

    z_ref[...] = _dot_nt(xn_ref[...], w_scr[...])


def _in_proj_moba_kernel(xn_ref, xs_ref, wa_ref, wb_ref, aq_ref, ak_ref, av_ref, zs_ref, w_scr):
    j = pl.program_id(0)

    @pl.when(pl.program_id(1) == 0)
    def _():
        _cast_weight_tile(wa_ref, wb_ref, w_scr, shifted=True)
        zs_ref[...] = _dot_nt(xs_ref[...], w_scr[...])

    for t, ref in enumerate((aq_ref, ak_ref, av_ref)):
        @pl.when(j == t)
        def _(ref=ref):
            ref[...] = _dot_nt(xn_ref[...], w_scr[...])


def _in_proj(xn, xs, w_in_t):
    m, ms = xn.shape[0], xs.shape[0]
    tn = _IN_TN
    nm = _N_MLSTM_TILES
    scratch = [pltpu.VMEM((tn, D_MODEL), BF16)]
    sample_rows = pl.BlockSpec((ms, D_MODEL), lambda j, i: (0, 0))
    sample_out = pl.BlockSpec((ms, tn), lambda j, i: (0, j))

    tm = min(512, m)
    assert m % tm == 0
    zm, zs_m = pl.pallas_call(
        _in_proj_mlstm_kernel,
        grid=(nm, m // tm),
        in_specs=[
            pl.BlockSpec((tm, D_MODEL), lambda j, i: (i, 0)),
            sample_rows,
            pl.BlockSpec((tn, D_MODEL), lambda j, i: (j, 0)),
        ],
        out_specs=[pl.BlockSpec((tm, tn), lambda j, i: (i, j)), sample_out],
        out_shape=[jax.ShapeDtypeStruct((m, _GATES), F32), jax.ShapeDtypeStruct((ms, _GATES), F32)],
        scratch_shapes=scratch,
        compiler_params=_cparams("arbitrary", "arbitrary"),
        name="in_proj_mlstm",
    )(xn, xs, w_in_t)

    tm = min(512, m)
    assert m % tm == 0
    last = m // tm - 1

    def section(t):
        return pl.BlockSpec((tm, tn), lambda j, i: (jnp.where(j < t, 0, jnp.where(j == t, i, last)), 0))

    sec_shape = jax.ShapeDtypeStruct((m, tn), F32)
    aq, ak, av, zs_a = pl.pallas_call(
        _in_proj_moba_kernel,
        grid=(_N_IN_TILES - nm, m // tm),
        in_specs=[
            pl.BlockSpec((tm, D_MODEL), lambda j, i: (i, 0)),
            sample_rows,
            pl.BlockSpec((tn, D_MODEL), lambda j, i: (nm + j, 0)),
            pl.BlockSpec((GATE_PAD, D_MODEL), lambda j, i: ((nm + j + 1) * (tn // GATE_PAD), 0)),
        ],
        out_specs=[section(0), section(1), section(2), sample_out],
        out_shape=[sec_shape, sec_shape, sec_shape, jax.ShapeDtypeStruct((ms, D_MAIN - _GATES), F32)],
        scratch_shapes=scratch,
        compiler_params=_cparams("arbitrary", "arbitrary"),
        name="in_proj_moba",
    )(xn, xs, w_in_t, w_in_t)
    return zm, aq, ak, av, zs_m, zs_a


def _log_sigmoid(x):
    return -(jnp.maximum(-x, 0.0) + jnp.log1p(jnp.exp(-jnp.abs(x))))


def _bdot(a, b):
    return lax.dot_general(a, b, (((2,), (1,)), ((0,), (0,))), preferred_element_type=F32)


def _bdot_nt(a, b):
    return lax.dot_general(a, b, (((2,), (2,)), ((0,), (0,))), preferred_element_type=F32)


def _bdot_tn(a, b):
    return lax.dot_general(a, b, (((1,), (1,)), ((0,), (0,))), preferred_element_type=F32)


def _mlstm_kernel(q_ref, k_ref, v_ref, o_ref, g_ref, gb_ref, gh_ref, c0_ref, m0_ref,
                  out_ref, cfin_ref, mfin_ref, c_scr, m_scr, *, chunk, n_chunks):
    L = chunk
    H = M_HEADS
    tb = L * n_chunks
    t = pl.program_id(1)

    @pl.when(t == 0)
    def _():
        c_scr[...] = c0_ref[0]
        m_scr[...] = m0_ref[0]

    g = g_ref[...] + gb_ref[...]
    lane = lax.broadcasted_iota(jnp.int32, (tb, GATE_PAD), 1)
    gl = jnp.where((lane >= H) & (lane < 2 * H), _log_sigmoid(g), g)
    r = lax.broadcasted_iota(jnp.int32, (tb, tb), 0)
    c = lax.broadcasted_iota(jnp.int32, (tb, tb), 1)
    tri = ((r // L == c // L) & (c <= r)).astype(F32)
    cum = _dot(tri, gl, precision=HIGHEST)

    def rows(ci):
        return slice(ci * L, (ci + 1) * L)

    def stack(f):
        return jnp.stack([f(ci, h) for ci in range(n_chunks) for h in range(H)])

    ones_col = (lax.broadcasted_iota(jnp.int32, (L, M_V_DIM), 1) == 0).astype(BF16)
    li = stack(lambda ci, h: gl[rows(ci), h:h + 1])
    b = stack(lambda ci, h: cum[rows(ci), H + h:H + h + 1])
    qb = stack(lambda ci, h: q_ref[rows(ci), h * M_QK_DIM:(h + 1) * M_QK_DIM]).astype(BF16)
    kc = stack(lambda ci, h: k_ref[rows(ci), h * M_QK_DIM:(h + 1) * M_QK_DIM]) * (M_QK_DIM ** -0.5)
    vaug = stack(lambda ci, h: jnp.concatenate(
        [v_ref[rows(ci), h * M_V_DIM:(h + 1) * M_V_DIM].astype(BF16), ones_col], axis=1))

    rl = lax.broadcasted_iota(jnp.int32, (L, L), 0)
    cl = lax.broadcasted_iota(jnp.int32, (L, L), 1)
    w_row = jnp.sum(jnp.where(rl == cl, li - b, 0.0), axis=1, keepdims=True)
    d = jnp.where(cl <= rl, b + w_row, NEG_INF)
    m_loc = jnp.max(d, axis=2, keepdims=True)
    s = _bdot_nt(qb, kc.astype(BF16)) * jnp.exp(d - m_loc)
    nd_loc = _bdot(s.astype(BF16), vaug)
    g_last = m_loc[:, L - 1:L, :]
    b_last = b[:, L - 1:L, :]
    ws = jnp.exp(b_last - b + li - g_last)
    dc_loc = _bdot_tn((kc * ws).astype(BF16), vaug)

    m_prev = m_scr[:, :, 0:1]
    c_aug = c_scr[...]
    for ci in range(n_chunks):
        grp = slice(ci * H, (ci + 1) * H)
        a = b[grp] + m_prev
        m_t = jnp.maximum(a, m_loc[grp])
        nd = (jnp.exp(a - m_t) * _bdot(qb[grp], c_aug.astype(BF16))
              + jnp.exp(m_loc[grp] - m_t) * nd_loc[grp])
        hh = nd[:, :, :M_V_DIM] / jnp.maximum(jnp.abs(nd[:, :, M_V_DIM:M_V_DIM + 1]), jnp.exp(-m_t))
        hn = _rms(hh, gh_ref[...])
        for h in range(H):
            og = jax.nn.sigmoid(o_ref[rows(ci), h * M_V_DIM:(h + 1) * M_V_DIM])
            out_ref[rows(ci), h * M_V_DIM:(h + 1) * M_V_DIM] = (og * hn[h]).astype(out_ref.dtype)
        m_new = jnp.maximum(b_last[grp] + m_prev, g_last[grp])
        c_aug = (jnp.exp(b_last[grp] + m_prev - m_new) * c_aug
                 + jnp.exp(g_last[grp] - m_new) * dc_loc[grp])
        m_prev = m_new
    c_scr[...] = c_aug
    m_scr[...] = jnp.broadcast_to(m_prev, m_scr.shape)

    @pl.when(t == pl.num_programs(1) - 1)
    def _():
        cfin_ref[0] = c_scr[...]
        mfin_ref[0] = m_scr[...]


def _mlstm(z, gates, gate_bias, g_mhead, c0_aug, m0, batch, seq):
    L = math.gcd(seq, M_CHUNK)
    n_chunks = max(1, min(seq // L, 256 // L))
    tb = L * n_chunks
    nt = seq // tb
    row = lambda b, t: b * nt + t
    kern = functools.partial(_mlstm_kernel, chunk=L, n_chunks=n_chunks)
    return pl.pallas_call(
        kern,
        grid=(batch, nt),
        in_specs=[
            pl.BlockSpec((tb, 512), lambda b, t: (row(b, t), _MQ // 512)),
            pl.BlockSpec((tb, 512), lambda b, t: (row(b, t), _MK // 512)),
            pl.BlockSpec((tb, 1024), lambda b, t: (row(b, t), _MV // 1024)),
            pl.BlockSpec((tb, 1024), lambda b, t: (row(b, t), _MO // 1024)),
            pl.BlockSpec((tb, GATE_PAD), lambda b, t: (row(b, t), 0)),
            pl.BlockSpec((1, GATE_PAD), lambda b, t: (0, 0)),
            pl.BlockSpec((M_HEADS, 1, M_V_DIM), lambda b, t: (0, 0, 0)),
            pl.BlockSpec((1, M_HEADS, M_QK_DIM, 2 * M_V_DIM), lambda b, t: (b, 0, 0, 0)),
            pl.BlockSpec((1, M_HEADS, 1, GATE_PAD), lambda b, t: (b, 0, 0, 0)),
        ],
        out_specs=[
            pl.BlockSpec((tb, M_HEADS * M_V_DIM), lambda b, t: (row(b, t), 0)),
            pl.BlockSpec((1, M_HEADS, M_QK_DIM, 2 * M_V_DIM), lambda b, t: (b, 0, 0, 0)),
            pl.BlockSpec((1, M_HEADS, 1, GATE_PAD), lambda b, t: (b, 0, 0, 0)),
        ],
        out_shape=[
            jax.ShapeDtypeStruct((batch * seq, M_HEADS * M_V_DIM), BF16 if tb % 16 == 0 else F32),
            jax.ShapeDtypeStruct((batch, M_HEADS, M_QK_DIM, 2 * M_V_DIM), F32),
            jax.ShapeDtypeStruct((batch, M_HEADS, 1, GATE_PAD), F32),
        ],
        scratch_shapes=[pltpu.VMEM((M_HEADS, M_QK_DIM, 2 * M_V_DIM), F32),
                        pltpu.VMEM((M_HEADS, 1, GATE_PAD), F32)],
        compiler_params=_cparams("parallel", "arbitrary"),
        name="mlstm",
    )(z, z, z, z, gates, gate_bias, g_mhead.reshape(M_HEADS, 1, M_V_DIM), c0_aug, m0)


def _t5_bucket(rel):
    n = jnp.maximum(rel, 0)
    nf = jnp.maximum(n, 1).astype(F32)
    large = MAX_EXACT + (jnp.log(nf / MAX_EXACT) / math.log(MAX_DISTANCE / MAX_EXACT)
                         * (NUM_BUCKETS - MAX_EXACT)).astype(jnp.int32)
    large = jnp.minimum(large, NUM_BUCKETS - 1)
    return jnp.where(n < MAX_EXACT, n, large)


def _bias_kernel(tab_ref, out_ref):
    h = pl.program_id(0)
    i = lax.broadcasted_iota(jnp.int32, (MOBA_BLOCK, MOBA_BLOCK), 1)
    j = lax.broadcasted_iota(jnp.int32, (MOBA_BLOCK, MOBA_BLOCK), 0)
    for kind, rel in ((0, i - j), (1, MOBA_BLOCK + i - j)):
        bucket = _t5_bucket(rel)
        vals = [tab_ref[h, b] for b in range(NUM_BUCKETS)]
        bit = 1
        while len(vals) > 1:
            odd = (bucket & bit) != 0
            vals = [jnp.where(odd, vals[2 * m + 1], vals[2 * m]) for m in range(len(vals) // 2)]
            bit *= 2
        out_ref[0, kind] = vals[0]
    out_ref[0, 2] = jnp.full((MOBA_BLOCK, MOBA_BLOCK), tab_ref[h, NUM_BUCKETS - 1], F32)


def _bias_tiles(rel_table):
    assert MOBA_BLOCK + 1 >= MAX_DISTANCE and NUM_BUCKETS & (NUM_BUCKETS - 1) == 0
    tab = rel_table.T.astype(F32)
    return pl.pallas_call(
        _bias_kernel,
        grid=(A_HEADS,),
        in_specs=[pl.BlockSpec(memory_space=pltpu.SMEM)],
        out_specs=pl.BlockSpec((1, 3, MOBA_BLOCK, MOBA_BLOCK), lambda h: (h, 0, 0, 0)),
        out_shape=jax.ShapeDtypeStruct((A_HEADS, 3, MOBA_BLOCK, MOBA_BLOCK), F32),
        compiler_params=_cparams("parallel"),
        name="t5_bias_tiles",
    )(tab)


_VT_ROWS = A_HEAD_DIM + 16


def _moba_prompt_kernel(q_ref, k_ref, v_ref, bias_ref, wu_ref, wd_ref, wg_ref, wo_ref,
                        o_ref, wub_ref, wdb_ref, wgb_ref, wob_ref,
                        kb_scr, qt_scr, vt_scr, gt_scr, s_scr, p_scr, *, n_blocks):
    nb = n_blocks
    blk = MOBA_BLOCK
    nbp = gt_scr.shape[0]
    wub_ref[...] = wu_ref[...].astype(BF16)
    wdb_ref[...] = wd_ref[...].astype(BF16)
    wgb_ref[...] = wg_ref[...].astype(BF16)
    wob_ref[...] = wo_ref[...].astype(BF16)

    kb_scr[...] = k_ref[...].astype(BF16)
    for c in range(nb):
        cols = slice(c * blk, (c + 1) * blk)
        qt_scr[:, cols] = (q_ref[cols, :] * (A_HEAD_DIM ** -0.5)).T.astype(BF16)
        vt_scr[0:A_HEAD_DIM, cols] = v_ref[cols, :].T.astype(BF16)
    ones_row = (lax.broadcasted_iota(jnp.int32, (_VT_ROWS - A_HEAD_DIM, nb * blk), 0) == 0).astype(BF16)
    vt_scr[A_HEAD_DIM:, :] = ones_row
    need_gate = nb - 1 > MOBA_TOP_K
    if need_gate:
        means = [jnp.mean(k_ref[n * blk:(n + 1) * blk, :], axis=0, keepdims=True) for n in range(nb)]
        kmean = jnp.concatenate(means + [jnp.zeros((128 - nb, A_HEAD_DIM), F32)], axis=0)
        for c in range(MOBA_TOP_K + 1, nb):
            cols = slice(c * blk, (c + 1) * blk)
            gate = _dot_nt(q_ref[cols, :] * (A_HEAD_DIM ** -0.5), kmean, precision=HIGHEST)
            gt_scr[:, cols] = gate.T[:nbp, :]
    key = lax.broadcasted_iota(jnp.int32, (blk, blk), 0)
    qry = lax.broadcasted_iota(jnp.int32, (blk, blk), 1)
    causal = key <= qry
    blk_id = lax.broadcasted_iota(jnp.int32, (nbp, blk), 0)

    def scores(i):
        cols = slice(i * blk, (i + 1) * blk)
        s_scr[i % 2, 0:(i + 1) * blk, :] = _dot(kb_scr[0:(i + 1) * blk, :], qt_scr[:, cols])

    def softmax(i):
        cols = slice(i * blk, (i + 1) * blk)
        s_buf, p_buf = s_scr.at[i % 2], p_scr.at[i % 2]
        selneg = None
        if i > MOBA_TOP_K:
            g = gt_scr[:, cols]
            rank = jnp.zeros((nbp, blk), F32)
            for j in range(i):
                gj = g[j:j + 1, :]
                rank = rank + ((gj > g) | ((gj == g) & (j < blk_id))).astype(F32)
            selneg = jnp.where(rank < MOBA_TOP_K, 0.0, NEG_INF)
        m = jnp.full((1, blk), NEG_INF, F32)
        for n in range(i + 1):
            rows = slice(n * blk, (n + 1) * blk)
            kind = 0 if n == i else (1 if n == i - 1 else 2)
            st = s_buf[rows, :] + bias_ref[0, kind]
            if n == i:
                st = jnp.where(causal, st, NEG_INF)
            elif selneg is not None:
                st = st + selneg[n:n + 1, :]
            s_buf[rows, :] = st
            m = jnp.maximum(m, jnp.max(st, axis=0, keepdims=True))
        for n in range(i + 1):
            rows = slice(n * blk, (n + 1) * blk)
            p_buf[rows, :] = jnp.exp(s_buf[rows, :] - m).astype(BF16)

    def values(i):
        cols = slice(i * blk, (i + 1) * blk)
        nd = _dot(vt_scr[:, 0:(i + 1) * blk], p_scr[i % 2, 0:(i + 1) * blk, :])
        out = nd[0:A_HEAD_DIM, :] / nd[A_HEAD_DIM:A_HEAD_DIM + 1, :]
        o_ref[cols, :] = out.T.astype(o_ref.dtype)

    scores(0)
    for i in range(nb):
        if i + 1 < nb:
            scores(i + 1)
        if i > 0:
            values(i - 1)
        softmax(i)
    values(nb - 1)


def _moba_prompt(aq, ak, av, bias_tiles, w_up, w_down, w_ple_gate, w_out, batch, seq):
    assert seq % MOBA_BLOCK == 0
    nb = seq // MOBA_BLOCK
    assert nb <= 128
    nbp = -(-nb // 8) * 8
    steps = batch * A_HEADS
    slab = D_FF // steps
    gate_rows = D_MODEL // steps
    assert D_FF % steps == 0 and slab % 128 == 0 and D_MODEL % steps == 0 and gate_rows % 16 == 0
    head = pl.BlockSpec((seq, A_HEAD_DIM), lambda b, h: (b, h))
    up_slab = pl.BlockSpec((D_MODEL, slab), lambda b, h: (0, b * A_HEADS + h))
    down_slab = pl.BlockSpec((slab, D_MODEL), lambda b, h: (b * A_HEADS + h, 0))
    gate_slab = pl.BlockSpec((gate_rows, D_MODEL), lambda b, h: (b * A_HEADS + h, 0))
    kern = functools.partial(_moba_prompt_kernel, n_blocks=nb)
    return pl.pallas_call(
        kern,
        grid=(batch, A_HEADS),
        in_specs=[head, head, head,
                  pl.BlockSpec((1, 3, MOBA_BLOCK, MOBA_BLOCK), lambda b, h: (h, 0, 0, 0)),
                  up_slab, down_slab, gate_slab, gate_slab],
        out_specs=[head, up_slab, down_slab, gate_slab, gate_slab],
        out_shape=[jax.ShapeDtypeStruct((batch * seq, A_HEADS * A_HEAD_DIM), BF16),
                   jax.ShapeDtypeStruct((D_MODEL, D_FF), BF16),
                   jax.ShapeDtypeStruct((D_FF, D_MODEL), BF16),
                   jax.ShapeDtypeStruct((D_MODEL, D_MODEL), BF16),
                   jax.ShapeDtypeStruct((D_MODEL, D_MODEL), BF16)],
        scratch_shapes=[pltpu.VMEM((seq, A_HEAD_DIM), BF16),
                        pltpu.VMEM((A_HEAD_DIM, seq), BF16),
                        pltpu.VMEM((_VT_ROWS, seq), BF16),
                        pltpu.VMEM((nbp, seq), F32),
                        pltpu.VMEM((2, seq, MOBA_BLOCK), F32),
                        pltpu.VMEM((2, seq, MOBA_BLOCK), BF16)],
        compiler_params=_cparams("parallel", "parallel"),
        name="moba_prompt",
    )(aq, ak, av, bias_tiles, w_up, w_down, w_ple_gate, w_out)


PAGES_PER_BLOCK = MOBA_BLOCK // PAGE_SIZE
PAST_BLOCKS_PER_STEP = 8


def _moba_past_kernel(pt_ref, q_ref, *refs, n_pages_step):
    del pt_ref
    k_refs = refs[:n_pages_step]
    v_refs = refs[n_pages_step:2 * n_pages_step]
    bias_ref, o_ref, m_ref, l_ref, km_ref = refs[2 * n_pages_step:]
    hq = q_ref.shape[1]
    nq = hq // A_HEADS
    qb = (q_ref[0] * (A_HEAD_DIM ** -0.5)).reshape(A_HEADS, nq, A_HEAD_DIM).astype(BF16)
    n_blocks = n_pages_step // PAGES_PER_BLOCK
    is_last_step = pl.program_id(1) == pl.num_programs(1) - 1

    def head_rows(ref, h):
        return ref[0, 0, pl.ds(h, PAGE_SIZE, stride=A_HEADS), :]

    for g in range(n_blocks):
        pages = range(g * PAGES_PER_BLOCK, (g + 1) * PAGES_PER_BLOCK)
        ksum = sum(jnp.sum(k_refs[p][0, 0].reshape(PAGE_SIZE, A_HEADS, A_HEAD_DIM), axis=0) for p in pages)
        km_ref[0, g] = ksum / MOBA_BLOCK
        kind = jnp.where(is_last_step, 1, 0) if g == n_blocks - 1 else 0
        kh = jnp.stack([jnp.concatenate([head_rows(k_refs[p], h) for p in pages], axis=0)
                        for h in range(A_HEADS)]).astype(BF16)
        vh = jnp.stack([jnp.concatenate([head_rows(v_refs[p], h) for p in pages], axis=0)
                        for h in range(A_HEADS)]).astype(BF16)
        lt = _bdot_nt(qb, kh) + bias_ref[kind]
        m = jnp.max(lt, axis=2, keepdims=True)
        p_ = jnp.exp(lt - m)
        l = jnp.sum(p_, axis=2, keepdims=True)
        o_ref[0, g] = _bdot(p_.astype(BF16), vh).reshape(hq, A_HEAD_DIM)
        m_ref[0, g] = jnp.broadcast_to(m, (A_HEADS, nq, A_HEAD_DIM)).reshape(hq, A_HEAD_DIM)
        l_ref[0, g] = jnp.broadcast_to(l, (A_HEADS, nq, A_HEAD_DIM)).reshape(hq, A_HEAD_DIM)


def _moba_past(page_table, q_hq, cache_k, cache_v, past_bias, layer):
    bs, n_pages = page_table.shape
    assert n_pages % PAGES_PER_BLOCK == 0
    nbp = n_pages // PAGES_PER_BLOCK
    gb = math.gcd(nbp, PAST_BLOCKS_PER_STEP)
    pps = gb * PAGES_PER_BLOCK
    hq = q_hq.shape[1]
    page = (1, 1, PAGE_SIZE * A_HEADS, A_HEAD_DIM)
    cache_k, cache_v = (c.reshape(c.shape[:2] + page[2:]) for c in (cache_k, cache_v))
    part = pl.BlockSpec((1, gb, hq, A_HEAD_DIM), lambda b, n, pt: (b, n, 0, 0))

    def page_spec(p):
        return pl.BlockSpec(page, lambda b, n, pt: (layer, pt[b, pps * n + p], 0, 0))

    grid_spec = pltpu.PrefetchScalarGridSpec(
        num_scalar_prefetch=1,
        grid=(bs, nbp // gb),
        in_specs=[pl.BlockSpec((1, hq, A_HEAD_DIM), lambda b, n, pt: (b, 0, 0))]
        + [page_spec(p) for p in range(pps)] * 2
        + [pl.BlockSpec((2, A_HEADS, hq // A_HEADS, MOBA_BLOCK), lambda b, n, pt: (0, 0, 0, 0))],
        out_specs=[part, part, part,
                   pl.BlockSpec((1, gb, A_HEADS, A_HEAD_DIM), lambda b, n, pt: (b, n, 0, 0))],
    )
    pshape = jax.ShapeDtypeStruct((bs, nbp, hq, A_HEAD_DIM), F32)
    return pl.pallas_call(
        functools.partial(_moba_past_kernel, n_pages_step=pps),
        grid_spec=grid_spec,
        out_shape=[pshape, pshape, pshape, jax.ShapeDtypeStruct((bs, nbp, A_HEADS, A_HEAD_DIM), F32)],
        compiler_params=_cparams("parallel", "parallel"),
        name="moba_past_blocks",
    )(page_table, q_hq, *([cache_k] * pps), *([cache_v] * pps), past_bias)


def _moba_combine_kernel(q_ref, km_ref, kn_ref, vn_ref, ob_ref, op_ref, mp_ref, lp_ref, out_ref):
    nbp = km_ref.shape[2]
    q = q_ref[0] * (A_HEAD_DIM ** -0.5)
    hq = q.shape[0]
    nq = hq // A_HEADS
    gate = jnp.concatenate(
        [_dot_nt(q[h * nq:(h + 1) * nq], km_ref[0, h], precision=HIGHEST) for h in range(A_HEADS)], axis=0)
    lane = lax.broadcasted_iota(jnp.int32, (hq, nbp), 1)
    rank = jnp.zeros((hq, nbp), F32)
    for j in range(nbp):
        gj = gate[:, j:j + 1]
        beats = (gj > gate) | ((gj == gate) & (j < lane))
        rank = rank + beats.astype(F32)
    selneg = jnp.where(rank < MOBA_TOP_K, 0.0, NEG_INF)
    qb = q.astype(BF16)
    lt = _dot_nt(qb, kn_ref[0].astype(BF16))
    r = lax.broadcasted_iota(jnp.int32, lt.shape, 0)
    c = lax.broadcasted_iota(jnp.int32, lt.shape, 1)
    ok = ((c % A_HEADS) == (r // nq)) & ((c // A_HEADS) <= (r % nq))
    lo = jnp.where(ok, lt + ob_ref[...], NEG_INF)
    m = jnp.max(lo, axis=1, keepdims=True)
    for n in range(nbp):
        m = jnp.maximum(m, mp_ref[0, n][:, 0:1] + selneg[:, n:n + 1])
    p = jnp.exp(lo - m)
    l = jnp.sum(p, axis=1, keepdims=True)
    acc = _dot(p.astype(BF16), vn_ref[0].astype(BF16))
    for n in range(nbp):
        w = jnp.exp(mp_ref[0, n][:, 0:1] + selneg[:, n:n + 1] - m)
        l = l + w * lp_ref[0, n][:, 0:1]
        acc = acc + w * op_ref[0, n]
    out_ref[0] = acc / l


def _moba_combine(q_hq, kmean_hn, k_new, v_new, own_bias, o_part, m_part, l_part):
    bs, hq, _ = q_hq.shape
    nbp = o_part.shape[1]
    part = pl.BlockSpec((1, nbp, hq, A_HEAD_DIM), lambda b: (b, 0, 0, 0))
    tok = pl.BlockSpec((1, hq, A_HEAD_DIM), lambda b: (b, 0, 0))
    return pl.pallas_call(
        _moba_combine_kernel,
        grid=(bs,),
        in_specs=[tok,
                  pl.BlockSpec((1, A_HEADS, nbp, A_HEAD_DIM), lambda b: (b, 0, 0, 0)),
                  tok, tok,
                  pl.BlockSpec((hq, hq), lambda b: (0, 0)),
                  part, part, part],
        out_specs=tok,
        out_shape=jax.ShapeDtypeStruct((bs, hq, A_HEAD_DIM), F32),
        compiler_params=_cparams("parallel"),
        name="moba_combine",
    )(q_hq, kmean_hn, k_new, v_new, own_bias, o_part, m_part, l_part)


def _out_proj_kernel(h_ref, om_ref, oa_ref, hs_ref, oms_ref, oas_ref, g_ref, w_ref, o_ref, xn_ref, os_ref, xns_ref):
    half = om_ref.shape[1]

    def project(h, om, oa, dst, xn_dst):
        r = (h[...] + _dot(om[...].astype(BF16), w_ref[0:half, :])
             + _dot(oa[...].astype(BF16), w_ref[half:, :]))
        dst[...] = r
        xn_dst[...] = _rms(r, g_ref[...]).astype(BF16)

    project(h_ref, om_ref, oa_ref, o_ref, xn_ref)

    @pl.when(pl.program_id(0) == 0)
    def _():
        project(hs_ref, oms_ref, oas_ref, os_ref, xns_ref)


def _out_proj(h, out_m, out_a, hs, out_ms, out_as, g_ffn, w_out_bf16):
    m, ms = h.shape[0], hs.shape[0]
    tm = min(512, m)
    assert m % tm == 0
    half = M_HEADS * M_V_DIM
    rows = pl.BlockSpec((tm, D_MODEL), lambda i: (i, 0))
    mix = pl.BlockSpec((tm, half), lambda i: (i, 0))
    rows_s = pl.BlockSpec((ms, D_MODEL), lambda i: (0, 0))
    mix_s = pl.BlockSpec((ms, half), lambda i: (0, 0))
    return pl.pallas_call(
        _out_proj_kernel,
        grid=(m // tm,),
        in_specs=[
            rows, mix, mix, rows_s, mix_s, mix_s,
            pl.BlockSpec((1, D_MODEL), lambda i: (0, 0)),
            pl.BlockSpec((D_MODEL, D_MODEL), lambda i: (0, 0), pipeline_mode=pl.Buffered(1)),
        ],
        out_specs=[rows, rows, rows_s, rows_s],
        out_shape=[jax.ShapeDtypeStruct((m, D_MODEL), F32), jax.ShapeDtypeStruct((m, D_MODEL), BF16),
                   jax.ShapeDtypeStruct((ms, D_MODEL), F32), jax.ShapeDtypeStruct((ms, D_MODEL), BF16)],
        compiler_params=_cparams("arbitrary"),
        name="out_proj",
    )(h, out_m, out_a, hs, out_ms, out_as, g_ffn, w_out_bf16)


def _mlp_kernel(xn_ref, xns_ref, wu_ref, wd_ref, o_ref, os_ref):
    f = pl.program_id(1)

    def accumulate(xn, dst):
        @pl.when(f == 0)
        def _():
            dst[...] = jnp.zeros(dst.shape, F32)

        u = _dot(xn[...], wu_ref[...])
        a = jnp.square(jnp.maximum(u, 0.0)).astype(BF16)
        dst[...] += _dot(a, wd_ref[...])

    accumulate(xn_ref, o_ref)

    @pl.when(pl.program_id(0) == 0)
    def _():
        accumulate(xns_ref, os_ref)


def _mlp(xn, xns, w_up, w_down):
    m, ms = xn.shape[0], xns.shape[0]
    tm = 1024 if m % 1024 == 0 else min(512, m)
    assert m % tm == 0
    tf = 1024
    return pl.pallas_call(
        _mlp_kernel,
        grid=(m // tm, D_FF // tf),
        in_specs=[
            pl.BlockSpec((tm, D_MODEL), lambda i, f: (i, 0)),
            pl.BlockSpec((ms, D_MODEL), lambda i, f: (0, 0)),
            pl.BlockSpec((D_MODEL, tf), lambda i, f: (0, f)),
            pl.BlockSpec((tf, D_MODEL), lambda i, f: (f, 0)),
        ],
        out_specs=[pl.BlockSpec((tm, D_MODEL), lambda i, f: (i, 0)),
                   pl.BlockSpec((ms, D_MODEL), lambda i, f: (0, 0))],
        out_shape=[jax.ShapeDtypeStruct((m, D_MODEL), F32), jax.ShapeDtypeStruct((ms, D_MODEL), F32)],
        compiler_params=_cparams("arbitrary", "arbitrary"),
        name="mlp",
    )(xn, xns, w_up, w_down)


def _ple_kernel(h_ref, d_ref, pe_ref, hs_ref, ds_ref, pes_ref, g_ref, wg_ref, wp_ref, gf_ref, y_ref, ys_ref):
    def gated_embedding(src, delta, pe, dst):
        h = src[...] + delta[...]
        gate = jax.nn.sigmoid(_dot(_rms(h, g_ref[...]).astype(BF16), wg_ref[...]))
        proj = _dot(pe[...].astype(BF16), wp_ref[...])
        dst[...] = _rms(h + gate * proj, gf_ref[...])

    gated_embedding(h_ref, d_ref, pe_ref, y_ref)

    @pl.when(pl.program_id(0) == 0)
    def _():
        gated_embedding(hs_ref, ds_ref, pes_ref, ys_ref)


def _ple_final(h, delta, pe, hs, deltas, pes, g_ple, w_gate, w_proj, g_final):
    m, ms = h.shape[0], hs.shape[0]
    tm = min(512, m)
    assert m % tm == 0
    vec = pl.BlockSpec((1, D_MODEL), lambda i: (0, 0))
    rows = pl.BlockSpec((tm, D_MODEL), lambda i: (i, 0))
    rows_s = pl.BlockSpec((ms, D_MODEL), lambda i: (0, 0))
    return pl.pallas_call(
        _ple_kernel,
        grid=(m // tm,),
        in_specs=[
            rows, rows,
            pl.BlockSpec((tm, PLE_DIM), lambda i: (i, 0)),
            rows_s, rows_s,
            pl.BlockSpec((ms, PLE_DIM), lambda i: (0, 0)),
            vec,
            pl.BlockSpec((D_MODEL, D_MODEL), lambda i: (0, 0), pipeline_mode=pl.Buffered(1)),
            pl.BlockSpec((PLE_DIM, D_MODEL), lambda i: (0, 0), pipeline_mode=pl.Buffered(1)),
            vec,
        ],
        out_specs=[rows, rows_s],
        out_shape=[jax.ShapeDtypeStruct((m, D_MODEL), F32), jax.ShapeDtypeStruct((ms, D_MODEL), F32)],
        compiler_params=_cparams("arbitrary"),
        name="ple_final_norm",
    )(h, delta, pe, hs, deltas, pes, g_ple, w_gate, w_proj, g_final)


def _layer_weights(w_in, b_igate, b_fgate, g_mix, g_mhead, w_out, g_ffn, w_up, w_down,
                   g_ple, w_ple_gate, w_ple_proj):
    gate_bias = jnp.pad(jnp.concatenate([b_igate, b_fgate]), (0, GATE_PAD - 2 * M_HEADS)).reshape(1, GATE_PAD)
    return dict(
        w_in=w_in.T, gate_bias=gate_bias.astype(F32),
        g_mix=g_mix.reshape(1, D_MODEL), g_mhead=g_mhead, w_out=w_out,
        g_ffn=g_ffn.reshape(1, D_MODEL), w_up=w_up, w_down=w_down,
        g_ple=g_ple.reshape(1, D_MODEL), w_ple_gate=w_ple_gate,
        w_ple_proj=w_ple_proj.astype(BF16))


def _tail(h, out_m, out_a, pe, hs, out_ms, out_as, pes, w, g_final):
    h, xn, hs, xns = _out_proj(h, out_m, out_a, hs, out_ms, out_as, w["g_ffn"], w["w_out_bf16"])
    delta, deltas = _mlp(xn, xns, w["w_up_bf16"], w["w_down_bf16"])
    return _ple_final(h, delta, pe, hs, deltas, pes, w["g_ple"], w["w_ple_gate_bf16"], w["w_ple_proj"], g_final)


def _split_state(c_aug, m_fin):
    return c_aug[..., :M_V_DIM], c_aug[..., M_V_DIM], m_fin[..., 0, 0]


def kernel(x_prompt, x_sample, cache_k, cache_v, state_C, state_n, state_m, page_table, p_prompt, p_sample, rel_bias_table, w_in, b_igate, b_fgate, g_mix, g_mhead, w_out, g_ffn, w_up, w_down, g_ple, w_ple_gate, w_ple_proj, g_final):
    depth = w_in.shape[0]
    assert depth == 1, "one decoder layer per call"
    layer = 0
    bp, tp, _ = x_prompt.shape
    bs, ts, _ = x_sample.shape
    past_len = page_table.shape[1] * PAGE_SIZE
    assert past_len % MOBA_BLOCK == 0 and ts <= MOBA_BLOCK
    kv_shape = (A_HEADS, A_HEAD_DIM)
    g_fin = g_final.reshape(1, D_MODEL)

    w = _layer_weights(w_in[layer], b_igate[layer], b_fgate[layer], g_mix[layer], g_mhead[layer], w_out[layer],
                       g_ffn[layer], w_up[layer], w_down[layer], g_ple[layer], w_ple_gate[layer],
                       w_ple_proj[layer])
    bias_tiles = _bias_tiles(rel_bias_table)

    hp = x_prompt.reshape(bp * tp, D_MODEL)
    hs = x_sample.reshape(bs * ts, D_MODEL)
    xn_p, gp, xn_s, gs = _norm_gate(hp, hs, w["g_mix"], w["w_in"])
    zp, aq_p, ak_p, av_p, zs, zs_a = _in_proj(xn_p, xn_s, w["w_in"])
    aq_s, ak_s, av_s = (zs_a[:, t * _IN_TN:(t + 1) * _IN_TN] for t in range(3))
    c0 = jnp.zeros((bp, M_HEADS, M_QK_DIM, 2 * M_V_DIM), F32)
    m0 = jnp.zeros((bp, M_HEADS, 1, GATE_PAD), F32)
    om_p, c_p, m_p = _mlstm(zp, gp, w["gate_bias"], w["g_mhead"], c0, m0, bp, tp)
    oa_p, w["w_up_bf16"], w["w_down_bf16"], w["w_ple_gate_bf16"], w["w_out_bf16"] = _moba_prompt(
        aq_p, ak_p, av_p, bias_tiles, w["w_up"], w["w_down"], w["w_ple_gate"], w["w_out"], bp, tp)
    k_p = ak_p.reshape((1, bp, tp) + kv_shape)
    v_p = av_p.reshape((1, bp, tp) + kv_shape)
    cp, np_, mp = _split_state(c_p, m_p)

    c0s = jnp.concatenate([state_C[layer], state_n[layer][..., None],
                           jnp.zeros((bs, M_HEADS, M_QK_DIM, M_V_DIM - 1), F32)], axis=-1)
    m0s = jnp.broadcast_to(state_m[layer][..., None, None], (bs, M_HEADS, 1, GATE_PAD))
    om_s, c_s, m_s = _mlstm(zs, gs, w["gate_bias"], w["g_mhead"], c0s, m0s, bs, ts)

    hq = A_HEADS * ts
    q_s = aq_s.reshape(bs, ts, A_HEADS, A_HEAD_DIM)
    q_hq = q_s.transpose(0, 2, 1, 3).reshape(bs, hq, A_HEAD_DIM)
    k_new = ak_s.reshape(bs, hq, A_HEAD_DIM)
    v_new = av_s.reshape(bs, hq, A_HEAD_DIM)
    rows = bias_tiles[:, :, :, :ts].transpose(0, 1, 3, 2)
    expand = lambda a: jnp.repeat(a, A_HEADS, axis=-1).reshape(hq, -1)
    past_bias = jnp.stack([rows[:, 2], rows[:, 1]])
    own_bias = expand(rows[:, 0, :, :ts])
    o_part, m_part, l_part, kmean = _moba_past(page_table, q_hq, cache_k, cache_v, past_bias, layer)
    oa_hq = _moba_combine(q_hq, kmean.transpose(0, 2, 1, 3), k_new, v_new, own_bias, o_part, m_part, l_part)
    oa_s = oa_hq.reshape(bs, A_HEADS, ts, A_HEAD_DIM).transpose(0, 2, 1, 3).reshape(bs * ts, A_HEADS * A_HEAD_DIM)
    y_p, y_s = _tail(hp, om_p, oa_p, p_prompt[layer].reshape(bp * tp, PLE_DIM),
                     hs, om_s, oa_s, p_sample[layer].reshape(bs * ts, PLE_DIM), w, g_fin)
    k_s = k_new.reshape((1, bs, ts) + kv_shape)
    v_s = v_new.reshape((1, bs, ts) + kv_shape)
    cs, ns, ms = _split_state(c_s, m_s)

    return (y_p.reshape(bp, tp, D_MODEL), y_s.reshape(bs, ts, D_MODEL),
            k_p, v_p, cp[None], np_[None], mp[None],
            k_s, v_s, cs[None], ns[None], ms[None])
```
